```python
import math
import jax, jax.numpy as jnp
from jax import lax
import numpy as np

D_MODEL = 1024
BATCH = 8
SEQ = 16384
DEPTH = 2

CONV_DIM = 512
CONV_WIDTH = 31
GDN_HEADS = 4
GDN_DK = 128
GDN_DV = 128
GDN_QK = GDN_HEADS * GDN_DK
GDN_V = GDN_HEADS * GDN_DV
GDN_CONV = 4
GDN_CHUNK = 64
SGU_GROUPS = 4
SGU_GROUP_DIM = 128
SGU_DIM = SGU_GROUPS * SGU_GROUP_DIM
SGU_CHUNK = 128
D_FF = 4 * D_MODEL
DN_ALPHA = (2 * DEPTH) ** 0.25
DN_BETA = (8 * DEPTH) ** -0.25
LN_EPS = 1e-5
RMS_EPS = 1e-6
PROJ_SIZES = (2 * CONV_DIM, GDN_QK, GDN_QK, GDN_V, GDN_V, GDN_HEADS, GDN_HEADS, 2 * SGU_DIM, 3 * D_MODEL)
PROJ_COLS = sum(PROJ_SIZES)

kernel_name = "gated_parallel_conv_deltanet_sgu_deepnorm"


def layer_norm(x, g, b):
    xf = x.astype(jnp.float32)
    mu = jnp.mean(xf, -1, keepdims=True)
    var = jnp.mean(jnp.square(xf - mu), -1, keepdims=True)
    return ((xf - mu) * lax.rsqrt(var + LN_EPS) * g.astype(jnp.float32) + b.astype(jnp.float32)).astype(x.dtype)


def causal_depthwise_conv(x, w):
    width, ch = w.shape
    return lax.conv_general_dilated(
        x, w[:, None, :].astype(x.dtype), window_strides=(1,), padding=[(width - 1, 0)],
        dimension_numbers=("NWC", "WIO", "NWC"), feature_group_count=ch)


def l2norm(x):
    xf = x.astype(jnp.float32)
    return xf * lax.rsqrt(jnp.sum(xf * xf, -1, keepdims=True) + RMS_EPS)


def conformer_conv(a, w_dw, b_dw, ln_g, ln_b):
    a1, a2 = jnp.split(a, 2, axis=-1)
    h = a1 * jax.nn.sigmoid(a2)
    h = causal_depthwise_conv(h, w_dw) + b_dw
    h = layer_norm(h, ln_g, ln_b)
    return jax.nn.silu(h)


def chunked_gated_delta_rule(q, k, v, g, beta):
    bsz, seq, nh, dk = q.shape
    dv = v.shape[-1]
    c = GDN_CHUNK
    n = seq // c

    def chunks(t):
        return jnp.swapaxes(t.reshape(bsz, n, c, nh, *t.shape[3:]), 2, 3)

    q, k, v, g, beta = chunks(q), chunks(k), chunks(v), chunks(g), chunks(beta)
    gam = jnp.cumsum(g, axis=-1)
    causal = jnp.tril(jnp.ones((c, c), bool))
    strict = jnp.tril(jnp.ones((c, c), bool), -1)
    decay = jnp.exp(jnp.where(causal, gam[..., :, None] - gam[..., None, :], -jnp.inf))
    kk = jnp.einsum("bnhid,bnhjd->bnhij", k, k)
    lower = jnp.where(strict, beta[..., None] * kk * decay, 0.0)
    t_mat = lower + jnp.eye(c, dtype=lower.dtype)
    rhs = jnp.concatenate([beta[..., None] * v, beta[..., None] * k * jnp.exp(gam)[..., None]], -1)
    sol = lax.linalg.triangular_solve(t_mat, rhs, left_side=True, lower=True, unit_diagonal=True)
    u, w = sol[..., :dv], sol[..., dv:]
    a_qk = jnp.einsum("bnhid,bnhjd->bnhij", q, k) * decay
    q_dec = q * jnp.exp(gam)[..., None]
    g_last = gam[..., -1]
    k_dec = k * jnp.exp(g_last[..., None] - gam)[..., None]

    def step(state, inp):
        q_i, k_i, u_i, w_i, a_i, gl = inp
        v_new = u_i - jnp.einsum("bhcd,bhde->bhce", w_i, state)
        o = jnp.einsum("bhcd,bhde->bhce", q_i, state) + jnp.einsum("bhij,bhje->bhie", a_i, v_new)
        state = state * jnp.exp(gl)[..., None, None] + jnp.einsum("bhcd,bhce->bhde", k_i, v_new)
        return state, o

    xs = (jnp.moveaxis(q_dec, 1, 0), jnp.moveaxis(k_dec, 1, 0), jnp.moveaxis(u, 1, 0),
          jnp.moveaxis(w, 1, 0), jnp.moveaxis(a_qk, 1, 0), jnp.moveaxis(g_last, 1, 0))
    s0 = jnp.zeros((bsz, nh, dk, dv), jnp.float32)
    _, o = lax.scan(step, s0, xs)
    return jnp.swapaxes(jnp.moveaxis(o, 0, 1), 2, 3).reshape(bsz, seq, nh, dv)


def gated_deltanet(q, k, v, z, b_logit, a_logit, conv_q, conv_k, conv_v, a_log, dt_bias, norm_g):
    bsz, seq, _ = q.shape
    dtype = q.dtype
    q = jax.nn.silu(causal_depthwise_conv(q, conv_q)).reshape(bsz, seq, GDN_HEADS, GDN_DK)
    k = jax.nn.silu(causal_depthwise_conv(k, conv_k)).reshape(bsz, seq, GDN_HEADS, GDN_DK)
    v = jax.nn.silu(causal_depthwise_conv(v, conv_v)).reshape(bsz, seq, GDN_HEADS, GDN_DV)
    q = l2norm(q) * (GDN_DK ** -0.5)
    k = l2norm(k)
    beta = jax.nn.sigmoid(b_logit.astype(jnp.float32))
    g = -jnp.exp(a_log.astype(jnp.float32)) * jax.nn.softplus(
        a_logit.astype(jnp.float32) + dt_bias.astype(jnp.float32))
    o = chunked_gated_delta_rule(q, k, v.astype(jnp.float32), g, beta)
    zf = z.reshape(bsz, seq, GDN_HEADS, GDN_DV).astype(jnp.float32)
    o = o * lax.rsqrt(jnp.mean(o * o, -1, keepdims=True) + RMS_EPS) * norm_g.astype(jnp.float32)
    o = o * jax.nn.silu(zf)
    return o.reshape(bsz, seq, GDN_V).astype(dtype)


def spatial_gating(uv, ln_g, ln_b, w_s, b_s):
    bsz, seq, _ = uv.shape
    n = seq // SGU_CHUNK
    uv = jax.nn.gelu(uv)
    u, v = jnp.split(uv, 2, axis=-1)
    v = layer_norm(v, ln_g, ln_b)
    v = v.reshape(bsz, n, SGU_CHUNK, SGU_GROUPS, SGU_GROUP_DIM)
    w_causal = jnp.tril(w_s).astype(v.dtype)
    mixed = jnp.einsum("gpq,bnqgc->bnpgc", w_causal, v) + jnp.swapaxes(b_s, 0, 1)[None, None, :, :, None].astype(v.dtype)
    return u * mixed.reshape(bsz, seq, SGU_DIM)


def token_mixer(x, w_in, b_gate, conv_dw_w, conv_dw_b, conv_ln_g, conv_ln_b, w_pa,
                gdn_conv_q, gdn_conv_k, gdn_conv_v, gdn_a_log, gdn_dt_bias, gdn_norm_g, w_pb,
                sgu_ln_g, sgu_ln_b, sgu_w_s, sgu_b_s, w_pc, w_o):
    p = x @ w_in
    split_points = [int(s) for s in np.cumsum(PROJ_SIZES)[:-1]]
    a, q, k, v, z, b_logit, a_logit, uv, gate_logit = jnp.split(p, split_points, axis=-1)
    y_a = conformer_conv(a, conv_dw_w, conv_dw_b, conv_ln_g, conv_ln_b) @ w_pa
    y_b = gated_deltanet(q, k, v, z, b_logit, a_logit, gdn_conv_q, gdn_conv_k, gdn_conv_v,
                         gdn_a_log, gdn_dt_bias, gdn_norm_g) @ w_pb
    y_c = spatial_gating(uv, sgu_ln_g, sgu_ln_b, sgu_w_s, sgu_b_s) @ w_pc
    s_a, s_b, s_c = jnp.split(jax.nn.sigmoid(gate_logit + b_gate), 3, axis=-1)
    return (s_a * y_a + s_b * y_b + s_c * y_c) @ w_o


def _fwd_setup_inputs(seed: int = 0) -> dict:
    key = jax.random.key(seed)
    ks = jax.random.split(key, 32)
    L = DEPTH
    f32 = jnp.float32

    def nrm(i, shape, scale):
        return jax.random.normal(ks[i], shape, f32) * scale

    dt = jnp.exp(jax.random.uniform(ks[14], (L, GDN_HEADS), f32) * (math.log(0.1) - math.log(0.001)) + math.log(0.001))
    return {
        "x": nrm(0, (BATCH, SEQ, D_MODEL), 1.0),
        "ln_in_g": 1.0 + nrm(1, (D_MODEL,), 0.02),
        "ln_in_b": nrm(2, (D_MODEL,), 0.02),
        "w_in": nrm(3, (L, D_MODEL, PROJ_COLS), D_MODEL ** -0.5),
        "b_gate": nrm(4, (L, 3 * D_MODEL), 0.1),
        "conv_dw_w": nrm(5, (L, CONV_WIDTH, CONV_DIM), CONV_WIDTH ** -0.5),
        "conv_dw_b": nrm(6, (L, CONV_DIM), 0.02),
        "conv_ln_g": 1.0 + nrm(7, (L, CONV_DIM), 0.02),
        "conv_ln_b": nrm(8, (L, CONV_DIM), 0.02),
        "w_pa": nrm(9, (L, CONV_DIM, D_MODEL), DN_BETA * CONV_DIM ** -0.5),
        "gdn_conv_q": nrm(10, (L, GDN_CONV, GDN_QK), GDN_CONV ** -0.5),
        "gdn_conv_k": nrm(11, (L, GDN_CONV, GDN_QK), GDN_CONV ** -0.5),
        "gdn_conv_v": nrm(12, (L, GDN_CONV, GDN_V), GDN_CONV ** -0.5),
        "gdn_a_log": jnp.log(jax.random.uniform(ks[13], (L, GDN_HEADS), f32, minval=1.0, maxval=16.0)),
        "gdn_dt_bias": dt + jnp.log(-jnp.expm1(-dt)),
        "gdn_norm_g": 1.0 + nrm(15, (L, GDN_DV), 0.02),
        "w_pb": nrm(16, (L, GDN_V, D_MODEL), DN_BETA * GDN_V ** -0.5),
        "sgu_ln_g": 1.0 + nrm(17, (L, SGU_DIM), 0.02),
        "sgu_ln_b": nrm(18, (L, SGU_DIM), 0.02),
        "sgu_w_s": nrm(19, (L, SGU_GROUPS, SGU_CHUNK, SGU_CHUNK), SGU_CHUNK ** -0.5),
        "sgu_b_s": 1.0 + nrm(20, (L, SGU_GROUPS, SGU_CHUNK), 0.02),
        "w_pc": nrm(21, (L, SGU_DIM, D_MODEL), DN_BETA * SGU_DIM ** -0.5),
        "w_o": nrm(22, (L, D_MODEL, D_MODEL), DN_BETA * D_MODEL ** -0.5),
        "ln1_g": 1.0 + nrm(23, (L, D_MODEL), 0.02),
        "ln1_b": nrm(24, (L, D_MODEL), 0.02),
        "w_ff1": nrm(25, (L, D_MODEL, D_FF), D_MODEL ** -0.5),
        "b_ff1": nrm(26, (L, D_FF), 0.02),
        "w_ff2": nrm(27, (L, D_FF, D_MODEL), DN_BETA * D_FF ** -0.5),
        "b_ff2": nrm(28, (L, D_MODEL), 0.02),
        "ln2_g": 1.0 + nrm(29, (L, D_MODEL), 0.02),
        "ln2_b": nrm(30, (L, D_MODEL), 0.02),
    }


def _fwd_reference(x, ln_in_g, ln_in_b, w_in, b_gate, conv_dw_w, conv_dw_b, conv_ln_g, conv_ln_b, w_pa,
              gdn_conv_q, gdn_conv_k, gdn_conv_v, gdn_a_log, gdn_dt_bias, gdn_norm_g, w_pb,
              sgu_ln_g, sgu_ln_b, sgu_w_s, sgu_b_s, w_pc, w_o, ln1_g, ln1_b,
              w_ff1, b_ff1, w_ff2, b_ff2, ln2_g, ln2_b):
    x = layer_norm(x, ln_in_g, ln_in_b)
    for l in range(DEPTH):
        m = token_mixer(x, w_in[l], b_gate[l], conv_dw_w[l], conv_dw_b[l], conv_ln_g[l], conv_ln_b[l], w_pa[l],
                        gdn_conv_q[l], gdn_conv_k[l], gdn_conv_v[l], gdn_a_log[l], gdn_dt_bias[l], gdn_norm_g[l], w_pb[l],
                        sgu_ln_g[l], sgu_ln_b[l], sgu_w_s[l], sgu_b_s[l], w_pc[l], w_o[l])
        x = layer_norm(DN_ALPHA * x + m, ln1_g[l], ln1_b[l])
        h = jnp.square(jax.nn.relu(x @ w_ff1[l] + b_ff1[l]))
        x = layer_norm(DN_ALPHA * x + (h @ w_ff2[l] + b_ff2[l]), ln2_g[l], ln2_b[l])
    return x


import jax as _jax
import jax.numpy as _jnp

TWIN_FORMAT = 'train_step'
FWD_PARAMS = ['x', 'ln_in_g', 'ln_in_b', 'w_in', 'b_gate', 'conv_dw_w', 'conv_dw_b', 'conv_ln_g', 'conv_ln_b', 'w_pa', 'gdn_conv_q', 'gdn_conv_k', 'gdn_conv_v', 'gdn_a_log', 'gdn_dt_bias', 'gdn_norm_g', 'w_pb', 'sgu_ln_g', 'sgu_ln_b', 'sgu_w_s', 'sgu_b_s', 'w_pc', 'w_o', 'ln1_g', 'ln1_b', 'w_ff1', 'b_ff1', 'w_ff2', 'b_ff2', 'ln2_g', 'ln2_b']
TWIN_WEIGHTS = ['ln_in_g', 'ln_in_b', 'w_in', 'b_gate', 'conv_dw_w', 'conv_dw_b', 'conv_ln_g', 'conv_ln_b', 'w_pa', 'gdn_conv_q', 'gdn_conv_k', 'gdn_conv_v', 'gdn_a_log', 'gdn_dt_bias', 'gdn_norm_g', 'w_pb', 'sgu_ln_g', 'sgu_ln_b', 'sgu_w_s', 'sgu_b_s', 'w_pc', 'w_o', 'ln1_g', 'ln1_b', 'w_ff1', 'b_ff1', 'w_ff2', 'b_ff2', 'ln2_g', 'ln2_b']
TWIN_DIFF_INPUT = 'x'
TWIN_INPUTS = ['x', 'ln_in_g', 'ln_in_b', 'w_in', 'b_gate', 'conv_dw_w', 'conv_dw_b', 'conv_ln_g', 'conv_ln_b', 'w_pa', 'gdn_conv_q', 'gdn_conv_k', 'gdn_conv_v', 'gdn_a_log', 'gdn_dt_bias', 'gdn_norm_g', 'w_pb', 'sgu_ln_g', 'sgu_ln_b', 'sgu_w_s', 'sgu_b_s', 'w_pc', 'w_o', 'ln1_g', 'ln1_b', 'w_ff1', 'b_ff1', 'w_ff2', 'b_ff2', 'ln2_g', 'ln2_b', 'loss_target', 'm_ln_in_g', 'm_ln_in_b', 'm_w_in', 'm_b_gate', 'm_conv_dw_w', 'm_conv_dw_b', 'm_conv_ln_g', 'm_conv_ln_b', 'm_w_pa', 'm_gdn_conv_q', 'm_gdn_conv_k', 'm_gdn_conv_v', 'm_gdn_a_log', 'm_gdn_dt_bias', 'm_gdn_norm_g', 'm_w_pb', 'm_sgu_ln_g', 'm_sgu_ln_b', 'm_sgu_w_s', 'm_sgu_b_s', 'm_w_pc', 'm_w_o', 'm_ln1_g', 'm_ln1_b', 'm_w_ff1', 'm_b_ff1', 'm_w_ff2', 'm_b_ff2', 'm_ln2_g', 'm_ln2_b', 'v_ln_in_g', 'v_ln_in_b', 'v_w_in', 'v_b_gate', 'v_conv_dw_w', 'v_conv_dw_b', 'v_conv_ln_g', 'v_conv_ln_b', 'v_w_pa', 'v_gdn_conv_q', 'v_gdn_conv_k', 'v_gdn_conv_v', 'v_gdn_a_log', 'v_gdn_dt_bias', 'v_gdn_norm_g', 'v_w_pb', 'v_sgu_ln_g', 'v_sgu_ln_b', 'v_sgu_w_s', 'v_sgu_b_s', 'v_w_pc', 'v_w_o', 'v_ln1_g', 'v_ln1_b', 'v_w_ff1', 'v_b_ff1', 'v_w_ff2', 'v_b_ff2', 'v_ln2_g', 'v_ln2_b']
TWIN_OUTPUTS = ['loss', 'grad_x', 'grad_ln_in_g', 'grad_ln_in_b', 'grad_w_in', 'grad_b_gate', 'grad_conv_dw_w', 'grad_conv_dw_b', 'grad_conv_ln_g', 'grad_conv_ln_b', 'grad_w_pa', 'grad_gdn_conv_q', 'grad_gdn_conv_k', 'grad_gdn_conv_v', 'grad_gdn_a_log', 'grad_gdn_dt_bias', 'grad_gdn_norm_g', 'grad_w_pb', 'grad_sgu_ln_g', 'grad_sgu_ln_b', 'grad_sgu_w_s', 'grad_sgu_b_s', 'grad_w_pc', 'grad_w_o', 'grad_ln1_g', 'grad_ln1_b', 'grad_w_ff1', 'grad_b_ff1', 'grad_w_ff2', 'grad_b_ff2', 'grad_ln2_g', 'grad_ln2_b', 'delta_ln_in_g', 'delta_ln_in_b', 'delta_w_in', 'delta_b_gate', 'delta_conv_dw_w', 'delta_conv_dw_b', 'delta_conv_ln_g', 'delta_conv_ln_b', 'delta_w_pa', 'delta_gdn_conv_q', 'delta_gdn_conv_k', 'delta_gdn_conv_v', 'delta_gdn_a_log', 'delta_gdn_dt_bias', 'delta_gdn_norm_g', 'delta_w_pb', 'delta_sgu_ln_g', 'delta_sgu_ln_b', 'delta_sgu_w_s', 'delta_sgu_b_s', 'delta_w_pc', 'delta_w_o', 'delta_ln1_g', 'delta_ln1_b', 'delta_w_ff1', 'delta_b_ff1', 'delta_w_ff2', 'delta_b_ff2', 'delta_ln2_g', 'delta_ln2_b', 'new_m_ln_in_g', 'new_m_ln_in_b', 'new_m_w_in', 'new_m_b_gate', 'new_m_conv_dw_w', 'new_m_conv_dw_b', 'new_m_conv_ln_g', 'new_m_conv_ln_b', 'new_m_w_pa', 'new_m_gdn_conv_q', 'new_m_gdn_conv_k', 'new_m_gdn_conv_v', 'new_m_gdn_a_log', 'new_m_gdn_dt_bias', 'new_m_gdn_norm_g', 'new_m_w_pb', 'new_m_sgu_ln_g', 'new_m_sgu_ln_b', 'new_m_sgu_w_s', 'new_m_sgu_b_s', 'new_m_w_pc', 'new_m_w_o', 'new_m_ln1_g', 'new_m_ln1_b', 'new_m_w_ff1', 'new_m_b_ff1', 'new_m_w_ff2', 'new_m_b_ff2', 'new_m_ln2_g', 'new_m_ln2_b', 'new_v_ln_in_g', 'new_v_ln_in_b', 'new_v_w_in', 'new_v_b_gate', 'new_v_conv_dw_w', 'new_v_conv_dw_b', 'new_v_conv_ln_g', 'new_v_conv_ln_b', 'new_v_w_pa', 'new_v_gdn_conv_q', 'new_v_gdn_conv_k', 'new_v_gdn_conv_v', 'new_v_gdn_a_log', 'new_v_gdn_dt_bias', 'new_v_gdn_norm_g', 'new_v_w_pb', 'new_v_sgu_ln_g', 'new_v_sgu_ln_b', 'new_v_sgu_w_s', 'new_v_sgu_b_s', 'new_v_w_pc', 'new_v_w_o', 'new_v_ln1_g', 'new_v_ln1_b', 'new_v_w_ff1', 'new_v_b_ff1', 'new_v_w_ff2', 'new_v_b_ff2', 'new_v_ln2_g', 'new_v_ln2_b']
TWIN_LEAF_KINDS = {'loss': 'loss', 'grad_x': 'grad_x', 'grad_ln_in_g': 'grad_w', 'grad_ln_in_b': 'grad_w', 'grad_w_in': 'grad_w', 'grad_b_gate': 'grad_w', 'grad_conv_dw_w': 'grad_w', 'grad_conv_dw_b': 'grad_w', 'grad_conv_ln_g': 'grad_w', 'grad_conv_ln_b': 'grad_w', 'grad_w_pa': 'grad_w', 'grad_gdn_conv_q': 'grad_w', 'grad_gdn_conv_k': 'grad_w', 'grad_gdn_conv_v': 'grad_w', 'grad_gdn_a_log': 'grad_w', 'grad_gdn_dt_bias': 'grad_w', 'grad_gdn_norm_g': 'grad_w', 'grad_w_pb': 'grad_w', 'grad_sgu_ln_g': 'grad_w', 'grad_sgu_ln_b': 'grad_w', 'grad_sgu_w_s': 'grad_w', 'grad_sgu_b_s': 'grad_w', 'grad_w_pc': 'grad_w', 'grad_w_o': 'grad_w', 'grad_ln1_g': 'grad_w', 'grad_ln1_b': 'grad_w', 'grad_w_ff1': 'grad_w', 'grad_b_ff1': 'grad_w', 'grad_w_ff2': 'grad_w', 'grad_b_ff2': 'grad_w', 'grad_ln2_g': 'grad_w', 'grad_ln2_b': 'grad_w', 'delta_ln_in_g': 'delta_w', 'delta_ln_in_b': 'delta_w', 'delta_w_in': 'delta_w', 'delta_b_gate': 'delta_w', 'delta_conv_dw_w': 'delta_w', 'delta_conv_dw_b': 'delta_w', 'delta_conv_ln_g': 'delta_w', 'delta_conv_ln_b': 'delta_w', 'delta_w_pa': 'delta_w', 'delta_gdn_conv_q': 'delta_w', 'delta_gdn_conv_k': 'delta_w', 'delta_gdn_conv_v': 'delta_w', 'delta_gdn_a_log': 'delta_w', 'delta_gdn_dt_bias': 'delta_w', 'delta_gdn_norm_g': 'delta_w', 'delta_w_pb': 'delta_w', 'delta_sgu_ln_g': 'delta_w', 'delta_sgu_ln_b': 'delta_w', 'delta_sgu_w_s': 'delta_w', 'delta_sgu_b_s': 'delta_w', 'delta_w_pc': 'delta_w', 'delta_w_o': 'delta_w', 'delta_ln1_g': 'delta_w', 'delta_ln1_b': 'delta_w', 'delta_w_ff1': 'delta_w', 'delta_b_ff1': 'delta_w', 'delta_w_ff2': 'delta_w', 'delta_b_ff2': 'delta_w', 'delta_ln2_g': 'delta_w', 'delta_ln2_b': 'delta_w', 'new_m_ln_in_g': 'new_m', 'new_m_ln_in_b': 'new_m', 'new_m_w_in': 'new_m', 'new_m_b_gate': 'new_m', 'new_m_conv_dw_w': 'new_m', 'new_m_conv_dw_b': 'new_m', 'new_m_conv_ln_g': 'new_m', 'new_m_conv_ln_b': 'new_m', 'new_m_w_pa': 'new_m', 'new_m_gdn_conv_q': 'new_m', 'new_m_gdn_conv_k': 'new_m', 'new_m_gdn_conv_v': 'new_m', 'new_m_gdn_a_log': 'new_m', 'new_m_gdn_dt_bias': 'new_m', 'new_m_gdn_norm_g': 'new_m', 'new_m_w_pb': 'new_m', 'new_m_sgu_ln_g': 'new_m', 'new_m_sgu_ln_b': 'new_m', 'new_m_sgu_w_s': 'new_m', 'new_m_sgu_b_s': 'new_m', 'new_m_w_pc': 'new_m', 'new_m_w_o': 'new_m', 'new_m_ln1_g': 'new_m', 'new_m_ln1_b': 'new_m', 'new_m_w_ff1': 'new_m', 'new_m_b_ff1': 'new_m', 'new_m_w_ff2': 'new_m', 'new_m_b_ff2': 'new_m', 'new_m_ln2_g': 'new_m', 'new_m_ln2_b': 'new_m', 'new_v_ln_in_g': 'new_v', 'new_v_ln_in_b': 'new_v', 'new_v_w_in': 'new_v', 'new_v_b_gate': 'new_v', 'new_v_conv_dw_w': 'new_v', 'new_v_conv_dw_b': 'new_v', 'new_v_conv_ln_g': 'new_v', 'new_v_conv_ln_b': 'new_v', 'new_v_w_pa': 'new_v', 'new_v_gdn_conv_q': 'new_v', 'new_v_gdn_conv_k': 'new_v', 'new_v_gdn_conv_v': 'new_v', 'new_v_gdn_a_log': 'new_v', 'new_v_gdn_dt_bias': 'new_v', 'new_v_gdn_norm_g': 'new_v', 'new_v_w_pb': 'new_v', 'new_v_sgu_ln_g': 'new_v', 'new_v_sgu_ln_b': 'new_v', 'new_v_sgu_w_s': 'new_v', 'new_v_sgu_b_s': 'new_v', 'new_v_w_pc': 'new_v', 'new_v_w_o': 'new_v', 'new_v_ln1_g': 'new_v', 'new_v_ln1_b': 'new_v', 'new_v_w_ff1': 'new_v', 'new_v_b_ff1': 'new_v', 'new_v_w_ff2': 'new_v', 'new_v_b_ff2': 'new_v', 'new_v_ln2_g': 'new_v', 'new_v_ln2_b': 'new_v'}


def _forward(args):
    return _fwd_reference(*[args[k] for k in FWD_PARAMS])


def _output_shape():
    def fwd():
        inp = _fwd_setup_inputs(0)
        return _fwd_reference(*[inp[k] for k in FWD_PARAMS])
    out = _jax.eval_shape(fwd)
    return out.shape, out.dtype

N_MICROBATCH = 1
ADAM_LR = 0.001
ADAM_B1 = 0.9
ADAM_B2 = 0.999
ADAM_EPS = 1e-08
ADAM_WD = 0.01
ADAM_STEP = 10
PER_EXAMPLE_BATCH_AXIS = {'x': 0, 'loss_target': 0}
SHARED_INPUTS = []
_WEIGHT_DTYPES = {'ln_in_g': _jnp.float32, 'ln_in_b': _jnp.float32, 'w_in': _jnp.float32, 'b_gate': _jnp.float32, 'conv_dw_w': _jnp.float32, 'conv_dw_b': _jnp.float32, 'conv_ln_g': _jnp.float32, 'conv_ln_b': _jnp.float32, 'w_pa': _jnp.float32, 'gdn_conv_q': _jnp.float32, 'gdn_conv_k': _jnp.float32, 'gdn_conv_v': _jnp.float32, 'gdn_a_log': _jnp.float32, 'gdn_dt_bias': _jnp.float32, 'gdn_norm_g': _jnp.float32, 'w_pb': _jnp.float32, 'sgu_ln_g': _jnp.float32, 'sgu_ln_b': _jnp.float32, 'sgu_w_s': _jnp.float32, 'sgu_b_s': _jnp.float32, 'w_pc': _jnp.float32, 'w_o': _jnp.float32, 'ln1_g': _jnp.float32, 'ln1_b': _jnp.float32, 'w_ff1': _jnp.float32, 'b_ff1': _jnp.float32, 'w_ff2': _jnp.float32, 'b_ff2': _jnp.float32, 'ln2_g': _jnp.float32, 'ln2_b': _jnp.float32}
MOMENT_SCALE = {'ln_in_g': 2.690591e+00, 'ln_in_b': 1.696726e+00, 'w_in': 2.173288e-02, 'b_gate': 1.306971e-02, 'conv_dw_w': 3.208074e-02, 'conv_dw_b': 1.555682e-01, 'conv_ln_g': 5.893294e-02, 'conv_ln_b': 1.020625e-01, 'w_pa': 6.145334e-02, 'gdn_conv_q': 1.977725e-02, 'gdn_conv_k': 2.061924e-02, 'gdn_conv_v': 3.256202e-02, 'gdn_a_log': 1.985630e-01, 'gdn_dt_bias': 2.046662e-01, 'gdn_norm_g': 6.422276e-02, 'w_pb': 4.854703e-02, 'sgu_ln_g': 2.472380e-02, 'sgu_ln_b': 2.517274e-02, 'sgu_w_s': 2.305990e-02, 'sgu_b_s': 3.370590e-02, 'w_pc': 8.109852e-02, 'w_o': 1.107796e-01, 'ln1_g': 3.223310e+00, 'ln1_b': 1.780558e+00, 'w_ff1': 8.530896e-02, 'b_ff1': 2.145190e-01, 'w_ff2': 5.186101e-01, 'b_ff2': 1.265918e+00, 'ln2_g': 9.097993e+01, 'ln2_b': 2.042686e+01}


def _to_microbatches(a, axis):
    t = _jnp.moveaxis(a, axis, 0)
    t = t.reshape((N_MICROBATCH, t.shape[0] // N_MICROBATCH) + t.shape[1:])
    return _jnp.moveaxis(t, 1, axis + 1)


def setup_inputs(seed: int = 0) -> dict:
    inp = _fwd_setup_inputs(seed)
    key = _jax.random.fold_in(_jax.random.key(seed), 7919)
    shape, _ = _output_shape()
    out = dict(inp)
    out["loss_target"] = _jax.random.normal(_jax.random.fold_in(key, 0), shape, _jnp.float32)
    for i, name in enumerate(TWIN_WEIGHTS):
        w = inp[name].astype(_jnp.float32)
        if MOMENT_SCALE is None:
            s = _jnp.sqrt(_jnp.mean(_jnp.square(w)) + 1e-30)
        else:
            s = MOMENT_SCALE[name]
        km, kv = _jax.random.split(_jax.random.fold_in(key, i + 1))
        out[name] = w
        out["m_" + name] = s * _jax.random.normal(km, w.shape, _jnp.float32)
        out["v_" + name] = (s * s) * _jax.random.uniform(kv, w.shape, _jnp.float32, 0.5, 1.5)
    if N_MICROBATCH > 1:
        for name, axis in PER_EXAMPLE_BATCH_AXIS.items():
            out[name] = _to_microbatches(out[name], axis)
    return {'x': out['x'], 'ln_in_g': out['ln_in_g'], 'ln_in_b': out['ln_in_b'], 'w_in': out['w_in'], 'b_gate': out['b_gate'], 'conv_dw_w': out['conv_dw_w'], 'conv_dw_b': out['conv_dw_b'], 'conv_ln_g': out['conv_ln_g'], 'conv_ln_b': out['conv_ln_b'], 'w_pa': out['w_pa'], 'gdn_conv_q': out['gdn_conv_q'], 'gdn_conv_k': out['gdn_conv_k'], 'gdn_conv_v': out['gdn_conv_v'], 'gdn_a_log': out['gdn_a_log'], 'gdn_dt_bias': out['gdn_dt_bias'], 'gdn_norm_g': out['gdn_norm_g'], 'w_pb': out['w_pb'], 'sgu_ln_g': out['sgu_ln_g'], 'sgu_ln_b': out['sgu_ln_b'], 'sgu_w_s': out['sgu_w_s'], 'sgu_b_s': out['sgu_b_s'], 'w_pc': out['w_pc'], 'w_o': out['w_o'], 'ln1_g': out['ln1_g'], 'ln1_b': out['ln1_b'], 'w_ff1': out['w_ff1'], 'b_ff1': out['b_ff1'], 'w_ff2': out['w_ff2'], 'b_ff2': out['b_ff2'], 'ln2_g': out['ln2_g'], 'ln2_b': out['ln2_b'], 'loss_target': out['loss_target'], 'm_ln_in_g': out['m_ln_in_g'], 'm_ln_in_b': out['m_ln_in_b'], 'm_w_in': out['m_w_in'], 'm_b_gate': out['m_b_gate'], 'm_conv_dw_w': out['m_conv_dw_w'], 'm_conv_dw_b': out['m_conv_dw_b'], 'm_conv_ln_g': out['m_conv_ln_g'], 'm_conv_ln_b': out['m_conv_ln_b'], 'm_w_pa': out['m_w_pa'], 'm_gdn_conv_q': out['m_gdn_conv_q'], 'm_gdn_conv_k': out['m_gdn_conv_k'], 'm_gdn_conv_v': out['m_gdn_conv_v'], 'm_gdn_a_log': out['m_gdn_a_log'], 'm_gdn_dt_bias': out['m_gdn_dt_bias'], 'm_gdn_norm_g': out['m_gdn_norm_g'], 'm_w_pb': out['m_w_pb'], 'm_sgu_ln_g': out['m_sgu_ln_g'], 'm_sgu_ln_b': out['m_sgu_ln_b'], 'm_sgu_w_s': out['m_sgu_w_s'], 'm_sgu_b_s': out['m_sgu_b_s'], 'm_w_pc': out['m_w_pc'], 'm_w_o': out['m_w_o'], 'm_ln1_g': out['m_ln1_g'], 'm_ln1_b': out['m_ln1_b'], 'm_w_ff1': out['m_w_ff1'], 'm_b_ff1': out['m_b_ff1'], 'm_w_ff2': out['m_w_ff2'], 'm_b_ff2': out['m_b_ff2'], 'm_ln2_g': out['m_ln2_g'], 'm_ln2_b': out['m_ln2_b'], 'v_ln_in_g': out['v_ln_in_g'], 'v_ln_in_b': out['v_ln_in_b'], 'v_w_in': out['v_w_in'], 'v_b_gate': out['v_b_gate'], 'v_conv_dw_w': out['v_conv_dw_w'], 'v_conv_dw_b': out['v_conv_dw_b'], 'v_conv_ln_g': out['v_conv_ln_g'], 'v_conv_ln_b': out['v_conv_ln_b'], 'v_w_pa': out['v_w_pa'], 'v_gdn_conv_q': out['v_gdn_conv_q'], 'v_gdn_conv_k': out['v_gdn_conv_k'], 'v_gdn_conv_v': out['v_gdn_conv_v'], 'v_gdn_a_log': out['v_gdn_a_log'], 'v_gdn_dt_bias': out['v_gdn_dt_bias'], 'v_gdn_norm_g': out['v_gdn_norm_g'], 'v_w_pb': out['v_w_pb'], 'v_sgu_ln_g': out['v_sgu_ln_g'], 'v_sgu_ln_b': out['v_sgu_ln_b'], 'v_sgu_w_s': out['v_sgu_w_s'], 'v_sgu_b_s': out['v_sgu_b_s'], 'v_w_pc': out['v_w_pc'], 'v_w_o': out['v_w_o'], 'v_ln1_g': out['v_ln1_g'], 'v_ln1_b': out['v_ln1_b'], 'v_w_ff1': out['v_w_ff1'], 'v_b_ff1': out['v_b_ff1'], 'v_w_ff2': out['v_w_ff2'], 'v_b_ff2': out['v_b_ff2'], 'v_ln2_g': out['v_ln2_g'], 'v_ln2_b': out['v_ln2_b']}


def _loss(weights, diff, rest, loss_target):
    with _jax.named_scope("forward"):
        args = {**rest, TWIN_DIFF_INPUT: diff, **{k: w.astype(_WEIGHT_DTYPES[k]) for k, w in weights.items()}}
        y = _forward(args)
    with _jax.named_scope("loss_head"):
        err = _jnp.square(y.astype(_jnp.float32) - loss_target)
        return 0.5 * _jnp.sum(_jnp.mean(err, axis=-1)) if err.ndim else 0.5 * err


def _adamw(w, g, m, v):
    m = ADAM_B1 * m + (1.0 - ADAM_B1) * g
    v = ADAM_B2 * v + (1.0 - ADAM_B2) * _jnp.square(g)
    m_hat = m / (1.0 - ADAM_B1 ** ADAM_STEP)
    v_hat = v / (1.0 - ADAM_B2 ** ADAM_STEP)
    delta = -ADAM_LR * (m_hat / (_jnp.sqrt(v_hat) + ADAM_EPS) + ADAM_WD * w)
    return delta, m, v


def reference(x, ln_in_g, ln_in_b, w_in, b_gate, conv_dw_w, conv_dw_b, conv_ln_g, conv_ln_b, w_pa, gdn_conv_q, gdn_conv_k, gdn_conv_v, gdn_a_log, gdn_dt_bias, gdn_norm_g, w_pb, sgu_ln_g, sgu_ln_b, sgu_w_s, sgu_b_s, w_pc, w_o, ln1_g, ln1_b, w_ff1, b_ff1, w_ff2, b_ff2, ln2_g, ln2_b, loss_target, m_ln_in_g, m_ln_in_b, m_w_in, m_b_gate, m_conv_dw_w, m_conv_dw_b, m_conv_ln_g, m_conv_ln_b, m_w_pa, m_gdn_conv_q, m_gdn_conv_k, m_gdn_conv_v, m_gdn_a_log, m_gdn_dt_bias, m_gdn_norm_g, m_w_pb, m_sgu_ln_g, m_sgu_ln_b, m_sgu_w_s, m_sgu_b_s, m_w_pc, m_w_o, m_ln1_g, m_ln1_b, m_w_ff1, m_b_ff1, m_w_ff2, m_b_ff2, m_ln2_g, m_ln2_b, v_ln_in_g, v_ln_in_b, v_w_in, v_b_gate, v_conv_dw_w, v_conv_dw_b, v_conv_ln_g, v_conv_ln_b, v_w_pa, v_gdn_conv_q, v_gdn_conv_k, v_gdn_conv_v, v_gdn_a_log, v_gdn_dt_bias, v_gdn_norm_g, v_w_pb, v_sgu_ln_g, v_sgu_ln_b, v_sgu_w_s, v_sgu_b_s, v_w_pc, v_w_o, v_ln1_g, v_ln1_b, v_w_ff1, v_b_ff1, v_w_ff2, v_b_ff2, v_ln2_g, v_ln2_b):
    given = dict(x=x, ln_in_g=ln_in_g, ln_in_b=ln_in_b, w_in=w_in, b_gate=b_gate, conv_dw_w=conv_dw_w, conv_dw_b=conv_dw_b, conv_ln_g=conv_ln_g, conv_ln_b=conv_ln_b, w_pa=w_pa, gdn_conv_q=gdn_conv_q, gdn_conv_k=gdn_conv_k, gdn_conv_v=gdn_conv_v, gdn_a_log=gdn_a_log, gdn_dt_bias=gdn_dt_bias, gdn_norm_g=gdn_norm_g, w_pb=w_pb, sgu_ln_g=sgu_ln_g, sgu_ln_b=sgu_ln_b, sgu_w_s=sgu_w_s, sgu_b_s=sgu_b_s, w_pc=w_pc, w_o=w_o, ln1_g=ln1_g, ln1_b=ln1_b, w_ff1=w_ff1, b_ff1=b_ff1, w_ff2=w_ff2, b_ff2=b_ff2, ln2_g=ln2_g, ln2_b=ln2_b, loss_target=loss_target, m_ln_in_g=m_ln_in_g, m_ln_in_b=m_ln_in_b, m_w_in=m_w_in, m_b_gate=m_b_gate, m_conv_dw_w=m_conv_dw_w, m_conv_dw_b=m_conv_dw_b, m_conv_ln_g=m_conv_ln_g, m_conv_ln_b=m_conv_ln_b, m_w_pa=m_w_pa, m_gdn_conv_q=m_gdn_conv_q, m_gdn_conv_k=m_gdn_conv_k, m_gdn_conv_v=m_gdn_conv_v, m_gdn_a_log=m_gdn_a_log, m_gdn_dt_bias=m_gdn_dt_bias, m_gdn_norm_g=m_gdn_norm_g, m_w_pb=m_w_pb, m_sgu_ln_g=m_sgu_ln_g, m_sgu_ln_b=m_sgu_ln_b, m_sgu_w_s=m_sgu_w_s, m_sgu_b_s=m_sgu_b_s, m_w_pc=m_w_pc, m_w_o=m_w_o, m_ln1_g=m_ln1_g, m_ln1_b=m_ln1_b, m_w_ff1=m_w_ff1, m_b_ff1=m_b_ff1, m_w_ff2=m_w_ff2, m_b_ff2=m_b_ff2, m_ln2_g=m_ln2_g, m_ln2_b=m_ln2_b, v_ln_in_g=v_ln_in_g, v_ln_in_b=v_ln_in_b, v_w_in=v_w_in, v_b_gate=v_b_gate, v_conv_dw_w=v_conv_dw_w, v_conv_dw_b=v_conv_dw_b, v_conv_ln_g=v_conv_ln_g, v_conv_ln_b=v_conv_ln_b, v_w_pa=v_w_pa, v_gdn_conv_q=v_gdn_conv_q, v_gdn_conv_k=v_gdn_conv_k, v_gdn_conv_v=v_gdn_conv_v, v_gdn_a_log=v_gdn_a_log, v_gdn_dt_bias=v_gdn_dt_bias, v_gdn_norm_g=v_gdn_norm_g, v_w_pb=v_w_pb, v_sgu_ln_g=v_sgu_ln_g, v_sgu_ln_b=v_sgu_ln_b, v_sgu_w_s=v_sgu_w_s, v_sgu_b_s=v_sgu_b_s, v_w_pc=v_w_pc, v_w_o=v_w_o, v_ln1_g=v_ln1_g, v_ln1_b=v_ln1_b, v_w_ff1=v_w_ff1, v_b_ff1=v_b_ff1, v_w_ff2=v_w_ff2, v_b_ff2=v_b_ff2, v_ln2_g=v_ln2_g, v_ln2_b=v_ln2_b)
    weights = {n: given[n] for n in TWIN_WEIGHTS}
    shared = {n: given[n] for n in SHARED_INPUTS}
    per_example = {n: given[n] for n in ['x']}
    grad_fn = _jax.value_and_grad(_loss, argnums=(0, 1))

    def one_microbatch(ex, loss_target):
        ex = dict(ex)
        diff = ex.pop(TWIN_DIFF_INPUT)
        return grad_fn(weights, diff, {**shared, **ex}, loss_target)

    if N_MICROBATCH == 1:
        loss, (grad_w, grad_x) = one_microbatch(per_example, given["loss_target"])
    else:
        def body(carry, xs):
            loss_sum, grad_sum = carry
            l_k, (gw_k, gx_k) = one_microbatch(xs[0], xs[1])
            with _jax.named_scope("update"):
                return (loss_sum + l_k, _jax.tree.map(_jnp.add, grad_sum, gw_k)), gx_k

        init = (_jnp.zeros((), _jnp.float32), _jax.tree.map(_jnp.zeros_like, weights))
        (loss, grad_w), grad_x = _jax.lax.scan(body, init, (per_example, given["loss_target"]))
    with _jax.named_scope("update"):
        delta_w, new_m, new_v = {}, {}, {}
        for n in TWIN_WEIGHTS:
            delta_w[n], new_m[n], new_v[n] = _adamw(weights[n], grad_w[n], given["m_" + n], given["v_" + n])
    return (loss, grad_x, *[grad_w[n] for n in TWIN_WEIGHTS], *[delta_w[n] for n in TWIN_WEIGHTS],
            *[new_m[n] for n in TWIN_WEIGHTS], *[new_v[n] for n in TWIN_WEIGHTS])
```

```python
import functools
import math

import jax
import jax.numpy as jnp
from jax import lax
from jax.experimental import pallas as pl
from jax.experimental.pallas import tpu as pltpu

F32 = jnp.float32
BF16 = jnp.bfloat16

N_DEV = 8
DEPTH = 2
D_MODEL = 1024
CONV_DIM = 512
CONV_WIDTH = 31
GDN_HEADS = 4
GDN_DK = 128
GDN_QK = 512
GDN_CONV = 4
GDN_CHUNK = 64
SGU_GROUPS = 4
SGU_GROUP_DIM = 128
SGU_DIM = 512
SGU_CHUNK = 128
D_FF = 4096
DN_ALPHA = (2 * DEPTH) ** 0.25
LN_EPS = 1e-5
RMS_EPS = 1e-6
PROJ_COLS = 7176
SHARD_COLS = PROJ_COLS // N_DEV

COL_A = 0
COL_Q = 1024
COL_Z = 2560
COL_UV = 3072
COL_GATE = 4096
COL_BA = 7168
P_COLS = 7296
P_TILE = 2432

ADAM_LR = 0.001
ADAM_B1 = 0.9
ADAM_B2 = 0.999
ADAM_EPS = 1e-08
ADAM_WD = 0.01
ADAM_STEP = 10

VMEM_LIMIT_BYTES = 56 * 1024 * 1024
LANES = 128
ROW_TILE = 512
GDN_BLOCK = 256
CONV_HALO = 32
GDN_HALO = 8


def _params(sem):
    return pltpu.CompilerParams(dimension_semantics=sem, vmem_limit_bytes=VMEM_LIMIT_BYTES)


def _dot(a, b):
    return jnp.dot(a, b, preferred_element_type=F32)


def _dot_nt(a, b):
    return lax.dot_general(a, b, (((1,), (1,)), ((), ())), preferred_element_type=F32)


def _dot_tn(a, b):
    return lax.dot_general(a, b, (((0,), (0,)), ((), ())), preferred_element_type=F32)


def _split(a):
    hi = a.astype(BF16)
    lo = (a - hi.astype(F32)).astype(BF16)
    return hi, lo


def _dot3(a, b, dot=_dot):
    ah, al = _split(a)
    bh, bl = _split(b)
    return dot(ah, bh) + (dot(ah, bl) + dot(al, bh))


def _bdot(a, b, dot=_dot):
    return dot(a.astype(BF16), b.astype(BF16))


def _sigmoid(x):
    return jax.nn.sigmoid(x)


def _silu(x):
    return x * _sigmoid(x)


def _dsilu(x):
    s = _sigmoid(x)
    return s * (1.0 + x * (1.0 - s))


_GELU_C = math.sqrt(2.0 / math.pi)


def _gelu(x):
    return 0.5 * x * (1.0 + jnp.tanh(_GELU_C * (x + 0.044715 * (x * x * x))))


def _dgelu(x):
    t = jnp.tanh(_GELU_C * (x + 0.044715 * (x * x * x)))
    return 0.5 * (1.0 + t) + 0.5 * x * (1.0 - t * t) * (_GELU_C * (1.0 + 3.0 * 0.044715 * (x * x)))


def _ln_stats(x):
    mu = jnp.mean(x, axis=-1, keepdims=True)
    xc = x - mu
    var = jnp.mean(xc * xc, axis=-1, keepdims=True)
    rstd = lax.rsqrt(var + LN_EPS)
    return xc * rstd, rstd


def _ln_bwd(dy, xhat, rstd, g):
    dxh = dy * g
    return rstd * (dxh - jnp.mean(dxh, axis=-1, keepdims=True) - xhat * jnp.mean(dxh * xhat, axis=-1, keepdims=True))


def _colsum(x):
    return jnp.sum(x, axis=0, keepdims=True)


def _row_spec(t, cols, col_block=0):
    return pl.BlockSpec((t, cols), lambda i, cb=col_block: (i, cb))


def _const_spec(shape):
    nd = len(shape)
    return pl.BlockSpec(shape, lambda *_: (0,) * nd)


def _matmul(a, w, *, name, tm, tn, tk, out_dtype=F32, a_col_block=0, add=None, add_scale=1.0):
    m = a.shape[0]
    k, n = w.shape
    nk = k // tk
    has_add = add is not None

    def body(*refs):
        if has_add:
            a_ref, w_ref, add_ref, o_ref, acc_ref = refs
        else:
            a_ref, w_ref, o_ref, acc_ref = refs
        kk = pl.program_id(2)

        @pl.when(kk == 0)
        def _():
            acc_ref[...] = jnp.zeros_like(acc_ref)

        acc_ref[...] += _dot(a_ref[...], w_ref[...])

        @pl.when(kk == nk - 1)
        def _():
            r = acc_ref[...]
            if has_add:
                r = r + add_scale * add_ref[...]
            o_ref[...] = r.astype(out_dtype)

    in_specs = [pl.BlockSpec((tm, tk), lambda i, j, kk: (i, kk + a_col_block)),
                pl.BlockSpec((tk, tn), lambda i, j, kk: (kk, j))]
    args = [a, w]
    if has_add:
        in_specs.append(pl.BlockSpec((tm, tn), lambda i, j, kk: (i, j)))
        args.append(add)
    return pl.pallas_call(
        body, name=name, grid=(m // tm, n // tn, nk), in_specs=in_specs,
        out_specs=pl.BlockSpec((tm, tn), lambda i, j, kk: (i, j)),
        out_shape=jax.ShapeDtypeStruct((m, n), out_dtype),
        scratch_shapes=[pltpu.VMEM((tm, tn), F32)],
        compiler_params=_params(("parallel", "parallel", "arbitrary")),
    )(*args)


def _matmul_tn(a, b, *, name, ka, tka, tn, ts, n=None, a_col_block=0, b_col_block=0):
    s = a.shape[0]
    n = b.shape[1] if n is None else n
    ns = s // ts

    def body(a_ref, b_ref, o_ref):
        @pl.when(pl.program_id(2) == 0)
        def _():
            o_ref[...] = jnp.zeros_like(o_ref)

        o_ref[...] += _dot_tn(a_ref[...], b_ref[...])

    return pl.pallas_call(
        body, name=name, grid=(ka // tka, n // tn, ns),
        in_specs=[pl.BlockSpec((ts, tka), lambda i, j, t: (t, i + a_col_block)),
                  pl.BlockSpec((ts, tn), lambda i, j, t: (t, j + b_col_block))],
        out_specs=pl.BlockSpec((tka, tn), lambda i, j, t: (i, j)),
        out_shape=jax.ShapeDtypeStruct((ka, n), F32),
        compiler_params=_params(("parallel", "parallel", "arbitrary")),
    )(a, b)


def _ln_in_fwd(x, g, b):
    s = x.shape[0]
    t = min(ROW_TILE, s)

    def body(x_ref, g_ref, b_ref, y_ref, ybf_ref):
        xhat, _ = _ln_stats(x_ref[...])
        y = xhat * g_ref[...] + b_ref[...]
        y_ref[...] = y
        ybf_ref[...] = y.astype(BF16)

    return pl.pallas_call(
        body, name="ln_in_fwd", grid=(s // t,),
        in_specs=[_row_spec(t, D_MODEL), _const_spec((1, D_MODEL)), _const_spec((1, D_MODEL))],
        out_specs=[_row_spec(t, D_MODEL), _row_spec(t, D_MODEL)],
        out_shape=[jax.ShapeDtypeStruct((s, D_MODEL), F32), jax.ShapeDtypeStruct((s, D_MODEL), BF16)],
        compiler_params=_params(("parallel",)),
    )(x, g, b)


def _prev_halo_spec(t, halo, cols, col_block):
    per = t // halo
    return pl.BlockSpec((halo, cols), lambda i, cb=col_block: (jnp.maximum(i * per - 1, 0), cb))


def _next_halo_spec(t, halo, cols, col_block, n_blocks):
    per = t // halo
    last = n_blocks * per - 1
    return pl.BlockSpec((halo, cols), lambda i, cb=col_block: (jnp.minimum((i + 1) * per, last), cb))


def _conv_a_fwd(p, w, b, ln_g, ln_b, tag):
    s = p.shape[0]
    t = min(ROW_TILE, s)
    width = CONV_WIDTH

    def body(a_ref, halo_ref, w_ref, b_ref, g_ref, bb_ref, h_ref, c_ref, ha_ref, ext):
        i = pl.program_id(0)
        a = a_ref[...]
        h = a[:, :CONV_DIM] * _sigmoid(a[:, CONV_DIM:])
        ah = halo_ref[...]
        hh = ah[:, :CONV_DIM] * _sigmoid(ah[:, CONV_DIM:])
        ext[0:CONV_HALO, :] = jnp.where(i == 0, 0.0, hh)
        ext[CONV_HALO:CONV_HALO + t, :] = h
        acc = jnp.zeros((t, CONV_DIM), F32)
        for j in range(width):
            acc = acc + w_ref[j:j + 1, :] * ext[pl.ds(CONV_HALO - (width - 1) + j, t), :]
        c = acc + b_ref[...]
        xhat, _ = _ln_stats(c)
        n = xhat * g_ref[...] + bb_ref[...]
        h_ref[...] = h
        c_ref[...] = c
        ha_ref[...] = _silu(n).astype(BF16)

    return pl.pallas_call(
        body, name="conv_a_fwd" + tag, grid=(s // t,),
        in_specs=[_row_spec(t, 2 * CONV_DIM, COL_A // (2 * CONV_DIM)),
                  _prev_halo_spec(t, CONV_HALO, 2 * CONV_DIM, COL_A // (2 * CONV_DIM)),
                  _const_spec((width, CONV_DIM)), _const_spec((1, CONV_DIM)),
                  _const_spec((1, CONV_DIM)), _const_spec((1, CONV_DIM))],
        out_specs=[_row_spec(t, CONV_DIM)] * 3,
        out_shape=[jax.ShapeDtypeStruct((s, CONV_DIM), F32), jax.ShapeDtypeStruct((s, CONV_DIM), F32),
                   jax.ShapeDtypeStruct((s, CONV_DIM), BF16)],
        scratch_shapes=[pltpu.VMEM((CONV_HALO + t, CONV_DIM), F32)],
        compiler_params=_params(("parallel",)),
    )(p, p, w, b, ln_g, ln_b)


def _sgu_mix(vn, wt_ref, bst_ref, t):
    row = lax.broadcasted_iota(jnp.int32, (SGU_CHUNK, SGU_CHUNK), 0)
    col = lax.broadcasted_iota(jnp.int32, (SGU_CHUNK, SGU_CHUNK), 1)
    chunks = []
    for ci in range(t // SGU_CHUNK):
        groups = []
        for g in range(SGU_GROUPS):
            wg = jnp.where(row >= col, wt_ref[g], 0.0).astype(BF16)
            v_cg = vn[ci * SGU_CHUNK:(ci + 1) * SGU_CHUNK, g * SGU_GROUP_DIM:(g + 1) * SGU_GROUP_DIM]
            groups.append(_dot(wg, v_cg.astype(BF16)) + bst_ref[:, g:g + 1])
        chunks.append(jnp.concatenate(groups, axis=1))
    return jnp.concatenate(chunks, axis=0)


def _sgu_fwd(p, ln_g, ln_b, w_s, b_s_t, tag):
    s = p.shape[0]
    t = min(ROW_TILE, s)

    def body(uv_ref, g_ref, b_ref, ws_ref, bst_ref, hc_ref):
        uv = uv_ref[...]
        u = _gelu(uv[:, :SGU_DIM])
        vv = _gelu(uv[:, SGU_DIM:])
        xhat, _ = _ln_stats(vv)
        vn = xhat * g_ref[...] + b_ref[...]
        mixed = _sgu_mix(vn, ws_ref, bst_ref, t)
        hc_ref[...] = (u * mixed).astype(BF16)

    return pl.pallas_call(
        body, name="sgu_fwd" + tag, grid=(s // t,),
        in_specs=[_row_spec(t, 2 * SGU_DIM, COL_UV // (2 * SGU_DIM)),
                  _const_spec((1, SGU_DIM)), _const_spec((1, SGU_DIM)),
                  _const_spec((SGU_GROUPS, SGU_CHUNK, SGU_CHUNK)), _const_spec((SGU_CHUNK, LANES))],
        out_specs=_row_spec(t, SGU_DIM),
        out_shape=jax.ShapeDtypeStruct((s, SGU_DIM), BF16),
        compiler_params=_params(("parallel",)),
    )(p, ln_g, ln_b, w_s, b_s_t)


def _merge_fwd(p, ha, hb, hc, w_pa, w_pb, w_pc, b_gate, tag):
    s = p.shape[0]
    t = min(ROW_TILE, s)
    gb = COL_GATE // D_MODEL

    def body(ga_ref, gb_ref, gc_ref, ha_ref, hb_ref, hc_ref, wa_ref, wb_ref, wc_ref, bg_ref, y_ref, m_ref):
        merged = jnp.zeros((t, D_MODEL), F32)
        for idx, (g_ref, h_ref, w_ref) in enumerate(((ga_ref, ha_ref, wa_ref), (gb_ref, hb_ref, wb_ref), (gc_ref, hc_ref, wc_ref))):
            y = _dot(h_ref[...], w_ref[...])
            sg = _sigmoid(g_ref[...] + bg_ref[:, idx * D_MODEL:(idx + 1) * D_MODEL])
            y_ref[:, idx * D_MODEL:(idx + 1) * D_MODEL] = y
            merged = merged + sg * y
        m_ref[...] = merged.astype(BF16)

    hspec = _row_spec(t, CONV_DIM)
    wspec = _const_spec((CONV_DIM, D_MODEL))
    return pl.pallas_call(
        body, name="merge_fwd" + tag, grid=(s // t,),
        in_specs=[_row_spec(t, D_MODEL, gb), _row_spec(t, D_MODEL, gb + 1), _row_spec(t, D_MODEL, gb + 2),
                  hspec, hspec, hspec, wspec, wspec, wspec, _const_spec((1, 3 * D_MODEL))],
        out_specs=[_row_spec(t, 3 * D_MODEL), _row_spec(t, D_MODEL)],
        out_shape=[jax.ShapeDtypeStruct((s, 3 * D_MODEL), F32), jax.ShapeDtypeStruct((s, D_MODEL), BF16)],
        compiler_params=_params(("parallel",)),
    )(p, p, p, ha, hb, hc, w_pa, w_pb, w_pc, b_gate)


def _matmul_res_ln(a, w, bias, x_res, ln_g, ln_b, *, name):
    s, k = a.shape
    t = min(ROW_TILE, s)

    def body(a_ref, w_ref, bias_ref, x_ref, g_ref, b_ref, r_ref, y_ref, ybf_ref):
        r = DN_ALPHA * x_ref[...] + _dot(a_ref[...], w_ref[...]) + bias_ref[...]
        xhat, _ = _ln_stats(r)
        y = xhat * g_ref[...] + b_ref[...]
        r_ref[...] = r
        y_ref[...] = y
        ybf_ref[...] = y.astype(BF16)

    vec = _const_spec((1, D_MODEL))
    return pl.pallas_call(
        body, name=name, grid=(s // t,),
        in_specs=[_row_spec(t, k), _const_spec((k, D_MODEL)), vec, _row_spec(t, D_MODEL), vec, vec],
        out_specs=[_row_spec(t, D_MODEL)] * 3,
        out_shape=[jax.ShapeDtypeStruct((s, D_MODEL), F32), jax.ShapeDtypeStruct((s, D_MODEL), F32),
                   jax.ShapeDtypeStruct((s, D_MODEL), BF16)],
        compiler_params=_params(("parallel",)),
    )(a, w, bias, x_res, ln_g, ln_b)


def _ff1_fwd(x_bf, w, b, tag):
    s = x_bf.shape[0]
    tm = min(1024, s)
    tn = 1024

    def body(x_ref, w_ref, b_ref, hp_ref, h_ref):
        hp = _dot(x_ref[...], w_ref[...]) + b_ref[...]
        hp_ref[...] = hp
        r = jnp.maximum(hp, 0.0)
        h_ref[...] = (r * r).astype(BF16)

    return pl.pallas_call(
        body, name="ff1_fwd" + tag, grid=(s // tm, D_FF // tn),
        in_specs=[pl.BlockSpec((tm, D_MODEL), lambda i, j: (i, 0)), pl.BlockSpec((D_MODEL, tn), lambda i, j: (0, j)),
                  pl.BlockSpec((1, tn), lambda i, j: (0, j))],
        out_specs=[pl.BlockSpec((tm, tn), lambda i, j: (i, j))] * 2,
        out_shape=[jax.ShapeDtypeStruct((s, D_FF), F32), jax.ShapeDtypeStruct((s, D_FF), BF16)],
        compiler_params=_params(("parallel", "parallel")),
    )(x_bf, w, b)


def _loss_fwd_bwd(y, target):
    s = y.shape[0]
    t = min(ROW_TILE, s)

    def body(y_ref, t_ref, dy_ref, loss_ref):
        @pl.when(pl.program_id(0) == 0)
        def _():
            loss_ref[...] = jnp.zeros_like(loss_ref)

        err = y_ref[...] - t_ref[...]
        dy_ref[...] = err * (1.0 / D_MODEL)
        per_row = jnp.mean(err * err, axis=-1, keepdims=True)
        loss_ref[...] += 0.5 * jnp.sum(per_row, axis=0, keepdims=True)

    return pl.pallas_call(
        body, name="loss_fwd_bwd", grid=(s // t,),
        in_specs=[_row_spec(t, D_MODEL), _row_spec(t, D_MODEL)],
        out_specs=[_row_spec(t, D_MODEL), _const_spec((8, LANES))],
        out_shape=[jax.ShapeDtypeStruct((s, D_MODEL), F32), jax.ShapeDtypeStruct((8, LANES), F32)],
        compiler_params=_params(("arbitrary",)),
    )(y, target)


def _softplus(x):
    return jnp.maximum(x, 0.0) + jnp.log1p(jnp.exp(-jnp.abs(x)))


def _gdn_conv_silu_norm(q_ref, k_ref, v_ref, hq_ref, hk_ref, hv_ref, cw_ref, ext, first):
    t = q_ref.shape[0]
    for n, (r, h) in enumerate(((q_ref, hq_ref), (k_ref, hk_ref), (v_ref, hv_ref))):
        ext[0:GDN_HALO, n * GDN_QK:(n + 1) * GDN_QK] = jnp.where(first, 0.0, h[...])
        ext[GDN_HALO:GDN_HALO + t, n * GDN_QK:(n + 1) * GDN_QK] = r[...]
    pre = jnp.zeros((t, 3 * GDN_QK), F32)
    for j in range(GDN_CONV):
        pre = pre + cw_ref[j:j + 1, :] * ext[pl.ds(GDN_HALO - (GDN_CONV - 1) + j, t), :]
    act = _silu(pre)
    rq, rk = [], []
    for h in range(GDN_HEADS):
        qh = act[:, h * GDN_DK:(h + 1) * GDN_DK]
        kh = act[:, GDN_QK + h * GDN_DK:GDN_QK + (h + 1) * GDN_DK]
        rq.append(lax.rsqrt(jnp.sum(qh * qh, axis=-1, keepdims=True) + RMS_EPS))
        rk.append(lax.rsqrt(jnp.sum(kh * kh, axis=-1, keepdims=True) + RMS_EPS))
    return pre, act, rq, rk


def _gdn_gates(ba, alog_ref, dtb_ref):
    lane = lax.broadcasted_iota(jnp.int32, ba.shape, 1)
    beta = _sigmoid(ba)
    g = -jnp.exp(alog_ref[...]) * _softplus(ba + dtb_ref[...])
    g = jnp.where((lane >= GDN_HEADS) & (lane < 2 * GDN_HEADS), g, 0.0)
    return beta, g


def _chunk_cumsum_matrix(t, upper):
    row = lax.broadcasted_iota(jnp.int32, (t, t), 0)
    col = lax.broadcasted_iota(jnp.int32, (t, t), 1)
    same = (row // GDN_CHUNK) == (col // GDN_CHUNK)
    tri = (col >= row) if upper else (col <= row)
    return jnp.where(same & tri, 1.0, 0.0).astype(F32)


def _gdn_prep(q, k, v, beta_c, gam_c, gam_r):
    c = GDN_CHUNK
    row = lax.broadcasted_iota(jnp.int32, (c, c), 0)
    col = lax.broadcasted_iota(jnp.int32, (c, c), 1)
    causal = row >= col
    strict = row > col
    decay = jnp.where(causal, jnp.exp(jnp.where(causal, gam_c - gam_r, 0.0)), 0.0)
    kb = k.astype(BF16)
    kk = _dot_nt(kb, kb)
    qk = _dot_nt(q.astype(BF16), kb)
    low = jnp.where(strict, beta_c * kk * decay, 0.0)
    a_qk = qk * decay
    gm = jnp.exp(gam_c)
    glast = gam_c[c - 1:c, :]
    elast = jnp.exp(glast - gam_c)
    return dict(causal=causal, strict=strict, decay=decay, kk=kk, low=low, a_qk=a_qk, gm=gm,
                glast_exp=jnp.exp(glast), elast=elast, ru=beta_c * v, rw=beta_c * k * gm, qd=q * gm, kd=k * elast)


def _unit_lower_inverse(low):
    c = GDN_CHUNK
    row = lax.broadcasted_iota(jnp.int32, (c, c), 0)
    col = lax.broadcasted_iota(jnp.int32, (c, c), 1)
    p = -low
    m = jnp.where(row == col, 1.0, 0.0) + p
    for _ in range(5):
        p = _dot3(p, p)
        m = m + _dot3(m, p)
    return m


def _gdn_fwd(p, conv_w, a_log, dt_bias, norm_g, tag):
    s = p.shape[0]
    t = min(GDN_BLOCK, s)
    nc = t // GDN_CHUNK
    nblk = s // t
    qb, kb_, vb, zb = COL_Q // GDN_QK, COL_Q // GDN_QK + 1, COL_Q // GDN_QK + 2, COL_Z // GDN_QK
    scale = GDN_DK ** -0.5

    def body(q_ref, k_ref, v_ref, hq_ref, hk_ref, hv_ref, z_ref, ba_ref, cw_ref, alog_ref, dtb_ref, ng_ref,
             o_ref, hb_ref, st_ref, m_ref, ext, qn_s, kn_s, vc_s, beta_s, gam_s, state):
        i = pl.program_id(0)

        @pl.when(i == 0)
        def _():
            state[...] = jnp.zeros_like(state)

        _, act, rq, rk = _gdn_conv_silu_norm(q_ref, k_ref, v_ref, hq_ref, hk_ref, hv_ref, cw_ref, ext, i == 0)
        for h in range(GDN_HEADS):
            sl = slice(h * GDN_DK, (h + 1) * GDN_DK)
            qn_s[:, sl] = act[:, sl] * (rq[h] * scale)
            kn_s[:, sl] = act[:, GDN_QK + h * GDN_DK:GDN_QK + (h + 1) * GDN_DK] * rk[h]
        vc_s[...] = act[:, 2 * GDN_QK:]
        beta, g = _gdn_gates(ba_ref[...], alog_ref, dtb_ref)
        beta_s[...] = beta
        gam_s[...] = jnp.dot(_chunk_cumsum_matrix(t, False), g, preferred_element_type=F32, precision=lax.Precision.HIGHEST)

        def chunk(ci, carry):
            r0 = pl.multiple_of(ci * GDN_CHUNK, GDN_CHUNK)
            rows = pl.ds(r0, GDN_CHUNK)
            gam_blk = gam_s[rows, :]
            gam_t = gam_blk.T
            beta_blk = beta_s[rows, :]
            minv = []
            for h in range(GDN_HEADS):
                sl = slice(h * GDN_DK, (h + 1) * GDN_DK)
                pr = _gdn_prep(qn_s[rows, sl], kn_s[rows, sl], vc_s[rows, sl], beta_blk[:, h:h + 1],
                               gam_blk[:, GDN_HEADS + h:GDN_HEADS + h + 1], gam_t[GDN_HEADS + h:GDN_HEADS + h + 1, :])
                m = _unit_lower_inverse(pr["low"])
                minv.append(m)
                x = _dot3(m, jnp.concatenate([pr["ru"], pr["rw"]], axis=1))
                u, w = x[:, :GDN_DK], x[:, GDN_DK:]
                st = state[h * GDN_DK:(h + 1) * GDN_DK, :]
                st_ref[pl.ds(pl.multiple_of((ci * GDN_HEADS + h) * GDN_DK, GDN_DK), GDN_DK), :] = st
                vn = u - _bdot(w, st)
                o_ref[rows, sl] = _bdot(pr["qd"], st) + _bdot(pr["a_qk"], vn)
                state[h * GDN_DK:(h + 1) * GDN_DK, :] = st * pr["glast_exp"] + _bdot(pr["kd"], vn, _dot_tn)
            m_ref[rows, :] = jnp.concatenate(minv, axis=1)
            return carry

        lax.fori_loop(0, nc, chunk, 0)

        z = z_ref[...]
        for h in range(GDN_HEADS):
            sl = slice(h * GDN_DK, (h + 1) * GDN_DK)
            o = o_ref[:, sl]
            on = o * lax.rsqrt(jnp.mean(o * o, axis=-1, keepdims=True) + RMS_EPS)
            hb_ref[:, sl] = (on * ng_ref[...] * _silu(z[:, sl])).astype(BF16)

    col = lambda cb: pl.BlockSpec((t, GDN_QK), lambda i, cb=cb: (i, cb))
    halo = lambda cb: _prev_halo_spec(t, GDN_HALO, GDN_QK, cb)
    vec = _const_spec((1, LANES))
    return pl.pallas_call(
        body, name="gdn_fwd" + tag, grid=(nblk,),
        in_specs=[col(qb), col(kb_), col(vb), halo(qb), halo(kb_), halo(vb), col(zb),
                  _row_spec(t, LANES, COL_BA // LANES), _const_spec((GDN_CONV, 3 * GDN_QK)), vec, vec, vec],
        out_specs=[_row_spec(t, GDN_QK), _row_spec(t, GDN_QK),
                   pl.BlockSpec((nc * GDN_HEADS * GDN_DK, GDN_DK), lambda i: (i, 0)),
                   _row_spec(t, GDN_HEADS * GDN_CHUNK)],
        out_shape=[jax.ShapeDtypeStruct((s, GDN_QK), F32), jax.ShapeDtypeStruct((s, GDN_QK), BF16),
                   jax.ShapeDtypeStruct((s // GDN_CHUNK * GDN_HEADS * GDN_DK, GDN_DK), F32),
                   jax.ShapeDtypeStruct((s, GDN_HEADS * GDN_CHUNK), F32)],
        scratch_shapes=[pltpu.VMEM((GDN_HALO + t, 3 * GDN_QK), F32), pltpu.VMEM((t, GDN_QK), F32),
                        pltpu.VMEM((t, GDN_QK), F32), pltpu.VMEM((t, GDN_QK), F32),
                        pltpu.VMEM((t, LANES), F32), pltpu.VMEM((t, LANES), F32),
                        pltpu.VMEM((GDN_HEADS * GDN_DK, GDN_DK), F32)],
        compiler_params=_params(("arbitrary",)),
    )(p, p, p, p, p, p, p, p, conv_w, a_log, dt_bias, norm_g)


def _lane_place(col, lane_idx, shape):
    lane = lax.broadcasted_iota(jnp.int32, shape, 1)
    return jnp.where(lane == lane_idx, col, 0.0)


def _gdn_bwd(dhb, p, o, states, minv, conv_w, a_log, dt_bias, norm_g, tag):
    s = p.shape[0]
    t = min(GDN_BLOCK, s)
    nc = t // GDN_CHUNK
    nblk = s // t
    qb, kb_, vb, zb = COL_Q // GDN_QK, COL_Q // GDN_QK + 1, COL_Q // GDN_QK + 2, COL_Z // GDN_QK
    scale = GDN_DK ** -0.5
    c = GDN_CHUNK

    def body(dhb_ref, q_ref, k_ref, v_ref, hq_ref, hk_ref, hv_ref, z_ref, ba_ref, o_ref, st_ref, m_ref,
             cw_ref, alog_ref, dtb_ref, ng_ref,
             dpre_ref, dz_ref, dba_ref, dng_ref, dalog_ref, ddtb_ref,
             ext, qn_s, kn_s, vc_s, beta_s, gam_s, do_s, dqn_s, dkn_s, dvc_s, dgam_s, dbeta_s, dstate):
        i = pl.program_id(0)

        @pl.when(i == 0)
        def _():
            dstate[...] = jnp.zeros_like(dstate)
            dng_ref[...] = jnp.zeros_like(dng_ref)
            dalog_ref[...] = jnp.zeros_like(dalog_ref)
            ddtb_ref[...] = jnp.zeros_like(ddtb_ref)

        pre, act, rq, rk = _gdn_conv_silu_norm(q_ref, k_ref, v_ref, hq_ref, hk_ref, hv_ref, cw_ref, ext, i == nblk - 1)
        for h in range(GDN_HEADS):
            sl = slice(h * GDN_DK, (h + 1) * GDN_DK)
            qn_s[:, sl] = act[:, sl] * (rq[h] * scale)
            kn_s[:, sl] = act[:, GDN_QK + h * GDN_DK:GDN_QK + (h + 1) * GDN_DK] * rk[h]
        vc_s[...] = act[:, 2 * GDN_QK:]
        ba = ba_ref[...]
        beta, g = _gdn_gates(ba, alog_ref, dtb_ref)
        beta_s[...] = beta
        gam_s[...] = jnp.dot(_chunk_cumsum_matrix(t, False), g, preferred_element_type=F32, precision=lax.Precision.HIGHEST)

        z = z_ref[...]
        dhb = dhb_ref[...]
        dng = jnp.zeros((1, GDN_DK), F32)
        for h in range(GDN_HEADS):
            sl = slice(h * GDN_DK, (h + 1) * GDN_DK)
            oh = o_ref[:, sl]
            r = lax.rsqrt(jnp.mean(oh * oh, axis=-1, keepdims=True) + RMS_EPS)
            on = oh * r
            sz = _silu(z[:, sl])
            dyh = dhb[:, sl]
            dng = dng + _colsum(dyh * on * sz)
            dz_ref[:, sl] = (dyh * on * ng_ref[...] * _dsilu(z[:, sl])).astype(BF16)
            don = dyh * ng_ref[...] * sz
            do_s[:, sl] = r * (don - on * jnp.mean(don * on, axis=-1, keepdims=True))
        dng_ref[...] += dng

        def chunk(cj, carry):
            ci = nc - 1 - cj
            r0 = pl.multiple_of(ci * c, c)
            rows = pl.ds(r0, c)
            gam_blk = gam_s[rows, :]
            gam_t = gam_blk.T
            beta_blk = beta_s[rows, :]
            m_blk = m_ref[rows, :]
            dgam_blk = jnp.zeros((c, LANES), F32)
            dbeta_blk = jnp.zeros((c, LANES), F32)
            rowi = lax.broadcasted_iota(jnp.int32, (c, 1), 0)
            for h in range(GDN_HEADS):
                sl = slice(h * GDN_DK, (h + 1) * GDN_DK)
                q, k, v = qn_s[rows, sl], kn_s[rows, sl], vc_s[rows, sl]
                beta_c = beta_blk[:, h:h + 1]
                pr = _gdn_prep(q, k, v, beta_c, gam_blk[:, GDN_HEADS + h:GDN_HEADS + h + 1],
                               gam_t[GDN_HEADS + h:GDN_HEADS + h + 1, :])
                m = m_blk[:, h * c:(h + 1) * c]
                x = _dot3(m, jnp.concatenate([pr["ru"], pr["rw"]], axis=1))
                u, w = x[:, :GDN_DK], x[:, GDN_DK:]
                st = st_ref[pl.ds(pl.multiple_of((ci * GDN_HEADS + h) * GDN_DK, GDN_DK), GDN_DK), :]
                ds = dstate[h * GDN_DK:(h + 1) * GDN_DK, :]
                do = do_s[rows, sl]
                vn = u - _bdot(w, st)
                dkd = _bdot(vn, ds, _dot_nt)
                dvn = _bdot(pr["kd"], ds) + _bdot(pr["a_qk"], do, _dot_tn)
                dqd = _bdot(do, st, _dot_nt)
                d_a = jnp.where(pr["causal"], _bdot(do, vn, _dot_nt), 0.0)
                dw = -_bdot(dvn, st, _dot_nt)
                dstate[h * GDN_DK:(h + 1) * GDN_DK, :] = (ds * pr["glast_exp"] + _bdot(pr["qd"], do, _dot_tn)
                                                          - _bdot(w, dvn, _dot_tn))
                d_r = _dot3(m, jnp.concatenate([dvn, dw], axis=1), _dot_tn)
                d_l = jnp.where(pr["strict"], -_bdot(d_r, x, _dot_nt), 0.0)
                d_l_kd = d_l * pr["kk"] * pr["decay"]
                dkk = d_l * beta_c * pr["decay"]
                dqk = d_a * pr["decay"]
                gmat = beta_c * d_l_kd + d_a * pr["a_qk"]
                d_ru, d_rw = d_r[:, :GDN_DK], d_r[:, GDN_DK:]
                rsum = lambda a: jnp.sum(a, axis=-1, keepdims=True)
                dkd_kd = rsum(dkd * pr["kd"])
                dgam = (rsum(gmat) - rsum(gmat.T) + rsum(d_rw * pr["rw"]) + rsum(dqd * pr["qd"]) - dkd_kd)
                dglast = jnp.sum(dkd_kd, axis=0, keepdims=True) + jnp.sum(rsum(ds * st), axis=0, keepdims=True) * pr["glast_exp"]
                dgam = dgam + jnp.where(rowi == c - 1, dglast, 0.0)
                dbeta = rsum(d_l_kd) + rsum(d_ru * v) + rsum(d_rw * k) * pr["gm"]
                dvc_s[rows, sl] = beta_c * d_ru
                dkn_s[rows, sl] = (_bdot(dkk, k) + _bdot(dkk, k, _dot_tn) + _bdot(dqk, q, _dot_tn)
                                   + d_rw * (beta_c * pr["gm"]) + dkd * pr["elast"])
                dqn_s[rows, sl] = _bdot(dqk, k) + dqd * pr["gm"]
                dgam_blk = dgam_blk + _lane_place(dgam, GDN_HEADS + h, (c, LANES))
                dbeta_blk = dbeta_blk + _lane_place(dbeta, h, (c, LANES))
            dgam_s[rows, :] = dgam_blk
            dbeta_s[rows, :] = dbeta_blk
            return carry

        lax.fori_loop(0, nc, chunk, 0)

        dg = jnp.dot(_chunk_cumsum_matrix(t, True), dgam_s[...], preferred_element_type=F32, precision=lax.Precision.HIGHEST)
        lane = lax.broadcasted_iota(jnp.int32, (t, LANES), 1)
        g_lanes = (lane >= GDN_HEADS) & (lane < 2 * GDN_HEADS)
        da_logit = jnp.where(g_lanes, dg * (-jnp.exp(alog_ref[...])) * _sigmoid(ba + dtb_ref[...]), 0.0)
        db_logit = jnp.where(lane < GDN_HEADS, dbeta_s[...] * beta * (1.0 - beta), 0.0)
        dba_ref[...] = (da_logit + db_logit).astype(BF16)
        dalog_ref[...] += _colsum(dg * g)
        ddtb_ref[...] += _colsum(da_logit)

        dact = []
        for n, (dn_s, rr, sc) in enumerate(((dqn_s, rq, scale), (dkn_s, rk, 1.0))):
            for h in range(GDN_HEADS):
                sl = slice(h * GDN_DK, (h + 1) * GDN_DK)
                y = act[:, n * GDN_QK + h * GDN_DK:n * GDN_QK + (h + 1) * GDN_DK] * rr[h]
                dy = dn_s[:, sl] * sc
                dact.append(rr[h] * (dy - y * jnp.sum(dy * y, axis=-1, keepdims=True)))
        dact.append(dvc_s[...])
        dpre_ref[...] = jnp.concatenate(dact, axis=1) * _dsilu(pre)

    rb = lambda i: nblk - 1 - i
    per = t // GDN_HALO
    col = lambda cb, wd=GDN_QK: pl.BlockSpec((t, wd), lambda i, cb=cb: (rb(i), cb))
    halo = lambda cb: pl.BlockSpec((GDN_HALO, GDN_QK), lambda i, cb=cb: (jnp.maximum(rb(i) * per - 1, 0), cb))
    vec = _const_spec((1, LANES))
    return pl.pallas_call(
        body, name="gdn_bwd" + tag, grid=(nblk,),
        in_specs=[col(0), col(qb), col(kb_), col(vb), halo(qb), halo(kb_), halo(vb), col(zb),
                  pl.BlockSpec((t, LANES), lambda i: (rb(i), COL_BA // LANES)), col(0),
                  pl.BlockSpec((nc * GDN_HEADS * GDN_DK, GDN_DK), lambda i: (rb(i), 0)),
                  pl.BlockSpec((t, GDN_HEADS * c), lambda i: (rb(i), 0)),
                  _const_spec((GDN_CONV, 3 * GDN_QK)), vec, vec, vec],
        out_specs=[pl.BlockSpec((t, 3 * GDN_QK), lambda i: (rb(i), 0)), col(0), pl.BlockSpec((t, LANES), lambda i: (rb(i), 0)),
                   vec, vec, vec],
        out_shape=[jax.ShapeDtypeStruct((s, 3 * GDN_QK), F32), jax.ShapeDtypeStruct((s, GDN_QK), BF16),
                   jax.ShapeDtypeStruct((s, LANES), BF16),
                   jax.ShapeDtypeStruct((1, LANES), F32), jax.ShapeDtypeStruct((1, LANES), F32), jax.ShapeDtypeStruct((1, LANES), F32)],
        scratch_shapes=[pltpu.VMEM((GDN_HALO + t, 3 * GDN_QK), F32)] + [pltpu.VMEM((t, GDN_QK), F32)] * 3
                       + [pltpu.VMEM((t, LANES), F32)] * 2 + [pltpu.VMEM((t, GDN_QK), F32)] * 4
                       + [pltpu.VMEM((t, LANES), F32)] * 2 + [pltpu.VMEM((GDN_HEADS * GDN_DK, GDN_DK), F32)],
        compiler_params=_params(("arbitrary",)),
    )(dhb, p, p, p, p, p, p, p, p, o, states, minv, conv_w, a_log, dt_bias, norm_g)


def _zero_at_first_step(*refs):
    @pl.when(pl.program_id(0) == 0)
    def _():
        for r in refs:
            r[...] = jnp.zeros_like(r)


def _ln_bwd_call(dy, x_in, g, *, name):
    s, d = dy.shape
    t = min(ROW_TILE, s)

    def body(dy_ref, x_ref, g_ref, dx_ref, dxbf_ref, dg_ref, db_ref, ds_ref):
        _zero_at_first_step(dg_ref, db_ref, ds_ref)
        dy = dy_ref[...]
        xhat, rstd = _ln_stats(x_ref[...])
        dx = _ln_bwd(dy, xhat, rstd, g_ref[...])
        dx_ref[...] = dx
        dxbf_ref[...] = dx.astype(BF16)
        dg_ref[...] += _colsum(dy * xhat)
        db_ref[...] += _colsum(dy)
        ds_ref[...] += _colsum(dx)

    vec = _const_spec((1, d))
    return pl.pallas_call(
        body, name=name, grid=(s // t,),
        in_specs=[_row_spec(t, d), _row_spec(t, d), vec],
        out_specs=[_row_spec(t, d), _row_spec(t, d), vec, vec, vec],
        out_shape=[jax.ShapeDtypeStruct((s, d), F32), jax.ShapeDtypeStruct((s, d), BF16)] + [jax.ShapeDtypeStruct((1, d), F32)] * 3,
        compiler_params=_params(("arbitrary",)),
    )(dy, x_in, g)


def _ff2_bwd(dr2_bf, w_ff2_t, hpre, tag):
    s = dr2_bf.shape[0]
    tm = min(1024, s)
    tn = 1024

    def body(d_ref, w_ref, hp_ref, o_ref, db_ref):
        @pl.when(pl.program_id(1) == 0)
        def _():
            db_ref[...] = jnp.zeros_like(db_ref)

        dh = _dot(d_ref[...], w_ref[...]) * (2.0 * jnp.maximum(hp_ref[...], 0.0))
        o_ref[...] = dh.astype(BF16)
        db_ref[...] += _colsum(dh)

    return pl.pallas_call(
        body, name="ff2_bwd" + tag, grid=(D_FF // tn, s // tm),
        in_specs=[pl.BlockSpec((tm, D_MODEL), lambda j, i: (i, 0)), pl.BlockSpec((D_MODEL, tn), lambda j, i: (0, j)),
                  pl.BlockSpec((tm, tn), lambda j, i: (i, j))],
        out_specs=[pl.BlockSpec((tm, tn), lambda j, i: (i, j)), pl.BlockSpec((1, tn), lambda j, i: (0, j))],
        out_shape=[jax.ShapeDtypeStruct((s, D_FF), BF16), jax.ShapeDtypeStruct((1, D_FF), F32)],
        compiler_params=_params(("parallel", "arbitrary")),
    )(dr2_bf, w_ff2_t, hpre)


def _ff1_bwd_ln(dhpre_bf, w_ff1_t, dr2, r1, ln1_g, tag):
    s = dr2.shape[0]
    t = min(ROW_TILE, s)

    def body(dh_ref, w_ref, dr2_ref, r1_ref, g_ref, dr_ref, drbf_ref, dg_ref, db_ref):
        _zero_at_first_step(dg_ref, db_ref)
        dx1 = DN_ALPHA * dr2_ref[...] + _dot(dh_ref[...], w_ref[...])
        xhat, rstd = _ln_stats(r1_ref[...])
        dr = _ln_bwd(dx1, xhat, rstd, g_ref[...])
        dr_ref[...] = dr
        drbf_ref[...] = dr.astype(BF16)
        dg_ref[...] += _colsum(dx1 * xhat)
        db_ref[...] += _colsum(dx1)

    vec = _const_spec((1, D_MODEL))
    return pl.pallas_call(
        body, name="ff1_bwd_ln" + tag, grid=(s // t,),
        in_specs=[_row_spec(t, D_FF), _const_spec((D_FF, D_MODEL)), _row_spec(t, D_MODEL), _row_spec(t, D_MODEL), vec],
        out_specs=[_row_spec(t, D_MODEL), _row_spec(t, D_MODEL), vec, vec],
        out_shape=[jax.ShapeDtypeStruct((s, D_MODEL), F32), jax.ShapeDtypeStruct((s, D_MODEL), BF16),
                   jax.ShapeDtypeStruct((1, D_MODEL), F32), jax.ShapeDtypeStruct((1, D_MODEL), F32)],
        compiler_params=_params(("arbitrary",)),
    )(dhpre_bf, w_ff1_t, dr2, r1, ln1_g)


def _merge_bwd(dr1_bf, w_o_t, y3, p, b_gate, w_pa_t, w_pb_t, w_pc_t, tag):
    s = p.shape[0]
    t = min(ROW_TILE, s)
    gb = COL_GATE // D_MODEL

    def body(dr_ref, wo_ref, y_ref, ga_ref, gb_ref, gc_ref, bg_ref, wa_ref, wb_ref, wc_ref,
             dgate_ref, dy_ref, dha_ref, dhb_ref, dhc_ref, dbg_ref):
        _zero_at_first_step(dbg_ref)
        dm = _dot(dr_ref[...], wo_ref[...])
        for idx, (g_ref, w_ref, dh_ref) in enumerate(((ga_ref, wa_ref, dha_ref), (gb_ref, wb_ref, dhb_ref), (gc_ref, wc_ref, dhc_ref))):
            sl = slice(idx * D_MODEL, (idx + 1) * D_MODEL)
            sg = _sigmoid(g_ref[...] + bg_ref[:, sl])
            dgate = dm * y_ref[:, sl] * sg * (1.0 - sg)
            dgate_ref[:, sl] = dgate.astype(BF16)
            dbg_ref[:, sl] += _colsum(dgate)
            dy = (dm * sg).astype(BF16)
            dy_ref[:, sl] = dy
            dh_ref[...] = _dot(dy, w_ref[...])

    hspec = _row_spec(t, CONV_DIM)
    wspec = _const_spec((D_MODEL, CONV_DIM))
    return pl.pallas_call(
        body, name="merge_bwd" + tag, grid=(s // t,),
        in_specs=[_row_spec(t, D_MODEL), _const_spec((D_MODEL, D_MODEL)), _row_spec(t, 3 * D_MODEL),
                  _row_spec(t, D_MODEL, gb), _row_spec(t, D_MODEL, gb + 1), _row_spec(t, D_MODEL, gb + 2),
                  _const_spec((1, 3 * D_MODEL)), wspec, wspec, wspec],
        out_specs=[_row_spec(t, 3 * D_MODEL), _row_spec(t, 3 * D_MODEL), hspec, hspec, hspec, _const_spec((1, 3 * D_MODEL))],
        out_shape=[jax.ShapeDtypeStruct((s, 3 * D_MODEL), BF16), jax.ShapeDtypeStruct((s, 3 * D_MODEL), BF16)]
                  + [jax.ShapeDtypeStruct((s, CONV_DIM), F32)] * 3 + [jax.ShapeDtypeStruct((1, 3 * D_MODEL), F32)],
        compiler_params=_params(("arbitrary",)),
    )(dr1_bf, w_o_t, y3, p, p, p, b_gate, w_pa_t, w_pb_t, w_pc_t)


def _conv_a_bwd_pre(dha, c, ln_g, ln_b, tag):
    s = c.shape[0]
    t = min(ROW_TILE, s)

    def body(dh_ref, c_ref, g_ref, b_ref, dc_ref, dg_ref, db_ref, ds_ref):
        _zero_at_first_step(dg_ref, db_ref, ds_ref)
        xhat, rstd = _ln_stats(c_ref[...])
        n = xhat * g_ref[...] + b_ref[...]
        dn = dh_ref[...] * _dsilu(n)
        dc = _ln_bwd(dn, xhat, rstd, g_ref[...])
        dc_ref[...] = dc
        dg_ref[...] += _colsum(dn * xhat)
        db_ref[...] += _colsum(dn)
        ds_ref[...] += _colsum(dc)

    vec = _const_spec((1, CONV_DIM))
    return pl.pallas_call(
        body, name="conv_a_bwd_pre" + tag, grid=(s // t,),
        in_specs=[_row_spec(t, CONV_DIM), _row_spec(t, CONV_DIM), vec, vec],
        out_specs=[_row_spec(t, CONV_DIM), vec, vec, vec],
        out_shape=[jax.ShapeDtypeStruct((s, CONV_DIM), F32)] + [jax.ShapeDtypeStruct((1, CONV_DIM), F32)] * 3,
        compiler_params=_params(("arbitrary",)),
    )(dha, c, ln_g, ln_b)


def _dwconv_bwd(dy, x, w, *, width, halo, x_col_block, glu_p, name):
    s, ctot = dy.shape
    ct = CONV_DIM
    t = min(ROW_TILE, s)
    nblk = s // t
    glu = glu_p is not None

    def body(*refs):
        if glu:
            dy_ref, dyh_ref, x_ref, xh_ref, w_ref, a_ref, dx_ref, dw_ref, dye, xe = refs
        else:
            dy_ref, dyh_ref, x_ref, xh_ref, w_ref, dx_ref, dw_ref, dye, xe = refs
        i = pl.program_id(1)

        @pl.when(i == 0)
        def _():
            dw_ref[...] = jnp.zeros_like(dw_ref)

        dyv = dy_ref[...]
        dye[0:t, :] = dyv
        dye[t:t + halo, :] = jnp.where(i == nblk - 1, 0.0, dyh_ref[...])
        xe[0:halo, :] = jnp.where(i == 0, 0.0, xh_ref[...])
        xe[halo:halo + t, :] = x_ref[...]
        dx = jnp.zeros((t, ct), F32)
        for j in range(width):
            dx = dx + w_ref[j:j + 1, :] * dye[pl.ds(width - 1 - j, t), :]
            dw_ref[j:j + 1, :] += _colsum(dyv * xe[pl.ds(halo - (width - 1) + j, t), :])
        if glu:
            a = a_ref[...]
            a1 = a[:, :ct]
            sg = _sigmoid(a[:, ct:])
            dx_ref[:, :ct] = (dx * sg).astype(BF16)
            dx_ref[:, ct:] = (dx * a1 * sg * (1.0 - sg)).astype(BF16)
        else:
            dx_ref[...] = dx.astype(BF16)

    per = t // halo
    in_specs = [pl.BlockSpec((t, ct), lambda cb, i: (i, cb)),
                pl.BlockSpec((halo, ct), lambda cb, i: (jnp.minimum((i + 1) * per, nblk * per - 1), cb)),
                pl.BlockSpec((t, ct), lambda cb, i: (i, cb + x_col_block)),
                pl.BlockSpec((halo, ct), lambda cb, i: (jnp.maximum(i * per - 1, 0), cb + x_col_block)),
                pl.BlockSpec((width, ct), lambda cb, i: (0, cb))]
    args = [dy, dy, x, x, w]
    out_cols = ctot
    if glu:
        in_specs.append(pl.BlockSpec((t, 2 * ct), lambda cb, i: (i, COL_A // (2 * ct))))
        args.append(glu_p)
        out_cols = 2 * ct
    ocol = 2 * ct if glu else ct
    return pl.pallas_call(
        body, name=name, grid=(ctot // ct, nblk), in_specs=in_specs,
        out_specs=[pl.BlockSpec((t, ocol), lambda cb, i: (i, cb)), pl.BlockSpec((width, ct), lambda cb, i: (0, cb))],
        out_shape=[jax.ShapeDtypeStruct((s, out_cols), BF16), jax.ShapeDtypeStruct((width, ctot), F32)],
        scratch_shapes=[pltpu.VMEM((t + halo, ct), F32), pltpu.VMEM((halo + t, ct), F32)],
        compiler_params=_params(("parallel", "arbitrary")),
    )(*args)


def _sgu_bwd(dhc, p, ln_g, ln_b, w_s, b_s_t, tag):
    s = p.shape[0]
    t = min(ROW_TILE, s)
    cs = SGU_CHUNK

    def body(dh_ref, uv_ref, g_ref, b_ref, ws_ref, bst_ref, duv_ref, dg_ref, db_ref, dws_ref, dbs_ref):
        _zero_at_first_step(dg_ref, db_ref, dws_ref, dbs_ref)
        uv = uv_ref[...]
        u_raw, v_raw = uv[:, :SGU_DIM], uv[:, SGU_DIM:]
        u = _gelu(u_raw)
        xhat, rstd = _ln_stats(_gelu(v_raw))
        vn = xhat * g_ref[...] + b_ref[...]
        mixed = _sgu_mix(vn, ws_ref, bst_ref, t)
        dh = dh_ref[...]
        duv_ref[:, :SGU_DIM] = (dh * mixed * _dgelu(u_raw)).astype(BF16)
        dmix = dh * u
        row = lax.broadcasted_iota(jnp.int32, (cs, cs), 0)
        col = lax.broadcasted_iota(jnp.int32, (cs, cs), 1)
        dbs = jnp.zeros((cs, LANES), F32)
        chunks = []
        for g in range(SGU_GROUPS):
            wg = jnp.where(row >= col, ws_ref[g], 0.0).astype(BF16)
            dwg = jnp.zeros((cs, cs), F32)
            parts = []
            for ci in range(t // cs):
                rs = slice(ci * cs, (ci + 1) * cs)
                cl = slice(g * SGU_GROUP_DIM, (g + 1) * SGU_GROUP_DIM)
                dm = dmix[rs, cl]
                dmb = dm.astype(BF16)
                dwg = dwg + _dot_nt(dmb, vn[rs, cl].astype(BF16))
                dbs = dbs + _lane_place(jnp.sum(dm, axis=-1, keepdims=True), g, (cs, LANES))
                parts.append(_dot_tn(wg, dmb))
            dws_ref[g] += jnp.where(row >= col, dwg, 0.0)
            chunks.append(jnp.concatenate(parts, axis=0))
        dbs_ref[...] += dbs
        dvn = jnp.concatenate(chunks, axis=1)
        dvv = _ln_bwd(dvn, xhat, rstd, g_ref[...])
        duv_ref[:, SGU_DIM:] = (dvv * _dgelu(v_raw)).astype(BF16)
        dg_ref[...] += _colsum(dvn * xhat)
        db_ref[...] += _colsum(dvn)

    vec = _const_spec((1, SGU_DIM))
    wss = _const_spec((SGU_GROUPS, cs, cs))
    return pl.pallas_call(
        body, name="sgu_bwd" + tag, grid=(s // t,),
        in_specs=[_row_spec(t, SGU_DIM), _row_spec(t, 2 * SGU_DIM, COL_UV // (2 * SGU_DIM)), vec, vec, wss, _const_spec((cs, LANES))],
        out_specs=[_row_spec(t, 2 * SGU_DIM), vec, vec, wss, _const_spec((cs, LANES))],
        out_shape=[jax.ShapeDtypeStruct((s, 2 * SGU_DIM), BF16), jax.ShapeDtypeStruct((1, SGU_DIM), F32),
                   jax.ShapeDtypeStruct((1, SGU_DIM), F32), jax.ShapeDtypeStruct((SGU_GROUPS, cs, cs), F32),
                   jax.ShapeDtypeStruct((cs, LANES), F32)],
        compiler_params=_params(("arbitrary",)),
    )(dhc, p, ln_g, ln_b, w_s, b_s_t)


def _reorder_proj_cols(w):
    pad = jnp.zeros(w.shape[:-1] + (P_COLS - PROJ_COLS,), w.dtype)
    return jnp.concatenate([w[..., :3072], w[..., 3080:PROJ_COLS], w[..., 3072:3080], pad], axis=-1)


def _restore_proj_cols(g):
    return jnp.concatenate([g[..., :3072], g[..., COL_BA:COL_BA + 8], g[..., 3072:COL_BA]], axis=-1)


def _pad_lanes(v, offset):
    return jnp.pad(v, (offset, LANES - offset - v.shape[0]))[None, :]


def _layer_weights(l, full, rep):
    row = lambda v: v[l][None, :]
    w_all = _reorder_proj_cols(full["w_in"][l])
    w = dict(
        w_all=w_all, w_all_t=w_all.T,
        conv_w=full["conv_dw_w"][l], conv_b=row(rep["conv_dw_b"]), conv_ln_g=row(rep["conv_ln_g"]), conv_ln_b=row(rep["conv_ln_b"]),
        w_pa=full["w_pa"][l], w_pb=full["w_pb"][l], w_pc=full["w_pc"][l],
        w_pa_t=full["w_pa"][l].T, w_pb_t=full["w_pb"][l].T, w_pc_t=full["w_pc"][l].T,
        gdn_cw=jnp.concatenate([full["gdn_conv_q"][l], full["gdn_conv_k"][l], full["gdn_conv_v"][l]], axis=-1),
        a_log=_pad_lanes(rep["gdn_a_log"][l], GDN_HEADS), dt_bias=_pad_lanes(rep["gdn_dt_bias"][l], GDN_HEADS),
        norm_g=row(rep["gdn_norm_g"]),
        sgu_ln_g=row(rep["sgu_ln_g"]), sgu_ln_b=row(rep["sgu_ln_b"]), sgu_w_s=rep["sgu_w_s"][l],
        sgu_b_s_t=jnp.pad(rep["sgu_b_s"][l].T, ((0, 0), (0, LANES - SGU_GROUPS))),
        b_gate=row(rep["b_gate"]),
        w_o=full["w_o"][l], w_o_t=full["w_o"][l].T, ln1_g=row(rep["ln1_g"]), ln1_b=row(rep["ln1_b"]),
        w_ff1=full["w_ff1"][l], w_ff1_t=full["w_ff1"][l].T, b_ff1=row(rep["b_ff1"]),
        w_ff2=full["w_ff2"][l], w_ff2_t=full["w_ff2"][l].T, b_ff2=row(rep["b_ff2"]),
        ln2_g=row(rep["ln2_g"]), ln2_b=row(rep["ln2_b"]),
    )
    return w


def _layer_fwd(x, x_bf, w, tag):
    s = x.shape[0]
    p = _matmul(x_bf, w["w_all"], name="proj_fwd" + tag, tm=min(ROW_TILE, s), tn=P_TILE, tk=D_MODEL)
    h_glu, c, ha = _conv_a_fwd(p, w["conv_w"], w["conv_b"], w["conv_ln_g"], w["conv_ln_b"], tag)
    o, hb, states, minv = _gdn_fwd(p, w["gdn_cw"], w["a_log"], w["dt_bias"], w["norm_g"], tag)
    hc = _sgu_fwd(p, w["sgu_ln_g"], w["sgu_ln_b"], w["sgu_w_s"], w["sgu_b_s_t"], tag)
    y3, merged = _merge_fwd(p, ha, hb, hc, w["w_pa"], w["w_pb"], w["w_pc"], w["b_gate"], tag)
    r1, x1, x1_bf = _matmul_res_ln(merged, w["w_o"], jnp.zeros((1, D_MODEL), F32), x, w["ln1_g"], w["ln1_b"], name="o_res_ln" + tag)
    hpre, h_bf = _ff1_fwd(x1_bf, w["w_ff1"], w["b_ff1"], tag)
    r2, x2, x2_bf = _matmul_res_ln(h_bf, w["w_ff2"], w["b_ff2"], x1, w["ln2_g"], w["ln2_b"], name="ff2_res_ln" + tag)
    saved = dict(x=x, x_bf=x_bf, p=p, h_glu=h_glu, c=c, ha=ha, o=o, hb=hb, states=states, minv=minv, hc=hc, y3=y3,
                 merged=merged, r1=r1, x1=x1, x1_bf=x1_bf, hpre=hpre, h_bf=h_bf, r2=r2)
    return x2, x2_bf, saved


def _layer_bwd(dx2, w, sv, tag):
    s = dx2.shape[0]
    ts = min(1024, s)
    p = sv["p"]
    dr2, dr2_bf, d_ln2_g, d_ln2_b, d_b_ff2 = _ln_bwd_call(dx2, sv["r2"], w["ln2_g"], name="ln2_bwd" + tag)
    dhpre_bf, d_b_ff1 = _ff2_bwd(dr2_bf, w["w_ff2_t"], sv["hpre"], tag)
    d_w_ff2 = _matmul_tn(sv["h_bf"], dr2_bf, name="dw_ff2" + tag, ka=D_FF, tka=1024, tn=1024, ts=ts)
    d_w_ff1 = _matmul_tn(sv["x1_bf"], dhpre_bf, name="dw_ff1" + tag, ka=D_MODEL, tka=1024, tn=1024, ts=ts)
    dr1, dr1_bf, d_ln1_g, d_ln1_b = _ff1_bwd_ln(dhpre_bf, w["w_ff1_t"], dr2, sv["r1"], w["ln1_g"], tag)
    d_w_o = _matmul_tn(sv["merged"], dr1_bf, name="dw_o" + tag, ka=D_MODEL, tka=1024, tn=1024, ts=ts)
    dgate_bf, dy3_bf, dha, dhb, dhc, d_b_gate = _merge_bwd(dr1_bf, w["w_o_t"], sv["y3"], p, w["b_gate"],
                                                          w["w_pa_t"], w["w_pb_t"], w["w_pc_t"], tag)
    d_w_p = [_matmul_tn(h, dy3_bf, name=f"dw_p{n}" + tag, ka=CONV_DIM, tka=CONV_DIM, tn=1024, ts=ts, n=D_MODEL, b_col_block=n)
             for n, h in enumerate((sv["ha"], sv["hb"], sv["hc"]))]
    dc, d_conv_ln_g, d_conv_ln_b, d_conv_b = _conv_a_bwd_pre(dha, sv["c"], w["conv_ln_g"], w["conv_ln_b"], tag)
    da_bf, d_conv_w = _dwconv_bwd(dc, sv["h_glu"], w["conv_w"], width=CONV_WIDTH, halo=CONV_HALO, x_col_block=0,
                                  glu_p=p, name="conv_a_bwd" + tag)
    duv_bf, d_sgu_ln_g, d_sgu_ln_b, d_sgu_w_s, d_sgu_b_s_t = _sgu_bwd(dhc, p, w["sgu_ln_g"], w["sgu_ln_b"], w["sgu_w_s"], w["sgu_b_s_t"], tag)
    dpre, dz_bf, dba_bf, d_norm_g, d_a_log, d_dt_bias = _gdn_bwd(dhb, p, sv["o"], sv["states"], sv["minv"], w["gdn_cw"],
                                                                 w["a_log"], w["dt_bias"], w["norm_g"], tag)
    dqkv_bf, d_gdn_cw = _dwconv_bwd(dpre, p, w["gdn_cw"], width=GDN_CONV, halo=GDN_HALO, x_col_block=COL_Q // CONV_DIM,
                                    glu_p=None, name="gdn_conv_bwd" + tag)
    dp_bf = jnp.concatenate([da_bf, dqkv_bf, dz_bf, duv_bf, dgate_bf, dba_bf], axis=1)
    dx = _matmul(dp_bf, w["w_all_t"], name="proj_bwd" + tag, tm=min(ROW_TILE, s), tn=D_MODEL, tk=P_TILE, add=dr1, add_scale=DN_ALPHA)
    d_w_all = _matmul_tn(sv["x_bf"], dp_bf, name="dw_proj" + tag, ka=D_MODEL, tka=512, tn=P_TILE, ts=ts)
    grads = dict(
        w_in=_restore_proj_cols(d_w_all), b_gate=d_b_gate[0], conv_dw_w=d_conv_w, conv_dw_b=d_conv_b[0],
        conv_ln_g=d_conv_ln_g[0], conv_ln_b=d_conv_ln_b[0], w_pa=d_w_p[0],
        gdn_conv_q=d_gdn_cw[:, :GDN_QK], gdn_conv_k=d_gdn_cw[:, GDN_QK:2 * GDN_QK], gdn_conv_v=d_gdn_cw[:, 2 * GDN_QK:],
        gdn_a_log=d_a_log[0, GDN_HEADS:2 * GDN_HEADS], gdn_dt_bias=d_dt_bias[0, GDN_HEADS:2 * GDN_HEADS], gdn_norm_g=d_norm_g[0],
        w_pb=d_w_p[1], sgu_ln_g=d_sgu_ln_g[0], sgu_ln_b=d_sgu_ln_b[0], sgu_w_s=d_sgu_w_s, sgu_b_s=d_sgu_b_s_t[:, :SGU_GROUPS].T,
        w_pc=d_w_p[2], w_o=d_w_o, ln1_g=d_ln1_g[0], ln1_b=d_ln1_b[0], w_ff1=d_w_ff1, b_ff1=d_b_ff1[0],
        w_ff2=d_w_ff2, b_ff2=d_b_ff2[0], ln2_g=d_ln2_g[0], ln2_b=d_ln2_b[0],
    )
    return dx, grads


def _local_step(x, target, full, rep):
    ln_g, ln_b = rep["ln_in_g"][None, :], rep["ln_in_b"][None, :]
    xs, xs_bf = _ln_in_fwd(x, ln_g, ln_b)
    ws, saves = [], []
    for l in range(DEPTH):
        ws.append(_layer_weights(l, full, rep))
        xs, xs_bf, sv = _layer_fwd(xs, xs_bf, ws[l], f"_l{l}")
        saves.append(sv)
    d, loss_acc = _loss_fwd_bwd(xs, target)
    layer_grads = [None] * DEPTH
    for l in reversed(range(DEPTH)):
        d, layer_grads[l] = _layer_bwd(d, ws[l], saves[l], f"_l{l}")
    dx, _, d_ln_in_g, d_ln_in_b, _ = _ln_bwd_call(d, x, ln_g, name="ln_in_bwd")
    grads = {k: jnp.stack([layer_grads[l][k] for l in range(DEPTH)]) for k in layer_grads[0]}
    grads["ln_in_g"] = d_ln_in_g[0]
    grads["ln_in_b"] = d_ln_in_b[0]
    return loss_acc[0, 0], dx, grads


MESH_AXES = ("x", "y", "c")


def _exchange(arrays, scatter, *, name):
    n = len(arrays)

    def body(*refs):
        ins, outs = refs[:n], refs[n:2 * n]
        send_sems, recv_sems, local_sems = refs[2 * n:]
        x, y, c = lax.axis_index("x"), lax.axis_index("y"), lax.axis_index("c")
        me = 4 * x + 2 * y + c

        def slot(a, d):
            return ins[a].at[d] if scatter[a] else ins[a]

        local = [pltpu.make_async_copy(slot(a, me), outs[a].at[me], local_sems.at[a]) for a in range(n)]
        for cp in local:
            cp.start()
        remote = []
        for k in range(1, N_DEV):
            px = 1 - x if k & 4 else x
            py = 1 - y if k & 2 else y
            pc = 1 - c if k & 1 else c
            peer = 4 * px + 2 * py + pc
            for a in range(n):
                send = pltpu.make_async_remote_copy(
                    src_ref=slot(a, peer), dst_ref=outs[a].at[me], send_sem=send_sems.at[a, k - 1],
                    recv_sem=recv_sems.at[a, k - 1], device_id=(px, py, pc), device_id_type=pl.DeviceIdType.MESH)
                send.start()
                arrival = pltpu.make_async_remote_copy(
                    src_ref=slot(a, peer), dst_ref=outs[a].at[peer], send_sem=send_sems.at[a, k - 1],
                    recv_sem=recv_sems.at[a, k - 1], device_id=(px, py, pc), device_id_type=pl.DeviceIdType.MESH)
                remote.append((send, arrival))
        for send, arrival in remote:
            arrival.wait_recv()
        for send, arrival in remote:
            send.wait_send()
        for cp in local:
            cp.wait()

    any_spec = pl.BlockSpec(memory_space=pl.ANY)
    out_shape = [jax.ShapeDtypeStruct(a.shape if s else (N_DEV,) + a.shape, a.dtype) for a, s in zip(arrays, scatter)]
    return pl.pallas_call(
        body, name=name, in_specs=[any_spec] * n, out_specs=[any_spec] * n, out_shape=out_shape,
        scratch_shapes=[pltpu.SemaphoreType.DMA((n, N_DEV - 1)), pltpu.SemaphoreType.DMA((n, N_DEV - 1)),
                        pltpu.SemaphoreType.DMA((n,))],
    )(*arrays)


def _adamw(w, m, v, g_parts, *, name):
    r, c = w.shape
    tr = 256 if r % 256 == 0 else r
    bc1 = 1.0 - ADAM_B1 ** ADAM_STEP
    bc2 = 1.0 - ADAM_B2 ** ADAM_STEP

    def body(w_ref, m_ref, v_ref, gp_ref, g_ref, d_ref, nm_ref, nv_ref):
        g = gp_ref[0]
        for d in range(1, N_DEV):
            g = g + gp_ref[d]
        nm = ADAM_B1 * m_ref[...] + (1.0 - ADAM_B1) * g
        nv = ADAM_B2 * v_ref[...] + (1.0 - ADAM_B2) * (g * g)
        g_ref[...] = g
        nm_ref[...] = nm
        nv_ref[...] = nv
        d_ref[...] = -ADAM_LR * ((nm / bc1) / (jnp.sqrt(nv / bc2) + ADAM_EPS) + ADAM_WD * w_ref[...])

    spec = pl.BlockSpec((tr, c), lambda i: (i, 0))
    return pl.pallas_call(
        body, name=name, grid=(r // tr,),
        in_specs=[spec, spec, spec, pl.BlockSpec((N_DEV, tr, c), lambda i: (0, i, 0))],
        out_specs=[spec] * 4, out_shape=[jax.ShapeDtypeStruct((r, c), F32)] * 4,
        compiler_params=_params(("parallel",)),
    )(w, m, v, g_parts)


SHARDED = dict(w_in=2, conv_dw_w=2, w_pa=2, gdn_conv_q=2, gdn_conv_k=2, gdn_conv_v=2, w_pb=2, w_pc=2, w_o=1, w_ff1=2, w_ff2=1)
WEIGHTS = ["ln_in_g", "ln_in_b", "w_in", "b_gate", "conv_dw_w", "conv_dw_b", "conv_ln_g", "conv_ln_b", "w_pa", "gdn_conv_q",
           "gdn_conv_k", "gdn_conv_v", "gdn_a_log", "gdn_dt_bias", "gdn_norm_g", "w_pb", "sgu_ln_g", "sgu_ln_b", "sgu_w_s",
           "sgu_b_s", "w_pc", "w_o", "ln1_g", "ln1_b", "w_ff1", "b_ff1", "w_ff2", "b_ff2", "ln2_g", "ln2_b"]
REPLICATED = [n for n in WEIGHTS if n not in SHARDED]
CONV_PACK = ["conv_dw_w", "gdn_conv_q", "gdn_conv_k", "gdn_conv_v"]
PROJ_PACK = ["w_pa", "w_pb", "w_pc"]


def _to_slots(full, axis):
    shp = full.shape
    split = full.reshape(shp[:axis] + (N_DEV, shp[axis] // N_DEV) + shp[axis + 1:])
    return jnp.moveaxis(split, axis, 0)


def _from_slots(slots, axis):
    merged = jnp.moveaxis(slots, 0, axis)
    shp = merged.shape
    return merged.reshape(shp[:axis] + (shp[axis] * shp[axis + 1],) + shp[axis + 2:])


def _pack_rows(arrs):
    rows = []
    for a in arrs:
        flat = a.reshape(-1)
        pad = (-flat.shape[0]) % LANES
        rows.append(jnp.pad(flat, (0, pad)).reshape(-1, LANES))
    out = jnp.concatenate(rows, axis=0)
    return jnp.pad(out, ((0, (-out.shape[0]) % 8), (0, 0)))


def _unpack_rows(packed, shapes):
    out, r = [], 0
    for shp in shapes:
        size = math.prod(shp)
        nrows = -(-size // LANES)
        out.append(packed[r:r + nrows].reshape(-1)[:size].reshape(shp))
        r += nrows
    return out


def kernel(x, ln_in_g, ln_in_b, w_in, b_gate, conv_dw_w, conv_dw_b, conv_ln_g, conv_ln_b, w_pa, gdn_conv_q, gdn_conv_k, gdn_conv_v, gdn_a_log, gdn_dt_bias, gdn_norm_g, w_pb, sgu_ln_g, sgu_ln_b, sgu_w_s, sgu_b_s, w_pc, w_o, ln1_g, ln1_b, w_ff1, b_ff1, w_ff2, b_ff2, ln2_g, ln2_b, loss_target, m_ln_in_g, m_ln_in_b, m_w_in, m_b_gate, m_conv_dw_w, m_conv_dw_b, m_conv_ln_g, m_conv_ln_b, m_w_pa, m_gdn_conv_q, m_gdn_conv_k, m_gdn_conv_v, m_gdn_a_log, m_gdn_dt_bias, m_gdn_norm_g, m_w_pb, m_sgu_ln_g, m_sgu_ln_b, m_sgu_w_s, m_sgu_b_s, m_w_pc, m_w_o, m_ln1_g, m_ln1_b, m_w_ff1, m_b_ff1, m_w_ff2, m_b_ff2, m_ln2_g, m_ln2_b, v_ln_in_g, v_ln_in_b, v_w_in, v_b_gate, v_conv_dw_w, v_conv_dw_b, v_conv_ln_g, v_conv_ln_b, v_w_pa, v_gdn_conv_q, v_gdn_conv_k, v_gdn_conv_v, v_gdn_a_log, v_gdn_dt_bias, v_gdn_norm_g, v_w_pb, v_sgu_ln_g, v_sgu_ln_b, v_sgu_w_s, v_sgu_b_s, v_w_pc, v_w_o, v_ln1_g, v_ln1_b, v_w_ff1, v_b_ff1, v_w_ff2, v_b_ff2, v_ln2_g, v_ln2_b):
    args = locals()
    w = {n: args[n] for n in WEIGHTS}
    m = {n: args["m_" + n] for n in WEIGHTS}
    v = {n: args["v_" + n] for n in WEIGHTS}

    conv_local = jnp.concatenate([w[n] for n in CONV_PACK], axis=1)
    proj_local = jnp.stack([w[n] for n in PROJ_PACK], axis=1).astype(BF16)
    big = ["w_in", "w_o", "w_ff1", "w_ff2"]
    gathered = _exchange([w[n].astype(BF16) for n in big] + [proj_local, conv_local], [False] * 6, name="gather_weights")
    full = {n: _from_slots(g, SHARDED[n]) for n, g in zip(big, gathered[:4])}
    proj_full = _from_slots(gathered[4], 3)
    for i, n in enumerate(PROJ_PACK):
        full[n] = proj_full[:, i]
    conv_full = _from_slots(gathered[5], 2)
    tap0 = 0
    for n in CONV_PACK:
        taps = w[n].shape[1]
        full[n] = conv_full[:, tap0:tap0 + taps]
        tap0 += taps
    rep = {n: w[n] for n in REPLICATED}

    loss_local, dx, grads = _local_step(x[0], loss_target[0], full, rep)
    loss = lax.psum(loss_local, MESH_AXES)

    conv_grad = jnp.concatenate([grads[n] for n in CONV_PACK], axis=1)
    proj_grad = jnp.stack([grads[n] for n in PROJ_PACK], axis=1)
    send = [_to_slots(grads[n], SHARDED[n]) for n in big] + [_to_slots(proj_grad, 3), _to_slots(conv_grad, 2),
                                                             _pack_rows([grads[n] for n in REPLICATED])]
    parts = _exchange(send, [True] * 6 + [False], name="exchange_grads")

    def adam_sharded(g_parts, w_l, m_l, v_l, name):
        shp = w_l.shape
        two_d = lambda a: a.reshape(-1, shp[-1])
        outs = _adamw(two_d(w_l), two_d(m_l), two_d(v_l), g_parts.reshape(N_DEV, -1, shp[-1]), name=name)
        return [o.reshape(shp) for o in outs]

    res = {}
    for n, gp in zip(big, parts[:4]):
        res[n] = adam_sharded(gp, w[n], m[n], v[n], "adamw_" + n)
    proj_res = adam_sharded(parts[4], jnp.stack([w[n] for n in PROJ_PACK], axis=1), jnp.stack([m[n] for n in PROJ_PACK], axis=1),
                            jnp.stack([v[n] for n in PROJ_PACK], axis=1), "adamw_proj")
    for i, n in enumerate(PROJ_PACK):
        res[n] = [o[:, i] for o in proj_res]
    conv_res = adam_sharded(parts[5], conv_local, jnp.concatenate([m[n] for n in CONV_PACK], axis=1),
                            jnp.concatenate([v[n] for n in CONV_PACK], axis=1), "adamw_conv")
    tap0 = 0
    for n in CONV_PACK:
        taps = w[n].shape[1]
        res[n] = [o[:, tap0:tap0 + taps] for o in conv_res]
        tap0 += taps
    rep_shapes = [w[n].shape for n in REPLICATED]
    rep_res = _adamw(_pack_rows([w[n] for n in REPLICATED]), _pack_rows([m[n] for n in REPLICATED]),
                     _pack_rows([v[n] for n in REPLICATED]), parts[6], name="adamw_replicated")
    rep_res = [_unpack_rows(o, rep_shapes) for o in rep_res]
    for i, n in enumerate(REPLICATED):
        res[n] = [o[i] for o in rep_res]

    outs = [loss, dx[None]]
    for j in range(4):
        outs += [res[n][j] for n in WEIGHTS]
    return tuple(outs)
```

```python
import functools
import math

import jax
import jax.numpy as jnp
from jax import lax
from jax.experimental import pallas as pl
from jax.experimental.pallas import tpu as pltpu

F32 = jnp.float32
BF16 = jnp.bfloat16

N_DEV = 8
DEPTH = 2
D_MODEL = 1024
CONV_DIM = 512
CONV_WIDTH = 31
GDN_HEADS = 4
GDN_DK = 128
GDN_QK = 512
GDN_CONV = 4
GDN_CHUNK = 64
SGU_GROUPS = 4
SGU_GROUP_DIM = 128
SGU_DIM = 512
SGU_CHUNK = 128
D_FF = 4096
DN_ALPHA = (2 * DEPTH) ** 0.25
LN_EPS = 1e-5
RMS_EPS = 1e-6
PROJ_COLS = 7176
SHARD_COLS = PROJ_COLS // N_DEV

COL_A = 0
COL_Q = 1024
COL_Z = 2560
COL_UV = 3072
COL_GATE = 4096
COL_BA = 7168
P_COLS = 7296
P_TILE = 2432

ADAM_LR = 0.001
ADAM_B1 = 0.9
ADAM_B2 = 0.999
ADAM_EPS = 1e-08
ADAM_WD = 0.01
ADAM_STEP = 10

VMEM_LIMIT_BYTES = 56 * 1024 * 1024
LANES = 128
ROW_TILE = 512
GDN_BLOCK = 256
CONV_HALO = 32
GDN_HALO = 8


def _params(sem):
    return pltpu.CompilerParams(dimension_semantics=sem, vmem_limit_bytes=VMEM_LIMIT_BYTES)


def _dot(a, b):
    return jnp.dot(a, b, preferred_element_type=F32)


def _dot_nt(a, b):
    return lax.dot_general(a, b, (((1,), (1,)), ((), ())), preferred_element_type=F32)


def _dot_tn(a, b):
    return lax.dot_general(a, b, (((0,), (0,)), ((), ())), preferred_element_type=F32)


def _split(a):
    hi = a.astype(BF16)
    lo = (a - hi.astype(F32)).astype(BF16)
    return hi, lo


def _dot3(a, b, dot=_dot):
    ah, al = _split(a)
    bh, bl = _split(b)
    return dot(ah, bh) + (dot(ah, bl) + dot(al, bh))


def _bdot(a, b, dot=_dot):
    return dot(a.astype(BF16), b.astype(BF16))


def _sigmoid(x):
    return jax.nn.sigmoid(x)


def _silu(x):
    return x * _sigmoid(x)


def _dsilu(x):
    s = _sigmoid(x)
    return s * (1.0 + x * (1.0 - s))


_GELU_C = math.sqrt(2.0 / math.pi)


def _gelu(x):
    return 0.5 * x * (1.0 + jnp.tanh(_GELU_C * (x + 0.044715 * (x * x * x))))


def _dgelu(x):
    t = jnp.tanh(_GELU_C * (x + 0.044715 * (x * x * x)))
    return 0.5 * (1.0 + t) + 0.5 * x * (1.0 - t * t) * (_GELU_C * (1.0 + 3.0 * 0.044715 * (x * x)))


def _ln_stats(x):
    mu = jnp.mean(x, axis=-1, keepdims=True)
    xc = x - mu
    var = jnp.mean(xc * xc, axis=-1, keepdims=True)
    rstd = lax.rsqrt(var + LN_EPS)
    return xc * rstd, rstd


def _ln_bwd(dy, xhat, rstd, g):
    dxh = dy * g
    return rstd * (dxh - jnp.mean(dxh, axis=-1, keepdims=True) - xhat * jnp.mean(dxh * xhat, axis=-1, keepdims=True))


def _colsum(x):
    return jnp.sum(x, axis=0, keepdims=True)


def _row_spec(t, cols, col_block=0):
    return pl.BlockSpec((t, cols), lambda i, cb=col_block: (i, cb))


def _const_spec(shape):
    nd = len(shape)
    return pl.BlockSpec(shape, lambda *_: (0,) * nd)


def _matmul(a, w, *, name, tm, tn, tk, out_dtype=F32, a_col_block=0, add=None, add_scale=1.0):
    m = a.shape[0]
    k, n = w.shape
    nk = k // tk
    has_add = add is not None

    def body(*refs):
        if has_add:
            a_ref, w_ref, add_ref, o_ref, acc_ref = refs
        else:
            a_ref, w_ref, o_ref, acc_ref = refs
        kk = pl.program_id(2)

        @pl.when(kk == 0)
        def _():
            acc_ref[...] = jnp.zeros_like(acc_ref)

        acc_ref[...] += _dot(a_ref[...], w_ref[...])

        @pl.when(kk == nk - 1)
        def _():
            r = acc_ref[...]
            if has_add:
                r = r + add_scale * add_ref[...]
            o_ref[...] = r.astype(out_dtype)

    in_specs = [pl.BlockSpec((tm, tk), lambda i, j, kk: (i, kk + a_col_block)),
                pl.BlockSpec((tk, tn), lambda i, j, kk: (kk, j))]
    args = [a, w]
    if has_add:
        in_specs.append(pl.BlockSpec((tm, tn), lambda i, j, kk: (i, j)))
        args.append(add)
    return pl.pallas_call(
        body, name=name, grid=(m // tm, n // tn, nk), in_specs=in_specs,
        out_specs=pl.BlockSpec((tm, tn), lambda i, j, kk: (i, j)),
        out_shape=jax.ShapeDtypeStruct((m, n), out_dtype),
        scratch_shapes=[pltpu.VMEM((tm, tn), F32)],
        compiler_params=_params(("parallel", "parallel", "arbitrary")),
    )(*args)


def _matmul_tn(a, b, *, name, ka, tka, tn, ts, n=None, a_col_block=0, b_col_block=0):
    s = a.shape[0]
    n = b.shape[1] if n is None else n
    ns = s // ts

    def body(a_ref, b_ref, o_ref):
        @pl.when(pl.program_id(2) == 0)
        def _():
            o_ref[...] = jnp.zeros_like(o_ref)

        o_ref[...] += _dot_tn(a_ref[...], b_ref[...])

    return pl.pallas_call(
        body, name=name, grid=(ka // tka, n // tn, ns),
        in_specs=[pl.BlockSpec((ts, tka), lambda i, j, t: (t, i + a_col_block)),
                  pl.BlockSpec((ts, tn), lambda i, j, t: (t, j + b_col_block))],
        out_specs=pl.BlockSpec((tka, tn), lambda i, j, t: (i, j)),
        out_shape=jax.ShapeDtypeStruct((ka, n), F32),
        compiler_params=_params(("parallel", "parallel", "arbitrary")),
    )(a, b)


def _ln_in_fwd(x, g, b):
    s = x.shape[0]
    t = min(ROW_TILE, s)

    def body(x_ref, g_ref, b_ref, y_ref, ybf_ref):
        xhat, _ = _ln_stats(x_ref[...])
        y = xhat * g_ref[...] + b_ref[...]
        y_ref[...] = y
        ybf_ref[...] = y.astype(BF16)

    return pl.pallas_call(
        body, name="ln_in_fwd", grid=(s // t,),
        in_specs=[_row_spec(t, D_MODEL), _const_spec((1, D_MODEL)), _const_spec((1, D_MODEL))],
        out_specs=[_row_spec(t, D_MODEL), _row_spec(t, D_MODEL)],
        out_shape=[jax.ShapeDtypeStruct((s, D_MODEL), F32), jax.ShapeDtypeStruct((s, D_MODEL), BF16)],
        compiler_params=_params(("parallel",)),
    )(x, g, b)


def _prev_halo_spec(t, halo, cols, col_block):
    per = t // halo
    return pl.BlockSpec((halo, cols), lambda i, cb=col_block: (jnp.maximum(i * per - 1, 0), cb))


def _next_halo_spec(t, halo, cols, col_block, n_blocks):
    per = t // halo
    last = n_blocks * per - 1
    return pl.BlockSpec((halo, cols), lambda i, cb=col_block: (jnp.minimum((i + 1) * per, last), cb))


def _conv_a_fwd(p, w, b, ln_g, ln_b, tag):
    s = p.shape[0]
    t = min(ROW_TILE, s)
    width = CONV_WIDTH

    def body(a_ref, halo_ref, w_ref, b_ref, g_ref, bb_ref, h_ref, c_ref, ha_ref, ext):
        i = pl.program_id(0)
        a = a_ref[...]
        h = a[:, :CONV_DIM] * _sigmoid(a[:, CONV_DIM:])
        ah = halo_ref[...]
        hh = ah[:, :CONV_DIM] * _sigmoid(ah[:, CONV_DIM:])
        ext[0:CONV_HALO, :] = jnp.where(i == 0, 0.0, hh)
        ext[CONV_HALO:CONV_HALO + t, :] = h
        acc = jnp.zeros((t, CONV_DIM), F32)
        for j in range(width):
            acc = acc + w_ref[j:j + 1, :] * ext[pl.ds(CONV_HALO - (width - 1) + j, t), :]
        c = acc + b_ref[...]
        xhat, _ = _ln_stats(c)
        n = xhat * g_ref[...] + bb_ref[...]
        h_ref[...] = h
        c_ref[...] = c
        ha_ref[...] = _silu(n).astype(BF16)

    return pl.pallas_call(
        body, name="conv_a_fwd" + tag, grid=(s // t,),
        in_specs=[_row_spec(t, 2 * CONV_DIM, COL_A // (2 * CONV_DIM)),
                  _prev_halo_spec(t, CONV_HALO, 2 * CONV_DIM, COL_A // (2 * CONV_DIM)),
                  _const_spec((width, CONV_DIM)), _const_spec((1, CONV_DIM)),
                  _const_spec((1, CONV_DIM)), _const_spec((1, CONV_DIM))],
        out_specs=[_row_spec(t, CONV_DIM)] * 3,
        out_shape=[jax.ShapeDtypeStruct((s, CONV_DIM), F32), jax.ShapeDtypeStruct((s, CONV_DIM), F32),
                   jax.ShapeDtypeStruct((s, CONV_DIM), BF16)],
        scratch_shapes=[pltpu.VMEM((CONV_HALO + t, CONV_DIM), F32)],
        compiler_params=_params(("parallel",)),
    )(p, p, w, b, ln_g, ln_b)


def _sgu_mix(vn, wt_ref, bst_ref, t):
    row = lax.broadcasted_iota(jnp.int32, (SGU_CHUNK, SGU_CHUNK), 0)
    col = lax.broadcasted_iota(jnp.int32, (SGU_CHUNK, SGU_CHUNK), 1)
    chunks = []
    for ci in range(t // SGU_CHUNK):
        groups = []
        for g in range(SGU_GROUPS):
            wg = jnp.where(row >= col, wt_ref[g], 0.0).astype(BF16)
            v_cg = vn[ci * SGU_CHUNK:(ci + 1) * SGU_CHUNK, g * SGU_GROUP_DIM:(g + 1) * SGU_GROUP_DIM]
            groups.append(_dot(wg, v_cg.astype(BF16)) + bst_ref[:, g:g + 1])
        chunks.append(jnp.concatenate(groups, axis=1))
    return jnp.concatenate(chunks, axis=0)


def _sgu_fwd(p, ln_g, ln_b, w_s, b_s_t, tag):
    s = p.shape[0]
    t = min(ROW_TILE, s)

    def body(uv_ref, g_ref, b_ref, ws_ref, bst_ref, hc_ref):
        uv = uv_ref[...]
        u = _gelu(uv[:, :SGU_DIM])
        vv = _gelu(uv[:, SGU_DIM:])
        xhat, _ = _ln_stats(vv)
        vn = xhat * g_ref[...] + b_ref[...]
        mixed = _sgu_mix(vn, ws_ref, bst_ref, t)
        hc_ref[...] = (u * mixed).astype(BF16)

    return pl.pallas_call(
        body, name="sgu_fwd" + tag, grid=(s // t,),
        in_specs=[_row_spec(t, 2 * SGU_DIM, COL_UV // (2 * SGU_DIM)),
                  _const_spec((1, SGU_DIM)), _const_spec((1, SGU_DIM)),
                  _const_spec((SGU_GROUPS, SGU_CHUNK, SGU_CHUNK)), _const_spec((SGU_CHUNK, LANES))],
        out_specs=_row_spec(t, SGU_DIM),
        out_shape=jax.ShapeDtypeStruct((s, SGU_DIM), BF16),
        compiler_params=_params(("parallel",)),
    )(p, ln_g, ln_b, w_s, b_s_t)


def _merge_fwd(p, ha, hb, hc, w_pa, w_pb, w_pc, b_gate, tag):
    s = p.shape[0]
    t = min(ROW_TILE, s)
    gb = COL_GATE // D_MODEL

    def body(ga_ref, gb_ref, gc_ref, ha_ref, hb_ref, hc_ref, wa_ref, wb_ref, wc_ref, bg_ref, y_ref, m_ref):
        merged = jnp.zeros((t, D_MODEL), F32)
        for idx, (g_ref, h_ref, w_ref) in enumerate(((ga_ref, ha_ref, wa_ref), (gb_ref, hb_ref, wb_ref), (gc_ref, hc_ref, wc_ref))):
            y = _dot(h_ref[...], w_ref[...])
            sg = _sigmoid(g_ref[...] + bg_ref[:, idx * D_MODEL:(idx + 1) * D_MODEL])
            y_ref[:, idx * D_MODEL:(idx + 1) * D_MODEL] = y
            merged = merged + sg * y
        m_ref[...] = merged.astype(BF16)

    hspec = _row_spec(t, CONV_DIM)
    wspec = _const_spec((CONV_DIM, D_MODEL))
    return pl.pallas_call(
        body, name="merge_fwd" + tag, grid=(s // t,),
        in_specs=[_row_spec(t, D_MODEL, gb), _row_spec(t, D_MODEL, gb + 1), _row_spec(t, D_MODEL, gb + 2),
                  hspec, hspec, hspec, wspec, wspec, wspec, _const_spec((1, 3 * D_MODEL))],
        out_specs=[_row_spec(t, 3 * D_MODEL), _row_spec(t, D_MODEL)],
        out_shape=[jax.ShapeDtypeStruct((s, 3 * D_MODEL), F32), jax.ShapeDtypeStruct((s, D_MODEL), BF16)],
        compiler_params=_params(("parallel",)),
    )(p, p, p, ha, hb, hc, w_pa, w_pb, w_pc, b_gate)


def _matmul_res_ln(a, w, bias, x_res, ln_g, ln_b, *, name):
    s, k = a.shape
    t = min(ROW_TILE, s)

    def body(a_ref, w_ref, bias_ref, x_ref, g_ref, b_ref, r_ref, y_ref, ybf_ref):
        r = DN_ALPHA * x_ref[...] + _dot(a_ref[...], w_ref[...]) + bias_ref[...]
        xhat, _ = _ln_stats(r)
        y = xhat * g_ref[...] + b_ref[...]
        r_ref[...] = r
        y_ref[...] = y
        ybf_ref[...] = y.astype(BF16)

    vec = _const_spec((1, D_MODEL))
    return pl.pallas_call(
        body, name=name, grid=(s // t,),
        in_specs=[_row_spec(t, k), _const_spec((k, D_MODEL)), vec, _row_spec(t, D_MODEL), vec, vec],
        out_specs=[_row_spec(t, D_MODEL)] * 3,
        out_shape=[jax.ShapeDtypeStruct((s, D_MODEL), F32), jax.ShapeDtypeStruct((s, D_MODEL), F32),
                   jax.ShapeDtypeStruct((s, D_MODEL), BF16)],
        compiler_params=_params(("parallel",)),
    )(a, w, bias, x_res, ln_g, ln_b)


def _ff1_fwd(x_bf, w, b, tag):
    s = x_bf.shape[0]
    tm = min(1024, s)
    tn = 1024

    def body(x_ref, w_ref, b_ref, hp_ref, h_ref):
        hp = _dot(x_ref[...], w_ref[...]) + b_ref[...]
        hp_ref[...] = hp
        r = jnp.maximum(hp, 0.0)
        h_ref[...] = (r * r).astype(BF16)

    return pl.pallas_call(
        body, name="ff1_fwd" + tag, grid=(s // tm, D_FF // tn),
        in_specs=[pl.BlockSpec((tm, D_MODEL), lambda i, j: (i, 0)), pl.BlockSpec((D_MODEL, tn), lambda i, j: (0, j)),
                  pl.BlockSpec((1, tn), lambda i, j: (0, j))],
        out_specs=[pl.BlockSpec((tm, tn), lambda i, j: (i, j))] * 2,
        out_shape=[jax.ShapeDtypeStruct((s, D_FF), F32), jax.ShapeDtypeStruct((s, D_FF), BF16)],
        compiler_params=_params(("parallel", "parallel")),
    )(x_bf, w, b)


def _loss_fwd_bwd(y, target):
    s = y.shape[0]
    t = min(ROW_TILE, s)

    def body(y_ref, t_ref, dy_ref, loss_ref):
        @pl.when(pl.program_id(0) == 0)
        def _():
            loss_ref[...] = jnp.zeros_like(loss_ref)

        err = y_ref[...] - t_ref[...]
        dy_ref[...] = err * (1.0 / D_MODEL)
        per_row = jnp.mean(err * err, axis=-1, keepdims=True)
        loss_ref[...] += 0.5 * jnp.sum(per_row, axis=0, keepdims=True)

    return pl.pallas_call(
        body, name="loss_fwd_bwd", grid=(s // t,),
        in_specs=[_row_spec(t, D_MODEL), _row_spec(t, D_MODEL)],
        out_specs=[_row_spec(t, D_MODEL), _const_spec((8, LANES))],
        out_shape=[jax.ShapeDtypeStruct((s, D_MODEL), F32), jax.ShapeDtypeStruct((8, LANES), F32)],
        compiler_params=_params(("arbitrary",)),
    )(y, target)


def _softplus(x):
    return jnp.maximum(x, 0.0) + jnp.log1p(jnp.exp(-jnp.abs(x)))


def _gdn_conv_silu_norm(q_ref, k_ref, v_ref, hq_ref, hk_ref, hv_ref, cw_ref, ext, first):
    t = q_ref.shape[0]
    for n, (r, h) in enumerate(((q_ref, hq_ref), (k_ref, hk_ref), (v_ref, hv_ref))):
        ext[0:GDN_HALO, n * GDN_QK:(n + 1) * GDN_QK] = jnp.where(first, 0.0, h[...])
        ext[GDN_HALO:GDN_HALO + t, n * GDN_QK:(n + 1) * GDN_QK] = r[...]
    pre = jnp.zeros((t, 3 * GDN_QK), F32)
    for j in range(GDN_CONV):
        pre = pre + cw_ref[j:j + 1, :] * ext[pl.ds(GDN_HALO - (GDN_CONV - 1) + j, t), :]
    act = _silu(pre)
    rq, rk = [], []
    for h in range(GDN_HEADS):
        qh = act[:, h * GDN_DK:(h + 1) * GDN_DK]
        kh = act[:, GDN_QK + h * GDN_DK:GDN_QK + (h + 1) * GDN_DK]
        rq.append(lax.rsqrt(jnp.sum(qh * qh, axis=-1, keepdims=True) + RMS_EPS))
        rk.append(lax.rsqrt(jnp.sum(kh * kh, axis=-1, keepdims=True) + RMS_EPS))
    return pre, act, rq, rk


def _gdn_gates(ba, alog_ref, dtb_ref):
    lane = lax.broadcasted_iota(jnp.int32, ba.shape, 1)
    beta = _sigmoid(ba)
    g = -jnp.exp(alog_ref[...]) * _softplus(ba + dtb_ref[...])
    g = jnp.where((lane >= GDN_HEADS) & (lane < 2 * GDN_HEADS), g, 0.0)
    return beta, g


def _chunk_cumsum_matrix(t, upper):
    row = lax.broadcasted_iota(jnp.int32, (t, t), 0)
    col = lax.broadcasted_iota(jnp.int32, (t, t), 1)
    same = (row // GDN_CHUNK) == (col // GDN_CHUNK)
    tri = (col >= row) if upper else (col <= row)
    return jnp.where(same & tri, 1.0, 0.0).astype(F32)


def _each(fn, *lists):
    return [fn(*a) for a in zip(*lists)]


def _gdn_pair_items(qn_s, kn_s, vc_s, beta_s, gam_s, nc):
    items = []
    for ci in range(nc):
        rows = slice(ci * GDN_CHUNK, (ci + 1) * GDN_CHUNK)
        gam_blk = gam_s[rows, :]
        gam_t = gam_blk.T
        beta_blk = beta_s[rows, :]
        for h in range(GDN_HEADS):
            sl = slice(h * GDN_DK, (h + 1) * GDN_DK)
            items.append((qn_s[rows, sl], kn_s[rows, sl], vc_s[rows, sl], beta_blk[:, h:h + 1],
                          gam_blk[:, GDN_HEADS + h:GDN_HEADS + h + 1], gam_t[GDN_HEADS + h:GDN_HEADS + h + 1, :]))
    return items


def _gdn_prep(items):
    c = GDN_CHUNK
    row = lax.broadcasted_iota(jnp.int32, (c, c), 0)
    col = lax.broadcasted_iota(jnp.int32, (c, c), 1)
    causal = row >= col
    strict = row > col
    kbs = [k.astype(BF16) for _, k, _, _, _, _ in items]
    kks = _each(_dot_nt, kbs, kbs)
    qks = _each(_dot_nt, [q.astype(BF16) for q, _, _, _, _, _ in items], kbs)
    out = []
    for (q, k, v, beta_c, gam_c, gam_r), kk, qk in zip(items, kks, qks):
        decay = jnp.where(causal, jnp.exp(jnp.where(causal, gam_c - gam_r, 0.0)), 0.0)
        gm = jnp.exp(gam_c)
        glast = gam_c[c - 1:c, :]
        elast = jnp.exp(glast - gam_c)
        out.append(dict(causal=causal, strict=strict, decay=decay, kk=kk, low=jnp.where(strict, beta_c * kk * decay, 0.0),
                        a_qk=qk * decay, gm=gm, glast_exp=jnp.exp(glast), elast=elast, q=q, k=k, v=v, beta_c=beta_c,
                        r=jnp.concatenate([beta_c * v, beta_c * k * gm], axis=1), qd=q * gm, kd=k * elast))
    return out


def _unit_lower_inverses_minus_identity(lows):
    ps = [-low for low in lows]
    mis = list(ps)
    for _ in range(5):
        ps = _each(_bdot, ps, ps)
        ts = _each(_bdot, mis, ps)
        mis = [mi + p + t for mi, p, t in zip(mis, ps, ts)]
    return mis


def _apply_inverses(mis, rs, dot=_dot):
    return [r + t for r, t in zip(rs, _each(functools.partial(_bdot, dot=dot), mis, rs))]


def _gdn_fwd(p, conv_w, a_log, dt_bias, norm_g, tag):
    s = p.shape[0]
    t = min(GDN_BLOCK, s)
    nc = t // GDN_CHUNK
    nblk = s // t
    qb, kb_, vb, zb = COL_Q // GDN_QK, COL_Q // GDN_QK + 1, COL_Q // GDN_QK + 2, COL_Z // GDN_QK
    scale = GDN_DK ** -0.5

    def body(q_ref, k_ref, v_ref, hq_ref, hk_ref, hv_ref, z_ref, ba_ref, cw_ref, alog_ref, dtb_ref, ng_ref,
             o_ref, hb_ref, st_ref, m_ref, ext, qn_s, kn_s, vc_s, beta_s, gam_s, state):
        i = pl.program_id(0)

        @pl.when(i == 0)
        def _():
            state[...] = jnp.zeros_like(state)

        _, act, rq, rk = _gdn_conv_silu_norm(q_ref, k_ref, v_ref, hq_ref, hk_ref, hv_ref, cw_ref, ext, i == 0)
        for h in range(GDN_HEADS):
            sl = slice(h * GDN_DK, (h + 1) * GDN_DK)
            qn_s[:, sl] = act[:, sl] * (rq[h] * scale)
            kn_s[:, sl] = act[:, GDN_QK + h * GDN_DK:GDN_QK + (h + 1) * GDN_DK] * rk[h]
        vc_s[...] = act[:, 2 * GDN_QK:]
        beta, g = _gdn_gates(ba_ref[...], alog_ref, dtb_ref)
        beta_s[...] = beta
        gam_s[...] = jnp.dot(_chunk_cumsum_matrix(t, False), g, preferred_element_type=F32, precision=lax.Precision.HIGHEST)

        prs = _gdn_prep(_gdn_pair_items(qn_s, kn_s, vc_s, beta_s, gam_s, nc))
        mis = _unit_lower_inverses_minus_identity([pr["low"] for pr in prs])
        xs = _apply_inverses(mis, [pr["r"] for pr in prs])
        heads = range(GDN_HEADS)
        for ci in range(nc):
            rows = slice(ci * GDN_CHUNK, (ci + 1) * GDN_CHUNK)
            pc, xc = prs[ci * GDN_HEADS:(ci + 1) * GDN_HEADS], xs[ci * GDN_HEADS:(ci + 1) * GDN_HEADS]
            m_ref[rows, :] = jnp.concatenate(mis[ci * GDN_HEADS:(ci + 1) * GDN_HEADS], axis=1)
            sts = [state[h * GDN_DK:(h + 1) * GDN_DK, :] for h in heads]
            for h in heads:
                st_ref[(ci * GDN_HEADS + h) * GDN_DK:(ci * GDN_HEADS + h + 1) * GDN_DK, :] = sts[h]
            w_st = _each(_bdot, [x[:, GDN_DK:] for x in xc], sts)
            q_st = _each(_bdot, [pr["qd"] for pr in pc], sts)
            vns = [x[:, :GDN_DK] - ws for x, ws in zip(xc, w_st)]
            a_vn = _each(_bdot, [pr["a_qk"] for pr in pc], vns)
            k_vn = _each(functools.partial(_bdot, dot=_dot_tn), [pr["kd"] for pr in pc], vns)
            for h in heads:
                o_ref[rows, h * GDN_DK:(h + 1) * GDN_DK] = q_st[h] + a_vn[h]
                state[h * GDN_DK:(h + 1) * GDN_DK, :] = sts[h] * pc[h]["glast_exp"] + k_vn[h]

        z = z_ref[...]
        for h in range(GDN_HEADS):
            sl = slice(h * GDN_DK, (h + 1) * GDN_DK)
            o = o_ref[:, sl]
            on = o * lax.rsqrt(jnp.mean(o * o, axis=-1, keepdims=True) + RMS_EPS)
            hb_ref[:, sl] = (on * ng_ref[...] * _silu(z[:, sl])).astype(BF16)

    col = lambda cb: pl.BlockSpec((t, GDN_QK), lambda i, cb=cb: (i, cb))
    halo = lambda cb: _prev_halo_spec(t, GDN_HALO, GDN_QK, cb)
    vec = _const_spec((1, LANES))
    return pl.pallas_call(
        body, name="gdn_fwd" + tag, grid=(nblk,),
        in_specs=[col(qb), col(kb_), col(vb), halo(qb), halo(kb_), halo(vb), col(zb),
                  _row_spec(t, LANES, COL_BA // LANES), _const_spec((GDN_CONV, 3 * GDN_QK)), vec, vec, vec],
        out_specs=[_row_spec(t, GDN_QK), _row_spec(t, GDN_QK),
                   pl.BlockSpec((nc * GDN_HEADS * GDN_DK, GDN_DK), lambda i: (i, 0)),
                   _row_spec(t, GDN_HEADS * GDN_CHUNK)],
        out_shape=[jax.ShapeDtypeStruct((s, GDN_QK), F32), jax.ShapeDtypeStruct((s, GDN_QK), BF16),
                   jax.ShapeDtypeStruct((s // GDN_CHUNK * GDN_HEADS * GDN_DK, GDN_DK), F32),
                   jax.ShapeDtypeStruct((s, GDN_HEADS * GDN_CHUNK), F32)],
        scratch_shapes=[pltpu.VMEM((GDN_HALO + t, 3 * GDN_QK), F32), pltpu.VMEM((t, GDN_QK), F32),
                        pltpu.VMEM((t, GDN_QK), F32), pltpu.VMEM((t, GDN_QK), F32),
                        pltpu.VMEM((t, LANES), F32), pltpu.VMEM((t, LANES), F32),
                        pltpu.VMEM((GDN_HEADS * GDN_DK, GDN_DK), F32)],
        compiler_params=_params(("arbitrary",)),
    )(p, p, p, p, p, p, p, p, conv_w, a_log, dt_bias, norm_g)


def _lane_place(col, lane_idx, shape):
    lane = lax.broadcasted_iota(jnp.int32, shape, 1)
    return jnp.where(lane == lane_idx, col, 0.0)


def _gdn_bwd(dhb, p, o, states, minv, conv_w, a_log, dt_bias, norm_g, tag):
    s = p.shape[0]
    t = min(GDN_BLOCK, s)
    nc = t // GDN_CHUNK
    nblk = s // t
    qb, kb_, vb, zb = COL_Q // GDN_QK, COL_Q // GDN_QK + 1, COL_Q // GDN_QK + 2, COL_Z // GDN_QK
    scale = GDN_DK ** -0.5
    c = GDN_CHUNK

    def body(dhb_ref, q_ref, k_ref, v_ref, hq_ref, hk_ref, hv_ref, z_ref, ba_ref, o_ref, st_ref, m_ref,
             cw_ref, alog_ref, dtb_ref, ng_ref,
             dpre_ref, dz_ref, dba_ref, dng_ref, dalog_ref, ddtb_ref,
             ext, qn_s, kn_s, vc_s, beta_s, gam_s, do_s, dqn_s, dkn_s, dvc_s, dgam_s, dbeta_s, dstate):
        i = pl.program_id(0)

        @pl.when(i == 0)
        def _():
            dstate[...] = jnp.zeros_like(dstate)
            dng_ref[...] = jnp.zeros_like(dng_ref)
            dalog_ref[...] = jnp.zeros_like(dalog_ref)
            ddtb_ref[...] = jnp.zeros_like(ddtb_ref)

        pre, act, rq, rk = _gdn_conv_silu_norm(q_ref, k_ref, v_ref, hq_ref, hk_ref, hv_ref, cw_ref, ext, i == nblk - 1)
        for h in range(GDN_HEADS):
            sl = slice(h * GDN_DK, (h + 1) * GDN_DK)
            qn_s[:, sl] = act[:, sl] * (rq[h] * scale)
            kn_s[:, sl] = act[:, GDN_QK + h * GDN_DK:GDN_QK + (h + 1) * GDN_DK] * rk[h]
        vc_s[...] = act[:, 2 * GDN_QK:]
        ba = ba_ref[...]
        beta, g = _gdn_gates(ba, alog_ref, dtb_ref)
        beta_s[...] = beta
        gam_s[...] = jnp.dot(_chunk_cumsum_matrix(t, False), g, preferred_element_type=F32, precision=lax.Precision.HIGHEST)

        z = z_ref[...]
        dhb = dhb_ref[...]
        dng = jnp.zeros((1, GDN_DK), F32)
        for h in range(GDN_HEADS):
            sl = slice(h * GDN_DK, (h + 1) * GDN_DK)
            oh = o_ref[:, sl]
            r = lax.rsqrt(jnp.mean(oh * oh, axis=-1, keepdims=True) + RMS_EPS)
            on = oh * r
            sz = _silu(z[:, sl])
            dyh = dhb[:, sl]
            dng = dng + _colsum(dyh * on * sz)
            dz_ref[:, sl] = (dyh * on * ng_ref[...] * _dsilu(z[:, sl])).astype(BF16)
            don = dyh * ng_ref[...] * sz
            do_s[:, sl] = r * (don - on * jnp.mean(don * on, axis=-1, keepdims=True))
        dng_ref[...] += dng

        heads = range(GDN_HEADS)
        npairs = nc * GDN_HEADS
        tn = functools.partial(_bdot, dot=_dot_tn)
        nt = functools.partial(_bdot, dot=_dot_nt)
        rsum = lambda a: jnp.sum(a, axis=-1, keepdims=True)
        left = lambda a: a[:, :GDN_DK]
        right = lambda a: a[:, GDN_DK:]

        prs = _gdn_prep(_gdn_pair_items(qn_s, kn_s, vc_s, beta_s, gam_s, nc))
        mis = [m_ref[(n // GDN_HEADS) * c:(n // GDN_HEADS + 1) * c, (n % GDN_HEADS) * c:(n % GDN_HEADS + 1) * c] for n in range(npairs)]
        xs = _apply_inverses(mis, [pr["r"] for pr in prs])
        sts = [st_ref[n * GDN_DK:(n + 1) * GDN_DK, :] for n in range(npairs)]
        dos = [do_s[(n // GDN_HEADS) * c:(n // GDN_HEADS + 1) * c, (n % GDN_HEADS) * GDN_DK:(n % GDN_HEADS + 1) * GDN_DK] for n in range(npairs)]
        w_st = _each(_bdot, [right(x) for x in xs], sts)
        vns = [left(x) - ws for x, ws in zip(xs, w_st)]
        at_do = _each(tn, [pr["a_qk"] for pr in prs], dos)
        dqds = _each(nt, dos, sts)
        d_as = [jnp.where(pr["causal"], a, 0.0) for pr, a in zip(prs, _each(nt, dos, vns))]
        qt_do = _each(tn, [pr["qd"] for pr in prs], dos)

        dvns, dkds, ds_st = [None] * npairs, [None] * npairs, [None] * npairs
        for ci in reversed(range(nc)):
            ids = [ci * GDN_HEADS + h for h in heads]
            dss = [dstate[h * GDN_DK:(h + 1) * GDN_DK, :] for h in heads]
            kd_ds = _each(_bdot, [prs[n]["kd"] for n in ids], dss)
            vn_ds = _each(nt, [vns[n] for n in ids], dss)
            for h, n in enumerate(ids):
                dvns[n] = kd_ds[h] + at_do[n]
                dkds[n] = vn_ds[h]
                ds_st[n] = jnp.sum(rsum(dss[h] * sts[n]), axis=0, keepdims=True)
            wt_dvn = _each(tn, [right(xs[n]) for n in ids], [dvns[n] for n in ids])
            for h, n in enumerate(ids):
                dstate[h * GDN_DK:(h + 1) * GDN_DK, :] = dss[h] * prs[n]["glast_exp"] + qt_do[n] - wt_dvn[h]

        dws = [-a for a in _each(nt, dvns, sts)]
        d_rs = _apply_inverses(mis, [jnp.concatenate([dvn, dw], axis=1) for dvn, dw in zip(dvns, dws)], _dot_tn)
        d_ls = [jnp.where(pr["strict"], -a, 0.0) for pr, a in zip(prs, _each(nt, d_rs, xs))]
        d_l_kds = [d_l * pr["kk"] * pr["decay"] for d_l, pr in zip(d_ls, prs)]
        dkks = [d_l * pr["beta_c"] * pr["decay"] for d_l, pr in zip(d_ls, prs)]
        dqks = [d_a * pr["decay"] for d_a, pr in zip(d_as, prs)]
        ks, qs = [pr["k"] for pr in prs], [pr["q"] for pr in prs]
        dk1, dk2, dk3 = _each(_bdot, dkks, ks), _each(tn, dkks, ks), _each(tn, dqks, qs)
        dq1 = _each(_bdot, dqks, ks)
        rowi = lax.broadcasted_iota(jnp.int32, (c, 1), 0)
        for ci in range(nc):
            rows = slice(ci * c, (ci + 1) * c)
            dgam_blk = jnp.zeros((c, LANES), F32)
            dbeta_blk = jnp.zeros((c, LANES), F32)
            for h in heads:
                n = ci * GDN_HEADS + h
                sl = slice(h * GDN_DK, (h + 1) * GDN_DK)
                pr, d_r = prs[n], d_rs[n]
                d_ru, d_rw = left(d_r), right(d_r)
                gmat = pr["beta_c"] * d_l_kds[n] + d_as[n] * pr["a_qk"]
                dkd_kd = rsum(dkds[n] * pr["kd"])
                dgam = rsum(gmat) - rsum(gmat.T) + rsum(d_rw * right(pr["r"])) + rsum(dqds[n] * pr["qd"]) - dkd_kd
                dglast = jnp.sum(dkd_kd, axis=0, keepdims=True) + ds_st[n] * pr["glast_exp"]
                dgam = dgam + jnp.where(rowi == c - 1, dglast, 0.0)
                dbeta = rsum(d_l_kds[n]) + rsum(d_ru * pr["v"]) + rsum(d_rw * pr["k"]) * pr["gm"]
                dvc_s[rows, sl] = pr["beta_c"] * d_ru
                dkn_s[rows, sl] = dk1[n] + dk2[n] + dk3[n] + d_rw * (pr["beta_c"] * pr["gm"]) + dkds[n] * pr["elast"]
                dqn_s[rows, sl] = dq1[n] + dqds[n] * pr["gm"]
                dgam_blk = dgam_blk + _lane_place(dgam, GDN_HEADS + h, (c, LANES))
                dbeta_blk = dbeta_blk + _lane_place(dbeta, h, (c, LANES))
            dgam_s[rows, :] = dgam_blk
            dbeta_s[rows, :] = dbeta_blk

        dg = jnp.dot(_chunk_cumsum_matrix(t, True), dgam_s[...], preferred_element_type=F32, precision=lax.Precision.HIGHEST)
        lane = lax.broadcasted_iota(jnp.int32, (t, LANES), 1)
        g_lanes = (lane >= GDN_HEADS) & (lane < 2 * GDN_HEADS)
        da_logit = jnp.where(g_lanes, dg * (-jnp.exp(alog_ref[...])) * _sigmoid(ba + dtb_ref[...]), 0.0)
        db_logit = jnp.where(lane < GDN_HEADS, dbeta_s[...] * beta * (1.0 - beta), 0.0)
        dba_ref[...] = (da_logit + db_logit).astype(BF16)
        dalog_ref[...] += _colsum(dg * g)
        ddtb_ref[...] += _colsum(da_logit)

        dact = []
        for n, (dn_s, rr, sc) in enumerate(((dqn_s, rq, scale), (dkn_s, rk, 1.0))):
            for h in range(GDN_HEADS):
                sl = slice(h * GDN_DK, (h + 1) * GDN_DK)
                y = act[:, n * GDN_QK + h * GDN_DK:n * GDN_QK + (h + 1) * GDN_DK] * rr[h]
                dy = dn_s[:, sl] * sc
                dact.append(rr[h] * (dy - y * jnp.sum(dy * y, axis=-1, keepdims=True)))
        dact.append(dvc_s[...])
        dpre_ref[...] = jnp.concatenate(dact, axis=1) * _dsilu(pre)

    rb = lambda i: nblk - 1 - i
    per = t // GDN_HALO
    col = lambda cb, wd=GDN_QK: pl.BlockSpec((t, wd), lambda i, cb=cb: (rb(i), cb))
    halo = lambda cb: pl.BlockSpec((GDN_HALO, GDN_QK), lambda i, cb=cb: (jnp.maximum(rb(i) * per - 1, 0), cb))
    vec = _const_spec((1, LANES))
    return pl.pallas_call(
        body, name="gdn_bwd" + tag, grid=(nblk,),
        in_specs=[col(0), col(qb), col(kb_), col(vb), halo(qb), halo(kb_), halo(vb), col(zb),
                  pl.BlockSpec((t, LANES), lambda i: (rb(i), COL_BA // LANES)), col(0),
                  pl.BlockSpec((nc * GDN_HEADS * GDN_DK, GDN_DK), lambda i: (rb(i), 0)),
                  pl.BlockSpec((t, GDN_HEADS * c), lambda i: (rb(i), 0)),
                  _const_spec((GDN_CONV, 3 * GDN_QK)), vec, vec, vec],
        out_specs=[pl.BlockSpec((t, 3 * GDN_QK), lambda i: (rb(i), 0)), col(0), pl.BlockSpec((t, LANES), lambda i: (rb(i), 0)),
                   vec, vec, vec],
        out_shape=[jax.ShapeDtypeStruct((s, 3 * GDN_QK), F32), jax.ShapeDtypeStruct((s, GDN_QK), BF16),
                   jax.ShapeDtypeStruct((s, LANES), BF16),
                   jax.ShapeDtypeStruct((1, LANES), F32), jax.ShapeDtypeStruct((1, LANES), F32), jax.ShapeDtypeStruct((1, LANES), F32)],
        scratch_shapes=[pltpu.VMEM((GDN_HALO + t, 3 * GDN_QK), F32)] + [pltpu.VMEM((t, GDN_QK), F32)] * 3
                       + [pltpu.VMEM((t, LANES), F32)] * 2 + [pltpu.VMEM((t, GDN_QK), F32)] * 4
                       + [pltpu.VMEM((t, LANES), F32)] * 2 + [pltpu.VMEM((GDN_HEADS * GDN_DK, GDN_DK), F32)],
        compiler_params=_params(("arbitrary",)),
    )(dhb, p, p, p, p, p, p, p, p, o, states, minv, conv_w, a_log, dt_bias, norm_g)


def _zero_at_first_step(*refs):
    @pl.when(pl.program_id(0) == 0)
    def _():
        for r in refs:
            r[...] = jnp.zeros_like(r)


def _ln_bwd_call(dy, x_in, g, *, name):
    s, d = dy.shape
    t = min(ROW_TILE, s)

    def body(dy_ref, x_ref, g_ref, dx_ref, dxbf_ref, dg_ref, db_ref, ds_ref):
        _zero_at_first_step(dg_ref, db_ref, ds_ref)
        dy = dy_ref[...]
        xhat, rstd = _ln_stats(x_ref[...])
        dx = _ln_bwd(dy, xhat, rstd, g_ref[...])
        dx_ref[...] = dx
        dxbf_ref[...] = dx.astype(BF16)
        dg_ref[...] += _colsum(dy * xhat)
        db_ref[...] += _colsum(dy)
        ds_ref[...] += _colsum(dx)

    vec = _const_spec((1, d))
    return pl.pallas_call(
        body, name=name, grid=(s // t,),
        in_specs=[_row_spec(t, d), _row_spec(t, d), vec],
        out_specs=[_row_spec(t, d), _row_spec(t, d), vec, vec, vec],
        out_shape=[jax.ShapeDtypeStruct((s, d), F32), jax.ShapeDtypeStruct((s, d), BF16)] + [jax.ShapeDtypeStruct((1, d), F32)] * 3,
        compiler_params=_params(("arbitrary",)),
    )(dy, x_in, g)


def _ff2_bwd(dr2_bf, w_ff2_t, hpre, tag):
    s = dr2_bf.shape[0]
    tm = min(1024, s)
    tn = 1024

    def body(d_ref, w_ref, hp_ref, o_ref, db_ref):
        @pl.when(pl.program_id(1) == 0)
        def _():
            db_ref[...] = jnp.zeros_like(db_ref)

        dh = _dot(d_ref[...], w_ref[...]) * (2.0 * jnp.maximum(hp_ref[...], 0.0))
        o_ref[...] = dh.astype(BF16)
        db_ref[...] += _colsum(dh)

    return pl.pallas_call(
        body, name="ff2_bwd" + tag, grid=(D_FF // tn, s // tm),
        in_specs=[pl.BlockSpec((tm, D_MODEL), lambda j, i: (i, 0)), pl.BlockSpec((D_MODEL, tn), lambda j, i: (0, j)),
                  pl.BlockSpec((tm, tn), lambda j, i: (i, j))],
        out_specs=[pl.BlockSpec((tm, tn), lambda j, i: (i, j)), pl.BlockSpec((1, tn), lambda j, i: (0, j))],
        out_shape=[jax.ShapeDtypeStruct((s, D_FF), BF16), jax.ShapeDtypeStruct((1, D_FF), F32)],
        compiler_params=_params(("parallel", "arbitrary")),
    )(dr2_bf, w_ff2_t, hpre)


def _ff1_bwd_ln(dhpre_bf, w_ff1_t, dr2, r1, ln1_g, tag):
    s = dr2.shape[0]
    t = min(ROW_TILE, s)

    def body(dh_ref, w_ref, dr2_ref, r1_ref, g_ref, dr_ref, drbf_ref, dg_ref, db_ref):
        _zero_at_first_step(dg_ref, db_ref)
        dx1 = DN_ALPHA * dr2_ref[...] + _dot(dh_ref[...], w_ref[...])
        xhat, rstd = _ln_stats(r1_ref[...])
        dr = _ln_bwd(dx1, xhat, rstd, g_ref[...])
        dr_ref[...] = dr
        drbf_ref[...] = dr.astype(BF16)
        dg_ref[...] += _colsum(dx1 * xhat)
        db_ref[...] += _colsum(dx1)

    vec = _const_spec((1, D_MODEL))
    return pl.pallas_call(
        body, name="ff1_bwd_ln" + tag, grid=(s // t,),
        in_specs=[_row_spec(t, D_FF), _const_spec((D_FF, D_MODEL)), _row_spec(t, D_MODEL), _row_spec(t, D_MODEL), vec],
        out_specs=[_row_spec(t, D_MODEL), _row_spec(t, D_MODEL), vec, vec],
        out_shape=[jax.ShapeDtypeStruct((s, D_MODEL), F32), jax.ShapeDtypeStruct((s, D_MODEL), BF16),
                   jax.ShapeDtypeStruct((1, D_MODEL), F32), jax.ShapeDtypeStruct((1, D_MODEL), F32)],
        compiler_params=_params(("arbitrary",)),
    )(dhpre_bf, w_ff1_t, dr2, r1, ln1_g)


def _merge_bwd(dr1_bf, w_o_t, y3, p, b_gate, w_pa_t, w_pb_t, w_pc_t, tag):
    s = p.shape[0]
    t = min(ROW_TILE, s)
    gb = COL_GATE // D_MODEL

    def body(dr_ref, wo_ref, y_ref, ga_ref, gb_ref, gc_ref, bg_ref, wa_ref, wb_ref, wc_ref,
             dgate_ref, dy_ref, dha_ref, dhb_ref, dhc_ref, dbg_ref):
        _zero_at_first_step(dbg_ref)
        dm = _dot(dr_ref[...], wo_ref[...])
        for idx, (g_ref, w_ref, dh_ref) in enumerate(((ga_ref, wa_ref, dha_ref), (gb_ref, wb_ref, dhb_ref), (gc_ref, wc_ref, dhc_ref))):
            sl = slice(idx * D_MODEL, (idx + 1) * D_MODEL)
            sg = _sigmoid(g_ref[...] + bg_ref[:, sl])
            dgate = dm * y_ref[:, sl] * sg * (1.0 - sg)
            dgate_ref[:, sl] = dgate.astype(BF16)
            dbg_ref[:, sl] += _colsum(dgate)
            dy = (dm * sg).astype(BF16)
            dy_ref[:, sl] = dy
            dh_ref[...] = _dot(dy, w_ref[...])

    hspec = _row_spec(t, CONV_DIM)
    wspec = _const_spec((D_MODEL, CONV_DIM))
    return pl.pallas_call(
        body, name="merge_bwd" + tag, grid=(s // t,),
        in_specs=[_row_spec(t, D_MODEL), _const_spec((D_MODEL, D_MODEL)), _row_spec(t, 3 * D_MODEL),
                  _row_spec(t, D_MODEL, gb), _row_spec(t, D_MODEL, gb + 1), _row_spec(t, D_MODEL, gb + 2),
                  _const_spec((1, 3 * D_MODEL)), wspec, wspec, wspec],
        out_specs=[_row_spec(t, 3 * D_MODEL), _row_spec(t, 3 * D_MODEL), hspec, hspec, hspec, _const_spec((1, 3 * D_MODEL))],
        out_shape=[jax.ShapeDtypeStruct((s, 3 * D_MODEL), BF16), jax.ShapeDtypeStruct((s, 3 * D_MODEL), BF16)]
                  + [jax.ShapeDtypeStruct((s, CONV_DIM), F32)] * 3 + [jax.ShapeDtypeStruct((1, 3 * D_MODEL), F32)],
        compiler_params=_params(("arbitrary",)),
    )(dr1_bf, w_o_t, y3, p, p, p, b_gate, w_pa_t, w_pb_t, w_pc_t)


def _conv_a_bwd_pre(dha, c, ln_g, ln_b, tag):
    s = c.shape[0]
    t = min(ROW_TILE, s)

    def body(dh_ref, c_ref, g_ref, b_ref, dc_ref, dg_ref, db_ref, ds_ref):
        _zero_at_first_step(dg_ref, db_ref, ds_ref)
        xhat, rstd = _ln_stats(c_ref[...])
        n = xhat * g_ref[...] + b_ref[...]
        dn = dh_ref[...] * _dsilu(n)
        dc = _ln_bwd(dn, xhat, rstd, g_ref[...])
        dc_ref[...] = dc
        dg_ref[...] += _colsum(dn * xhat)
        db_ref[...] += _colsum(dn)
        ds_ref[...] += _colsum(dc)

    vec = _const_spec((1, CONV_DIM))
    return pl.pallas_call(
        body, name="conv_a_bwd_pre" + tag, grid=(s // t,),
        in_specs=[_row_spec(t, CONV_DIM), _row_spec(t, CONV_DIM), vec, vec],
        out_specs=[_row_spec(t, CONV_DIM), vec, vec, vec],
        out_shape=[jax.ShapeDtypeStruct((s, CONV_DIM), F32)] + [jax.ShapeDtypeStruct((1, CONV_DIM), F32)] * 3,
        compiler_params=_params(("arbitrary",)),
    )(dha, c, ln_g, ln_b)


def _dwconv_bwd(dy, x, w, *, width, halo, x_col_block, glu_p, name):
    s, ctot = dy.shape
    ct = CONV_DIM
    t = min(ROW_TILE, s)
    nblk = s // t
    glu = glu_p is not None

    def body(*refs):
        if glu:
            dy_ref, dyh_ref, x_ref, xh_ref, w_ref, a_ref, dx_ref, dw_ref, dye, xe = refs
        else:
            dy_ref, dyh_ref, x_ref, xh_ref, w_ref, dx_ref, dw_ref, dye, xe = refs
        i = pl.program_id(1)

        @pl.when(i == 0)
        def _():
            dw_ref[...] = jnp.zeros_like(dw_ref)

        dyv = dy_ref[...]
        dye[0:t, :] = dyv
        dye[t:t + halo, :] = jnp.where(i == nblk - 1, 0.0, dyh_ref[...])
        xe[0:halo, :] = jnp.where(i == 0, 0.0, xh_ref[...])
        xe[halo:halo + t, :] = x_ref[...]
        dx = jnp.zeros((t, ct), F32)
        for j in range(width):
            dx = dx + w_ref[j:j + 1, :] * dye[pl.ds(width - 1 - j, t), :]
            dw_ref[j:j + 1, :] += _colsum(dyv * xe[pl.ds(halo - (width - 1) + j, t), :])
        if glu:
            a = a_ref[...]
            a1 = a[:, :ct]
            sg = _sigmoid(a[:, ct:])
            dx_ref[:, :ct] = (dx * sg).astype(BF16)
            dx_ref[:, ct:] = (dx * a1 * sg * (1.0 - sg)).astype(BF16)
        else:
            dx_ref[...] = dx.astype(BF16)

    per = t // halo
    in_specs = [pl.BlockSpec((t, ct), lambda cb, i: (i, cb)),
                pl.BlockSpec((halo, ct), lambda cb, i: (jnp.minimum((i + 1) * per, nblk * per - 1), cb)),
                pl.BlockSpec((t, ct), lambda cb, i: (i, cb + x_col_block)),
                pl.BlockSpec((halo, ct), lambda cb, i: (jnp.maximum(i * per - 1, 0), cb + x_col_block)),
                pl.BlockSpec((width, ct), lambda cb, i: (0, cb))]
    args = [dy, dy, x, x, w]
    out_cols = ctot
    if glu:
        in_specs.append(pl.BlockSpec((t, 2 * ct), lambda cb, i: (i, COL_A // (2 * ct))))
        args.append(glu_p)
        out_cols = 2 * ct
    ocol = 2 * ct if glu else ct
    return pl.pallas_call(
        body, name=name, grid=(ctot // ct, nblk), in_specs=in_specs,
        out_specs=[pl.BlockSpec((t, ocol), lambda cb, i: (i, cb)), pl.BlockSpec((width, ct), lambda cb, i: (0, cb))],
        out_shape=[jax.ShapeDtypeStruct((s, out_cols), BF16), jax.ShapeDtypeStruct((width, ctot), F32)],
        scratch_shapes=[pltpu.VMEM((t + halo, ct), F32), pltpu.VMEM((halo + t, ct), F32)],
        compiler_params=_params(("parallel", "arbitrary")),
    )(*args)


def _sgu_bwd(dhc, p, ln_g, ln_b, w_s, b_s_t, tag):
    s = p.shape[0]
    t = min(ROW_TILE, s)
    cs = SGU_CHUNK

    def body(dh_ref, uv_ref, g_ref, b_ref, ws_ref, bst_ref, duv_ref, dg_ref, db_ref, dws_ref, dbs_ref):
        _zero_at_first_step(dg_ref, db_ref, dws_ref, dbs_ref)
        uv = uv_ref[...]
        u_raw, v_raw = uv[:, :SGU_DIM], uv[:, SGU_DIM:]
        u = _gelu(u_raw)
        xhat, rstd = _ln_stats(_gelu(v_raw))
        vn = xhat * g_ref[...] + b_ref[...]
        mixed = _sgu_mix(vn, ws_ref, bst_ref, t)
        dh = dh_ref[...]
        duv_ref[:, :SGU_DIM] = (dh * mixed * _dgelu(u_raw)).astype(BF16)
        dmix = dh * u
        row = lax.broadcasted_iota(jnp.int32, (cs, cs), 0)
        col = lax.broadcasted_iota(jnp.int32, (cs, cs), 1)
        dbs = jnp.zeros((cs, LANES), F32)
        chunks = []
        for g in range(SGU_GROUPS):
            wg = jnp.where(row >= col, ws_ref[g], 0.0).astype(BF16)
            dwg = jnp.zeros((cs, cs), F32)
            parts = []
            for ci in range(t // cs):
                rs = slice(ci * cs, (ci + 1) * cs)
                cl = slice(g * SGU_GROUP_DIM, (g + 1) * SGU_GROUP_DIM)
                dm = dmix[rs, cl]
                dmb = dm.astype(BF16)
                dwg = dwg + _dot_nt(dmb, vn[rs, cl].astype(BF16))
                dbs = dbs + _lane_place(jnp.sum(dm, axis=-1, keepdims=True), g, (cs, LANES))
                parts.append(_dot_tn(wg, dmb))
            dws_ref[g] += jnp.where(row >= col, dwg, 0.0)
            chunks.append(jnp.concatenate(parts, axis=0))
        dbs_ref[...] += dbs
        dvn = jnp.concatenate(chunks, axis=1)
        dvv = _ln_bwd(dvn, xhat, rstd, g_ref[...])
        duv_ref[:, SGU_DIM:] = (dvv * _dgelu(v_raw)).astype(BF16)
        dg_ref[...] += _colsum(dvn * xhat)
        db_ref[...] += _colsum(dvn)

    vec = _const_spec((1, SGU_DIM))
    wss = _const_spec((SGU_GROUPS, cs, cs))
    return pl.pallas_call(
        body, name="sgu_bwd" + tag, grid=(s // t,),
        in_specs=[_row_spec(t, SGU_DIM), _row_spec(t, 2 * SGU_DIM, COL_UV // (2 * SGU_DIM)), vec, vec, wss, _const_spec((cs, LANES))],
        out_specs=[_row_spec(t, 2 * SGU_DIM), vec, vec, wss, _const_spec((cs, LANES))],
        out_shape=[jax.ShapeDtypeStruct((s, 2 * SGU_DIM), BF16), jax.ShapeDtypeStruct((1, SGU_DIM), F32),
                   jax.ShapeDtypeStruct((1, SGU_DIM), F32), jax.ShapeDtypeStruct((SGU_GROUPS, cs, cs), F32),
                   jax.ShapeDtypeStruct((cs, LANES), F32)],
        compiler_params=_params(("arbitrary",)),
    )(dhc, p, ln_g, ln_b, w_s, b_s_t)


def _reorder_proj_cols(w):
    pad = jnp.zeros(w.shape[:-1] + (P_COLS - PROJ_COLS,), w.dtype)
    return jnp.concatenate([w[..., :3072], w[..., 3080:PROJ_COLS], w[..., 3072:3080], pad], axis=-1)


def _restore_proj_cols(g):
    return jnp.concatenate([g[..., :3072], g[..., COL_BA:COL_BA + 8], g[..., 3072:COL_BA]], axis=-1)


def _pad_lanes(v, offset):
    return jnp.pad(v, (offset, LANES - offset - v.shape[0]))[None, :]


def _layer_weights(l, full, rep):
    row = lambda v: v[l][None, :]
    w_all = _reorder_proj_cols(full["w_in"][l])
    w = dict(
        w_all=w_all, w_all_t=w_all.T,
        conv_w=full["conv_dw_w"][l], conv_b=row(rep["conv_dw_b"]), conv_ln_g=row(rep["conv_ln_g"]), conv_ln_b=row(rep["conv_ln_b"]),
        w_pa=full["w_pa"][l], w_pb=full["w_pb"][l], w_pc=full["w_pc"][l],
        w_pa_t=full["w_pa"][l].T, w_pb_t=full["w_pb"][l].T, w_pc_t=full["w_pc"][l].T,
        gdn_cw=jnp.concatenate([full["gdn_conv_q"][l], full["gdn_conv_k"][l], full["gdn_conv_v"][l]], axis=-1),
        a_log=_pad_lanes(rep["gdn_a_log"][l], GDN_HEADS), dt_bias=_pad_lanes(rep["gdn_dt_bias"][l], GDN_HEADS),
        norm_g=row(rep["gdn_norm_g"]),
        sgu_ln_g=row(rep["sgu_ln_g"]), sgu_ln_b=row(rep["sgu_ln_b"]), sgu_w_s=rep["sgu_w_s"][l],
        sgu_b_s_t=jnp.pad(rep["sgu_b_s"][l].T, ((0, 0), (0, LANES - SGU_GROUPS))),
        b_gate=row(rep["b_gate"]),
        w_o=full["w_o"][l], w_o_t=full["w_o"][l].T, ln1_g=row(rep["ln1_g"]), ln1_b=row(rep["ln1_b"]),
        w_ff1=full["w_ff1"][l], w_ff1_t=full["w_ff1"][l].T, b_ff1=row(rep["b_ff1"]),
        w_ff2=full["w_ff2"][l], w_ff2_t=full["w_ff2"][l].T, b_ff2=row(rep["b_ff2"]),
        ln2_g=row(rep["ln2_g"]), ln2_b=row(rep["ln2_b"]),
    )
    return w


def _layer_fwd(x, x_bf, w, tag):
    s = x.shape[0]
    p = _matmul(x_bf, w["w_all"], name="proj_fwd" + tag, tm=min(ROW_TILE, s), tn=P_TILE, tk=D_MODEL)
    h_glu, c, ha = _conv_a_fwd(p, w["conv_w"], w["conv_b"], w["conv_ln_g"], w["conv_ln_b"], tag)
    o, hb, states, minv = _gdn_fwd(p, w["gdn_cw"], w["a_log"], w["dt_bias"], w["norm_g"], tag)
    hc = _sgu_fwd(p, w["sgu_ln_g"], w["sgu_ln_b"], w["sgu_w_s"], w["sgu_b_s_t"], tag)
    y3, merged = _merge_fwd(p, ha, hb, hc, w["w_pa"], w["w_pb"], w["w_pc"], w["b_gate"], tag)
    r1, x1, x1_bf = _matmul_res_ln(merged, w["w_o"], jnp.zeros((1, D_MODEL), F32), x, w["ln1_g"], w["ln1_b"], name="o_res_ln" + tag)
    hpre, h_bf = _ff1_fwd(x1_bf, w["w_ff1"], w["b_ff1"], tag)
    r2, x2, x2_bf = _matmul_res_ln(h_bf, w["w_ff2"], w["b_ff2"], x1, w["ln2_g"], w["ln2_b"], name="ff2_res_ln" + tag)
    saved = dict(x=x, x_bf=x_bf, p=p, h_glu=h_glu, c=c, ha=ha, o=o, hb=hb, states=states, minv=minv, hc=hc, y3=y3,
                 merged=merged, r1=r1, x1=x1, x1_bf=x1_bf, hpre=hpre, h_bf=h_bf, r2=r2)
    return x2, x2_bf, saved


def _layer_bwd(dx2, w, sv, tag):
    s = dx2.shape[0]
    ts = min(1024, s)
    p = sv["p"]
    dr2, dr2_bf, d_ln2_g, d_ln2_b, d_b_ff2 = _ln_bwd_call(dx2, sv["r2"], w["ln2_g"], name="ln2_bwd" + tag)
    dhpre_bf, d_b_ff1 = _ff2_bwd(dr2_bf, w["w_ff2_t"], sv["hpre"], tag)
    d_w_ff2 = _matmul_tn(sv["h_bf"], dr2_bf, name="dw_ff2" + tag, ka=D_FF, tka=1024, tn=1024, ts=ts)
    d_w_ff1 = _matmul_tn(sv["x1_bf"], dhpre_bf, name="dw_ff1" + tag, ka=D_MODEL, tka=1024, tn=1024, ts=ts)
    dr1, dr1_bf, d_ln1_g, d_ln1_b = _ff1_bwd_ln(dhpre_bf, w["w_ff1_t"], dr2, sv["r1"], w["ln1_g"], tag)
    d_w_o = _matmul_tn(sv["merged"], dr1_bf, name="dw_o" + tag, ka=D_MODEL, tka=1024, tn=1024, ts=ts)
    dgate_bf, dy3_bf, dha, dhb, dhc, d_b_gate = _merge_bwd(dr1_bf, w["w_o_t"], sv["y3"], p, w["b_gate"],
                                                          w["w_pa_t"], w["w_pb_t"], w["w_pc_t"], tag)
    d_w_p = [_matmul_tn(h, dy3_bf, name=f"dw_p{n}" + tag, ka=CONV_DIM, tka=CONV_DIM, tn=1024, ts=ts, n=D_MODEL, b_col_block=n)
             for n, h in enumerate((sv["ha"], sv["hb"], sv["hc"]))]
    dc, d_conv_ln_g, d_conv_ln_b, d_conv_b = _conv_a_bwd_pre(dha, sv["c"], w["conv_ln_g"], w["conv_ln_b"], tag)
    da_bf, d_conv_w = _dwconv_bwd(dc, sv["h_glu"], w["conv_w"], width=CONV_WIDTH, halo=CONV_HALO, x_col_block=0,
                                  glu_p=p, name="conv_a_bwd" + tag)
    duv_bf, d_sgu_ln_g, d_sgu_ln_b, d_sgu_w_s, d_sgu_b_s_t = _sgu_bwd(dhc, p, w["sgu_ln_g"], w["sgu_ln_b"], w["sgu_w_s"], w["sgu_b_s_t"], tag)
    dpre, dz_bf, dba_bf, d_norm_g, d_a_log, d_dt_bias = _gdn_bwd(dhb, p, sv["o"], sv["states"], sv["minv"], w["gdn_cw"],
                                                                 w["a_log"], w["dt_bias"], w["norm_g"], tag)
    dqkv_bf, d_gdn_cw = _dwconv_bwd(dpre, p, w["gdn_cw"], width=GDN_CONV, halo=GDN_HALO, x_col_block=COL_Q // CONV_DIM,
                                    glu_p=None, name="gdn_conv_bwd" + tag)
    dp_bf = jnp.concatenate([da_bf, dqkv_bf, dz_bf, duv_bf, dgate_bf, dba_bf], axis=1)
    dx = _matmul(dp_bf, w["w_all_t"], name="proj_bwd" + tag, tm=min(ROW_TILE, s), tn=D_MODEL, tk=P_TILE, add=dr1, add_scale=DN_ALPHA)
    d_w_all = _matmul_tn(sv["x_bf"], dp_bf, name="dw_proj" + tag, ka=D_MODEL, tka=512, tn=P_TILE, ts=ts)
    grads = dict(
        w_in=_restore_proj_cols(d_w_all), b_gate=d_b_gate[0], conv_dw_w=d_conv_w, conv_dw_b=d_conv_b[0],
        conv_ln_g=d_conv_ln_g[0], conv_ln_b=d_conv_ln_b[0], w_pa=d_w_p[0],
        gdn_conv_q=d_gdn_cw[:, :GDN_QK], gdn_conv_k=d_gdn_cw[:, GDN_QK:2 * GDN_QK], gdn_conv_v=d_gdn_cw[:, 2 * GDN_QK:],
        gdn_a_log=d_a_log[0, GDN_HEADS:2 * GDN_HEADS], gdn_dt_bias=d_dt_bias[0, GDN_HEADS:2 * GDN_HEADS], gdn_norm_g=d_norm_g[0],
        w_pb=d_w_p[1], sgu_ln_g=d_sgu_ln_g[0], sgu_ln_b=d_sgu_ln_b[0], sgu_w_s=d_sgu_w_s, sgu_b_s=d_sgu_b_s_t[:, :SGU_GROUPS].T,
        w_pc=d_w_p[2], w_o=d_w_o, ln1_g=d_ln1_g[0], ln1_b=d_ln1_b[0], w_ff1=d_w_ff1, b_ff1=d_b_ff1[0],
        w_ff2=d_w_ff2, b_ff2=d_b_ff2[0], ln2_g=d_ln2_g[0], ln2_b=d_ln2_b[0],
    )
    return dx, grads


def _local_step(x, target, full, rep):
    ln_g, ln_b = rep["ln_in_g"][None, :], rep["ln_in_b"][None, :]
    xs, xs_bf = _ln_in_fwd(x, ln_g, ln_b)
    ws, saves = [], []
    for l in range(DEPTH):
        ws.append(_layer_weights(l, full, rep))
        xs, xs_bf, sv = _layer_fwd(xs, xs_bf, ws[l], f"_l{l}")
        saves.append(sv)
    d, loss_acc = _loss_fwd_bwd(xs, target)
    layer_grads = [None] * DEPTH
    for l in reversed(range(DEPTH)):
        d, layer_grads[l] = _layer_bwd(d, ws[l], saves[l], f"_l{l}")
    dx, _, d_ln_in_g, d_ln_in_b, _ = _ln_bwd_call(d, x, ln_g, name="ln_in_bwd")
    grads = {k: jnp.stack([layer_grads[l][k] for l in range(DEPTH)]) for k in layer_grads[0]}
    grads["ln_in_g"] = d_ln_in_g[0]
    grads["ln_in_b"] = d_ln_in_b[0]
    return loss_acc[0, 0], dx, grads


MESH_AXES = ("x", "y", "c")


def _exchange(arrays, scatter, *, name):
    n = len(arrays)

    def body(*refs):
        ins, outs = refs[:n], refs[n:2 * n]
        send_sems, recv_sems, local_sems = refs[2 * n:]
        x, y, c = lax.axis_index("x"), lax.axis_index("y"), lax.axis_index("c")
        me = 4 * x + 2 * y + c

        def slot(a, d):
            return ins[a].at[d] if scatter[a] else ins[a]

        local = [pltpu.make_async_copy(slot(a, me), outs[a].at[me], local_sems.at[a]) for a in range(n)]
        for cp in local:
            cp.start()
        remote = []
        for k in range(1, N_DEV):
            px = 1 - x if k & 4 else x
            py = 1 - y if k & 2 else y
            pc = 1 - c if k & 1 else c
            peer = 4 * px + 2 * py + pc
            for a in range(n):
                send = pltpu.make_async_remote_copy(
                    src_ref=slot(a, peer), dst_ref=outs[a].at[me], send_sem=send_sems.at[a, k - 1],
                    recv_sem=recv_sems.at[a, k - 1], device_id=(px, py, pc), device_id_type=pl.DeviceIdType.MESH)
                send.start()
                arrival = pltpu.make_async_remote_copy(
                    src_ref=slot(a, peer), dst_ref=outs[a].at[peer], send_sem=send_sems.at[a, k - 1],
                    recv_sem=recv_sems.at[a, k - 1], device_id=(px, py, pc), device_id_type=pl.DeviceIdType.MESH)
                remote.append((send, arrival))
        for send, arrival in remote:
            arrival.wait_recv()
        for send, arrival in remote:
            send.wait_send()
        for cp in local:
            cp.wait()

    any_spec = pl.BlockSpec(memory_space=pl.ANY)
    out_shape = [jax.ShapeDtypeStruct(a.shape if s else (N_DEV,) + a.shape, a.dtype) for a, s in zip(arrays, scatter)]
    return pl.pallas_call(
        body, name=name, in_specs=[any_spec] * n, out_specs=[any_spec] * n, out_shape=out_shape,
        scratch_shapes=[pltpu.SemaphoreType.DMA((n, N_DEV - 1)), pltpu.SemaphoreType.DMA((n, N_DEV - 1)),
                        pltpu.SemaphoreType.DMA((n,))],
    )(*arrays)


def _adamw(w, m, v, g_parts, *, name):
    r, c = w.shape
    tr = 256 if r % 256 == 0 else r
    bc1 = 1.0 - ADAM_B1 ** ADAM_STEP
    bc2 = 1.0 - ADAM_B2 ** ADAM_STEP

    def body(w_ref, m_ref, v_ref, gp_ref, g_ref, d_ref, nm_ref, nv_ref):
        g = gp_ref[0].astype(F32)
        for d in range(1, N_DEV):
            g = g + gp_ref[d].astype(F32)
        nm = ADAM_B1 * m_ref[...] + (1.0 - ADAM_B1) * g
        nv = ADAM_B2 * v_ref[...] + (1.0 - ADAM_B2) * (g * g)
        g_ref[...] = g
        nm_ref[...] = nm
        nv_ref[...] = nv
        d_ref[...] = -ADAM_LR * ((nm / bc1) / (jnp.sqrt(nv / bc2) + ADAM_EPS) + ADAM_WD * w_ref[...])

    spec = pl.BlockSpec((tr, c), lambda i: (i, 0))
    return pl.pallas_call(
        body, name=name, grid=(r // tr,),
        in_specs=[spec, spec, spec, pl.BlockSpec((N_DEV, tr, c), lambda i: (0, i, 0))],
        out_specs=[spec] * 4, out_shape=[jax.ShapeDtypeStruct((r, c), F32)] * 4,
        compiler_params=_params(("parallel",)),
    )(w, m, v, g_parts)


SHARDED = dict(w_in=2, conv_dw_w=2, w_pa=2, gdn_conv_q=2, gdn_conv_k=2, gdn_conv_v=2, w_pb=2, w_pc=2, w_o=1, w_ff1=2, w_ff2=1)
WEIGHTS = ["ln_in_g", "ln_in_b", "w_in", "b_gate", "conv_dw_w", "conv_dw_b", "conv_ln_g", "conv_ln_b", "w_pa", "gdn_conv_q",
           "gdn_conv_k", "gdn_conv_v", "gdn_a_log", "gdn_dt_bias", "gdn_norm_g", "w_pb", "sgu_ln_g", "sgu_ln_b", "sgu_w_s",
           "sgu_b_s", "w_pc", "w_o", "ln1_g", "ln1_b", "w_ff1", "b_ff1", "w_ff2", "b_ff2", "ln2_g", "ln2_b"]
REPLICATED = [n for n in WEIGHTS if n not in SHARDED]
CONV_PACK = ["conv_dw_w", "gdn_conv_q", "gdn_conv_k", "gdn_conv_v"]
PROJ_PACK = ["w_pa", "w_pb", "w_pc"]


def _to_slots(full, axis):
    shp = full.shape
    split = full.reshape(shp[:axis] + (N_DEV, shp[axis] // N_DEV) + shp[axis + 1:])
    return jnp.moveaxis(split, axis, 0)


def _from_slots(slots, axis):
    merged = jnp.moveaxis(slots, 0, axis)
    shp = merged.shape
    return merged.reshape(shp[:axis] + (shp[axis] * shp[axis + 1],) + shp[axis + 2:])


def _pack_rows(arrs):
    rows = []
    for a in arrs:
        flat = a.reshape(-1)
        pad = (-flat.shape[0]) % LANES
        rows.append(jnp.pad(flat, (0, pad)).reshape(-1, LANES))
    out = jnp.concatenate(rows, axis=0)
    return jnp.pad(out, ((0, (-out.shape[0]) % 8), (0, 0)))


def _unpack_rows(packed, shapes):
    out, r = [], 0
    for shp in shapes:
        size = math.prod(shp)
        nrows = -(-size // LANES)
        out.append(packed[r:r + nrows].reshape(-1)[:size].reshape(shp))
        r += nrows
    return out


def kernel(x, ln_in_g, ln_in_b, w_in, b_gate, conv_dw_w, conv_dw_b, conv_ln_g, conv_ln_b, w_pa, gdn_conv_q, gdn_conv_k, gdn_conv_v, gdn_a_log, gdn_dt_bias, gdn_norm_g, w_pb, sgu_ln_g, sgu_ln_b, sgu_w_s, sgu_b_s, w_pc, w_o, ln1_g, ln1_b, w_ff1, b_ff1, w_ff2, b_ff2, ln2_g, ln2_b, loss_target, m_ln_in_g, m_ln_in_b, m_w_in, m_b_gate, m_conv_dw_w, m_conv_dw_b, m_conv_ln_g, m_conv_ln_b, m_w_pa, m_gdn_conv_q, m_gdn_conv_k, m_gdn_conv_v, m_gdn_a_log, m_gdn_dt_bias, m_gdn_norm_g, m_w_pb, m_sgu_ln_g, m_sgu_ln_b, m_sgu_w_s, m_sgu_b_s, m_w_pc, m_w_o, m_ln1_g, m_ln1_b, m_w_ff1, m_b_ff1, m_w_ff2, m_b_ff2, m_ln2_g, m_ln2_b, v_ln_in_g, v_ln_in_b, v_w_in, v_b_gate, v_conv_dw_w, v_conv_dw_b, v_conv_ln_g, v_conv_ln_b, v_w_pa, v_gdn_conv_q, v_gdn_conv_k, v_gdn_conv_v, v_gdn_a_log, v_gdn_dt_bias, v_gdn_norm_g, v_w_pb, v_sgu_ln_g, v_sgu_ln_b, v_sgu_w_s, v_sgu_b_s, v_w_pc, v_w_o, v_ln1_g, v_ln1_b, v_w_ff1, v_b_ff1, v_w_ff2, v_b_ff2, v_ln2_g, v_ln2_b):
    args = locals()
    w = {n: args[n] for n in WEIGHTS}
    m = {n: args["m_" + n] for n in WEIGHTS}
    v = {n: args["v_" + n] for n in WEIGHTS}

    conv_local = jnp.concatenate([w[n] for n in CONV_PACK], axis=1)
    proj_local = jnp.stack([w[n] for n in PROJ_PACK], axis=1).astype(BF16)
    big = ["w_in", "w_o", "w_ff1", "w_ff2"]
    gathered = _exchange([w[n].astype(BF16) for n in big] + [proj_local, conv_local], [False] * 6, name="gather_weights")
    full = {n: _from_slots(g, SHARDED[n]) for n, g in zip(big, gathered[:4])}
    proj_full = _from_slots(gathered[4], 3)
    for i, n in enumerate(PROJ_PACK):
        full[n] = proj_full[:, i]
    conv_full = _from_slots(gathered[5], 2)
    tap0 = 0
    for n in CONV_PACK:
        taps = w[n].shape[1]
        full[n] = conv_full[:, tap0:tap0 + taps]
        tap0 += taps
    rep = {n: w[n] for n in REPLICATED}

    loss_local, dx, grads = _local_step(x[0], loss_target[0], full, rep)
    loss = lax.psum(loss_local, MESH_AXES)

    conv_grad = jnp.concatenate([grads[n] for n in CONV_PACK], axis=1)
    proj_grad = jnp.stack([grads[n] for n in PROJ_PACK], axis=1)
    send = [_to_slots(grads[n].astype(BF16), SHARDED[n]) for n in big] + [
        _to_slots(proj_grad.astype(BF16), 3), _to_slots(conv_grad, 2), _pack_rows([grads[n] for n in REPLICATED])]
    parts = _exchange(send, [True] * 6 + [False], name="exchange_grads")

    def adam_sharded(g_parts, w_l, m_l, v_l, name):
        shp = w_l.shape
        two_d = lambda a: a.reshape(-1, shp[-1])
        outs = _adamw(two_d(w_l), two_d(m_l), two_d(v_l), g_parts.reshape(N_DEV, -1, shp[-1]), name=name)
        return [o.reshape(shp) for o in outs]

    res = {}
    for n, gp in zip(big, parts[:4]):
        res[n] = adam_sharded(gp, w[n], m[n], v[n], "adamw_" + n)
    proj_res = adam_sharded(parts[4], jnp.stack([w[n] for n in PROJ_PACK], axis=1), jnp.stack([m[n] for n in PROJ_PACK], axis=1),
                            jnp.stack([v[n] for n in PROJ_PACK], axis=1), "adamw_proj")
    for i, n in enumerate(PROJ_PACK):
        res[n] = [o[:, i] for o in proj_res]
    conv_res = adam_sharded(parts[5], conv_local, jnp.concatenate([m[n] for n in CONV_PACK], axis=1),
                            jnp.concatenate([v[n] for n in CONV_PACK], axis=1), "adamw_conv")
    tap0 = 0
    for n in CONV_PACK:
        taps = w[n].shape[1]
        res[n] = [o[:, tap0:tap0 + taps] for o in conv_res]
        tap0 += taps
    rep_shapes = [w[n].shape for n in REPLICATED]
    rep_res = _adamw(_pack_rows([w[n] for n in REPLICATED]), _pack_rows([m[n] for n in REPLICATED]),
                     _pack_rows([v[n] for n in REPLICATED]), parts[6], name="adamw_replicated")
    rep_res = [_unpack_rows(o, rep_shapes) for o in rep_res]
    for i, n in enumerate(REPLICATED):
        res[n] = [o[i] for o in rep_res]

    outs = [loss, dx[None]]
    for j in range(4):
        outs += [res[n][j] for n in WEIGHTS]
    return tuple(outs)
```

```python
import functools
import math

import jax
import jax.numpy as jnp
from jax import lax
from jax.experimental import pallas as pl
from jax.experimental.pallas import tpu as pltpu

F32 = jnp.float32
BF16 = jnp.bfloat16

N_DEV = 8
DEPTH = 2
D_MODEL = 1024
CONV_DIM = 512
CONV_WIDTH = 31
GDN_HEADS = 4
GDN_DK = 128
GDN_QK = 512
GDN_CONV = 4
GDN_CHUNK = 64
SGU_GROUPS = 4
SGU_GROUP_DIM = 128
SGU_DIM = 512
SGU_CHUNK = 128
D_FF = 4096
DN_ALPHA = (2 * DEPTH) ** 0.25
LN_EPS = 1e-5
RMS_EPS = 1e-6
PROJ_COLS = 7176
SHARD_COLS = PROJ_COLS // N_DEV

COL_A = 0
COL_Q = 1024
COL_Z = 2560
COL_UV = 3072
COL_GATE = 4096
COL_BA = 7168
P_COLS = 7296
P_TILE = 2432

ADAM_LR = 0.001
ADAM_B1 = 0.9
ADAM_B2 = 0.999
ADAM_EPS = 1e-08
ADAM_WD = 0.01
ADAM_STEP = 10

VMEM_LIMIT_BYTES = 56 * 1024 * 1024
LANES = 128
ROW_TILE = 512
GDN_BLOCK = 256
CONV_HALO = 32
GDN_HALO = 8


def _params(sem):
    return pltpu.CompilerParams(dimension_semantics=sem, vmem_limit_bytes=VMEM_LIMIT_BYTES)


def _dot(a, b):
    return jnp.dot(a, b, preferred_element_type=F32)


def _dot_nt(a, b):
    return lax.dot_general(a, b, (((1,), (1,)), ((), ())), preferred_element_type=F32)


def _dot_tn(a, b):
    return lax.dot_general(a, b, (((0,), (0,)), ((), ())), preferred_element_type=F32)


def _split(a):
    hi = a.astype(BF16)
    lo = (a - hi.astype(F32)).astype(BF16)
    return hi, lo


def _dot3(a, b, dot=_dot):
    ah, al = _split(a)
    bh, bl = _split(b)
    return dot(ah, bh) + (dot(ah, bl) + dot(al, bh))


def _bdot(a, b, dot=_dot):
    return dot(a.astype(BF16), b.astype(BF16))


def _sigmoid(x):
    return jax.nn.sigmoid(x)


def _silu(x):
    return x * _sigmoid(x)


def _dsilu(x):
    s = _sigmoid(x)
    return s * (1.0 + x * (1.0 - s))


_GELU_C = math.sqrt(2.0 / math.pi)


def _gelu(x):
    return 0.5 * x * (1.0 + jnp.tanh(_GELU_C * (x + 0.044715 * (x * x * x))))


def _dgelu(x):
    t = jnp.tanh(_GELU_C * (x + 0.044715 * (x * x * x)))
    return 0.5 * (1.0 + t) + 0.5 * x * (1.0 - t * t) * (_GELU_C * (1.0 + 3.0 * 0.044715 * (x * x)))


def _ln_stats(x):
    mu = jnp.mean(x, axis=-1, keepdims=True)
    xc = x - mu
    var = jnp.mean(xc * xc, axis=-1, keepdims=True)
    rstd = lax.rsqrt(var + LN_EPS)
    return xc * rstd, rstd


def _ln_bwd(dy, xhat, rstd, g):
    dxh = dy * g
    return rstd * (dxh - jnp.mean(dxh, axis=-1, keepdims=True) - xhat * jnp.mean(dxh * xhat, axis=-1, keepdims=True))


def _colsum(x):
    return jnp.sum(x, axis=0, keepdims=True)


def _row_spec(t, cols, col_block=0):
    return pl.BlockSpec((t, cols), lambda i, cb=col_block: (i, cb))


def _const_spec(shape):
    nd = len(shape)
    return pl.BlockSpec(shape, lambda *_: (0,) * nd)


def _call(body, *, name, grid, in_specs, out_specs, out_shape, args, sem, scratch=(), exchange=None):
    n_in, n_out, n_sc = len(in_specs), len(out_specs), len(scratch)
    if exchange is None:
        def plain(*refs):
            body(refs[:n_in], refs[n_in:n_in + n_out], refs[n_in + n_out:])

        return pl.pallas_call(plain, name=name, grid=grid, in_specs=in_specs, out_specs=out_specs, out_shape=out_shape,
                              scratch_shapes=list(scratch), compiler_params=_params(sem))(*args)
    nex = exchange.n

    def carrying(*refs):
        ins, ex_ins = refs[:n_in], refs[n_in:n_in + nex]
        outs, ex_outs = refs[n_in + nex:n_in + nex + n_out], refs[n_in + nex + n_out:n_in + 2 * nex + n_out]
        sc, sems = refs[n_in + 2 * nex + n_out:n_in + 2 * nex + n_out + n_sc], refs[n_in + 2 * nex + n_out + n_sc:]
        ids = [pl.program_id(d) for d in range(len(grid))]
        first = functools.reduce(jnp.logical_and, [i == 0 for i in ids])
        last = functools.reduce(jnp.logical_and, [i == g - 1 for i, g in zip(ids, grid)])

        @pl.when(first)
        def _():
            _exchange_start(exchange, ex_ins, ex_outs, sems)

        body(ins, outs, sc)

        @pl.when(last)
        def _():
            _exchange_wait(exchange, ex_ins, ex_outs, sems)

    res = pl.pallas_call(
        carrying, name=name, grid=grid, in_specs=list(in_specs) + exchange.in_specs(),
        out_specs=list(out_specs) + exchange.in_specs(), out_shape=list(out_shape) + exchange.out_shapes(),
        scratch_shapes=list(scratch) + exchange.scratch(), compiler_params=_params(("arbitrary",) * len(grid)),
    )(*args, *exchange.arrays)
    return res[:n_out], res[n_out:]


def _matmul(a, w, *, name, tm, tn, tk, out_dtype=F32, a_col_block=0, add=None, add_scale=1.0, exchange=None):
    m = a.shape[0]
    k, n = w.shape
    nk = k // tk
    has_add = add is not None

    def body(ins, outs, scratch):
        a_ref, w_ref = ins[:2]
        o_ref, acc_ref = outs[0], scratch[0]
        if has_add:
            add_ref = ins[2]
        kk = pl.program_id(2)

        @pl.when(kk == 0)
        def _():
            acc_ref[...] = jnp.zeros_like(acc_ref)

        acc_ref[...] += _dot(a_ref[...], w_ref[...])

        @pl.when(kk == nk - 1)
        def _():
            r = acc_ref[...]
            if has_add:
                r = r + add_scale * add_ref[...]
            o_ref[...] = r.astype(out_dtype)

    in_specs = [pl.BlockSpec((tm, tk), lambda i, j, kk: (i, kk + a_col_block)),
                pl.BlockSpec((tk, tn), lambda i, j, kk: (kk, j))]
    args = [a, w]
    if has_add:
        in_specs.append(pl.BlockSpec((tm, tn), lambda i, j, kk: (i, j)))
        args.append(add)
    res = _call(body, name=name, grid=(m // tm, n // tn, nk), in_specs=in_specs,
                out_specs=[pl.BlockSpec((tm, tn), lambda i, j, kk: (i, j))],
                out_shape=[jax.ShapeDtypeStruct((m, n), out_dtype)], scratch=[pltpu.VMEM((tm, tn), F32)],
                args=args, sem=("parallel", "parallel", "arbitrary"), exchange=exchange)
    return res[0] if exchange is None else (res[0][0], res[1])


def _matmul_tn(a, b, *, name, ka, tka, tn, ts, n=None, a_col_block=0, b_col_block=0, exchange=None):
    s = a.shape[0]
    n = b.shape[1] if n is None else n
    ns = s // ts

    def body(ins, outs, scratch):
        a_ref, b_ref = ins
        o_ref = outs[0]

        @pl.when(pl.program_id(2) == 0)
        def _():
            o_ref[...] = jnp.zeros_like(o_ref)

        o_ref[...] += _dot_tn(a_ref[...], b_ref[...])

    res = _call(body, name=name, grid=(ka // tka, n // tn, ns),
                in_specs=[pl.BlockSpec((ts, tka), lambda i, j, t: (t, i + a_col_block)),
                          pl.BlockSpec((ts, tn), lambda i, j, t: (t, j + b_col_block))],
                out_specs=[pl.BlockSpec((tka, tn), lambda i, j, t: (i, j))],
                out_shape=[jax.ShapeDtypeStruct((ka, n), F32)], args=[a, b],
                sem=("parallel", "parallel", "arbitrary"), exchange=exchange)
    return res[0] if exchange is None else (res[0][0], res[1])


def _ln_in_fwd(x, g, b):
    s = x.shape[0]
    t = min(ROW_TILE, s)

    def body(x_ref, g_ref, b_ref, y_ref, ybf_ref):
        xhat, _ = _ln_stats(x_ref[...])
        y = xhat * g_ref[...] + b_ref[...]
        y_ref[...] = y
        ybf_ref[...] = y.astype(BF16)

    return pl.pallas_call(
        body, name="ln_in_fwd", grid=(s // t,),
        in_specs=[_row_spec(t, D_MODEL), _const_spec((1, D_MODEL)), _const_spec((1, D_MODEL))],
        out_specs=[_row_spec(t, D_MODEL), _row_spec(t, D_MODEL)],
        out_shape=[jax.ShapeDtypeStruct((s, D_MODEL), F32), jax.ShapeDtypeStruct((s, D_MODEL), BF16)],
        compiler_params=_params(("parallel",)),
    )(x, g, b)


def _prev_halo_spec(t, halo, cols, col_block):
    per = t // halo
    return pl.BlockSpec((halo, cols), lambda i, cb=col_block: (jnp.maximum(i * per - 1, 0), cb))


def _next_halo_spec(t, halo, cols, col_block, n_blocks):
    per = t // halo
    last = n_blocks * per - 1
    return pl.BlockSpec((halo, cols), lambda i, cb=col_block: (jnp.minimum((i + 1) * per, last), cb))


CONV_ROWS = 32


def _tap_groups(offsets):
    by_shift = {}
    for j, off in enumerate(offsets):
        by_shift.setdefault(off % 8, []).append((j, off // 8))
    groups = []
    for shift, taps in sorted(by_shift.items()):
        first = min(a for _, a in taps)
        last = max(a for _, a in taps)
        groups.append((shift, 8 * first, 8 * (last - first), [(j, 8 * (a - first)) for j, a in taps]))
    return groups


def _tap_windows(ext, groups, r0):
    return [ext[r0 + first + shift:r0 + first + shift + CONV_ROWS + extra, :] for shift, first, extra, _ in groups]


def _tap_sum(ext, w_ref, groups, r0):
    acc = jnp.zeros((CONV_ROWS, ext.shape[1]), F32)
    for (_, _, _, taps), win in zip(groups, _tap_windows(ext, groups, r0)):
        for j, a in taps:
            acc = acc + w_ref[j:j + 1, :] * win[a:a + CONV_ROWS]
    return acc


def _conv_a_fwd(p, w, b, ln_g, ln_b, tag):
    s = p.shape[0]
    t = min(ROW_TILE, s)
    width = CONV_WIDTH

    def body(a_ref, halo_ref, w_ref, b_ref, g_ref, bb_ref, h_ref, c_ref, ha_ref, ext):
        i = pl.program_id(0)
        a = a_ref[...]
        h = a[:, :CONV_DIM] * _sigmoid(a[:, CONV_DIM:])
        ah = halo_ref[...]
        hh = ah[:, :CONV_DIM] * _sigmoid(ah[:, CONV_DIM:])
        ext[0:CONV_HALO, :] = jnp.where(i == 0, 0.0, hh)
        ext[CONV_HALO:CONV_HALO + t, :] = h
        h_ref[...] = h
        groups = _tap_groups([CONV_HALO - (width - 1) + j for j in range(width)])
        for r0 in range(0, t, CONV_ROWS):
            c = _tap_sum(ext, w_ref, groups, r0) + b_ref[...]
            xhat, _ = _ln_stats(c)
            n = xhat * g_ref[...] + bb_ref[...]
            c_ref[r0:r0 + CONV_ROWS, :] = c
            ha_ref[r0:r0 + CONV_ROWS, :] = _silu(n).astype(BF16)

    return pl.pallas_call(
        body, name="conv_a_fwd" + tag, grid=(s // t,),
        in_specs=[_row_spec(t, 2 * CONV_DIM, COL_A // (2 * CONV_DIM)),
                  _prev_halo_spec(t, CONV_HALO, 2 * CONV_DIM, COL_A // (2 * CONV_DIM)),
                  _const_spec((width, CONV_DIM)), _const_spec((1, CONV_DIM)),
                  _const_spec((1, CONV_DIM)), _const_spec((1, CONV_DIM))],
        out_specs=[_row_spec(t, CONV_DIM)] * 3,
        out_shape=[jax.ShapeDtypeStruct((s, CONV_DIM), F32), jax.ShapeDtypeStruct((s, CONV_DIM), F32),
                   jax.ShapeDtypeStruct((s, CONV_DIM), BF16)],
        scratch_shapes=[pltpu.VMEM((CONV_HALO + t, CONV_DIM), F32)],
        compiler_params=_params(("parallel",)),
    )(p, p, w, b, ln_g, ln_b)


def _sgu_mix(vn, wt_ref, bst_ref, t):
    row = lax.broadcasted_iota(jnp.int32, (SGU_CHUNK, SGU_CHUNK), 0)
    col = lax.broadcasted_iota(jnp.int32, (SGU_CHUNK, SGU_CHUNK), 1)
    chunks = []
    for ci in range(t // SGU_CHUNK):
        groups = []
        for g in range(SGU_GROUPS):
            wg = jnp.where(row >= col, wt_ref[g], 0.0).astype(BF16)
            v_cg = vn[ci * SGU_CHUNK:(ci + 1) * SGU_CHUNK, g * SGU_GROUP_DIM:(g + 1) * SGU_GROUP_DIM]
            groups.append(_dot(wg, v_cg.astype(BF16)) + bst_ref[:, g:g + 1])
        chunks.append(jnp.concatenate(groups, axis=1))
    return jnp.concatenate(chunks, axis=0)


def _sgu_fwd(p, ln_g, ln_b, w_s, b_s_t, tag):
    s = p.shape[0]
    t = min(ROW_TILE, s)

    def body(uv_ref, g_ref, b_ref, ws_ref, bst_ref, hc_ref):
        uv = uv_ref[...]
        u = _gelu(uv[:, :SGU_DIM])
        vv = _gelu(uv[:, SGU_DIM:])
        xhat, _ = _ln_stats(vv)
        vn = xhat * g_ref[...] + b_ref[...]
        mixed = _sgu_mix(vn, ws_ref, bst_ref, t)
        hc_ref[...] = (u * mixed).astype(BF16)

    return pl.pallas_call(
        body, name="sgu_fwd" + tag, grid=(s // t,),
        in_specs=[_row_spec(t, 2 * SGU_DIM, COL_UV // (2 * SGU_DIM)),
                  _const_spec((1, SGU_DIM)), _const_spec((1, SGU_DIM)),
                  _const_spec((SGU_GROUPS, SGU_CHUNK, SGU_CHUNK)), _const_spec((SGU_CHUNK, LANES))],
        out_specs=_row_spec(t, SGU_DIM),
        out_shape=jax.ShapeDtypeStruct((s, SGU_DIM), BF16),
        compiler_params=_params(("parallel",)),
    )(p, ln_g, ln_b, w_s, b_s_t)


def _merge_fwd(p, ha, hb, hc, w_pa, w_pb, w_pc, b_gate, tag):
    s = p.shape[0]
    t = min(ROW_TILE, s)
    gb = COL_GATE // D_MODEL

    def body(ga_ref, gb_ref, gc_ref, ha_ref, hb_ref, hc_ref, wa_ref, wb_ref, wc_ref, bg_ref, y_ref, m_ref):
        merged = jnp.zeros((t, D_MODEL), F32)
        for idx, (g_ref, h_ref, w_ref) in enumerate(((ga_ref, ha_ref, wa_ref), (gb_ref, hb_ref, wb_ref), (gc_ref, hc_ref, wc_ref))):
            y = _dot(h_ref[...], w_ref[...])
            sg = _sigmoid(g_ref[...] + bg_ref[:, idx * D_MODEL:(idx + 1) * D_MODEL])
            y_ref[:, idx * D_MODEL:(idx + 1) * D_MODEL] = y
            merged = merged + sg * y
        m_ref[...] = merged.astype(BF16)

    hspec = _row_spec(t, CONV_DIM)
    wspec = _const_spec((CONV_DIM, D_MODEL))
    return pl.pallas_call(
        body, name="merge_fwd" + tag, grid=(s // t,),
        in_specs=[_row_spec(t, D_MODEL, gb), _row_spec(t, D_MODEL, gb + 1), _row_spec(t, D_MODEL, gb + 2),
                  hspec, hspec, hspec, wspec, wspec, wspec, _const_spec((1, 3 * D_MODEL))],
        out_specs=[_row_spec(t, 3 * D_MODEL), _row_spec(t, D_MODEL)],
        out_shape=[jax.ShapeDtypeStruct((s, 3 * D_MODEL), F32), jax.ShapeDtypeStruct((s, D_MODEL), BF16)],
        compiler_params=_params(("parallel",)),
    )(p, p, p, ha, hb, hc, w_pa, w_pb, w_pc, b_gate)


def _matmul_res_ln(a, w, bias, x_res, ln_g, ln_b, *, name):
    s, k = a.shape
    t = min(ROW_TILE, s)

    def body(a_ref, w_ref, bias_ref, x_ref, g_ref, b_ref, r_ref, y_ref, ybf_ref):
        r = DN_ALPHA * x_ref[...] + _dot(a_ref[...], w_ref[...]) + bias_ref[...]
        xhat, _ = _ln_stats(r)
        y = xhat * g_ref[...] + b_ref[...]
        r_ref[...] = r
        y_ref[...] = y
        ybf_ref[...] = y.astype(BF16)

    vec = _const_spec((1, D_MODEL))
    return pl.pallas_call(
        body, name=name, grid=(s // t,),
        in_specs=[_row_spec(t, k), _const_spec((k, D_MODEL)), vec, _row_spec(t, D_MODEL), vec, vec],
        out_specs=[_row_spec(t, D_MODEL)] * 3,
        out_shape=[jax.ShapeDtypeStruct((s, D_MODEL), F32), jax.ShapeDtypeStruct((s, D_MODEL), F32),
                   jax.ShapeDtypeStruct((s, D_MODEL), BF16)],
        compiler_params=_params(("parallel",)),
    )(a, w, bias, x_res, ln_g, ln_b)


def _ff1_fwd(x_bf, w, b, tag, exchange=None):
    s = x_bf.shape[0]
    tm = min(1024, s)
    tn = 1024

    def body(ins, outs, scratch):
        x_ref, w_ref, b_ref = ins
        hp_ref, h_ref = outs
        hp = _dot(x_ref[...], w_ref[...]) + b_ref[...]
        hp_ref[...] = hp
        r = jnp.maximum(hp, 0.0)
        h_ref[...] = (r * r).astype(BF16)

    res = _call(
        body, name="ff1_fwd" + tag, grid=(s // tm, D_FF // tn),
        in_specs=[pl.BlockSpec((tm, D_MODEL), lambda i, j: (i, 0)), pl.BlockSpec((D_MODEL, tn), lambda i, j: (0, j)),
                  pl.BlockSpec((1, tn), lambda i, j: (0, j))],
        out_specs=[pl.BlockSpec((tm, tn), lambda i, j: (i, j))] * 2,
        out_shape=[jax.ShapeDtypeStruct((s, D_FF), F32), jax.ShapeDtypeStruct((s, D_FF), BF16)],
        args=[x_bf, w, b], sem=("parallel", "parallel"), exchange=exchange)
    return (res[0], res[1], None) if exchange is None else (res[0][0], res[0][1], res[1])


def _loss_fwd_bwd(y, target):
    s = y.shape[0]
    t = min(ROW_TILE, s)

    def body(y_ref, t_ref, dy_ref, loss_ref):
        @pl.when(pl.program_id(0) == 0)
        def _():
            loss_ref[...] = jnp.zeros_like(loss_ref)

        err = y_ref[...] - t_ref[...]
        dy_ref[...] = err * (1.0 / D_MODEL)
        per_row = jnp.mean(err * err, axis=-1, keepdims=True)
        loss_ref[...] += 0.5 * jnp.sum(per_row, axis=0, keepdims=True)

    return pl.pallas_call(
        body, name="loss_fwd_bwd", grid=(s // t,),
        in_specs=[_row_spec(t, D_MODEL), _row_spec(t, D_MODEL)],
        out_specs=[_row_spec(t, D_MODEL), _const_spec((8, LANES))],
        out_shape=[jax.ShapeDtypeStruct((s, D_MODEL), F32), jax.ShapeDtypeStruct((8, LANES), F32)],
        compiler_params=_params(("arbitrary",)),
    )(y, target)


def _softplus(x):
    return jnp.maximum(x, 0.0) + jnp.log1p(jnp.exp(-jnp.abs(x)))


def _gdn_conv_silu_norm(q_ref, k_ref, v_ref, hq_ref, hk_ref, hv_ref, cw_ref, ext, first):
    t = q_ref.shape[0]
    for n, (r, h) in enumerate(((q_ref, hq_ref), (k_ref, hk_ref), (v_ref, hv_ref))):
        ext[0:GDN_HALO, n * GDN_QK:(n + 1) * GDN_QK] = jnp.where(first, 0.0, h[...])
        ext[GDN_HALO:GDN_HALO + t, n * GDN_QK:(n + 1) * GDN_QK] = r[...]
    pre = jnp.zeros((t, 3 * GDN_QK), F32)
    for j in range(GDN_CONV):
        pre = pre + cw_ref[j:j + 1, :] * ext[pl.ds(GDN_HALO - (GDN_CONV - 1) + j, t), :]
    act = _silu(pre)
    rq, rk = [], []
    for h in range(GDN_HEADS):
        qh = act[:, h * GDN_DK:(h + 1) * GDN_DK]
        kh = act[:, GDN_QK + h * GDN_DK:GDN_QK + (h + 1) * GDN_DK]
        rq.append(lax.rsqrt(jnp.sum(qh * qh, axis=-1, keepdims=True) + RMS_EPS))
        rk.append(lax.rsqrt(jnp.sum(kh * kh, axis=-1, keepdims=True) + RMS_EPS))
    return pre, act, rq, rk


def _gdn_gates(ba, alog_ref, dtb_ref):
    lane = lax.broadcasted_iota(jnp.int32, ba.shape, 1)
    beta = _sigmoid(ba)
    g = -jnp.exp(alog_ref[...]) * _softplus(ba + dtb_ref[...])
    g = jnp.where((lane >= GDN_HEADS) & (lane < 2 * GDN_HEADS), g, 0.0)
    return beta, g


def _chunk_cumsum_matrix(t, upper):
    row = lax.broadcasted_iota(jnp.int32, (t, t), 0)
    col = lax.broadcasted_iota(jnp.int32, (t, t), 1)
    same = (row // GDN_CHUNK) == (col // GDN_CHUNK)
    tri = (col >= row) if upper else (col <= row)
    return jnp.where(same & tri, 1.0, 0.0).astype(F32)


def _each(fn, *lists):
    return [fn(*a) for a in zip(*lists)]


def _gdn_pair_items(qn_s, kn_s, vc_s, beta_s, gam_s, nc):
    items = []
    for ci in range(nc):
        rows = slice(ci * GDN_CHUNK, (ci + 1) * GDN_CHUNK)
        gam_blk = gam_s[rows, :]
        gam_t = gam_blk.T
        beta_blk = beta_s[rows, :]
        for h in range(GDN_HEADS):
            sl = slice(h * GDN_DK, (h + 1) * GDN_DK)
            items.append((qn_s[rows, sl], kn_s[rows, sl], vc_s[rows, sl], beta_blk[:, h:h + 1],
                          gam_blk[:, GDN_HEADS + h:GDN_HEADS + h + 1], gam_t[GDN_HEADS + h:GDN_HEADS + h + 1, :]))
    return items


def _gdn_prep(items):
    c = GDN_CHUNK
    row = lax.broadcasted_iota(jnp.int32, (c, c), 0)
    col = lax.broadcasted_iota(jnp.int32, (c, c), 1)
    causal = row >= col
    strict = row > col
    kbs = [k.astype(BF16) for _, k, _, _, _, _ in items]
    kks = _each(_dot_nt, kbs, kbs)
    qks = _each(_dot_nt, [q.astype(BF16) for q, _, _, _, _, _ in items], kbs)
    out = []
    for (q, k, v, beta_c, gam_c, gam_r), kk, qk in zip(items, kks, qks):
        decay = jnp.where(causal, jnp.exp(jnp.where(causal, gam_c - gam_r, 0.0)), 0.0)
        gm = jnp.exp(gam_c)
        glast = gam_c[c - 1:c, :]
        elast = jnp.exp(glast - gam_c)
        out.append(dict(causal=causal, strict=strict, decay=decay, kk=kk, low=jnp.where(strict, beta_c * kk * decay, 0.0),
                        a_qk=qk * decay, gm=gm, glast_exp=jnp.exp(glast), elast=elast, q=q, k=k, v=v, beta_c=beta_c,
                        r=jnp.concatenate([beta_c * v, beta_c * k * gm], axis=1), qd=q * gm, kd=k * elast))
    return out


def _unit_lower_inverses_minus_identity(lows):
    ps = [-low for low in lows]
    mis = list(ps)
    for _ in range(5):
        ps = _each(_bdot, ps, ps)
        ts = _each(_bdot, mis, ps)
        mis = [mi + p + t for mi, p, t in zip(mis, ps, ts)]
    return mis


def _apply_inverses(mis, rs, dot=_dot):
    return [r + t for r, t in zip(rs, _each(functools.partial(_bdot, dot=dot), mis, rs))]


def _gdn_fwd(p, conv_w, a_log, dt_bias, norm_g, tag):
    s = p.shape[0]
    t = min(GDN_BLOCK, s)
    nc = t // GDN_CHUNK
    nblk = s // t
    qb, kb_, vb, zb = COL_Q // GDN_QK, COL_Q // GDN_QK + 1, COL_Q // GDN_QK + 2, COL_Z // GDN_QK
    scale = GDN_DK ** -0.5

    def body(q_ref, k_ref, v_ref, hq_ref, hk_ref, hv_ref, z_ref, ba_ref, cw_ref, alog_ref, dtb_ref, ng_ref,
             o_ref, hb_ref, st_ref, m_ref, ext, qn_s, kn_s, vc_s, beta_s, gam_s, state):
        i = pl.program_id(0)

        @pl.when(i == 0)
        def _():
            state[...] = jnp.zeros_like(state)

        _, act, rq, rk = _gdn_conv_silu_norm(q_ref, k_ref, v_ref, hq_ref, hk_ref, hv_ref, cw_ref, ext, i == 0)
        for h in range(GDN_HEADS):
            sl = slice(h * GDN_DK, (h + 1) * GDN_DK)
            qn_s[:, sl] = act[:, sl] * (rq[h] * scale)
            kn_s[:, sl] = act[:, GDN_QK + h * GDN_DK:GDN_QK + (h + 1) * GDN_DK] * rk[h]
        vc_s[...] = act[:, 2 * GDN_QK:]
        beta, g = _gdn_gates(ba_ref[...], alog_ref, dtb_ref)
        beta_s[...] = beta
        gam_s[...] = jnp.dot(_chunk_cumsum_matrix(t, False), g, preferred_element_type=F32, precision=lax.Precision.HIGHEST)

        prs = _gdn_prep(_gdn_pair_items(qn_s, kn_s, vc_s, beta_s, gam_s, nc))
        mis = _unit_lower_inverses_minus_identity([pr["low"] for pr in prs])
        xs = _apply_inverses(mis, [pr["r"] for pr in prs])
        heads = range(GDN_HEADS)
        for ci in range(nc):
            rows = slice(ci * GDN_CHUNK, (ci + 1) * GDN_CHUNK)
            pc, xc = prs[ci * GDN_HEADS:(ci + 1) * GDN_HEADS], xs[ci * GDN_HEADS:(ci + 1) * GDN_HEADS]
            m_ref[rows, :] = jnp.concatenate(mis[ci * GDN_HEADS:(ci + 1) * GDN_HEADS], axis=1)
            sts = [state[h * GDN_DK:(h + 1) * GDN_DK, :] for h in heads]
            for h in heads:
                st_ref[(ci * GDN_HEADS + h) * GDN_DK:(ci * GDN_HEADS + h + 1) * GDN_DK, :] = sts[h]
            w_st = _each(_bdot, [x[:, GDN_DK:] for x in xc], sts)
            q_st = _each(_bdot, [pr["qd"] for pr in pc], sts)
            vns = [x[:, :GDN_DK] - ws for x, ws in zip(xc, w_st)]
            a_vn = _each(_bdot, [pr["a_qk"] for pr in pc], vns)
            k_vn = _each(functools.partial(_bdot, dot=_dot_tn), [pr["kd"] for pr in pc], vns)
            for h in heads:
                o_ref[rows, h * GDN_DK:(h + 1) * GDN_DK] = q_st[h] + a_vn[h]
                state[h * GDN_DK:(h + 1) * GDN_DK, :] = sts[h] * pc[h]["glast_exp"] + k_vn[h]

        z = z_ref[...]
        for h in range(GDN_HEADS):
            sl = slice(h * GDN_DK, (h + 1) * GDN_DK)
            o = o_ref[:, sl]
            on = o * lax.rsqrt(jnp.mean(o * o, axis=-1, keepdims=True) + RMS_EPS)
            hb_ref[:, sl] = (on * ng_ref[...] * _silu(z[:, sl])).astype(BF16)

    col = lambda cb: pl.BlockSpec((t, GDN_QK), lambda i, cb=cb: (i, cb))
    halo = lambda cb: _prev_halo_spec(t, GDN_HALO, GDN_QK, cb)
    vec = _const_spec((1, LANES))
    return pl.pallas_call(
        body, name="gdn_fwd" + tag, grid=(nblk,),
        in_specs=[col(qb), col(kb_), col(vb), halo(qb), halo(kb_), halo(vb), col(zb),
                  _row_spec(t, LANES, COL_BA // LANES), _const_spec((GDN_CONV, 3 * GDN_QK)), vec, vec, vec],
        out_specs=[_row_spec(t, GDN_QK), _row_spec(t, GDN_QK),
                   pl.BlockSpec((nc * GDN_HEADS * GDN_DK, GDN_DK), lambda i: (i, 0)),
                   _row_spec(t, GDN_HEADS * GDN_CHUNK)],
        out_shape=[jax.ShapeDtypeStruct((s, GDN_QK), F32), jax.ShapeDtypeStruct((s, GDN_QK), BF16),
                   jax.ShapeDtypeStruct((s // GDN_CHUNK * GDN_HEADS * GDN_DK, GDN_DK), F32),
                   jax.ShapeDtypeStruct((s, GDN_HEADS * GDN_CHUNK), F32)],
        scratch_shapes=[pltpu.VMEM((GDN_HALO + t, 3 * GDN_QK), F32), pltpu.VMEM((t, GDN_QK), F32),
                        pltpu.VMEM((t, GDN_QK), F32), pltpu.VMEM((t, GDN_QK), F32),
                        pltpu.VMEM((t, LANES), F32), pltpu.VMEM((t, LANES), F32),
                        pltpu.VMEM((GDN_HEADS * GDN_DK, GDN_DK), F32)],
        compiler_params=_params(("arbitrary",)),
    )(p, p, p, p, p, p, p, p, conv_w, a_log, dt_bias, norm_g)


def _lane_place(col, lane_idx, shape):
    lane = lax.broadcasted_iota(jnp.int32, shape, 1)
    return jnp.where(lane == lane_idx, col, 0.0)


def _gdn_bwd(dhb, p, o, states, minv, conv_w, a_log, dt_bias, norm_g, tag):
    s = p.shape[0]
    t = min(GDN_BLOCK, s)
    nc = t // GDN_CHUNK
    nblk = s // t
    qb, kb_, vb, zb = COL_Q // GDN_QK, COL_Q // GDN_QK + 1, COL_Q // GDN_QK + 2, COL_Z // GDN_QK
    scale = GDN_DK ** -0.5
    c = GDN_CHUNK

    def body(dhb_ref, q_ref, k_ref, v_ref, hq_ref, hk_ref, hv_ref, z_ref, ba_ref, o_ref, st_ref, m_ref,
             cw_ref, alog_ref, dtb_ref, ng_ref,
             dpre_ref, dz_ref, dba_ref, dng_ref, dalog_ref, ddtb_ref,
             ext, qn_s, kn_s, vc_s, beta_s, gam_s, do_s, dqn_s, dkn_s, dvc_s, dgam_s, dbeta_s, dstate):
        i = pl.program_id(0)

        @pl.when(i == 0)
        def _():
            dstate[...] = jnp.zeros_like(dstate)
            dng_ref[...] = jnp.zeros_like(dng_ref)
            dalog_ref[...] = jnp.zeros_like(dalog_ref)
            ddtb_ref[...] = jnp.zeros_like(ddtb_ref)

        pre, act, rq, rk = _gdn_conv_silu_norm(q_ref, k_ref, v_ref, hq_ref, hk_ref, hv_ref, cw_ref, ext, i == nblk - 1)
        for h in range(GDN_HEADS):
            sl = slice(h * GDN_DK, (h + 1) * GDN_DK)
            qn_s[:, sl] = act[:, sl] * (rq[h] * scale)
            kn_s[:, sl] = act[:, GDN_QK + h * GDN_DK:GDN_QK + (h + 1) * GDN_DK] * rk[h]
        vc_s[...] = act[:, 2 * GDN_QK:]
        ba = ba_ref[...]
        beta, g = _gdn_gates(ba, alog_ref, dtb_ref)
        beta_s[...] = beta
        gam_s[...] = jnp.dot(_chunk_cumsum_matrix(t, False), g, preferred_element_type=F32, precision=lax.Precision.HIGHEST)

        z = z_ref[...]
        dhb = dhb_ref[...]
        dng = jnp.zeros((1, GDN_DK), F32)
        for h in range(GDN_HEADS):
            sl = slice(h * GDN_DK, (h + 1) * GDN_DK)
            oh = o_ref[:, sl]
            r = lax.rsqrt(jnp.mean(oh * oh, axis=-1, keepdims=True) + RMS_EPS)
            on = oh * r
            sz = _silu(z[:, sl])
            dyh = dhb[:, sl]
            dng = dng + _colsum(dyh * on * sz)
            dz_ref[:, sl] = (dyh * on * ng_ref[...] * _dsilu(z[:, sl])).astype(BF16)
            don = dyh * ng_ref[...] * sz
            do_s[:, sl] = r * (don - on * jnp.mean(don * on, axis=-1, keepdims=True))
        dng_ref[...] += dng

        heads = range(GDN_HEADS)
        npairs = nc * GDN_HEADS
        tn = functools.partial(_bdot, dot=_dot_tn)
        nt = functools.partial(_bdot, dot=_dot_nt)
        rsum = lambda a: jnp.sum(a, axis=-1, keepdims=True)
        left = lambda a: a[:, :GDN_DK]
        right = lambda a: a[:, GDN_DK:]

        prs = _gdn_prep(_gdn_pair_items(qn_s, kn_s, vc_s, beta_s, gam_s, nc))
        mis = [m_ref[(n // GDN_HEADS) * c:(n // GDN_HEADS + 1) * c, (n % GDN_HEADS) * c:(n % GDN_HEADS + 1) * c] for n in range(npairs)]
        xs = _apply_inverses(mis, [pr["r"] for pr in prs])
        sts = [st_ref[n * GDN_DK:(n + 1) * GDN_DK, :] for n in range(npairs)]
        dos = [do_s[(n // GDN_HEADS) * c:(n // GDN_HEADS + 1) * c, (n % GDN_HEADS) * GDN_DK:(n % GDN_HEADS + 1) * GDN_DK] for n in range(npairs)]
        w_st = _each(_bdot, [right(x) for x in xs], sts)
        vns = [left(x) - ws for x, ws in zip(xs, w_st)]
        at_do = _each(tn, [pr["a_qk"] for pr in prs], dos)
        dqds = _each(nt, dos, sts)
        d_as = [jnp.where(pr["causal"], a, 0.0) for pr, a in zip(prs, _each(nt, dos, vns))]
        qt_do = _each(tn, [pr["qd"] for pr in prs], dos)

        dvns, dkds, ds_st = [None] * npairs, [None] * npairs, [None] * npairs
        for ci in reversed(range(nc)):
            ids = [ci * GDN_HEADS + h for h in heads]
            dss = [dstate[h * GDN_DK:(h + 1) * GDN_DK, :] for h in heads]
            kd_ds = _each(_bdot, [prs[n]["kd"] for n in ids], dss)
            vn_ds = _each(nt, [vns[n] for n in ids], dss)
            for h, n in enumerate(ids):
                dvns[n] = kd_ds[h] + at_do[n]
                dkds[n] = vn_ds[h]
                ds_st[n] = jnp.sum(rsum(dss[h] * sts[n]), axis=0, keepdims=True)
            wt_dvn = _each(tn, [right(xs[n]) for n in ids], [dvns[n] for n in ids])
            for h, n in enumerate(ids):
                dstate[h * GDN_DK:(h + 1) * GDN_DK, :] = dss[h] * prs[n]["glast_exp"] + qt_do[n] - wt_dvn[h]

        dws = [-a for a in _each(nt, dvns, sts)]
        d_rs = _apply_inverses(mis, [jnp.concatenate([dvn, dw], axis=1) for dvn, dw in zip(dvns, dws)], _dot_tn)
        d_ls = [jnp.where(pr["strict"], -a, 0.0) for pr, a in zip(prs, _each(nt, d_rs, xs))]
        d_l_kds = [d_l * pr["kk"] * pr["decay"] for d_l, pr in zip(d_ls, prs)]
        dkks = [d_l * pr["beta_c"] * pr["decay"] for d_l, pr in zip(d_ls, prs)]
        dqks = [d_a * pr["decay"] for d_a, pr in zip(d_as, prs)]
        ks, qs = [pr["k"] for pr in prs], [pr["q"] for pr in prs]
        dk1, dk2, dk3 = _each(_bdot, dkks, ks), _each(tn, dkks, ks), _each(tn, dqks, qs)
        dq1 = _each(_bdot, dqks, ks)
        rowi = lax.broadcasted_iota(jnp.int32, (c, 1), 0)
        for ci in range(nc):
            rows = slice(ci * c, (ci + 1) * c)
            dgam_blk = jnp.zeros((c, LANES), F32)
            dbeta_blk = jnp.zeros((c, LANES), F32)
            for h in heads:
                n = ci * GDN_HEADS + h
                sl = slice(h * GDN_DK, (h + 1) * GDN_DK)
                pr, d_r = prs[n], d_rs[n]
                d_ru, d_rw = left(d_r), right(d_r)
                gmat = pr["beta_c"] * d_l_kds[n] + d_as[n] * pr["a_qk"]
                dkd_kd = rsum(dkds[n] * pr["kd"])
                dgam = rsum(gmat) - rsum(gmat.T) + rsum(d_rw * right(pr["r"])) + rsum(dqds[n] * pr["qd"]) - dkd_kd
                dglast = jnp.sum(dkd_kd, axis=0, keepdims=True) + ds_st[n] * pr["glast_exp"]
                dgam = dgam + jnp.where(rowi == c - 1, dglast, 0.0)
                dbeta = rsum(d_l_kds[n]) + rsum(d_ru * pr["v"]) + rsum(d_rw * pr["k"]) * pr["gm"]
                dvc_s[rows, sl] = pr["beta_c"] * d_ru
                dkn_s[rows, sl] = dk1[n] + dk2[n] + dk3[n] + d_rw * (pr["beta_c"] * pr["gm"]) + dkds[n] * pr["elast"]
                dqn_s[rows, sl] = dq1[n] + dqds[n] * pr["gm"]
                dgam_blk = dgam_blk + _lane_place(dgam, GDN_HEADS + h, (c, LANES))
                dbeta_blk = dbeta_blk + _lane_place(dbeta, h, (c, LANES))
            dgam_s[rows, :] = dgam_blk
            dbeta_s[rows, :] = dbeta_blk

        dg = jnp.dot(_chunk_cumsum_matrix(t, True), dgam_s[...], preferred_element_type=F32, precision=lax.Precision.HIGHEST)
        lane = lax.broadcasted_iota(jnp.int32, (t, LANES), 1)
        g_lanes = (lane >= GDN_HEADS) & (lane < 2 * GDN_HEADS)
        da_logit = jnp.where(g_lanes, dg * (-jnp.exp(alog_ref[...])) * _sigmoid(ba + dtb_ref[...]), 0.0)
        db_logit = jnp.where(lane < GDN_HEADS, dbeta_s[...] * beta * (1.0 - beta), 0.0)
        dba_ref[...] = (da_logit + db_logit).astype(BF16)
        dalog_ref[...] += _colsum(dg * g)
        ddtb_ref[...] += _colsum(da_logit)

        dact = []
        for n, (dn_s, rr, sc) in enumerate(((dqn_s, rq, scale), (dkn_s, rk, 1.0))):
            for h in range(GDN_HEADS):
                sl = slice(h * GDN_DK, (h + 1) * GDN_DK)
                y = act[:, n * GDN_QK + h * GDN_DK:n * GDN_QK + (h + 1) * GDN_DK] * rr[h]
                dy = dn_s[:, sl] * sc
                dact.append(rr[h] * (dy - y * jnp.sum(dy * y, axis=-1, keepdims=True)))
        dact.append(dvc_s[...])
        dpre_ref[...] = jnp.concatenate(dact, axis=1) * _dsilu(pre)

    rb = lambda i: nblk - 1 - i
    per = t // GDN_HALO
    col = lambda cb, wd=GDN_QK: pl.BlockSpec((t, wd), lambda i, cb=cb: (rb(i), cb))
    halo = lambda cb: pl.BlockSpec((GDN_HALO, GDN_QK), lambda i, cb=cb: (jnp.maximum(rb(i) * per - 1, 0), cb))
    vec = _const_spec((1, LANES))
    return pl.pallas_call(
        body, name="gdn_bwd" + tag, grid=(nblk,),
        in_specs=[col(0), col(qb), col(kb_), col(vb), halo(qb), halo(kb_), halo(vb), col(zb),
                  pl.BlockSpec((t, LANES), lambda i: (rb(i), COL_BA // LANES)), col(0),
                  pl.BlockSpec((nc * GDN_HEADS * GDN_DK, GDN_DK), lambda i: (rb(i), 0)),
                  pl.BlockSpec((t, GDN_HEADS * c), lambda i: (rb(i), 0)),
                  _const_spec((GDN_CONV, 3 * GDN_QK)), vec, vec, vec],
        out_specs=[pl.BlockSpec((t, 3 * GDN_QK), lambda i: (rb(i), 0)), col(0), pl.BlockSpec((t, LANES), lambda i: (rb(i), 0)),
                   vec, vec, vec],
        out_shape=[jax.ShapeDtypeStruct((s, 3 * GDN_QK), F32), jax.ShapeDtypeStruct((s, GDN_QK), BF16),
                   jax.ShapeDtypeStruct((s, LANES), BF16),
                   jax.ShapeDtypeStruct((1, LANES), F32), jax.ShapeDtypeStruct((1, LANES), F32), jax.ShapeDtypeStruct((1, LANES), F32)],
        scratch_shapes=[pltpu.VMEM((GDN_HALO + t, 3 * GDN_QK), F32)] + [pltpu.VMEM((t, GDN_QK), F32)] * 3
                       + [pltpu.VMEM((t, LANES), F32)] * 2 + [pltpu.VMEM((t, GDN_QK), F32)] * 4
                       + [pltpu.VMEM((t, LANES), F32)] * 2 + [pltpu.VMEM((GDN_HEADS * GDN_DK, GDN_DK), F32)],
        compiler_params=_params(("arbitrary",)),
    )(dhb, p, p, p, p, p, p, p, p, o, states, minv, conv_w, a_log, dt_bias, norm_g)


def _zero_at_first_step(*refs):
    @pl.when(pl.program_id(0) == 0)
    def _():
        for r in refs:
            r[...] = jnp.zeros_like(r)


def _ln_bwd_call(dy, x_in, g, *, name):
    s, d = dy.shape
    t = min(ROW_TILE, s)

    def body(dy_ref, x_ref, g_ref, dx_ref, dxbf_ref, dg_ref, db_ref, ds_ref):
        _zero_at_first_step(dg_ref, db_ref, ds_ref)
        dy = dy_ref[...]
        xhat, rstd = _ln_stats(x_ref[...])
        dx = _ln_bwd(dy, xhat, rstd, g_ref[...])
        dx_ref[...] = dx
        dxbf_ref[...] = dx.astype(BF16)
        dg_ref[...] += _colsum(dy * xhat)
        db_ref[...] += _colsum(dy)
        ds_ref[...] += _colsum(dx)

    vec = _const_spec((1, d))
    return pl.pallas_call(
        body, name=name, grid=(s // t,),
        in_specs=[_row_spec(t, d), _row_spec(t, d), vec],
        out_specs=[_row_spec(t, d), _row_spec(t, d), vec, vec, vec],
        out_shape=[jax.ShapeDtypeStruct((s, d), F32), jax.ShapeDtypeStruct((s, d), BF16)] + [jax.ShapeDtypeStruct((1, d), F32)] * 3,
        compiler_params=_params(("arbitrary",)),
    )(dy, x_in, g)


def _ff2_bwd(dr2_bf, w_ff2_t, hpre, tag):
    s = dr2_bf.shape[0]
    tm = min(1024, s)
    tn = 1024

    def body(d_ref, w_ref, hp_ref, o_ref, db_ref):
        @pl.when(pl.program_id(1) == 0)
        def _():
            db_ref[...] = jnp.zeros_like(db_ref)

        dh = _dot(d_ref[...], w_ref[...]) * (2.0 * jnp.maximum(hp_ref[...], 0.0))
        o_ref[...] = dh.astype(BF16)
        db_ref[...] += _colsum(dh)

    return pl.pallas_call(
        body, name="ff2_bwd" + tag, grid=(D_FF // tn, s // tm),
        in_specs=[pl.BlockSpec((tm, D_MODEL), lambda j, i: (i, 0)), pl.BlockSpec((D_MODEL, tn), lambda j, i: (0, j)),
                  pl.BlockSpec((tm, tn), lambda j, i: (i, j))],
        out_specs=[pl.BlockSpec((tm, tn), lambda j, i: (i, j)), pl.BlockSpec((1, tn), lambda j, i: (0, j))],
        out_shape=[jax.ShapeDtypeStruct((s, D_FF), BF16), jax.ShapeDtypeStruct((1, D_FF), F32)],
        compiler_params=_params(("parallel", "arbitrary")),
    )(dr2_bf, w_ff2_t, hpre)


def _ff1_bwd_ln(dhpre_bf, w_ff1_t, dr2, r1, ln1_g, tag):
    s = dr2.shape[0]
    t = min(ROW_TILE, s)

    def body(dh_ref, w_ref, dr2_ref, r1_ref, g_ref, dr_ref, drbf_ref, dg_ref, db_ref):
        _zero_at_first_step(dg_ref, db_ref)
        dx1 = DN_ALPHA * dr2_ref[...] + _dot(dh_ref[...], w_ref[...])
        xhat, rstd = _ln_stats(r1_ref[...])
        dr = _ln_bwd(dx1, xhat, rstd, g_ref[...])
        dr_ref[...] = dr
        drbf_ref[...] = dr.astype(BF16)
        dg_ref[...] += _colsum(dx1 * xhat)
        db_ref[...] += _colsum(dx1)

    vec = _const_spec((1, D_MODEL))
    return pl.pallas_call(
        body, name="ff1_bwd_ln" + tag, grid=(s // t,),
        in_specs=[_row_spec(t, D_FF), _const_spec((D_FF, D_MODEL)), _row_spec(t, D_MODEL), _row_spec(t, D_MODEL), vec],
        out_specs=[_row_spec(t, D_MODEL), _row_spec(t, D_MODEL), vec, vec],
        out_shape=[jax.ShapeDtypeStruct((s, D_MODEL), F32), jax.ShapeDtypeStruct((s, D_MODEL), BF16),
                   jax.ShapeDtypeStruct((1, D_MODEL), F32), jax.ShapeDtypeStruct((1, D_MODEL), F32)],
        compiler_params=_params(("arbitrary",)),
    )(dhpre_bf, w_ff1_t, dr2, r1, ln1_g)


def _merge_bwd(dr1_bf, w_o_t, y3, p, b_gate, w_pa_t, w_pb_t, w_pc_t, tag):
    s = p.shape[0]
    t = min(ROW_TILE, s)
    gb = COL_GATE // D_MODEL

    def body(dr_ref, wo_ref, y_ref, ga_ref, gb_ref, gc_ref, bg_ref, wa_ref, wb_ref, wc_ref,
             dgate_ref, dy_ref, dha_ref, dhb_ref, dhc_ref, dbg_ref):
        _zero_at_first_step(dbg_ref)
        dm = _dot(dr_ref[...], wo_ref[...])
        for idx, (g_ref, w_ref, dh_ref) in enumerate(((ga_ref, wa_ref, dha_ref), (gb_ref, wb_ref, dhb_ref), (gc_ref, wc_ref, dhc_ref))):
            sl = slice(idx * D_MODEL, (idx + 1) * D_MODEL)
            sg = _sigmoid(g_ref[...] + bg_ref[:, sl])
            dgate = dm * y_ref[:, sl] * sg * (1.0 - sg)
            dgate_ref[:, sl] = dgate.astype(BF16)
            dbg_ref[:, sl] += _colsum(dgate)
            dy = (dm * sg).astype(BF16)
            dy_ref[:, sl] = dy
            dh_ref[...] = _dot(dy, w_ref[...])

    hspec = _row_spec(t, CONV_DIM)
    wspec = _const_spec((D_MODEL, CONV_DIM))
    return pl.pallas_call(
        body, name="merge_bwd" + tag, grid=(s // t,),
        in_specs=[_row_spec(t, D_MODEL), _const_spec((D_MODEL, D_MODEL)), _row_spec(t, 3 * D_MODEL),
                  _row_spec(t, D_MODEL, gb), _row_spec(t, D_MODEL, gb + 1), _row_spec(t, D_MODEL, gb + 2),
                  _const_spec((1, 3 * D_MODEL)), wspec, wspec, wspec],
        out_specs=[_row_spec(t, 3 * D_MODEL), _row_spec(t, 3 * D_MODEL), hspec, hspec, hspec, _const_spec((1, 3 * D_MODEL))],
        out_shape=[jax.ShapeDtypeStruct((s, 3 * D_MODEL), BF16), jax.ShapeDtypeStruct((s, 3 * D_MODEL), BF16)]
                  + [jax.ShapeDtypeStruct((s, CONV_DIM), F32)] * 3 + [jax.ShapeDtypeStruct((1, 3 * D_MODEL), F32)],
        compiler_params=_params(("arbitrary",)),
    )(dr1_bf, w_o_t, y3, p, p, p, b_gate, w_pa_t, w_pb_t, w_pc_t)


def _conv_a_bwd_pre(dha, c, ln_g, ln_b, tag):
    s = c.shape[0]
    t = min(ROW_TILE, s)

    def body(dh_ref, c_ref, g_ref, b_ref, dc_ref, dg_ref, db_ref, ds_ref):
        _zero_at_first_step(dg_ref, db_ref, ds_ref)
        xhat, rstd = _ln_stats(c_ref[...])
        n = xhat * g_ref[...] + b_ref[...]
        dn = dh_ref[...] * _dsilu(n)
        dc = _ln_bwd(dn, xhat, rstd, g_ref[...])
        dc_ref[...] = dc
        dg_ref[...] += _colsum(dn * xhat)
        db_ref[...] += _colsum(dn)
        ds_ref[...] += _colsum(dc)

    vec = _const_spec((1, CONV_DIM))
    return pl.pallas_call(
        body, name="conv_a_bwd_pre" + tag, grid=(s // t,),
        in_specs=[_row_spec(t, CONV_DIM), _row_spec(t, CONV_DIM), vec, vec],
        out_specs=[_row_spec(t, CONV_DIM), vec, vec, vec],
        out_shape=[jax.ShapeDtypeStruct((s, CONV_DIM), F32)] + [jax.ShapeDtypeStruct((1, CONV_DIM), F32)] * 3,
        compiler_params=_params(("arbitrary",)),
    )(dha, c, ln_g, ln_b)


def _dwconv_bwd(dy, x, w, *, width, halo, x_col_block, glu_p, name):
    s, ctot = dy.shape
    ct = CONV_DIM
    t = min(ROW_TILE, s)
    nblk = s // t
    glu = glu_p is not None

    def body(*refs):
        if glu:
            dy_ref, dyh_ref, x_ref, xh_ref, w_ref, a_ref, dx_ref, dw_ref, dye, xe, dwacc = refs
        else:
            dy_ref, dyh_ref, x_ref, xh_ref, w_ref, dx_ref, dw_ref, dye, xe, dwacc = refs
        i = pl.program_id(1)

        @pl.when(i == 0)
        def _():
            dw_ref[...] = jnp.zeros_like(dw_ref)

        dye[0:t, :] = dy_ref[...]
        dye[t:t + halo, :] = jnp.where(i == nblk - 1, 0.0, dyh_ref[...])
        xe[0:halo, :] = jnp.where(i == 0, 0.0, xh_ref[...])
        xe[halo:halo + t, :] = x_ref[...]
        dwacc[...] = jnp.zeros_like(dwacc)
        dx_groups = _tap_groups([width - 1 - j for j in range(width)])
        dw_groups = _tap_groups([halo - (width - 1) + j for j in range(width)])
        for r0 in range(0, t, CONV_ROWS):
            rows = slice(r0, r0 + CONV_ROWS)
            dx = _tap_sum(dye, w_ref, dx_groups, r0)
            if glu:
                a = a_ref[rows, :]
                a1 = a[:, :ct]
                sg = _sigmoid(a[:, ct:])
                dx_ref[rows, :ct] = (dx * sg).astype(BF16)
                dx_ref[rows, ct:] = (dx * a1 * sg * (1.0 - sg)).astype(BF16)
            else:
                dx_ref[rows, :] = dx.astype(BF16)
            dyt = dy_ref[rows, :]
            for (_, _, _, taps), win in zip(dw_groups, _tap_windows(xe, dw_groups, r0)):
                for j, a in taps:
                    prod = dyt * win[a:a + CONV_ROWS]
                    part = prod[0:8]
                    for q in range(8, CONV_ROWS, 8):
                        part = part + prod[q:q + 8]
                    dwacc[8 * j:8 * j + 8, :] += part
        for j in range(width):
            dw_ref[j:j + 1, :] += _colsum(dwacc[8 * j:8 * j + 8, :])

    per = t // halo
    in_specs = [pl.BlockSpec((t, ct), lambda cb, i: (i, cb)),
                pl.BlockSpec((halo, ct), lambda cb, i: (jnp.minimum((i + 1) * per, nblk * per - 1), cb)),
                pl.BlockSpec((t, ct), lambda cb, i: (i, cb + x_col_block)),
                pl.BlockSpec((halo, ct), lambda cb, i: (jnp.maximum(i * per - 1, 0), cb + x_col_block)),
                pl.BlockSpec((width, ct), lambda cb, i: (0, cb))]
    args = [dy, dy, x, x, w]
    out_cols = ctot
    if glu:
        in_specs.append(pl.BlockSpec((t, 2 * ct), lambda cb, i: (i, COL_A // (2 * ct))))
        args.append(glu_p)
        out_cols = 2 * ct
    ocol = 2 * ct if glu else ct
    return pl.pallas_call(
        body, name=name, grid=(ctot // ct, nblk), in_specs=in_specs,
        out_specs=[pl.BlockSpec((t, ocol), lambda cb, i: (i, cb)), pl.BlockSpec((width, ct), lambda cb, i: (0, cb))],
        out_shape=[jax.ShapeDtypeStruct((s, out_cols), BF16), jax.ShapeDtypeStruct((width, ctot), F32)],
        scratch_shapes=[pltpu.VMEM((t + halo, ct), F32), pltpu.VMEM((halo + t, ct), F32), pltpu.VMEM((8 * width, ct), F32)],
        compiler_params=_params(("parallel", "arbitrary")),
    )(*args)


def _sgu_bwd(dhc, p, ln_g, ln_b, w_s, b_s_t, tag):
    s = p.shape[0]
    t = min(ROW_TILE, s)
    cs = SGU_CHUNK

    def body(dh_ref, uv_ref, g_ref, b_ref, ws_ref, bst_ref, duv_ref, dg_ref, db_ref, dws_ref, dbs_ref):
        _zero_at_first_step(dg_ref, db_ref, dws_ref, dbs_ref)
        uv = uv_ref[...]
        u_raw, v_raw = uv[:, :SGU_DIM], uv[:, SGU_DIM:]
        u = _gelu(u_raw)
        xhat, rstd = _ln_stats(_gelu(v_raw))
        vn = xhat * g_ref[...] + b_ref[...]
        mixed = _sgu_mix(vn, ws_ref, bst_ref, t)
        dh = dh_ref[...]
        duv_ref[:, :SGU_DIM] = (dh * mixed * _dgelu(u_raw)).astype(BF16)
        dmix = dh * u
        row = lax.broadcasted_iota(jnp.int32, (cs, cs), 0)
        col = lax.broadcasted_iota(jnp.int32, (cs, cs), 1)
        dbs = jnp.zeros((cs, LANES), F32)
        chunks = []
        for g in range(SGU_GROUPS):
            wg = jnp.where(row >= col, ws_ref[g], 0.0).astype(BF16)
            dwg = jnp.zeros((cs, cs), F32)
            parts = []
            for ci in range(t // cs):
                rs = slice(ci * cs, (ci + 1) * cs)
                cl = slice(g * SGU_GROUP_DIM, (g + 1) * SGU_GROUP_DIM)
                dm = dmix[rs, cl]
                dmb = dm.astype(BF16)
                dwg = dwg + _dot_nt(dmb, vn[rs, cl].astype(BF16))
                dbs = dbs + _lane_place(jnp.sum(dm, axis=-1, keepdims=True), g, (cs, LANES))
                parts.append(_dot_tn(wg, dmb))
            dws_ref[g] += jnp.where(row >= col, dwg, 0.0)
            chunks.append(jnp.concatenate(parts, axis=0))
        dbs_ref[...] += dbs
        dvn = jnp.concatenate(chunks, axis=1)
        dvv = _ln_bwd(dvn, xhat, rstd, g_ref[...])
        duv_ref[:, SGU_DIM:] = (dvv * _dgelu(v_raw)).astype(BF16)
        dg_ref[...] += _colsum(dvn * xhat)
        db_ref[...] += _colsum(dvn)

    vec = _const_spec((1, SGU_DIM))
    wss = _const_spec((SGU_GROUPS, cs, cs))
    return pl.pallas_call(
        body, name="sgu_bwd" + tag, grid=(s // t,),
        in_specs=[_row_spec(t, SGU_DIM), _row_spec(t, 2 * SGU_DIM, COL_UV // (2 * SGU_DIM)), vec, vec, wss, _const_spec((cs, LANES))],
        out_specs=[_row_spec(t, 2 * SGU_DIM), vec, vec, wss, _const_spec((cs, LANES))],
        out_shape=[jax.ShapeDtypeStruct((s, 2 * SGU_DIM), BF16), jax.ShapeDtypeStruct((1, SGU_DIM), F32),
                   jax.ShapeDtypeStruct((1, SGU_DIM), F32), jax.ShapeDtypeStruct((SGU_GROUPS, cs, cs), F32),
                   jax.ShapeDtypeStruct((cs, LANES), F32)],
        compiler_params=_params(("arbitrary",)),
    )(dhc, p, ln_g, ln_b, w_s, b_s_t)


def _reorder_proj_cols(w):
    pad = jnp.zeros(w.shape[:-1] + (P_COLS - PROJ_COLS,), w.dtype)
    return jnp.concatenate([w[..., :3072], w[..., 3080:PROJ_COLS], w[..., 3072:3080], pad], axis=-1)


def _restore_proj_cols(g):
    return jnp.concatenate([g[..., :3072], g[..., COL_BA:COL_BA + 8], g[..., 3072:COL_BA]], axis=-1)


def _pad_lanes(v, offset):
    return jnp.pad(v, (offset, LANES - offset - v.shape[0]))[None, :]


def _proj_weights(w_in_l):
    w_all = _reorder_proj_cols(w_in_l)
    return dict(w_all=w_all, w_all_t=w_all.T)


def _rest_weights(l, full, rep):
    row = lambda v: v[l][None, :]
    return dict(
        conv_w=full["conv_dw_w"], conv_b=row(rep["conv_dw_b"]), conv_ln_g=row(rep["conv_ln_g"]), conv_ln_b=row(rep["conv_ln_b"]),
        w_pa=full["w_pa"], w_pb=full["w_pb"], w_pc=full["w_pc"],
        w_pa_t=full["w_pa"].T, w_pb_t=full["w_pb"].T, w_pc_t=full["w_pc"].T,
        gdn_cw=jnp.concatenate([full["gdn_conv_q"], full["gdn_conv_k"], full["gdn_conv_v"]], axis=-1),
        a_log=_pad_lanes(rep["gdn_a_log"][l], GDN_HEADS), dt_bias=_pad_lanes(rep["gdn_dt_bias"][l], GDN_HEADS),
        norm_g=row(rep["gdn_norm_g"]),
        sgu_ln_g=row(rep["sgu_ln_g"]), sgu_ln_b=row(rep["sgu_ln_b"]), sgu_w_s=rep["sgu_w_s"][l],
        sgu_b_s_t=jnp.pad(rep["sgu_b_s"][l].T, ((0, 0), (0, LANES - SGU_GROUPS))),
        b_gate=row(rep["b_gate"]),
        w_o=full["w_o"], w_o_t=full["w_o"].T, ln1_g=row(rep["ln1_g"]), ln1_b=row(rep["ln1_b"]),
        w_ff1=full["w_ff1"], w_ff1_t=full["w_ff1"].T, b_ff1=row(rep["b_ff1"]),
        w_ff2=full["w_ff2"], w_ff2_t=full["w_ff2"].T, b_ff2=row(rep["b_ff2"]),
        ln2_g=row(rep["ln2_g"]), ln2_b=row(rep["ln2_b"]),
    )


def _layer_fwd(x, x_bf, w_proj, rest_of, tag, proj_exchange=None, ff1_exchange=None):
    s = x.shape[0]
    p = _matmul(x_bf, w_proj["w_all"], name="proj_fwd" + tag, tm=min(ROW_TILE, s), tn=P_TILE, tk=D_MODEL, exchange=proj_exchange)
    got_proj = None
    if proj_exchange is not None:
        p, got_proj = p
    w = dict(w_proj, **rest_of(got_proj))
    h_glu, c, ha = _conv_a_fwd(p, w["conv_w"], w["conv_b"], w["conv_ln_g"], w["conv_ln_b"], tag)
    o, hb, states, minv = _gdn_fwd(p, w["gdn_cw"], w["a_log"], w["dt_bias"], w["norm_g"], tag)
    hc = _sgu_fwd(p, w["sgu_ln_g"], w["sgu_ln_b"], w["sgu_w_s"], w["sgu_b_s_t"], tag)
    y3, merged = _merge_fwd(p, ha, hb, hc, w["w_pa"], w["w_pb"], w["w_pc"], w["b_gate"], tag)
    r1, x1, x1_bf = _matmul_res_ln(merged, w["w_o"], jnp.zeros((1, D_MODEL), F32), x, w["ln1_g"], w["ln1_b"], name="o_res_ln" + tag)
    hpre, h_bf, got_ff1 = _ff1_fwd(x1_bf, w["w_ff1"], w["b_ff1"], tag, exchange=ff1_exchange)
    r2, x2, x2_bf = _matmul_res_ln(h_bf, w["w_ff2"], w["b_ff2"], x1, w["ln2_g"], w["ln2_b"], name="ff2_res_ln" + tag)
    saved = dict(x=x, x_bf=x_bf, p=p, h_glu=h_glu, c=c, ha=ha, o=o, hb=hb, states=states, minv=minv, hc=hc, y3=y3,
                 merged=merged, r1=r1, x1=x1, x1_bf=x1_bf, hpre=hpre, h_bf=h_bf, r2=r2)
    return x2, x2_bf, saved, w, got_ff1


def _layer_bwd(dx2, w, sv, tag, dw_proj_exchange=None, proj_exchange_of=None):
    s = dx2.shape[0]
    ts = min(1024, s)
    p = sv["p"]
    dr2, dr2_bf, d_ln2_g, d_ln2_b, d_b_ff2 = _ln_bwd_call(dx2, sv["r2"], w["ln2_g"], name="ln2_bwd" + tag)
    dhpre_bf, d_b_ff1 = _ff2_bwd(dr2_bf, w["w_ff2_t"], sv["hpre"], tag)
    d_w_ff2 = _matmul_tn(sv["h_bf"], dr2_bf, name="dw_ff2" + tag, ka=D_FF, tka=1024, tn=1024, ts=ts)
    d_w_ff1 = _matmul_tn(sv["x1_bf"], dhpre_bf, name="dw_ff1" + tag, ka=D_MODEL, tka=1024, tn=1024, ts=ts)
    dr1, dr1_bf, d_ln1_g, d_ln1_b = _ff1_bwd_ln(dhpre_bf, w["w_ff1_t"], dr2, sv["r1"], w["ln1_g"], tag)
    d_w_o = _matmul_tn(sv["merged"], dr1_bf, name="dw_o" + tag, ka=D_MODEL, tka=1024, tn=1024, ts=ts)
    dgate_bf, dy3_bf, dha, dhb, dhc, d_b_gate = _merge_bwd(dr1_bf, w["w_o_t"], sv["y3"], p, w["b_gate"],
                                                          w["w_pa_t"], w["w_pb_t"], w["w_pc_t"], tag)
    d_w_p = [_matmul_tn(h, dy3_bf, name=f"dw_p{n}" + tag, ka=CONV_DIM, tka=CONV_DIM, tn=1024, ts=ts, n=D_MODEL, b_col_block=n)
             for n, h in enumerate((sv["ha"], sv["hb"], sv["hc"]))]
    dc, d_conv_ln_g, d_conv_ln_b, d_conv_b = _conv_a_bwd_pre(dha, sv["c"], w["conv_ln_g"], w["conv_ln_b"], tag)
    da_bf, d_conv_w = _dwconv_bwd(dc, sv["h_glu"], w["conv_w"], width=CONV_WIDTH, halo=CONV_HALO, x_col_block=0,
                                  glu_p=p, name="conv_a_bwd" + tag)
    duv_bf, d_sgu_ln_g, d_sgu_ln_b, d_sgu_w_s, d_sgu_b_s_t = _sgu_bwd(dhc, p, w["sgu_ln_g"], w["sgu_ln_b"], w["sgu_w_s"], w["sgu_b_s_t"], tag)
    dpre, dz_bf, dba_bf, d_norm_g, d_a_log, d_dt_bias = _gdn_bwd(dhb, p, sv["o"], sv["states"], sv["minv"], w["gdn_cw"],
                                                                 w["a_log"], w["dt_bias"], w["norm_g"], tag)
    dqkv_bf, d_gdn_cw = _dwconv_bwd(dpre, p, w["gdn_cw"], width=GDN_CONV, halo=GDN_HALO, x_col_block=COL_Q // CONV_DIM,
                                    glu_p=None, name="gdn_conv_bwd" + tag)
    dp_bf = jnp.concatenate([da_bf, dqkv_bf, dz_bf, duv_bf, dgate_bf, dba_bf], axis=1)
    d_w_all = _matmul_tn(sv["x_bf"], dp_bf, name="dw_proj" + tag, ka=D_MODEL, tka=512, tn=P_TILE, ts=ts, exchange=dw_proj_exchange)
    got_dw = None
    if dw_proj_exchange is not None:
        d_w_all, got_dw = d_w_all
    grads = dict(
        w_in=_restore_proj_cols(d_w_all), b_gate=d_b_gate[0], conv_dw_w=d_conv_w, conv_dw_b=d_conv_b[0],
        conv_ln_g=d_conv_ln_g[0], conv_ln_b=d_conv_ln_b[0], w_pa=d_w_p[0],
        gdn_conv_q=d_gdn_cw[:, :GDN_QK], gdn_conv_k=d_gdn_cw[:, GDN_QK:2 * GDN_QK], gdn_conv_v=d_gdn_cw[:, 2 * GDN_QK:],
        gdn_a_log=d_a_log[0, GDN_HEADS:2 * GDN_HEADS], gdn_dt_bias=d_dt_bias[0, GDN_HEADS:2 * GDN_HEADS], gdn_norm_g=d_norm_g[0],
        w_pb=d_w_p[1], sgu_ln_g=d_sgu_ln_g[0], sgu_ln_b=d_sgu_ln_b[0], sgu_w_s=d_sgu_w_s, sgu_b_s=d_sgu_b_s_t[:, :SGU_GROUPS].T,
        w_pc=d_w_p[2], w_o=d_w_o, ln1_g=d_ln1_g[0], ln1_b=d_ln1_b[0], w_ff1=d_w_ff1, b_ff1=d_b_ff1[0],
        w_ff2=d_w_ff2, b_ff2=d_b_ff2[0], ln2_g=d_ln2_g[0], ln2_b=d_ln2_b[0],
    )
    proj_exchange = None if proj_exchange_of is None else proj_exchange_of(grads)
    dx = _matmul(dp_bf, w["w_all_t"], name="proj_bwd" + tag, tm=min(ROW_TILE, s), tn=D_MODEL, tk=P_TILE, add=dr1,
                 add_scale=DN_ALPHA, exchange=proj_exchange)
    got_proj = None
    if proj_exchange is not None:
        dx, got_proj = dx
    return dx, grads, got_dw, got_proj


MESH_AXES = ("x", "y", "c")


def _exchange(arrays, scatter, *, name):
    n = len(arrays)
    ex = _Exchange(arrays, scatter)

    def body(*refs):
        ins, outs, sems = refs[:n], refs[n:2 * n], refs[2 * n:]
        _exchange_start(ex, ins, outs, sems)
        _exchange_wait(ex, ins, outs, sems)

    return pl.pallas_call(
        body, name=name, in_specs=ex.in_specs(), out_specs=ex.in_specs(), out_shape=ex.out_shapes(),
        scratch_shapes=ex.scratch(),
    )(*arrays)


class _Exchange:
    def __init__(self, arrays, scatter):
        self.arrays = list(arrays)
        self.scatter = list(scatter)
        self.n = len(self.arrays)

    def in_specs(self):
        return [pl.BlockSpec(memory_space=pl.ANY)] * self.n

    def out_shapes(self):
        return [jax.ShapeDtypeStruct(a.shape if s else (N_DEV,) + a.shape, a.dtype) for a, s in zip(self.arrays, self.scatter)]

    def scratch(self):
        return [pltpu.SemaphoreType.DMA((self.n, N_DEV - 1)), pltpu.SemaphoreType.DMA((self.n, N_DEV - 1)),
                pltpu.SemaphoreType.DMA((self.n,))]


def _exchange_copies(ex, ins, outs, sems, with_arrivals):
    send_sems, recv_sems, local_sems = sems
    x, y, c = lax.axis_index("x"), lax.axis_index("y"), lax.axis_index("c")
    me = 4 * x + 2 * y + c

    def slot(a, d):
        return ins[a].at[d] if ex.scatter[a] else ins[a]

    local = [pltpu.make_async_copy(slot(a, me), outs[a].at[me], local_sems.at[a]) for a in range(ex.n)]
    remote = []
    for k in range(1, N_DEV):
        px = 1 - x if k & 4 else x
        py = 1 - y if k & 2 else y
        pc = 1 - c if k & 1 else c
        peer = 4 * px + 2 * py + pc
        for a in range(ex.n):
            send = pltpu.make_async_remote_copy(
                src_ref=slot(a, peer), dst_ref=outs[a].at[me], send_sem=send_sems.at[a, k - 1],
                recv_sem=recv_sems.at[a, k - 1], device_id=(px, py, pc), device_id_type=pl.DeviceIdType.MESH)
            arrival = pltpu.make_async_remote_copy(
                src_ref=slot(a, peer), dst_ref=outs[a].at[peer], send_sem=send_sems.at[a, k - 1],
                recv_sem=recv_sems.at[a, k - 1], device_id=(px, py, pc), device_id_type=pl.DeviceIdType.MESH) if with_arrivals else None
            remote.append((send, arrival))
    return local, remote


def _exchange_start(ex, ins, outs, sems):
    local, remote = _exchange_copies(ex, ins, outs, sems, False)
    for cp in local:
        cp.start()
    for send, _ in remote:
        send.start()


def _exchange_wait(ex, ins, outs, sems):
    local, remote = _exchange_copies(ex, ins, outs, sems, True)
    for _, arrival in remote:
        arrival.wait_recv()
    for send, _ in remote:
        send.wait_send()
    for cp in local:
        cp.wait()


def _adamw(w, m, v, g_parts, *, name):
    r, c = w.shape
    tr = 256 if r % 256 == 0 else r
    bc1 = 1.0 - ADAM_B1 ** ADAM_STEP
    bc2 = 1.0 - ADAM_B2 ** ADAM_STEP

    def body(w_ref, m_ref, v_ref, gp_ref, g_ref, d_ref, nm_ref, nv_ref):
        g = gp_ref[0].astype(F32)
        for d in range(1, N_DEV):
            g = g + gp_ref[d].astype(F32)
        nm = ADAM_B1 * m_ref[...] + (1.0 - ADAM_B1) * g
        nv = ADAM_B2 * v_ref[...] + (1.0 - ADAM_B2) * (g * g)
        g_ref[...] = g
        nm_ref[...] = nm
        nv_ref[...] = nv
        d_ref[...] = -ADAM_LR * ((nm / bc1) / (jnp.sqrt(nv / bc2) + ADAM_EPS) + ADAM_WD * w_ref[...])

    spec = pl.BlockSpec((tr, c), lambda i: (i, 0))
    return pl.pallas_call(
        body, name=name, grid=(r // tr,),
        in_specs=[spec, spec, spec, pl.BlockSpec((N_DEV, tr, c), lambda i: (0, i, 0))],
        out_specs=[spec] * 4, out_shape=[jax.ShapeDtypeStruct((r, c), F32)] * 4,
        compiler_params=_params(("parallel",)),
    )(w, m, v, g_parts)


SHARDED = dict(w_in=2, conv_dw_w=2, w_pa=2, gdn_conv_q=2, gdn_conv_k=2, gdn_conv_v=2, w_pb=2, w_pc=2, w_o=1, w_ff1=2, w_ff2=1)
WEIGHTS = ["ln_in_g", "ln_in_b", "w_in", "b_gate", "conv_dw_w", "conv_dw_b", "conv_ln_g", "conv_ln_b", "w_pa", "gdn_conv_q",
           "gdn_conv_k", "gdn_conv_v", "gdn_a_log", "gdn_dt_bias", "gdn_norm_g", "w_pb", "sgu_ln_g", "sgu_ln_b", "sgu_w_s",
           "sgu_b_s", "w_pc", "w_o", "ln1_g", "ln1_b", "w_ff1", "b_ff1", "w_ff2", "b_ff2", "ln2_g", "ln2_b"]
REPLICATED = [n for n in WEIGHTS if n not in SHARDED]
CONV_PACK = ["conv_dw_w", "gdn_conv_q", "gdn_conv_k", "gdn_conv_v"]
PROJ_PACK = ["w_pa", "w_pb", "w_pc"]


def _to_slots(full, axis):
    shp = full.shape
    split = full.reshape(shp[:axis] + (N_DEV, shp[axis] // N_DEV) + shp[axis + 1:])
    return jnp.moveaxis(split, axis, 0)


def _from_slots(slots, axis):
    merged = jnp.moveaxis(slots, 0, axis)
    shp = merged.shape
    return merged.reshape(shp[:axis] + (shp[axis] * shp[axis + 1],) + shp[axis + 2:])


def _pack_rows(arrs):
    rows = []
    for a in arrs:
        flat = a.reshape(-1)
        pad = (-flat.shape[0]) % LANES
        rows.append(jnp.pad(flat, (0, pad)).reshape(-1, LANES))
    out = jnp.concatenate(rows, axis=0)
    return jnp.pad(out, ((0, (-out.shape[0]) % 8), (0, 0)))


def _unpack_rows(packed, shapes):
    out, r = [], 0
    for shp in shapes:
        size = math.prod(shp)
        nrows = -(-size // LANES)
        out.append(packed[r:r + nrows].reshape(-1)[:size].reshape(shp))
        r += nrows
    return out


def kernel(x, ln_in_g, ln_in_b, w_in, b_gate, conv_dw_w, conv_dw_b, conv_ln_g, conv_ln_b, w_pa, gdn_conv_q, gdn_conv_k, gdn_conv_v, gdn_a_log, gdn_dt_bias, gdn_norm_g, w_pb, sgu_ln_g, sgu_ln_b, sgu_w_s, sgu_b_s, w_pc, w_o, ln1_g, ln1_b, w_ff1, b_ff1, w_ff2, b_ff2, ln2_g, ln2_b, loss_target, m_ln_in_g, m_ln_in_b, m_w_in, m_b_gate, m_conv_dw_w, m_conv_dw_b, m_conv_ln_g, m_conv_ln_b, m_w_pa, m_gdn_conv_q, m_gdn_conv_k, m_gdn_conv_v, m_gdn_a_log, m_gdn_dt_bias, m_gdn_norm_g, m_w_pb, m_sgu_ln_g, m_sgu_ln_b, m_sgu_w_s, m_sgu_b_s, m_w_pc, m_w_o, m_ln1_g, m_ln1_b, m_w_ff1, m_b_ff1, m_w_ff2, m_b_ff2, m_ln2_g, m_ln2_b, v_ln_in_g, v_ln_in_b, v_w_in, v_b_gate, v_conv_dw_w, v_conv_dw_b, v_conv_ln_g, v_conv_ln_b, v_w_pa, v_gdn_conv_q, v_gdn_conv_k, v_gdn_conv_v, v_gdn_a_log, v_gdn_dt_bias, v_gdn_norm_g, v_w_pb, v_sgu_ln_g, v_sgu_ln_b, v_sgu_w_s, v_sgu_b_s, v_w_pc, v_w_o, v_ln1_g, v_ln1_b, v_w_ff1, v_b_ff1, v_w_ff2, v_b_ff2, v_ln2_g, v_ln2_b):
    args = locals()
    w = {n: args[n] for n in WEIGHTS}
    m = {n: args["m_" + n] for n in WEIGHTS}
    v = {n: args["v_" + n] for n in WEIGHTS}

    rep = {n: w[n] for n in REPLICATED}
    conv_local = jnp.concatenate([w[n] for n in CONV_PACK], axis=1)
    proj_local = jnp.stack([w[n] for n in PROJ_PACK], axis=1).astype(BF16)
    big = ["w_in", "w_o", "w_ff1", "w_ff2"]
    big_local = {n: w[n].astype(BF16) for n in big}
    rest_local = lambda l: [big_local[n][l] for n in big[1:]] + [proj_local[l], conv_local[l]]

    def rest_full(got):
        full_l = {n: _from_slots(g, SHARDED[n] - 1) for n, g in zip(big[1:], got[:3])}
        proj_full = _from_slots(got[3], 2)
        for i, n in enumerate(PROJ_PACK):
            full_l[n] = proj_full[i]
        conv_full = _from_slots(got[4], 1)
        tap0 = 0
        for n in CONV_PACK:
            taps = w[n].shape[1]
            full_l[n] = conv_full[tap0:tap0 + taps]
            tap0 += taps
        return full_l

    ln_g, ln_b = w["ln_in_g"][None, :], w["ln_in_b"][None, :]
    xs, xs_bf = _ln_in_fwd(x[0], ln_g, ln_b)
    (w_in0,) = _exchange([big_local["w_in"][0]], [False], name="gather_w_in_l0")
    behind_proj = {}

    def rest_of_l0(got):
        behind_proj["w_in1"] = got[5]
        return _rest_weights(0, rest_full(got[:5]), rep)

    xs, xs_bf, sv0, w0, got_ff1 = _layer_fwd(
        xs, xs_bf, _proj_weights(_from_slots(w_in0, 1)), rest_of_l0, "_l0",
        proj_exchange=_Exchange(rest_local(0) + [big_local["w_in"][1]], [False] * 6),
        ff1_exchange=_Exchange(rest_local(1), [False] * 5))
    xs, xs_bf, sv1, w1, _ = _layer_fwd(
        xs, xs_bf, _proj_weights(_from_slots(behind_proj["w_in1"], 1)),
        lambda _: _rest_weights(1, rest_full(got_ff1), rep), "_l1")
    d, loss_acc = _loss_fwd_bwd(xs, loss_target[0])
    loss = lax.psum(loss_acc[0, 0], MESH_AXES)

    def matrix_grad_slots(g):
        proj_grad = jnp.stack([g[n] for n in PROJ_PACK], axis=0)
        return [_to_slots(g[n].astype(BF16), SHARDED[n] - 1) for n in big] + [_to_slots(proj_grad.astype(BF16), 2)]

    d, grads1, _, _ = _layer_bwd(d, w1, sv1, "_l1")
    d, grads0, parts1, parts0 = _layer_bwd(
        d, w0, sv0, "_l0", dw_proj_exchange=_Exchange(matrix_grad_slots(grads1), [True] * 5),
        proj_exchange_of=lambda g: _Exchange(matrix_grad_slots(g), [True] * 5))
    dx, _, d_ln_in_g, d_ln_in_b, _ = _ln_bwd_call(d, x[0], ln_g, name="ln_in_bwd")
    grads = {k: jnp.stack([grads0[k], grads1[k]]) for k in grads0}
    grads["ln_in_g"] = d_ln_in_g[0]
    grads["ln_in_b"] = d_ln_in_b[0]
    conv_grad = jnp.concatenate([grads[n] for n in CONV_PACK], axis=1)
    small_parts = _exchange([_to_slots(conv_grad, 2), _pack_rows([grads[n] for n in REPLICATED])], [True, False],
                            name="exchange_small_grads")
    parts = [jnp.stack([p0, p1], axis=1) for p0, p1 in zip(parts0, parts1)] + list(small_parts)

    def adam_sharded(g_parts, w_l, m_l, v_l, name):
        shp = w_l.shape
        two_d = lambda a: a.reshape(-1, shp[-1])
        outs = _adamw(two_d(w_l), two_d(m_l), two_d(v_l), g_parts.reshape(N_DEV, -1, shp[-1]), name=name)
        return [o.reshape(shp) for o in outs]

    res = {}
    for n, gp in zip(big, parts[:4]):
        res[n] = adam_sharded(gp, w[n], m[n], v[n], "adamw_" + n)
    proj_res = adam_sharded(parts[4], jnp.stack([w[n] for n in PROJ_PACK], axis=1), jnp.stack([m[n] for n in PROJ_PACK], axis=1),
                            jnp.stack([v[n] for n in PROJ_PACK], axis=1), "adamw_proj")
    for i, n in enumerate(PROJ_PACK):
        res[n] = [o[:, i] for o in proj_res]
    conv_res = adam_sharded(parts[5], conv_local, jnp.concatenate([m[n] for n in CONV_PACK], axis=1),
                            jnp.concatenate([v[n] for n in CONV_PACK], axis=1), "adamw_conv")
    tap0 = 0
    for n in CONV_PACK:
        taps = w[n].shape[1]
        res[n] = [o[:, tap0:tap0 + taps] for o in conv_res]
        tap0 += taps
    rep_shapes = [w[n].shape for n in REPLICATED]
    rep_res = _adamw(_pack_rows([w[n] for n in REPLICATED]), _pack_rows([m[n] for n in REPLICATED]),
                     _pack_rows([v[n] for n in REPLICATED]), parts[6], name="adamw_replicated")
    rep_res = [_unpack_rows(o, rep_shapes) for o in rep_res]
    for i, n in enumerate(REPLICATED):
        res[n] = [o[i] for o in rep_res]

    outs = [loss, dx[None]]
    for j in range(4):
        outs += [res[n][j] for n in WEIGHTS]
    return tuple(outs)
```

```python
import functools
import math

import jax
import jax.numpy as jnp
from jax import lax
from jax.experimental import pallas as pl
from jax.experimental.pallas import tpu as pltpu

F32 = jnp.float32
BF16 = jnp.bfloat16

N_DEV = 8
DEPTH = 2
D_MODEL = 1024
CONV_DIM = 512
CONV_WIDTH = 31
GDN_HEADS = 4
GDN_DK = 128
GDN_QK = 512
GDN_CONV = 4
GDN_CHUNK = 64
SGU_GROUPS = 4
SGU_GROUP_DIM = 128
SGU_DIM = 512
SGU_CHUNK = 128
D_FF = 4096
DN_ALPHA = (2 * DEPTH) ** 0.25
LN_EPS = 1e-5
RMS_EPS = 1e-6
PROJ_COLS = 7176
SHARD_COLS = PROJ_COLS // N_DEV

COL_A = 0
COL_Q = 1024
COL_Z = 2560
COL_UV = 3072
COL_GATE = 4096
COL_BA = 7168
P_COLS = 7296
P_TILE = 2432

ADAM_LR = 0.001
ADAM_B1 = 0.9
ADAM_B2 = 0.999
ADAM_EPS = 1e-08
ADAM_WD = 0.01
ADAM_STEP = 10

VMEM_LIMIT_BYTES = 56 * 1024 * 1024
LANES = 128
ROW_TILE = 512
GDN_BLOCK = 256
CONV_HALO = 32
GDN_HALO = 16


def _params(sem):
    return pltpu.CompilerParams(dimension_semantics=sem, vmem_limit_bytes=VMEM_LIMIT_BYTES)


def _dot(a, b):
    return jnp.dot(a, b, preferred_element_type=F32)


def _dot_nt(a, b):
    return lax.dot_general(a, b, (((1,), (1,)), ((), ())), preferred_element_type=F32)


def _dot_tn(a, b):
    return lax.dot_general(a, b, (((0,), (0,)), ((), ())), preferred_element_type=F32)


def _split(a):
    hi = a.astype(BF16)
    lo = (a - hi.astype(F32)).astype(BF16)
    return hi, lo


def _dot3(a, b, dot=_dot):
    ah, al = _split(a)
    bh, bl = _split(b)
    return dot(ah, bh) + (dot(ah, bl) + dot(al, bh))


def _bdot(a, b, dot=_dot):
    return dot(a.astype(BF16), b.astype(BF16))


def _sigmoid(x):
    return jax.nn.sigmoid(x)


def _silu(x):
    return x * _sigmoid(x)


def _dsilu(x):
    s = _sigmoid(x)
    return s * (1.0 + x * (1.0 - s))


_GELU_C = math.sqrt(2.0 / math.pi)


def _gelu(x):
    return 0.5 * x * (1.0 + jnp.tanh(_GELU_C * (x + 0.044715 * (x * x * x))))


def _dgelu(x):
    t = jnp.tanh(_GELU_C * (x + 0.044715 * (x * x * x)))
    return 0.5 * (1.0 + t) + 0.5 * x * (1.0 - t * t) * (_GELU_C * (1.0 + 3.0 * 0.044715 * (x * x)))


def _ln_stats(x):
    mu = jnp.mean(x, axis=-1, keepdims=True)
    xc = x - mu
    var = jnp.mean(xc * xc, axis=-1, keepdims=True)
    rstd = lax.rsqrt(var + LN_EPS)
    return xc * rstd, rstd


def _ln_bwd(dy, xhat, rstd, g):
    dxh = dy * g
    return rstd * (dxh - jnp.mean(dxh, axis=-1, keepdims=True) - xhat * jnp.mean(dxh * xhat, axis=-1, keepdims=True))


def _colsum(x):
    return jnp.sum(x, axis=0, keepdims=True)


def _row_spec(t, cols, col_block=0):
    return pl.BlockSpec((t, cols), lambda i, cb=col_block: (i, cb))


def _const_spec(shape):
    nd = len(shape)
    return pl.BlockSpec(shape, lambda *_: (0,) * nd)


def _call(body, *, name, grid, in_specs, out_specs, out_shape, args, sem, scratch=(), exchange=None):
    n_in, n_out, n_sc = len(in_specs), len(out_specs), len(scratch)
    if exchange is None:
        def plain(*refs):
            body(refs[:n_in], refs[n_in:n_in + n_out], refs[n_in + n_out:])

        return pl.pallas_call(plain, name=name, grid=grid, in_specs=in_specs, out_specs=out_specs, out_shape=out_shape,
                              scratch_shapes=list(scratch), compiler_params=_params(sem))(*args)
    nex = exchange.n

    def carrying(*refs):
        ins, ex_ins = refs[:n_in], refs[n_in:n_in + nex]
        outs, ex_outs = refs[n_in + nex:n_in + nex + n_out], refs[n_in + nex + n_out:n_in + 2 * nex + n_out]
        sc, sems = refs[n_in + 2 * nex + n_out:n_in + 2 * nex + n_out + n_sc], refs[n_in + 2 * nex + n_out + n_sc:]
        ids = [pl.program_id(d) for d in range(len(grid))]
        first = functools.reduce(jnp.logical_and, [i == 0 for i in ids])
        last = functools.reduce(jnp.logical_and, [i == g - 1 for i, g in zip(ids, grid)])

        @pl.when(first)
        def _():
            _exchange_start(exchange, ex_ins, ex_outs, sems)

        body(ins, outs, sc)

        @pl.when(last)
        def _():
            _exchange_wait(exchange, ex_ins, ex_outs, sems)

    res = pl.pallas_call(
        carrying, name=name, grid=grid, in_specs=list(in_specs) + exchange.in_specs(),
        out_specs=list(out_specs) + exchange.in_specs(), out_shape=list(out_shape) + exchange.out_shapes(),
        scratch_shapes=list(scratch) + exchange.scratch(), compiler_params=_params(("arbitrary",) * len(grid)),
    )(*args, *exchange.arrays)
    return res[:n_out], res[n_out:]


def _matmul(a, w, *, name, tm, tn, tk, out_dtype=F32, a_col_block=0, add=None, add_scale=1.0, w_is_nk=False, exchange=None):
    m = a.shape[0]
    k, n = w.shape[::-1] if w_is_nk else w.shape
    nk = k // tk
    has_add = add is not None

    def body(ins, outs, scratch):
        a_ref, w_ref = ins[:2]
        o_ref, acc_ref = outs[0], scratch[0]
        if has_add:
            add_ref = ins[2]
        kk = pl.program_id(2)

        @pl.when(kk == 0)
        def _():
            acc_ref[...] = jnp.zeros_like(acc_ref)

        acc_ref[...] += (_dot_nt if w_is_nk else _dot)(a_ref[...], w_ref[...])

        @pl.when(kk == nk - 1)
        def _():
            r = acc_ref[...]
            if has_add:
                r = r + add_scale * add_ref[...]
            o_ref[...] = r.astype(out_dtype)

    in_specs = [pl.BlockSpec((tm, tk), lambda j, i, kk: (i, kk + a_col_block)),
                pl.BlockSpec((tn, tk), lambda j, i, kk: (j, kk)) if w_is_nk else pl.BlockSpec((tk, tn), lambda j, i, kk: (kk, j))]
    args = [a, w]
    if has_add:
        in_specs.append(pl.BlockSpec((tm, tn), lambda j, i, kk: (i, j)))
        args.append(add)
    res = _call(body, name=name, grid=(n // tn, m // tm, nk), in_specs=in_specs,
                out_specs=[pl.BlockSpec((tm, tn), lambda j, i, kk: (i, j))],
                out_shape=[jax.ShapeDtypeStruct((m, n), out_dtype)], scratch=[pltpu.VMEM((tm, tn), F32)],
                args=args, sem=("parallel", "parallel", "arbitrary"), exchange=exchange)
    return res[0] if exchange is None else (res[0][0], res[1])


def _matmul_tn(a, b, *, name, ka, tka, tn, ts, n=None, a_col_block=0, b_col_block=0, exchange=None):
    s = a.shape[0]
    n = b.shape[1] if n is None else n
    ns = s // ts

    def body(ins, outs, scratch):
        a_ref, b_ref = ins
        o_ref = outs[0]

        @pl.when(pl.program_id(2) == 0)
        def _():
            o_ref[...] = jnp.zeros_like(o_ref)

        o_ref[...] += _dot_tn(a_ref[...], b_ref[...])

    res = _call(body, name=name, grid=(ka // tka, n // tn, ns),
                in_specs=[pl.BlockSpec((ts, tka), lambda i, j, t: (t, i + a_col_block)),
                          pl.BlockSpec((ts, tn), lambda i, j, t: (t, j + b_col_block))],
                out_specs=[pl.BlockSpec((tka, tn), lambda i, j, t: (i, j))],
                out_shape=[jax.ShapeDtypeStruct((ka, n), F32)], args=[a, b],
                sem=("parallel", "parallel", "arbitrary"), exchange=exchange)
    return res[0] if exchange is None else (res[0][0], res[1])


def _ln_in_fwd(x, g, b):
    s = x.shape[0]
    t = min(ROW_TILE, s)

    def body(x_ref, g_ref, b_ref, y_ref, ybf_ref):
        xhat, _ = _ln_stats(x_ref[...])
        y = xhat * g_ref[...] + b_ref[...]
        y_ref[...] = y
        ybf_ref[...] = y.astype(BF16)

    return pl.pallas_call(
        body, name="ln_in_fwd", grid=(s // t,),
        in_specs=[_row_spec(t, D_MODEL), _const_spec((1, D_MODEL)), _const_spec((1, D_MODEL))],
        out_specs=[_row_spec(t, D_MODEL), _row_spec(t, D_MODEL)],
        out_shape=[jax.ShapeDtypeStruct((s, D_MODEL), F32), jax.ShapeDtypeStruct((s, D_MODEL), BF16)],
        compiler_params=_params(("parallel",)),
    )(x, g, b)


def _prev_halo_spec(t, halo, cols, col_block):
    per = t // halo
    return pl.BlockSpec((halo, cols), lambda i, cb=col_block: (jnp.maximum(i * per - 1, 0), cb))


def _next_halo_spec(t, halo, cols, col_block, n_blocks):
    per = t // halo
    last = n_blocks * per - 1
    return pl.BlockSpec((halo, cols), lambda i, cb=col_block: (jnp.minimum((i + 1) * per, last), cb))


CONV_ROWS = 32


def _tap_groups(offsets):
    by_shift = {}
    for j, off in enumerate(offsets):
        by_shift.setdefault(off % 8, []).append((j, off // 8))
    groups = []
    for shift, taps in sorted(by_shift.items()):
        first = min(a for _, a in taps)
        last = max(a for _, a in taps)
        groups.append((shift, 8 * first, 8 * (last - first), [(j, 8 * (a - first)) for j, a in taps]))
    return groups


def _tap_windows(ext, groups, r0):
    return [ext[r0 + first + shift:r0 + first + shift + CONV_ROWS + extra, :] for shift, first, extra, _ in groups]


def _tap_sum(ext, w_ref, groups, r0):
    acc = jnp.zeros((CONV_ROWS, ext.shape[1]), F32)
    for (_, _, _, taps), win in zip(groups, _tap_windows(ext, groups, r0)):
        for j, a in taps:
            acc = acc + w_ref[j:j + 1, :] * win[a:a + CONV_ROWS]
    return acc


def _conv_a_fwd(p, w, b, ln_g, ln_b, tag):
    s = p.shape[0]
    t = min(ROW_TILE, s)
    width = CONV_WIDTH

    def body(a_ref, halo_ref, w_ref, b_ref, g_ref, bb_ref, h_ref, c_ref, ha_ref, ext):
        i = pl.program_id(0)
        a = a_ref[...].astype(F32)
        h = a[:, :CONV_DIM] * _sigmoid(a[:, CONV_DIM:])
        ah = halo_ref[...].astype(F32)
        hh = ah[:, :CONV_DIM] * _sigmoid(ah[:, CONV_DIM:])
        ext[0:CONV_HALO, :] = jnp.where(i == 0, 0.0, hh)
        ext[CONV_HALO:CONV_HALO + t, :] = h
        h_ref[...] = h
        groups = _tap_groups([CONV_HALO - (width - 1) + j for j in range(width)])
        for r0 in range(0, t, CONV_ROWS):
            c = _tap_sum(ext, w_ref, groups, r0) + b_ref[...]
            xhat, _ = _ln_stats(c)
            n = xhat * g_ref[...] + bb_ref[...]
            c_ref[r0:r0 + CONV_ROWS, :] = c
            ha_ref[r0:r0 + CONV_ROWS, :] = _silu(n).astype(BF16)

    return pl.pallas_call(
        body, name="conv_a_fwd" + tag, grid=(s // t,),
        in_specs=[_row_spec(t, 2 * CONV_DIM, COL_A // (2 * CONV_DIM)),
                  _prev_halo_spec(t, CONV_HALO, 2 * CONV_DIM, COL_A // (2 * CONV_DIM)),
                  _const_spec((width, CONV_DIM)), _const_spec((1, CONV_DIM)),
                  _const_spec((1, CONV_DIM)), _const_spec((1, CONV_DIM))],
        out_specs=[_row_spec(t, CONV_DIM)] * 3,
        out_shape=[jax.ShapeDtypeStruct((s, CONV_DIM), F32), jax.ShapeDtypeStruct((s, CONV_DIM), F32),
                   jax.ShapeDtypeStruct((s, CONV_DIM), BF16)],
        scratch_shapes=[pltpu.VMEM((CONV_HALO + t, CONV_DIM), F32)],
        compiler_params=_params(("parallel",)),
    )(p, p, w, b, ln_g, ln_b)


def _sgu_mix(vn, wt_ref, bst_ref, t):
    row = lax.broadcasted_iota(jnp.int32, (SGU_CHUNK, SGU_CHUNK), 0)
    col = lax.broadcasted_iota(jnp.int32, (SGU_CHUNK, SGU_CHUNK), 1)
    chunks = []
    for ci in range(t // SGU_CHUNK):
        groups = []
        for g in range(SGU_GROUPS):
            wg = jnp.where(row >= col, wt_ref[g], 0.0).astype(BF16)
            v_cg = vn[ci * SGU_CHUNK:(ci + 1) * SGU_CHUNK, g * SGU_GROUP_DIM:(g + 1) * SGU_GROUP_DIM]
            groups.append(_dot(wg, v_cg.astype(BF16)) + bst_ref[:, g:g + 1])
        chunks.append(jnp.concatenate(groups, axis=1))
    return jnp.concatenate(chunks, axis=0)


def _sgu_fwd(p, ln_g, ln_b, w_s, b_s_t, tag):
    s = p.shape[0]
    t = min(ROW_TILE, s)

    def body(uv_ref, g_ref, b_ref, ws_ref, bst_ref, hc_ref):
        uv = uv_ref[...].astype(F32)
        u = _gelu(uv[:, :SGU_DIM])
        vv = _gelu(uv[:, SGU_DIM:])
        xhat, _ = _ln_stats(vv)
        vn = xhat * g_ref[...] + b_ref[...]
        mixed = _sgu_mix(vn, ws_ref, bst_ref, t)
        hc_ref[...] = (u * mixed).astype(BF16)

    return pl.pallas_call(
        body, name="sgu_fwd" + tag, grid=(s // t,),
        in_specs=[_row_spec(t, 2 * SGU_DIM, COL_UV // (2 * SGU_DIM)),
                  _const_spec((1, SGU_DIM)), _const_spec((1, SGU_DIM)),
                  _const_spec((SGU_GROUPS, SGU_CHUNK, SGU_CHUNK)), _const_spec((SGU_CHUNK, LANES))],
        out_specs=_row_spec(t, SGU_DIM),
        out_shape=jax.ShapeDtypeStruct((s, SGU_DIM), BF16),
        compiler_params=_params(("parallel",)),
    )(p, ln_g, ln_b, w_s, b_s_t)


def _merge_fwd(p, ha, hb, hc, w_pa, w_pb, w_pc, b_gate, tag):
    s = p.shape[0]
    t = min(ROW_TILE, s)
    gb = COL_GATE // D_MODEL

    def body(ga_ref, gb_ref, gc_ref, ha_ref, hb_ref, hc_ref, wa_ref, wb_ref, wc_ref, bg_ref, y_ref, m_ref):
        merged = jnp.zeros((t, D_MODEL), F32)
        for idx, (g_ref, h_ref, w_ref) in enumerate(((ga_ref, ha_ref, wa_ref), (gb_ref, hb_ref, wb_ref), (gc_ref, hc_ref, wc_ref))):
            y = _dot(h_ref[...], w_ref[...])
            sg = _sigmoid(g_ref[...].astype(F32) + bg_ref[:, idx * D_MODEL:(idx + 1) * D_MODEL])
            y_ref[:, idx * D_MODEL:(idx + 1) * D_MODEL] = y.astype(BF16)
            merged = merged + sg * y
        m_ref[...] = merged.astype(BF16)

    hspec = _row_spec(t, CONV_DIM)
    wspec = _const_spec((CONV_DIM, D_MODEL))
    return pl.pallas_call(
        body, name="merge_fwd" + tag, grid=(s // t,),
        in_specs=[_row_spec(t, D_MODEL, gb), _row_spec(t, D_MODEL, gb + 1), _row_spec(t, D_MODEL, gb + 2),
                  hspec, hspec, hspec, wspec, wspec, wspec, _const_spec((1, 3 * D_MODEL))],
        out_specs=[_row_spec(t, 3 * D_MODEL), _row_spec(t, D_MODEL)],
        out_shape=[jax.ShapeDtypeStruct((s, 3 * D_MODEL), BF16), jax.ShapeDtypeStruct((s, D_MODEL), BF16)],
        compiler_params=_params(("parallel",)),
    )(p, p, p, ha, hb, hc, w_pa, w_pb, w_pc, b_gate)


def _matmul_res_ln(a, w, bias, x_res, ln_g, ln_b, *, name):
    s, k = a.shape
    t = min(ROW_TILE, s)

    def body(a_ref, w_ref, bias_ref, x_ref, g_ref, b_ref, r_ref, y_ref, ybf_ref):
        r = DN_ALPHA * x_ref[...] + _dot(a_ref[...], w_ref[...]) + bias_ref[...]
        xhat, _ = _ln_stats(r)
        y = xhat * g_ref[...] + b_ref[...]
        r_ref[...] = r
        y_ref[...] = y
        ybf_ref[...] = y.astype(BF16)

    vec = _const_spec((1, D_MODEL))
    return pl.pallas_call(
        body, name=name, grid=(s // t,),
        in_specs=[_row_spec(t, k), _const_spec((k, D_MODEL)), vec, _row_spec(t, D_MODEL), vec, vec],
        out_specs=[_row_spec(t, D_MODEL)] * 3,
        out_shape=[jax.ShapeDtypeStruct((s, D_MODEL), F32), jax.ShapeDtypeStruct((s, D_MODEL), F32),
                   jax.ShapeDtypeStruct((s, D_MODEL), BF16)],
        compiler_params=_params(("parallel",)),
    )(a, w, bias, x_res, ln_g, ln_b)


def _ff1_fwd(x_bf, w, b, tag, exchange=None):
    s = x_bf.shape[0]
    tm = min(1024, s)
    tn = 1024

    def body(ins, outs, scratch):
        x_ref, w_ref, b_ref = ins
        hp_ref, h_ref = outs
        hp = _dot(x_ref[...], w_ref[...]) + b_ref[...]
        hp_ref[...] = hp.astype(BF16)
        r = jnp.maximum(hp, 0.0)
        h_ref[...] = (r * r).astype(BF16)

    res = _call(
        body, name="ff1_fwd" + tag, grid=(s // tm, D_FF // tn),
        in_specs=[pl.BlockSpec((tm, D_MODEL), lambda i, j: (i, 0)), pl.BlockSpec((D_MODEL, tn), lambda i, j: (0, j)),
                  pl.BlockSpec((1, tn), lambda i, j: (0, j))],
        out_specs=[pl.BlockSpec((tm, tn), lambda i, j: (i, j))] * 2,
        out_shape=[jax.ShapeDtypeStruct((s, D_FF), BF16), jax.ShapeDtypeStruct((s, D_FF), BF16)],
        args=[x_bf, w, b], sem=("parallel", "parallel"), exchange=exchange)
    return (res[0], res[1], None) if exchange is None else (res[0][0], res[0][1], res[1])


def _loss_fwd_bwd(y, target):
    s = y.shape[0]
    t = min(ROW_TILE, s)

    def body(y_ref, t_ref, dy_ref, loss_ref):
        @pl.when(pl.program_id(0) == 0)
        def _():
            loss_ref[...] = jnp.zeros_like(loss_ref)

        err = y_ref[...] - t_ref[...]
        dy_ref[...] = err * (1.0 / D_MODEL)
        per_row = jnp.mean(err * err, axis=-1, keepdims=True)
        loss_ref[...] += 0.5 * jnp.sum(per_row, axis=0, keepdims=True)

    return pl.pallas_call(
        body, name="loss_fwd_bwd", grid=(s // t,),
        in_specs=[_row_spec(t, D_MODEL), _row_spec(t, D_MODEL)],
        out_specs=[_row_spec(t, D_MODEL), _const_spec((8, LANES))],
        out_shape=[jax.ShapeDtypeStruct((s, D_MODEL), F32), jax.ShapeDtypeStruct((8, LANES), F32)],
        compiler_params=_params(("arbitrary",)),
    )(y, target)


def _softplus(x):
    return jnp.maximum(x, 0.0) + jnp.log1p(jnp.exp(-jnp.abs(x)))


def _gdn_conv_silu_norm(q_ref, k_ref, v_ref, hq_ref, hk_ref, hv_ref, cw_ref, ext, first):
    t = q_ref.shape[0]
    for n, (r, h) in enumerate(((q_ref, hq_ref), (k_ref, hk_ref), (v_ref, hv_ref))):
        ext[0:GDN_HALO, n * GDN_QK:(n + 1) * GDN_QK] = jnp.where(first, 0.0, h[...].astype(F32))
        ext[GDN_HALO:GDN_HALO + t, n * GDN_QK:(n + 1) * GDN_QK] = r[...].astype(F32)
    pre = jnp.zeros((t, 3 * GDN_QK), F32)
    for j in range(GDN_CONV):
        pre = pre + cw_ref[j:j + 1, :] * ext[pl.ds(GDN_HALO - (GDN_CONV - 1) + j, t), :]
    act = _silu(pre)
    rq, rk = [], []
    for h in range(GDN_HEADS):
        qh = act[:, h * GDN_DK:(h + 1) * GDN_DK]
        kh = act[:, GDN_QK + h * GDN_DK:GDN_QK + (h + 1) * GDN_DK]
        rq.append(lax.rsqrt(jnp.sum(qh * qh, axis=-1, keepdims=True) + RMS_EPS))
        rk.append(lax.rsqrt(jnp.sum(kh * kh, axis=-1, keepdims=True) + RMS_EPS))
    return pre, act, rq, rk


def _gdn_gates(ba, alog_ref, dtb_ref):
    lane = lax.broadcasted_iota(jnp.int32, ba.shape, 1)
    beta = _sigmoid(ba)
    g = -jnp.exp(alog_ref[...]) * _softplus(ba + dtb_ref[...])
    g = jnp.where((lane >= GDN_HEADS) & (lane < 2 * GDN_HEADS), g, 0.0)
    return beta, g


def _chunk_cumsum_matrix(t, upper):
    row = lax.broadcasted_iota(jnp.int32, (t, t), 0)
    col = lax.broadcasted_iota(jnp.int32, (t, t), 1)
    same = (row // GDN_CHUNK) == (col // GDN_CHUNK)
    tri = (col >= row) if upper else (col <= row)
    return jnp.where(same & tri, 1.0, 0.0).astype(F32)


def _each(fn, *lists):
    return [fn(*a) for a in zip(*lists)]


def _gdn_pair_items(qn_s, kn_s, vc_s, beta_s, gam_s, nc):
    items = []
    for ci in range(nc):
        rows = slice(ci * GDN_CHUNK, (ci + 1) * GDN_CHUNK)
        gam_blk = gam_s[rows, :]
        gam_t = gam_blk.T
        beta_blk = beta_s[rows, :]
        for h in range(GDN_HEADS):
            sl = slice(h * GDN_DK, (h + 1) * GDN_DK)
            items.append((qn_s[rows, sl], kn_s[rows, sl], vc_s[rows, sl], beta_blk[:, h:h + 1],
                          gam_blk[:, GDN_HEADS + h:GDN_HEADS + h + 1], gam_t[GDN_HEADS + h:GDN_HEADS + h + 1, :]))
    return items


def _gdn_prep(items):
    c = GDN_CHUNK
    row = lax.broadcasted_iota(jnp.int32, (c, c), 0)
    col = lax.broadcasted_iota(jnp.int32, (c, c), 1)
    causal = row >= col
    strict = row > col
    kbs = [k.astype(BF16) for _, k, _, _, _, _ in items]
    kks = _each(_dot_nt, kbs, kbs)
    qks = _each(_dot_nt, [q.astype(BF16) for q, _, _, _, _, _ in items], kbs)
    out = []
    for (q, k, v, beta_c, gam_c, gam_r), kk, qk in zip(items, kks, qks):
        decay = jnp.where(causal, jnp.exp(jnp.where(causal, gam_c - gam_r, 0.0)), 0.0)
        gm = jnp.exp(gam_c)
        glast = gam_c[c - 1:c, :]
        elast = jnp.exp(glast - gam_c)
        out.append(dict(causal=causal, strict=strict, decay=decay, kk=kk, low=jnp.where(strict, beta_c * kk * decay, 0.0),
                        a_qk=qk * decay, gm=gm, glast_exp=jnp.exp(glast), elast=elast, q=q, k=k, v=v, beta_c=beta_c,
                        r=jnp.concatenate([beta_c * v, beta_c * k * gm], axis=1), qd=q * gm, kd=k * elast))
    return out


def _unit_lower_inverses_minus_identity(lows):
    ps = [-low for low in lows]
    mis = list(ps)
    for _ in range(5):
        ps = _each(_bdot, ps, ps)
        ts = _each(_bdot, mis, ps)
        mis = [mi + p + t for mi, p, t in zip(mis, ps, ts)]
    return mis


def _apply_inverses(mis, rs, dot=_dot):
    return [r + t for r, t in zip(rs, _each(functools.partial(_bdot, dot=dot), mis, rs))]


def _gdn_fwd(p, p_ba, conv_w, a_log, dt_bias, norm_g, tag):
    s = p.shape[0]
    t = min(GDN_BLOCK, s)
    nc = t // GDN_CHUNK
    nblk = s // t
    qb, kb_, vb, zb = COL_Q // GDN_QK, COL_Q // GDN_QK + 1, COL_Q // GDN_QK + 2, COL_Z // GDN_QK
    scale = GDN_DK ** -0.5

    def body(q_ref, k_ref, v_ref, hq_ref, hk_ref, hv_ref, z_ref, ba_ref, cw_ref, alog_ref, dtb_ref, ng_ref,
             o_ref, hb_ref, st_ref, m_ref, ext, qn_s, kn_s, vc_s, beta_s, gam_s, state):
        i = pl.program_id(0)

        @pl.when(i == 0)
        def _():
            state[...] = jnp.zeros_like(state)

        _, act, rq, rk = _gdn_conv_silu_norm(q_ref, k_ref, v_ref, hq_ref, hk_ref, hv_ref, cw_ref, ext, i == 0)
        for h in range(GDN_HEADS):
            sl = slice(h * GDN_DK, (h + 1) * GDN_DK)
            qn_s[:, sl] = act[:, sl] * (rq[h] * scale)
            kn_s[:, sl] = act[:, GDN_QK + h * GDN_DK:GDN_QK + (h + 1) * GDN_DK] * rk[h]
        vc_s[...] = act[:, 2 * GDN_QK:]
        beta, g = _gdn_gates(ba_ref[...], alog_ref, dtb_ref)
        beta_s[...] = beta
        gam_s[...] = jnp.dot(_chunk_cumsum_matrix(t, False), g, preferred_element_type=F32, precision=lax.Precision.HIGHEST)

        prs = _gdn_prep(_gdn_pair_items(qn_s, kn_s, vc_s, beta_s, gam_s, nc))
        mis = _unit_lower_inverses_minus_identity([pr["low"] for pr in prs])
        xs = _apply_inverses(mis, [pr["r"] for pr in prs])
        heads = range(GDN_HEADS)
        for ci in range(nc):
            rows = slice(ci * GDN_CHUNK, (ci + 1) * GDN_CHUNK)
            pc, xc = prs[ci * GDN_HEADS:(ci + 1) * GDN_HEADS], xs[ci * GDN_HEADS:(ci + 1) * GDN_HEADS]
            m_ref[rows, :] = jnp.concatenate(mis[ci * GDN_HEADS:(ci + 1) * GDN_HEADS], axis=1)
            sts = [state[h * GDN_DK:(h + 1) * GDN_DK, :] for h in heads]
            for h in heads:
                st_ref[(ci * GDN_HEADS + h) * GDN_DK:(ci * GDN_HEADS + h + 1) * GDN_DK, :] = sts[h]
            w_st = _each(_bdot, [x[:, GDN_DK:] for x in xc], sts)
            q_st = _each(_bdot, [pr["qd"] for pr in pc], sts)
            vns = [x[:, :GDN_DK] - ws for x, ws in zip(xc, w_st)]
            a_vn = _each(_bdot, [pr["a_qk"] for pr in pc], vns)
            k_vn = _each(functools.partial(_bdot, dot=_dot_tn), [pr["kd"] for pr in pc], vns)
            for h in heads:
                o_ref[rows, h * GDN_DK:(h + 1) * GDN_DK] = q_st[h] + a_vn[h]
                state[h * GDN_DK:(h + 1) * GDN_DK, :] = sts[h] * pc[h]["glast_exp"] + k_vn[h]

        z = z_ref[...].astype(F32)
        for h in range(GDN_HEADS):
            sl = slice(h * GDN_DK, (h + 1) * GDN_DK)
            o = o_ref[:, sl]
            on = o * lax.rsqrt(jnp.mean(o * o, axis=-1, keepdims=True) + RMS_EPS)
            hb_ref[:, sl] = (on * ng_ref[...] * _silu(z[:, sl])).astype(BF16)

    col = lambda cb: pl.BlockSpec((t, GDN_QK), lambda i, cb=cb: (i, cb))
    halo = lambda cb: _prev_halo_spec(t, GDN_HALO, GDN_QK, cb)
    vec = _const_spec((1, LANES))
    return pl.pallas_call(
        body, name="gdn_fwd" + tag, grid=(nblk,),
        in_specs=[col(qb), col(kb_), col(vb), halo(qb), halo(kb_), halo(vb), col(zb),
                  _row_spec(t, LANES), _const_spec((GDN_CONV, 3 * GDN_QK)), vec, vec, vec],
        out_specs=[_row_spec(t, GDN_QK), _row_spec(t, GDN_QK),
                   pl.BlockSpec((nc * GDN_HEADS * GDN_DK, GDN_DK), lambda i: (i, 0)),
                   _row_spec(t, GDN_HEADS * GDN_CHUNK)],
        out_shape=[jax.ShapeDtypeStruct((s, GDN_QK), F32), jax.ShapeDtypeStruct((s, GDN_QK), BF16),
                   jax.ShapeDtypeStruct((s // GDN_CHUNK * GDN_HEADS * GDN_DK, GDN_DK), F32),
                   jax.ShapeDtypeStruct((s, GDN_HEADS * GDN_CHUNK), F32)],
        scratch_shapes=[pltpu.VMEM((GDN_HALO + t, 3 * GDN_QK), F32), pltpu.VMEM((t, GDN_QK), F32),
                        pltpu.VMEM((t, GDN_QK), F32), pltpu.VMEM((t, GDN_QK), F32),
                        pltpu.VMEM((t, LANES), F32), pltpu.VMEM((t, LANES), F32),
                        pltpu.VMEM((GDN_HEADS * GDN_DK, GDN_DK), F32)],
        compiler_params=_params(("arbitrary",)),
    )(p, p, p, p, p, p, p, p_ba, conv_w, a_log, dt_bias, norm_g)


def _lane_place(col, lane_idx, shape):
    lane = lax.broadcasted_iota(jnp.int32, shape, 1)
    return jnp.where(lane == lane_idx, col, 0.0)


def _gdn_bwd(dhb, p, p_ba, o, states, minv, conv_w, a_log, dt_bias, norm_g, tag):
    s = p.shape[0]
    t = min(GDN_BLOCK, s)
    nc = t // GDN_CHUNK
    nblk = s // t
    qb, kb_, vb, zb = COL_Q // GDN_QK, COL_Q // GDN_QK + 1, COL_Q // GDN_QK + 2, COL_Z // GDN_QK
    scale = GDN_DK ** -0.5
    c = GDN_CHUNK

    def body(dhb_ref, q_ref, k_ref, v_ref, hq_ref, hk_ref, hv_ref, z_ref, ba_ref, o_ref, st_ref, m_ref,
             cw_ref, alog_ref, dtb_ref, ng_ref,
             dpre_ref, dz_ref, dba_ref, dng_ref, dalog_ref, ddtb_ref,
             ext, qn_s, kn_s, vc_s, beta_s, gam_s, do_s, dqn_s, dkn_s, dvc_s, dgam_s, dbeta_s, dstate):
        i = pl.program_id(0)

        @pl.when(i == 0)
        def _():
            dstate[...] = jnp.zeros_like(dstate)
            dng_ref[...] = jnp.zeros_like(dng_ref)
            dalog_ref[...] = jnp.zeros_like(dalog_ref)
            ddtb_ref[...] = jnp.zeros_like(ddtb_ref)

        pre, act, rq, rk = _gdn_conv_silu_norm(q_ref, k_ref, v_ref, hq_ref, hk_ref, hv_ref, cw_ref, ext, i == nblk - 1)
        for h in range(GDN_HEADS):
            sl = slice(h * GDN_DK, (h + 1) * GDN_DK)
            qn_s[:, sl] = act[:, sl] * (rq[h] * scale)
            kn_s[:, sl] = act[:, GDN_QK + h * GDN_DK:GDN_QK + (h + 1) * GDN_DK] * rk[h]
        vc_s[...] = act[:, 2 * GDN_QK:]
        ba = ba_ref[...]
        beta, g = _gdn_gates(ba, alog_ref, dtb_ref)
        beta_s[...] = beta
        gam_s[...] = jnp.dot(_chunk_cumsum_matrix(t, False), g, preferred_element_type=F32, precision=lax.Precision.HIGHEST)

        z = z_ref[...].astype(F32)
        dhb = dhb_ref[...]
        dng = jnp.zeros((1, GDN_DK), F32)
        for h in range(GDN_HEADS):
            sl = slice(h * GDN_DK, (h + 1) * GDN_DK)
            oh = o_ref[:, sl]
            r = lax.rsqrt(jnp.mean(oh * oh, axis=-1, keepdims=True) + RMS_EPS)
            on = oh * r
            sz = _silu(z[:, sl])
            dyh = dhb[:, sl]
            dng = dng + _colsum(dyh * on * sz)
            dz_ref[:, sl] = (dyh * on * ng_ref[...] * _dsilu(z[:, sl])).astype(BF16)
            don = dyh * ng_ref[...] * sz
            do_s[:, sl] = r * (don - on * jnp.mean(don * on, axis=-1, keepdims=True))
        dng_ref[...] += dng

        heads = range(GDN_HEADS)
        npairs = nc * GDN_HEADS
        tn = functools.partial(_bdot, dot=_dot_tn)
        nt = functools.partial(_bdot, dot=_dot_nt)
        rsum = lambda a: jnp.sum(a, axis=-1, keepdims=True)
        left = lambda a: a[:, :GDN_DK]
        right = lambda a: a[:, GDN_DK:]

        prs = _gdn_prep(_gdn_pair_items(qn_s, kn_s, vc_s, beta_s, gam_s, nc))
        mis = [m_ref[(n // GDN_HEADS) * c:(n // GDN_HEADS + 1) * c, (n % GDN_HEADS) * c:(n % GDN_HEADS + 1) * c] for n in range(npairs)]
        xs = _apply_inverses(mis, [pr["r"] for pr in prs])
        sts = [st_ref[n * GDN_DK:(n + 1) * GDN_DK, :] for n in range(npairs)]
        dos = [do_s[(n // GDN_HEADS) * c:(n // GDN_HEADS + 1) * c, (n % GDN_HEADS) * GDN_DK:(n % GDN_HEADS + 1) * GDN_DK] for n in range(npairs)]
        w_st = _each(_bdot, [right(x) for x in xs], sts)
        vns = [left(x) - ws for x, ws in zip(xs, w_st)]
        at_do = _each(tn, [pr["a_qk"] for pr in prs], dos)
        dqds = _each(nt, dos, sts)
        d_as = [jnp.where(pr["causal"], a, 0.0) for pr, a in zip(prs, _each(nt, dos, vns))]
        qt_do = _each(tn, [pr["qd"] for pr in prs], dos)

        dvns, dkds, ds_st = [None] * npairs, [None] * npairs, [None] * npairs
        for ci in reversed(range(nc)):
            ids = [ci * GDN_HEADS + h for h in heads]
            dss = [dstate[h * GDN_DK:(h + 1) * GDN_DK, :] for h in heads]
            kd_ds = _each(_bdot, [prs[n]["kd"] for n in ids], dss)
            vn_ds = _each(nt, [vns[n] for n in ids], dss)
            for h, n in enumerate(ids):
                dvns[n] = kd_ds[h] + at_do[n]
                dkds[n] = vn_ds[h]
                ds_st[n] = jnp.sum(rsum(dss[h] * sts[n]), axis=0, keepdims=True)
            wt_dvn = _each(tn, [right(xs[n]) for n in ids], [dvns[n] for n in ids])
            for h, n in enumerate(ids):
                dstate[h * GDN_DK:(h + 1) * GDN_DK, :] = dss[h] * prs[n]["glast_exp"] + qt_do[n] - wt_dvn[h]

        dws = [-a for a in _each(nt, dvns, sts)]
        d_rs = _apply_inverses(mis, [jnp.concatenate([dvn, dw], axis=1) for dvn, dw in zip(dvns, dws)], _dot_tn)
        d_ls = [jnp.where(pr["strict"], -a, 0.0) for pr, a in zip(prs, _each(nt, d_rs, xs))]
        d_l_kds = [d_l * pr["kk"] * pr["decay"] for d_l, pr in zip(d_ls, prs)]
        dkks = [d_l * pr["beta_c"] * pr["decay"] for d_l, pr in zip(d_ls, prs)]
        dqks = [d_a * pr["decay"] for d_a, pr in zip(d_as, prs)]
        ks, qs = [pr["k"] for pr in prs], [pr["q"] for pr in prs]
        dk1, dk2, dk3 = _each(_bdot, dkks, ks), _each(tn, dkks, ks), _each(tn, dqks, qs)
        dq1 = _each(_bdot, dqks, ks)
        rowi = lax.broadcasted_iota(jnp.int32, (c, 1), 0)
        for ci in range(nc):
            rows = slice(ci * c, (ci + 1) * c)
            dgam_blk = jnp.zeros((c, LANES), F32)
            dbeta_blk = jnp.zeros((c, LANES), F32)
            for h in heads:
                n = ci * GDN_HEADS + h
                sl = slice(h * GDN_DK, (h + 1) * GDN_DK)
                pr, d_r = prs[n], d_rs[n]
                d_ru, d_rw = left(d_r), right(d_r)
                gmat = pr["beta_c"] * d_l_kds[n] + d_as[n] * pr["a_qk"]
                dkd_kd = rsum(dkds[n] * pr["kd"])
                dgam = rsum(gmat) - rsum(gmat.T) + rsum(d_rw * right(pr["r"])) + rsum(dqds[n] * pr["qd"]) - dkd_kd
                dglast = jnp.sum(dkd_kd, axis=0, keepdims=True) + ds_st[n] * pr["glast_exp"]
                dgam = dgam + jnp.where(rowi == c - 1, dglast, 0.0)
                dbeta = rsum(d_l_kds[n]) + rsum(d_ru * pr["v"]) + rsum(d_rw * pr["k"]) * pr["gm"]
                dvc_s[rows, sl] = pr["beta_c"] * d_ru
                dkn_s[rows, sl] = dk1[n] + dk2[n] + dk3[n] + d_rw * (pr["beta_c"] * pr["gm"]) + dkds[n] * pr["elast"]
                dqn_s[rows, sl] = dq1[n] + dqds[n] * pr["gm"]
                dgam_blk = dgam_blk + _lane_place(dgam, GDN_HEADS + h, (c, LANES))
                dbeta_blk = dbeta_blk + _lane_place(dbeta, h, (c, LANES))
            dgam_s[rows, :] = dgam_blk
            dbeta_s[rows, :] = dbeta_blk

        dg = jnp.dot(_chunk_cumsum_matrix(t, True), dgam_s[...], preferred_element_type=F32, precision=lax.Precision.HIGHEST)
        lane = lax.broadcasted_iota(jnp.int32, (t, LANES), 1)
        g_lanes = (lane >= GDN_HEADS) & (lane < 2 * GDN_HEADS)
        da_logit = jnp.where(g_lanes, dg * (-jnp.exp(alog_ref[...])) * _sigmoid(ba + dtb_ref[...]), 0.0)
        db_logit = jnp.where(lane < GDN_HEADS, dbeta_s[...] * beta * (1.0 - beta), 0.0)
        dba_ref[...] = (da_logit + db_logit).astype(BF16)
        dalog_ref[...] += _colsum(dg * g)
        ddtb_ref[...] += _colsum(da_logit)

        dact = []
        for n, (dn_s, rr, sc) in enumerate(((dqn_s, rq, scale), (dkn_s, rk, 1.0))):
            for h in range(GDN_HEADS):
                sl = slice(h * GDN_DK, (h + 1) * GDN_DK)
                y = act[:, n * GDN_QK + h * GDN_DK:n * GDN_QK + (h + 1) * GDN_DK] * rr[h]
                dy = dn_s[:, sl] * sc
                dact.append(rr[h] * (dy - y * jnp.sum(dy * y, axis=-1, keepdims=True)))
        dact.append(dvc_s[...])
        dpre_ref[...] = jnp.concatenate(dact, axis=1) * _dsilu(pre)

    rb = lambda i: nblk - 1 - i
    per = t // GDN_HALO
    col = lambda cb, wd=GDN_QK: pl.BlockSpec((t, wd), lambda i, cb=cb: (rb(i), cb))
    halo = lambda cb: pl.BlockSpec((GDN_HALO, GDN_QK), lambda i, cb=cb: (jnp.maximum(rb(i) * per - 1, 0), cb))
    vec = _const_spec((1, LANES))
    return pl.pallas_call(
        body, name="gdn_bwd" + tag, grid=(nblk,),
        in_specs=[col(0), col(qb), col(kb_), col(vb), halo(qb), halo(kb_), halo(vb), col(zb),
                  pl.BlockSpec((t, LANES), lambda i: (rb(i), 0)), col(0),
                  pl.BlockSpec((nc * GDN_HEADS * GDN_DK, GDN_DK), lambda i: (rb(i), 0)),
                  pl.BlockSpec((t, GDN_HEADS * c), lambda i: (rb(i), 0)),
                  _const_spec((GDN_CONV, 3 * GDN_QK)), vec, vec, vec],
        out_specs=[pl.BlockSpec((t, 3 * GDN_QK), lambda i: (rb(i), 0)), col(0), pl.BlockSpec((t, LANES), lambda i: (rb(i), 0)),
                   vec, vec, vec],
        out_shape=[jax.ShapeDtypeStruct((s, 3 * GDN_QK), F32), jax.ShapeDtypeStruct((s, GDN_QK), BF16),
                   jax.ShapeDtypeStruct((s, LANES), BF16),
                   jax.ShapeDtypeStruct((1, LANES), F32), jax.ShapeDtypeStruct((1, LANES), F32), jax.ShapeDtypeStruct((1, LANES), F32)],
        scratch_shapes=[pltpu.VMEM((GDN_HALO + t, 3 * GDN_QK), F32)] + [pltpu.VMEM((t, GDN_QK), F32)] * 3
                       + [pltpu.VMEM((t, LANES), F32)] * 2 + [pltpu.VMEM((t, GDN_QK), F32)] * 4
                       + [pltpu.VMEM((t, LANES), F32)] * 2 + [pltpu.VMEM((GDN_HEADS * GDN_DK, GDN_DK), F32)],
        compiler_params=_params(("arbitrary",)),
    )(dhb, p, p, p, p, p, p, p, p_ba, o, states, minv, conv_w, a_log, dt_bias, norm_g)


def _zero_at_first_step(*refs):
    @pl.when(pl.program_id(0) == 0)
    def _():
        for r in refs:
            r[...] = jnp.zeros_like(r)


def _ln_bwd_call(dy, x_in, g, *, name):
    s, d = dy.shape
    t = min(ROW_TILE, s)

    def body(dy_ref, x_ref, g_ref, dx_ref, dxbf_ref, dg_ref, db_ref, ds_ref):
        _zero_at_first_step(dg_ref, db_ref, ds_ref)
        dy = dy_ref[...]
        xhat, rstd = _ln_stats(x_ref[...])
        dx = _ln_bwd(dy, xhat, rstd, g_ref[...])
        dx_ref[...] = dx
        dxbf_ref[...] = dx.astype(BF16)
        dg_ref[...] += _colsum(dy * xhat)
        db_ref[...] += _colsum(dy)
        ds_ref[...] += _colsum(dx)

    vec = _const_spec((1, d))
    return pl.pallas_call(
        body, name=name, grid=(s // t,),
        in_specs=[_row_spec(t, d), _row_spec(t, d), vec],
        out_specs=[_row_spec(t, d), _row_spec(t, d), vec, vec, vec],
        out_shape=[jax.ShapeDtypeStruct((s, d), F32), jax.ShapeDtypeStruct((s, d), BF16)] + [jax.ShapeDtypeStruct((1, d), F32)] * 3,
        compiler_params=_params(("arbitrary",)),
    )(dy, x_in, g)


def _ff2_bwd(dr2_bf, w_ff2, hpre, tag):
    s = dr2_bf.shape[0]
    tm = min(1024, s)
    tn = 1024

    def body(d_ref, w_ref, hp_ref, o_ref, db_ref):
        @pl.when(pl.program_id(1) == 0)
        def _():
            db_ref[...] = jnp.zeros_like(db_ref)

        dh = _dot_nt(d_ref[...], w_ref[...]) * (2.0 * jnp.maximum(hp_ref[...].astype(F32), 0.0))
        o_ref[...] = dh.astype(BF16)
        db_ref[...] += _colsum(dh)

    return pl.pallas_call(
        body, name="ff2_bwd" + tag, grid=(D_FF // tn, s // tm),
        in_specs=[pl.BlockSpec((tm, D_MODEL), lambda j, i: (i, 0)), pl.BlockSpec((tn, D_MODEL), lambda j, i: (j, 0)),
                  pl.BlockSpec((tm, tn), lambda j, i: (i, j))],
        out_specs=[pl.BlockSpec((tm, tn), lambda j, i: (i, j)), pl.BlockSpec((1, tn), lambda j, i: (0, j))],
        out_shape=[jax.ShapeDtypeStruct((s, D_FF), BF16), jax.ShapeDtypeStruct((1, D_FF), F32)],
        compiler_params=_params(("parallel", "arbitrary")),
    )(dr2_bf, w_ff2, hpre)


def _ff1_bwd_ln(dhpre_bf, w_ff1, dr2, r1, ln1_g, tag):
    s = dr2.shape[0]
    t = min(ROW_TILE, s)

    def body(dh_ref, w_ref, dr2_ref, r1_ref, g_ref, dr_ref, drbf_ref, dg_ref, db_ref):
        _zero_at_first_step(dg_ref, db_ref)
        dx1 = DN_ALPHA * dr2_ref[...] + _dot_nt(dh_ref[...], w_ref[...])
        xhat, rstd = _ln_stats(r1_ref[...])
        dr = _ln_bwd(dx1, xhat, rstd, g_ref[...])
        dr_ref[...] = dr
        drbf_ref[...] = dr.astype(BF16)
        dg_ref[...] += _colsum(dx1 * xhat)
        db_ref[...] += _colsum(dx1)

    vec = _const_spec((1, D_MODEL))
    return pl.pallas_call(
        body, name="ff1_bwd_ln" + tag, grid=(s // t,),
        in_specs=[_row_spec(t, D_FF), _const_spec((D_MODEL, D_FF)), _row_spec(t, D_MODEL), _row_spec(t, D_MODEL), vec],
        out_specs=[_row_spec(t, D_MODEL), _row_spec(t, D_MODEL), vec, vec],
        out_shape=[jax.ShapeDtypeStruct((s, D_MODEL), F32), jax.ShapeDtypeStruct((s, D_MODEL), BF16),
                   jax.ShapeDtypeStruct((1, D_MODEL), F32), jax.ShapeDtypeStruct((1, D_MODEL), F32)],
        compiler_params=_params(("arbitrary",)),
    )(dhpre_bf, w_ff1, dr2, r1, ln1_g)


def _merge_bwd(dr1_bf, w_o, y3, p, b_gate, w_pa, w_pb, w_pc, tag):
    s = p.shape[0]
    t = min(ROW_TILE, s)
    gb = COL_GATE // D_MODEL

    def body(dr_ref, wo_ref, y_ref, ga_ref, gb_ref, gc_ref, bg_ref, wa_ref, wb_ref, wc_ref,
             dgate_ref, dy_ref, dha_ref, dhb_ref, dhc_ref, dbg_ref):
        _zero_at_first_step(dbg_ref)
        dm = _dot_nt(dr_ref[...], wo_ref[...])
        for idx, (g_ref, w_ref, dh_ref) in enumerate(((ga_ref, wa_ref, dha_ref), (gb_ref, wb_ref, dhb_ref), (gc_ref, wc_ref, dhc_ref))):
            sl = slice(idx * D_MODEL, (idx + 1) * D_MODEL)
            sg = _sigmoid(g_ref[...].astype(F32) + bg_ref[:, sl])
            dgate = dm * y_ref[:, sl].astype(F32) * sg * (1.0 - sg)
            dgate_ref[:, sl] = dgate.astype(BF16)
            dbg_ref[:, sl] += _colsum(dgate)
            dy = (dm * sg).astype(BF16)
            dy_ref[:, sl] = dy
            dh_ref[...] = _dot_nt(dy, w_ref[...])

    hspec = _row_spec(t, CONV_DIM)
    wspec = _const_spec((CONV_DIM, D_MODEL))
    return pl.pallas_call(
        body, name="merge_bwd" + tag, grid=(s // t,),
        in_specs=[_row_spec(t, D_MODEL), _const_spec((D_MODEL, D_MODEL)), _row_spec(t, 3 * D_MODEL),
                  _row_spec(t, D_MODEL, gb), _row_spec(t, D_MODEL, gb + 1), _row_spec(t, D_MODEL, gb + 2),
                  _const_spec((1, 3 * D_MODEL)), wspec, wspec, wspec],
        out_specs=[_row_spec(t, 3 * D_MODEL), _row_spec(t, 3 * D_MODEL), hspec, hspec, hspec, _const_spec((1, 3 * D_MODEL))],
        out_shape=[jax.ShapeDtypeStruct((s, 3 * D_MODEL), BF16), jax.ShapeDtypeStruct((s, 3 * D_MODEL), BF16)]
                  + [jax.ShapeDtypeStruct((s, CONV_DIM), F32)] * 3 + [jax.ShapeDtypeStruct((1, 3 * D_MODEL), F32)],
        compiler_params=_params(("arbitrary",)),
    )(dr1_bf, w_o, y3, p, p, p, b_gate, w_pa, w_pb, w_pc)


def _conv_a_bwd_pre(dha, c, ln_g, ln_b, tag):
    s = c.shape[0]
    t = min(ROW_TILE, s)

    def body(dh_ref, c_ref, g_ref, b_ref, dc_ref, dg_ref, db_ref, ds_ref):
        _zero_at_first_step(dg_ref, db_ref, ds_ref)
        xhat, rstd = _ln_stats(c_ref[...])
        n = xhat * g_ref[...] + b_ref[...]
        dn = dh_ref[...] * _dsilu(n)
        dc = _ln_bwd(dn, xhat, rstd, g_ref[...])
        dc_ref[...] = dc
        dg_ref[...] += _colsum(dn * xhat)
        db_ref[...] += _colsum(dn)
        ds_ref[...] += _colsum(dc)

    vec = _const_spec((1, CONV_DIM))
    return pl.pallas_call(
        body, name="conv_a_bwd_pre" + tag, grid=(s // t,),
        in_specs=[_row_spec(t, CONV_DIM), _row_spec(t, CONV_DIM), vec, vec],
        out_specs=[_row_spec(t, CONV_DIM), vec, vec, vec],
        out_shape=[jax.ShapeDtypeStruct((s, CONV_DIM), F32)] + [jax.ShapeDtypeStruct((1, CONV_DIM), F32)] * 3,
        compiler_params=_params(("arbitrary",)),
    )(dha, c, ln_g, ln_b)


def _dwconv_bwd(dy, x, w, *, width, halo, x_col_block, glu_p, name):
    s, ctot = dy.shape
    ct = CONV_DIM
    t = min(ROW_TILE, s)
    nblk = s // t
    glu = glu_p is not None

    def body(*refs):
        if glu:
            dy_ref, dyh_ref, x_ref, xh_ref, w_ref, a_ref, dx_ref, dw_ref, dye, xe, dwacc = refs
        else:
            dy_ref, dyh_ref, x_ref, xh_ref, w_ref, dx_ref, dw_ref, dye, xe, dwacc = refs
        i = pl.program_id(1)

        @pl.when(i == 0)
        def _():
            dw_ref[...] = jnp.zeros_like(dw_ref)

        dye[0:t, :] = dy_ref[...]
        dye[t:t + halo, :] = jnp.where(i == nblk - 1, 0.0, dyh_ref[...])
        xe[0:halo, :] = jnp.where(i == 0, 0.0, xh_ref[...].astype(F32))
        xe[halo:halo + t, :] = x_ref[...].astype(F32)
        dwacc[...] = jnp.zeros_like(dwacc)
        dx_groups = _tap_groups([width - 1 - j for j in range(width)])
        dw_groups = _tap_groups([halo - (width - 1) + j for j in range(width)])
        for r0 in range(0, t, CONV_ROWS):
            rows = slice(r0, r0 + CONV_ROWS)
            dx = _tap_sum(dye, w_ref, dx_groups, r0)
            if glu:
                a = a_ref[rows, :].astype(F32)
                a1 = a[:, :ct]
                sg = _sigmoid(a[:, ct:])
                dx_ref[rows, :ct] = (dx * sg).astype(BF16)
                dx_ref[rows, ct:] = (dx * a1 * sg * (1.0 - sg)).astype(BF16)
            else:
                dx_ref[rows, :] = dx.astype(BF16)
            dyt = dy_ref[rows, :]
            for (_, _, _, taps), win in zip(dw_groups, _tap_windows(xe, dw_groups, r0)):
                for j, a in taps:
                    prod = dyt * win[a:a + CONV_ROWS]
                    part = prod[0:8]
                    for q in range(8, CONV_ROWS, 8):
                        part = part + prod[q:q + 8]
                    dwacc[8 * j:8 * j + 8, :] += part
        for j in range(width):
            dw_ref[j:j + 1, :] += _colsum(dwacc[8 * j:8 * j + 8, :])

    per = t // halo
    in_specs = [pl.BlockSpec((t, ct), lambda cb, i: (i, cb)),
                pl.BlockSpec((halo, ct), lambda cb, i: (jnp.minimum((i + 1) * per, nblk * per - 1), cb)),
                pl.BlockSpec((t, ct), lambda cb, i: (i, cb + x_col_block)),
                pl.BlockSpec((halo, ct), lambda cb, i: (jnp.maximum(i * per - 1, 0), cb + x_col_block)),
                pl.BlockSpec((width, ct), lambda cb, i: (0, cb))]
    args = [dy, dy, x, x, w]
    out_cols = ctot
    if glu:
        in_specs.append(pl.BlockSpec((t, 2 * ct), lambda cb, i: (i, COL_A // (2 * ct))))
        args.append(glu_p)
        out_cols = 2 * ct
    ocol = 2 * ct if glu else ct
    return pl.pallas_call(
        body, name=name, grid=(ctot // ct, nblk), in_specs=in_specs,
        out_specs=[pl.BlockSpec((t, ocol), lambda cb, i: (i, cb)), pl.BlockSpec((width, ct), lambda cb, i: (0, cb))],
        out_shape=[jax.ShapeDtypeStruct((s, out_cols), BF16), jax.ShapeDtypeStruct((width, ctot), F32)],
        scratch_shapes=[pltpu.VMEM((t + halo, ct), F32), pltpu.VMEM((halo + t, ct), F32), pltpu.VMEM((8 * width, ct), F32)],
        compiler_params=_params(("parallel", "arbitrary")),
    )(*args)


def _sgu_bwd(dhc, p, ln_g, ln_b, w_s, b_s_t, tag):
    s = p.shape[0]
    t = min(ROW_TILE, s)
    cs = SGU_CHUNK

    def body(dh_ref, uv_ref, g_ref, b_ref, ws_ref, bst_ref, duv_ref, dg_ref, db_ref, dws_ref, dbs_ref):
        _zero_at_first_step(dg_ref, db_ref, dws_ref, dbs_ref)
        uv = uv_ref[...].astype(F32)
        u_raw, v_raw = uv[:, :SGU_DIM], uv[:, SGU_DIM:]
        u = _gelu(u_raw)
        xhat, rstd = _ln_stats(_gelu(v_raw))
        vn = xhat * g_ref[...] + b_ref[...]
        mixed = _sgu_mix(vn, ws_ref, bst_ref, t)
        dh = dh_ref[...]
        duv_ref[:, :SGU_DIM] = (dh * mixed * _dgelu(u_raw)).astype(BF16)
        dmix = dh * u
        row = lax.broadcasted_iota(jnp.int32, (cs, cs), 0)
        col = lax.broadcasted_iota(jnp.int32, (cs, cs), 1)
        dbs = jnp.zeros((cs, LANES), F32)
        chunks = []
        for g in range(SGU_GROUPS):
            wg = jnp.where(row >= col, ws_ref[g], 0.0).astype(BF16)
            dwg = jnp.zeros((cs, cs), F32)
            parts = []
            for ci in range(t // cs):
                rs = slice(ci * cs, (ci + 1) * cs)
                cl = slice(g * SGU_GROUP_DIM, (g + 1) * SGU_GROUP_DIM)
                dm = dmix[rs, cl]
                dmb = dm.astype(BF16)
                dwg = dwg + _dot_nt(dmb, vn[rs, cl].astype(BF16))
                dbs = dbs + _lane_place(jnp.sum(dm, axis=-1, keepdims=True), g, (cs, LANES))
                parts.append(_dot_tn(wg, dmb))
            dws_ref[g] += jnp.where(row >= col, dwg, 0.0)
            chunks.append(jnp.concatenate(parts, axis=0))
        dbs_ref[...] += dbs
        dvn = jnp.concatenate(chunks, axis=1)
        dvv = _ln_bwd(dvn, xhat, rstd, g_ref[...])
        duv_ref[:, SGU_DIM:] = (dvv * _dgelu(v_raw)).astype(BF16)
        dg_ref[...] += _colsum(dvn * xhat)
        db_ref[...] += _colsum(dvn)

    vec = _const_spec((1, SGU_DIM))
    wss = _const_spec((SGU_GROUPS, cs, cs))
    return pl.pallas_call(
        body, name="sgu_bwd" + tag, grid=(s // t,),
        in_specs=[_row_spec(t, SGU_DIM), _row_spec(t, 2 * SGU_DIM, COL_UV // (2 * SGU_DIM)), vec, vec, wss, _const_spec((cs, LANES))],
        out_specs=[_row_spec(t, 2 * SGU_DIM), vec, vec, wss, _const_spec((cs, LANES))],
        out_shape=[jax.ShapeDtypeStruct((s, 2 * SGU_DIM), BF16), jax.ShapeDtypeStruct((1, SGU_DIM), F32),
                   jax.ShapeDtypeStruct((1, SGU_DIM), F32), jax.ShapeDtypeStruct((SGU_GROUPS, cs, cs), F32),
                   jax.ShapeDtypeStruct((cs, LANES), F32)],
        compiler_params=_params(("arbitrary",)),
    )(dhc, p, ln_g, ln_b, w_s, b_s_t)


def _reorder_proj_cols(w):
    pad = jnp.zeros(w.shape[:-1] + (P_COLS - PROJ_COLS,), w.dtype)
    return jnp.concatenate([w[..., :3072], w[..., 3080:PROJ_COLS], w[..., 3072:3080], pad], axis=-1)


def _restore_proj_cols(g):
    return jnp.concatenate([g[..., :3072], g[..., COL_BA:COL_BA + 8], g[..., 3072:COL_BA]], axis=-1)


def _pad_lanes(v, offset):
    return jnp.pad(v, (offset, LANES - offset - v.shape[0]))[None, :]


def _proj_weights(w_in_l):
    w_all = _reorder_proj_cols(w_in_l)
    return dict(w_all=w_all, w_ba=w_all[:, COL_BA:])


def _rest_weights(l, full, rep):
    row = lambda v: v[l][None, :]
    return dict(
        conv_w=full["conv_dw_w"], conv_b=row(rep["conv_dw_b"]), conv_ln_g=row(rep["conv_ln_g"]), conv_ln_b=row(rep["conv_ln_b"]),
        w_pa=full["w_pa"], w_pb=full["w_pb"], w_pc=full["w_pc"],
        gdn_cw=jnp.concatenate([full["gdn_conv_q"], full["gdn_conv_k"], full["gdn_conv_v"]], axis=-1),
        a_log=_pad_lanes(rep["gdn_a_log"][l], GDN_HEADS), dt_bias=_pad_lanes(rep["gdn_dt_bias"][l], GDN_HEADS),
        norm_g=row(rep["gdn_norm_g"]),
        sgu_ln_g=row(rep["sgu_ln_g"]), sgu_ln_b=row(rep["sgu_ln_b"]), sgu_w_s=rep["sgu_w_s"][l],
        sgu_b_s_t=jnp.pad(rep["sgu_b_s"][l].T, ((0, 0), (0, LANES - SGU_GROUPS))),
        b_gate=row(rep["b_gate"]),
        w_o=full["w_o"], ln1_g=row(rep["ln1_g"]), ln1_b=row(rep["ln1_b"]),
        w_ff1=full["w_ff1"], b_ff1=row(rep["b_ff1"]),
        w_ff2=full["w_ff2"], b_ff2=row(rep["b_ff2"]),
        ln2_g=row(rep["ln2_g"]), ln2_b=row(rep["ln2_b"]),
    )


def _layer_fwd(x, x_bf, w_proj, rest_of, tag, proj_exchange=None, ff1_exchange=None):
    s = x.shape[0]
    p = _matmul(x_bf, w_proj["w_all"], name="proj_fwd" + tag, tm=min(1024, s), tn=P_TILE, tk=D_MODEL, out_dtype=BF16,
                exchange=proj_exchange)
    p_ba = _matmul(x_bf, w_proj["w_ba"], name="proj_ba_fwd" + tag, tm=min(2048, s), tn=LANES, tk=D_MODEL)
    got_proj = None
    if proj_exchange is not None:
        p, got_proj = p
    w = dict(w_proj, **rest_of(got_proj))
    h_glu, c, ha = _conv_a_fwd(p, w["conv_w"], w["conv_b"], w["conv_ln_g"], w["conv_ln_b"], tag)
    o, hb, states, minv = _gdn_fwd(p, p_ba, w["gdn_cw"], w["a_log"], w["dt_bias"], w["norm_g"], tag)
    hc = _sgu_fwd(p, w["sgu_ln_g"], w["sgu_ln_b"], w["sgu_w_s"], w["sgu_b_s_t"], tag)
    y3, merged = _merge_fwd(p, ha, hb, hc, w["w_pa"], w["w_pb"], w["w_pc"], w["b_gate"], tag)
    r1, x1, x1_bf = _matmul_res_ln(merged, w["w_o"], jnp.zeros((1, D_MODEL), F32), x, w["ln1_g"], w["ln1_b"], name="o_res_ln" + tag)
    hpre, h_bf, got_ff1 = _ff1_fwd(x1_bf, w["w_ff1"], w["b_ff1"], tag, exchange=ff1_exchange)
    r2, x2, x2_bf = _matmul_res_ln(h_bf, w["w_ff2"], w["b_ff2"], x1, w["ln2_g"], w["ln2_b"], name="ff2_res_ln" + tag)
    saved = dict(x=x, x_bf=x_bf, p=p, p_ba=p_ba, h_glu=h_glu, c=c, ha=ha, o=o, hb=hb, states=states, minv=minv, hc=hc, y3=y3,
                 merged=merged, r1=r1, x1=x1, x1_bf=x1_bf, hpre=hpre, h_bf=h_bf, r2=r2)
    return x2, x2_bf, saved, w, got_ff1


def _layer_bwd(dx2, w, sv, tag, dw_proj_exchange=None, proj_exchange_of=None):
    s = dx2.shape[0]
    ts = min(1024, s)
    p = sv["p"]
    dr2, dr2_bf, d_ln2_g, d_ln2_b, d_b_ff2 = _ln_bwd_call(dx2, sv["r2"], w["ln2_g"], name="ln2_bwd" + tag)
    dhpre_bf, d_b_ff1 = _ff2_bwd(dr2_bf, w["w_ff2"], sv["hpre"], tag)
    d_w_ff2 = _matmul_tn(sv["h_bf"], dr2_bf, name="dw_ff2" + tag, ka=D_FF, tka=1024, tn=1024, ts=ts)
    d_w_ff1 = _matmul_tn(sv["x1_bf"], dhpre_bf, name="dw_ff1" + tag, ka=D_MODEL, tka=1024, tn=1024, ts=ts)
    dr1, dr1_bf, d_ln1_g, d_ln1_b = _ff1_bwd_ln(dhpre_bf, w["w_ff1"], dr2, sv["r1"], w["ln1_g"], tag)
    d_w_o = _matmul_tn(sv["merged"], dr1_bf, name="dw_o" + tag, ka=D_MODEL, tka=1024, tn=1024, ts=ts)
    dgate_bf, dy3_bf, dha, dhb, dhc, d_b_gate = _merge_bwd(dr1_bf, w["w_o"], sv["y3"], p, w["b_gate"],
                                                          w["w_pa"], w["w_pb"], w["w_pc"], tag)
    d_w_p = [_matmul_tn(h, dy3_bf, name=f"dw_p{n}" + tag, ka=CONV_DIM, tka=CONV_DIM, tn=1024, ts=ts, n=D_MODEL, b_col_block=n)
             for n, h in enumerate((sv["ha"], sv["hb"], sv["hc"]))]
    dc, d_conv_ln_g, d_conv_ln_b, d_conv_b = _conv_a_bwd_pre(dha, sv["c"], w["conv_ln_g"], w["conv_ln_b"], tag)
    da_bf, d_conv_w = _dwconv_bwd(dc, sv["h_glu"], w["conv_w"], width=CONV_WIDTH, halo=CONV_HALO, x_col_block=0,
                                  glu_p=p, name="conv_a_bwd" + tag)
    duv_bf, d_sgu_ln_g, d_sgu_ln_b, d_sgu_w_s, d_sgu_b_s_t = _sgu_bwd(dhc, p, w["sgu_ln_g"], w["sgu_ln_b"], w["sgu_w_s"], w["sgu_b_s_t"], tag)
    dpre, dz_bf, dba_bf, d_norm_g, d_a_log, d_dt_bias = _gdn_bwd(dhb, p, sv["p_ba"], sv["o"], sv["states"], sv["minv"], w["gdn_cw"],
                                                                 w["a_log"], w["dt_bias"], w["norm_g"], tag)
    dqkv_bf, d_gdn_cw = _dwconv_bwd(dpre, p, w["gdn_cw"], width=GDN_CONV, halo=GDN_HALO, x_col_block=COL_Q // CONV_DIM,
                                    glu_p=None, name="gdn_conv_bwd" + tag)
    dp_bf = jnp.concatenate([da_bf, dqkv_bf, dz_bf, duv_bf, dgate_bf, dba_bf], axis=1)
    d_w_all = _matmul_tn(sv["x_bf"], dp_bf, name="dw_proj" + tag, ka=D_MODEL, tka=1024, tn=P_TILE, ts=ts, exchange=dw_proj_exchange)
    got_dw = None
    if dw_proj_exchange is not None:
        d_w_all, got_dw = d_w_all
    grads = dict(
        w_in=_restore_proj_cols(d_w_all), b_gate=d_b_gate[0], conv_dw_w=d_conv_w, conv_dw_b=d_conv_b[0],
        conv_ln_g=d_conv_ln_g[0], conv_ln_b=d_conv_ln_b[0], w_pa=d_w_p[0],
        gdn_conv_q=d_gdn_cw[:, :GDN_QK], gdn_conv_k=d_gdn_cw[:, GDN_QK:2 * GDN_QK], gdn_conv_v=d_gdn_cw[:, 2 * GDN_QK:],
        gdn_a_log=d_a_log[0, GDN_HEADS:2 * GDN_HEADS], gdn_dt_bias=d_dt_bias[0, GDN_HEADS:2 * GDN_HEADS], gdn_norm_g=d_norm_g[0],
        w_pb=d_w_p[1], sgu_ln_g=d_sgu_ln_g[0], sgu_ln_b=d_sgu_ln_b[0], sgu_w_s=d_sgu_w_s, sgu_b_s=d_sgu_b_s_t[:, :SGU_GROUPS].T,
        w_pc=d_w_p[2], w_o=d_w_o, ln1_g=d_ln1_g[0], ln1_b=d_ln1_b[0], w_ff1=d_w_ff1, b_ff1=d_b_ff1[0],
        w_ff2=d_w_ff2, b_ff2=d_b_ff2[0], ln2_g=d_ln2_g[0], ln2_b=d_ln2_b[0],
    )
    proj_exchange = None if proj_exchange_of is None else proj_exchange_of(grads)
    dx = _matmul(dp_bf, w["w_all"], name="proj_bwd" + tag, tm=min(1024, s), tn=D_MODEL, tk=P_TILE, add=dr1,
                 add_scale=DN_ALPHA, w_is_nk=True, exchange=proj_exchange)
    got_proj = None
    if proj_exchange is not None:
        dx, got_proj = dx
    return dx, grads, got_dw, got_proj


MESH_AXES = ("x", "y", "c")


def _exchange(arrays, scatter, *, name):
    n = len(arrays)
    ex = _Exchange(arrays, scatter)

    def body(*refs):
        ins, outs, sems = refs[:n], refs[n:2 * n], refs[2 * n:]
        _exchange_start(ex, ins, outs, sems)
        _exchange_wait(ex, ins, outs, sems)

    return pl.pallas_call(
        body, name=name, in_specs=ex.in_specs(), out_specs=ex.in_specs(), out_shape=ex.out_shapes(),
        scratch_shapes=ex.scratch(),
    )(*arrays)


class _Exchange:
    def __init__(self, arrays, scatter):
        self.arrays = list(arrays)
        self.scatter = list(scatter)
        self.n = len(self.arrays)

    def in_specs(self):
        return [pl.BlockSpec(memory_space=pl.ANY)] * self.n

    def out_shapes(self):
        return [jax.ShapeDtypeStruct(a.shape if s else (N_DEV,) + a.shape, a.dtype) for a, s in zip(self.arrays, self.scatter)]

    def scratch(self):
        return [pltpu.SemaphoreType.DMA((self.n, N_DEV - 1)), pltpu.SemaphoreType.DMA((self.n, N_DEV - 1)),
                pltpu.SemaphoreType.DMA((self.n,))]


def _exchange_copies(ex, ins, outs, sems, with_arrivals):
    send_sems, recv_sems, local_sems = sems
    x, y, c = lax.axis_index("x"), lax.axis_index("y"), lax.axis_index("c")
    me = 4 * x + 2 * y + c

    def slot(a, d):
        return ins[a].at[d] if ex.scatter[a] else ins[a]

    local = [pltpu.make_async_copy(slot(a, me), outs[a].at[me], local_sems.at[a]) for a in range(ex.n)]
    remote = []
    for k in range(1, N_DEV):
        px = 1 - x if k & 4 else x
        py = 1 - y if k & 2 else y
        pc = 1 - c if k & 1 else c
        peer = 4 * px + 2 * py + pc
        for a in range(ex.n):
            send = pltpu.make_async_remote_copy(
                src_ref=slot(a, peer), dst_ref=outs[a].at[me], send_sem=send_sems.at[a, k - 1],
                recv_sem=recv_sems.at[a, k - 1], device_id=(px, py, pc), device_id_type=pl.DeviceIdType.MESH)
            arrival = pltpu.make_async_remote_copy(
                src_ref=slot(a, peer), dst_ref=outs[a].at[peer], send_sem=send_sems.at[a, k - 1],
                recv_sem=recv_sems.at[a, k - 1], device_id=(px, py, pc), device_id_type=pl.DeviceIdType.MESH) if with_arrivals else None
            remote.append((send, arrival))
    return local, remote


def _exchange_start(ex, ins, outs, sems):
    local, remote = _exchange_copies(ex, ins, outs, sems, False)
    for cp in local:
        cp.start()
    for send, _ in remote:
        send.start()


def _exchange_wait(ex, ins, outs, sems):
    local, remote = _exchange_copies(ex, ins, outs, sems, True)
    for _, arrival in remote:
        arrival.wait_recv()
    for send, _ in remote:
        send.wait_send()
    for cp in local:
        cp.wait()


def _adamw(w, m, v, g_parts, *, name):
    r, c = w.shape
    tr = 256 if r % 256 == 0 else r
    bc1 = 1.0 - ADAM_B1 ** ADAM_STEP
    bc2 = 1.0 - ADAM_B2 ** ADAM_STEP

    def body(w_ref, m_ref, v_ref, gp_ref, g_ref, d_ref, nm_ref, nv_ref):
        g = gp_ref[0].astype(F32)
        for d in range(1, N_DEV):
            g = g + gp_ref[d].astype(F32)
        nm = ADAM_B1 * m_ref[...] + (1.0 - ADAM_B1) * g
        nv = ADAM_B2 * v_ref[...] + (1.0 - ADAM_B2) * (g * g)
        g_ref[...] = g
        nm_ref[...] = nm
        nv_ref[...] = nv
        d_ref[...] = -ADAM_LR * ((nm / bc1) / (jnp.sqrt(nv / bc2) + ADAM_EPS) + ADAM_WD * w_ref[...])

    spec = pl.BlockSpec((tr, c), lambda i: (i, 0))
    return pl.pallas_call(
        body, name=name, grid=(r // tr,),
        in_specs=[spec, spec, spec, pl.BlockSpec((N_DEV, tr, c), lambda i: (0, i, 0))],
        out_specs=[spec] * 4, out_shape=[jax.ShapeDtypeStruct((r, c), F32)] * 4,
        compiler_params=_params(("parallel",)),
    )(w, m, v, g_parts)


SHARDED = dict(w_in=2, conv_dw_w=2, w_pa=2, gdn_conv_q=2, gdn_conv_k=2, gdn_conv_v=2, w_pb=2, w_pc=2, w_o=1, w_ff1=2, w_ff2=1)
WEIGHTS = ["ln_in_g", "ln_in_b", "w_in", "b_gate", "conv_dw_w", "conv_dw_b", "conv_ln_g", "conv_ln_b", "w_pa", "gdn_conv_q",
           "gdn_conv_k", "gdn_conv_v", "gdn_a_log", "gdn_dt_bias", "gdn_norm_g", "w_pb", "sgu_ln_g", "sgu_ln_b", "sgu_w_s",
           "sgu_b_s", "w_pc", "w_o", "ln1_g", "ln1_b", "w_ff1", "b_ff1", "w_ff2", "b_ff2", "ln2_g", "ln2_b"]
REPLICATED = [n for n in WEIGHTS if n not in SHARDED]
CONV_PACK = ["conv_dw_w", "gdn_conv_q", "gdn_conv_k", "gdn_conv_v"]
PROJ_PACK = ["w_pa", "w_pb", "w_pc"]


def _to_slots(full, axis):
    shp = full.shape
    split = full.reshape(shp[:axis] + (N_DEV, shp[axis] // N_DEV) + shp[axis + 1:])
    return jnp.moveaxis(split, axis, 0)


def _from_slots(slots, axis):
    merged = jnp.moveaxis(slots, 0, axis)
    shp = merged.shape
    return merged.reshape(shp[:axis] + (shp[axis] * shp[axis + 1],) + shp[axis + 2:])


def _pack_rows(arrs):
    rows = []
    for a in arrs:
        flat = a.reshape(-1)
        pad = (-flat.shape[0]) % LANES
        rows.append(jnp.pad(flat, (0, pad)).reshape(-1, LANES))
    out = jnp.concatenate(rows, axis=0)
    return jnp.pad(out, ((0, (-out.shape[0]) % 8), (0, 0)))


def _unpack_rows(packed, shapes):
    out, r = [], 0
    for shp in shapes:
        size = math.prod(shp)
        nrows = -(-size // LANES)
        out.append(packed[r:r + nrows].reshape(-1)[:size].reshape(shp))
        r += nrows
    return out


def kernel(x, ln_in_g, ln_in_b, w_in, b_gate, conv_dw_w, conv_dw_b, conv_ln_g, conv_ln_b, w_pa, gdn_conv_q, gdn_conv_k, gdn_conv_v, gdn_a_log, gdn_dt_bias, gdn_norm_g, w_pb, sgu_ln_g, sgu_ln_b, sgu_w_s, sgu_b_s, w_pc, w_o, ln1_g, ln1_b, w_ff1, b_ff1, w_ff2, b_ff2, ln2_g, ln2_b, loss_target, m_ln_in_g, m_ln_in_b, m_w_in, m_b_gate, m_conv_dw_w, m_conv_dw_b, m_conv_ln_g, m_conv_ln_b, m_w_pa, m_gdn_conv_q, m_gdn_conv_k, m_gdn_conv_v, m_gdn_a_log, m_gdn_dt_bias, m_gdn_norm_g, m_w_pb, m_sgu_ln_g, m_sgu_ln_b, m_sgu_w_s, m_sgu_b_s, m_w_pc, m_w_o, m_ln1_g, m_ln1_b, m_w_ff1, m_b_ff1, m_w_ff2, m_b_ff2, m_ln2_g, m_ln2_b, v_ln_in_g, v_ln_in_b, v_w_in, v_b_gate, v_conv_dw_w, v_conv_dw_b, v_conv_ln_g, v_conv_ln_b, v_w_pa, v_gdn_conv_q, v_gdn_conv_k, v_gdn_conv_v, v_gdn_a_log, v_gdn_dt_bias, v_gdn_norm_g, v_w_pb, v_sgu_ln_g, v_sgu_ln_b, v_sgu_w_s, v_sgu_b_s, v_w_pc, v_w_o, v_ln1_g, v_ln1_b, v_w_ff1, v_b_ff1, v_w_ff2, v_b_ff2, v_ln2_g, v_ln2_b):
    args = locals()
    w = {n: args[n] for n in WEIGHTS}
    m = {n: args["m_" + n] for n in WEIGHTS}
    v = {n: args["v_" + n] for n in WEIGHTS}

    rep = {n: w[n] for n in REPLICATED}
    conv_local = jnp.concatenate([w[n] for n in CONV_PACK], axis=1)
    proj_local = jnp.stack([w[n] for n in PROJ_PACK], axis=1).astype(BF16)
    big = ["w_in", "w_o", "w_ff1", "w_ff2"]
    big_local = {n: w[n].astype(BF16) for n in big}
    rest_local = lambda l: [big_local[n][l] for n in big[1:]] + [proj_local[l], conv_local[l]]

    def rest_full(got):
        full_l = {n: _from_slots(g, SHARDED[n] - 1) for n, g in zip(big[1:], got[:3])}
        proj_full = _from_slots(got[3], 2)
        for i, n in enumerate(PROJ_PACK):
            full_l[n] = proj_full[i]
        conv_full = _from_slots(got[4], 1)
        tap0 = 0
        for n in CONV_PACK:
            taps = w[n].shape[1]
            full_l[n] = conv_full[tap0:tap0 + taps]
            tap0 += taps
        return full_l

    ln_g, ln_b = w["ln_in_g"][None, :], w["ln_in_b"][None, :]
    xs, xs_bf = _ln_in_fwd(x[0], ln_g, ln_b)
    (w_in0,) = _exchange([big_local["w_in"][0]], [False], name="gather_w_in_l0")
    behind_proj = {}

    def rest_of_l0(got):
        behind_proj["w_in1"] = got[5]
        return _rest_weights(0, rest_full(got[:5]), rep)

    xs, xs_bf, sv0, w0, got_ff1 = _layer_fwd(
        xs, xs_bf, _proj_weights(_from_slots(w_in0, 1)), rest_of_l0, "_l0",
        proj_exchange=_Exchange(rest_local(0) + [big_local["w_in"][1]], [False] * 6),
        ff1_exchange=_Exchange(rest_local(1), [False] * 5))
    xs, xs_bf, sv1, w1, _ = _layer_fwd(
        xs, xs_bf, _proj_weights(_from_slots(behind_proj["w_in1"], 1)),
        lambda _: _rest_weights(1, rest_full(got_ff1), rep), "_l1")
    d, loss_acc = _loss_fwd_bwd(xs, loss_target[0])
    loss = lax.psum(loss_acc[0, 0], MESH_AXES)

    def matrix_grad_slots(g):
        proj_grad = jnp.stack([g[n] for n in PROJ_PACK], axis=0)
        return [_to_slots(g[n].astype(BF16), SHARDED[n] - 1) for n in big] + [_to_slots(proj_grad.astype(BF16), 2)]

    d, grads1, _, _ = _layer_bwd(d, w1, sv1, "_l1")
    d, grads0, parts1, parts0 = _layer_bwd(
        d, w0, sv0, "_l0", dw_proj_exchange=_Exchange(matrix_grad_slots(grads1), [True] * 5),
        proj_exchange_of=lambda g: _Exchange(matrix_grad_slots(g), [True] * 5))
    dx, _, d_ln_in_g, d_ln_in_b, _ = _ln_bwd_call(d, x[0], ln_g, name="ln_in_bwd")
    grads = {k: jnp.stack([grads0[k], grads1[k]]) for k in grads0}
    grads["ln_in_g"] = d_ln_in_g[0]
    grads["ln_in_b"] = d_ln_in_b[0]
    conv_grad = jnp.concatenate([grads[n] for n in CONV_PACK], axis=1)
    small_parts = _exchange([_to_slots(conv_grad, 2), _pack_rows([grads[n] for n in REPLICATED])], [True, False],
                            name="exchange_small_grads")
    parts = [jnp.stack([p0, p1], axis=1) for p0, p1 in zip(parts0, parts1)] + list(small_parts)

    def adam_sharded(g_parts, w_l, m_l, v_l, name):
        shp = w_l.shape
        two_d = lambda a: a.reshape(-1, shp[-1])
        outs = _adamw(two_d(w_l), two_d(m_l), two_d(v_l), g_parts.reshape(N_DEV, -1, shp[-1]), name=name)
        return [o.reshape(shp) for o in outs]

    res = {}
    for n, gp in zip(big, parts[:4]):
        res[n] = adam_sharded(gp, w[n], m[n], v[n], "adamw_" + n)
    proj_res = adam_sharded(parts[4], jnp.stack([w[n] for n in PROJ_PACK], axis=1), jnp.stack([m[n] for n in PROJ_PACK], axis=1),
                            jnp.stack([v[n] for n in PROJ_PACK], axis=1), "adamw_proj")
    for i, n in enumerate(PROJ_PACK):
        res[n] = [o[:, i] for o in proj_res]
    conv_res = adam_sharded(parts[5], conv_local, jnp.concatenate([m[n] for n in CONV_PACK], axis=1),
                            jnp.concatenate([v[n] for n in CONV_PACK], axis=1), "adamw_conv")
    tap0 = 0
    for n in CONV_PACK:
        taps = w[n].shape[1]
        res[n] = [o[:, tap0:tap0 + taps] for o in conv_res]
        tap0 += taps
    rep_shapes = [w[n].shape for n in REPLICATED]
    rep_res = _adamw(_pack_rows([w[n] for n in REPLICATED]), _pack_rows([m[n] for n in REPLICATED]),
                     _pack_rows([v[n] for n in REPLICATED]), parts[6], name="adamw_replicated")
    rep_res = [_unpack_rows(o, rep_shapes) for o in rep_res]
    for i, n in enumerate(REPLICATED):
        res[n] = [o[i] for o in rep_res]

    outs = [loss, dx[None]]
    for j in range(4):
        outs += [res[n][j] for n in WEIGHTS]
    return tuple(outs)
```

```python
import functools
import math

import jax
import jax.numpy as jnp
from jax import lax
from jax.experimental import pallas as pl
from jax.experimental.pallas import tpu as pltpu

F32 = jnp.float32
BF16 = jnp.bfloat16

N_DEV = 8
DEPTH = 2
D_MODEL = 1024
CONV_DIM = 512
CONV_WIDTH = 31
GDN_HEADS = 4
GDN_DK = 128
GDN_QK = 512
GDN_CONV = 4
GDN_CHUNK = 64
SGU_GROUPS = 4
SGU_GROUP_DIM = 128
SGU_DIM = 512
SGU_CHUNK = 128
D_FF = 4096
DN_ALPHA = (2 * DEPTH) ** 0.25
LN_EPS = 1e-5
RMS_EPS = 1e-6
PROJ_COLS = 7176
SHARD_COLS = PROJ_COLS // N_DEV

COL_A = 0
COL_Q = 1024
COL_Z = 2560
COL_UV = 3072
COL_GATE = 4096
COL_BA = 7168
P_COLS = 7296
P_TILE = 2432

ADAM_LR = 0.001
ADAM_B1 = 0.9
ADAM_B2 = 0.999
ADAM_EPS = 1e-08
ADAM_WD = 0.01
ADAM_STEP = 10

VMEM_LIMIT_BYTES = 56 * 1024 * 1024
LANES = 128
ROW_TILE = 512
GDN_BLOCK = 256
CONV_HALO = 32
GDN_HALO = 16


def _params(sem):
    return pltpu.CompilerParams(dimension_semantics=sem, vmem_limit_bytes=VMEM_LIMIT_BYTES)


def _dot(a, b):
    return jnp.dot(a, b, preferred_element_type=F32)


def _dot_nt(a, b):
    return lax.dot_general(a, b, (((1,), (1,)), ((), ())), preferred_element_type=F32)


def _dot_tn(a, b):
    return lax.dot_general(a, b, (((0,), (0,)), ((), ())), preferred_element_type=F32)


def _split(a):
    hi = a.astype(BF16)
    lo = (a - hi.astype(F32)).astype(BF16)
    return hi, lo


def _dot3(a, b, dot=_dot):
    ah, al = _split(a)
    bh, bl = _split(b)
    return dot(ah, bh) + (dot(ah, bl) + dot(al, bh))


def _bdot(a, b, dot=_dot):
    return dot(a.astype(BF16), b.astype(BF16))


def _sigmoid(x):
    return jax.nn.sigmoid(x)


def _silu(x):
    return x * _sigmoid(x)


def _dsilu(x):
    s = _sigmoid(x)
    return s * (1.0 + x * (1.0 - s))


_GELU_C = math.sqrt(2.0 / math.pi)


def _gelu(x):
    return 0.5 * x * (1.0 + jnp.tanh(_GELU_C * (x + 0.044715 * (x * x * x))))


def _dgelu(x):
    t = jnp.tanh(_GELU_C * (x + 0.044715 * (x * x * x)))
    return 0.5 * (1.0 + t) + 0.5 * x * (1.0 - t * t) * (_GELU_C * (1.0 + 3.0 * 0.044715 * (x * x)))


def _ln_stats(x):
    mu = jnp.mean(x, axis=-1, keepdims=True)
    xc = x - mu
    var = jnp.mean(xc * xc, axis=-1, keepdims=True)
    rstd = lax.rsqrt(var + LN_EPS)
    return xc * rstd, rstd


def _ln_bwd(dy, xhat, rstd, g):
    dxh = dy * g
    return rstd * (dxh - jnp.mean(dxh, axis=-1, keepdims=True) - xhat * jnp.mean(dxh * xhat, axis=-1, keepdims=True))


def _colsum(x):
    return jnp.sum(x, axis=0, keepdims=True)


def _row_spec(t, cols, col_block=0):
    return pl.BlockSpec((t, cols), lambda i, cb=col_block: (i, cb))


def _const_spec(shape):
    nd = len(shape)
    return pl.BlockSpec(shape, lambda *_: (0,) * nd)


def _call(body, *, name, grid, in_specs, out_specs, out_shape, args, sem, scratch=(), exchange=None):
    n_in, n_out, n_sc = len(in_specs), len(out_specs), len(scratch)
    if exchange is None:
        def plain(*refs):
            body(refs[:n_in], refs[n_in:n_in + n_out], refs[n_in + n_out:])

        return pl.pallas_call(plain, name=name, grid=grid, in_specs=in_specs, out_specs=out_specs, out_shape=out_shape,
                              scratch_shapes=list(scratch), compiler_params=_params(sem))(*args)
    nex = exchange.n

    def carrying(*refs):
        ins, ex_ins = refs[:n_in], refs[n_in:n_in + nex]
        outs, ex_outs = refs[n_in + nex:n_in + nex + n_out], refs[n_in + nex + n_out:n_in + 2 * nex + n_out]
        sc, sems = refs[n_in + 2 * nex + n_out:n_in + 2 * nex + n_out + n_sc], refs[n_in + 2 * nex + n_out + n_sc:]
        ids = [pl.program_id(d) for d in range(len(grid))]
        first = functools.reduce(jnp.logical_and, [i == 0 for i in ids])
        last = functools.reduce(jnp.logical_and, [i == g - 1 for i, g in zip(ids, grid)])

        @pl.when(first)
        def _():
            _exchange_start(exchange, ex_ins, ex_outs, sems)

        body(ins, outs, sc)

        @pl.when(last)
        def _():
            _exchange_wait(exchange, ex_ins, ex_outs, sems)

    res = pl.pallas_call(
        carrying, name=name, grid=grid, in_specs=list(in_specs) + exchange.in_specs(),
        out_specs=list(out_specs) + exchange.in_specs(), out_shape=list(out_shape) + exchange.out_shapes(),
        scratch_shapes=list(scratch) + exchange.scratch(), compiler_params=_params(("arbitrary",) * len(grid)),
    )(*args, *exchange.arrays)
    return res[:n_out], res[n_out:]


def _matmul(a, w, *, name, tm, tn, tk, out_dtype=F32, a_col_block=0, add=None, add_scale=1.0, w_is_nk=False, exchange=None):
    m = a.shape[0]
    k, n = w.shape[::-1] if w_is_nk else w.shape
    nk = k // tk
    has_add = add is not None

    def body(ins, outs, scratch):
        a_ref, w_ref = ins[:2]
        o_ref, acc_ref = outs[0], scratch[0]
        if has_add:
            add_ref = ins[2]
        kk = pl.program_id(2)

        @pl.when(kk == 0)
        def _():
            acc_ref[...] = jnp.zeros_like(acc_ref)

        acc_ref[...] += (_dot_nt if w_is_nk else _dot)(a_ref[...], w_ref[...])

        @pl.when(kk == nk - 1)
        def _():
            r = acc_ref[...]
            if has_add:
                r = r + add_scale * add_ref[...]
            o_ref[...] = r.astype(out_dtype)

    in_specs = [pl.BlockSpec((tm, tk), lambda j, i, kk: (i, kk + a_col_block)),
                pl.BlockSpec((tn, tk), lambda j, i, kk: (j, kk)) if w_is_nk else pl.BlockSpec((tk, tn), lambda j, i, kk: (kk, j))]
    args = [a, w]
    if has_add:
        in_specs.append(pl.BlockSpec((tm, tn), lambda j, i, kk: (i, j)))
        args.append(add)
    res = _call(body, name=name, grid=(n // tn, m // tm, nk), in_specs=in_specs,
                out_specs=[pl.BlockSpec((tm, tn), lambda j, i, kk: (i, j))],
                out_shape=[jax.ShapeDtypeStruct((m, n), out_dtype)], scratch=[pltpu.VMEM((tm, tn), F32)],
                args=args, sem=("parallel", "parallel", "arbitrary"), exchange=exchange)
    return res[0] if exchange is None else (res[0][0], res[1])


def _matmul_tn(a, b, *, name, ka, tka, tn, ts, n=None, a_col_block=0, b_col_block=0, exchange=None):
    s = a.shape[0]
    n = b.shape[1] if n is None else n
    ns = s // ts

    def body(ins, outs, scratch):
        a_ref, b_ref = ins
        o_ref = outs[0]

        @pl.when(pl.program_id(2) == 0)
        def _():
            o_ref[...] = jnp.zeros_like(o_ref)

        o_ref[...] += _dot_tn(a_ref[...], b_ref[...])

    res = _call(body, name=name, grid=(ka // tka, n // tn, ns),
                in_specs=[pl.BlockSpec((ts, tka), lambda i, j, t: (t, i + a_col_block)),
                          pl.BlockSpec((ts, tn), lambda i, j, t: (t, j + b_col_block))],
                out_specs=[pl.BlockSpec((tka, tn), lambda i, j, t: (i, j))],
                out_shape=[jax.ShapeDtypeStruct((ka, n), F32)], args=[a, b],
                sem=("parallel", "parallel", "arbitrary"), exchange=exchange)
    return res[0] if exchange is None else (res[0][0], res[1])


def _ln_in_fwd(x, g, b):
    s = x.shape[0]
    t = min(ROW_TILE, s)

    def body(x_ref, g_ref, b_ref, y_ref, ybf_ref):
        xhat, _ = _ln_stats(x_ref[...])
        y = xhat * g_ref[...] + b_ref[...]
        y_ref[...] = y
        ybf_ref[...] = y.astype(BF16)

    return pl.pallas_call(
        body, name="ln_in_fwd", grid=(s // t,),
        in_specs=[_row_spec(t, D_MODEL), _const_spec((1, D_MODEL)), _const_spec((1, D_MODEL))],
        out_specs=[_row_spec(t, D_MODEL), _row_spec(t, D_MODEL)],
        out_shape=[jax.ShapeDtypeStruct((s, D_MODEL), F32), jax.ShapeDtypeStruct((s, D_MODEL), BF16)],
        compiler_params=_params(("parallel",)),
    )(x, g, b)


def _prev_halo_spec(t, halo, cols, col_block):
    per = t // halo
    return pl.BlockSpec((halo, cols), lambda i, cb=col_block: (jnp.maximum(i * per - 1, 0), cb))


def _next_halo_spec(t, halo, cols, col_block, n_blocks):
    per = t // halo
    last = n_blocks * per - 1
    return pl.BlockSpec((halo, cols), lambda i, cb=col_block: (jnp.minimum((i + 1) * per, last), cb))


CONV_ROWS = 32


def _tap_groups(offsets):
    by_shift = {}
    for j, off in enumerate(offsets):
        by_shift.setdefault(off % 8, []).append((j, off // 8))
    groups = []
    for shift, taps in sorted(by_shift.items()):
        first = min(a for _, a in taps)
        last = max(a for _, a in taps)
        groups.append((shift, 8 * first, 8 * (last - first), [(j, 8 * (a - first)) for j, a in taps]))
    return groups


def _tap_windows(ext, groups, r0):
    return [ext[r0 + first + shift:r0 + first + shift + CONV_ROWS + extra, :] for shift, first, extra, _ in groups]


def _tap_sum(ext, w_ref, groups, r0):
    acc = jnp.zeros((CONV_ROWS, ext.shape[1]), F32)
    for (_, _, _, taps), win in zip(groups, _tap_windows(ext, groups, r0)):
        for j, a in taps:
            acc = acc + w_ref[j:j + 1, :] * win[a:a + CONV_ROWS]
    return acc


def _conv_a_fwd(p, w, b, ln_g, ln_b, tag):
    s = p.shape[0]
    t = min(ROW_TILE, s)
    width = CONV_WIDTH

    def body(a_ref, halo_ref, w_ref, b_ref, g_ref, bb_ref, h_ref, c_ref, ha_ref, ext):
        i = pl.program_id(0)
        a = a_ref[...].astype(F32)
        h = a[:, :CONV_DIM] * _sigmoid(a[:, CONV_DIM:])
        ah = halo_ref[...].astype(F32)
        hh = ah[:, :CONV_DIM] * _sigmoid(ah[:, CONV_DIM:])
        ext[0:CONV_HALO, :] = jnp.where(i == 0, 0.0, hh)
        ext[CONV_HALO:CONV_HALO + t, :] = h
        h_ref[...] = h
        groups = _tap_groups([CONV_HALO - (width - 1) + j for j in range(width)])
        for r0 in range(0, t, CONV_ROWS):
            c = _tap_sum(ext, w_ref, groups, r0) + b_ref[...]
            xhat, _ = _ln_stats(c)
            n = xhat * g_ref[...] + bb_ref[...]
            c_ref[r0:r0 + CONV_ROWS, :] = c
            ha_ref[r0:r0 + CONV_ROWS, :] = _silu(n).astype(BF16)

    return pl.pallas_call(
        body, name="conv_a_fwd" + tag, grid=(s // t,),
        in_specs=[_row_spec(t, 2 * CONV_DIM, COL_A // (2 * CONV_DIM)),
                  _prev_halo_spec(t, CONV_HALO, 2 * CONV_DIM, COL_A // (2 * CONV_DIM)),
                  _const_spec((width, CONV_DIM)), _const_spec((1, CONV_DIM)),
                  _const_spec((1, CONV_DIM)), _const_spec((1, CONV_DIM))],
        out_specs=[_row_spec(t, CONV_DIM)] * 3,
        out_shape=[jax.ShapeDtypeStruct((s, CONV_DIM), F32), jax.ShapeDtypeStruct((s, CONV_DIM), F32),
                   jax.ShapeDtypeStruct((s, CONV_DIM), BF16)],
        scratch_shapes=[pltpu.VMEM((CONV_HALO + t, CONV_DIM), F32)],
        compiler_params=_params(("parallel",)),
    )(p, p, w, b, ln_g, ln_b)


def _sgu_mix(vn, wt_ref, bst_ref, t):
    row = lax.broadcasted_iota(jnp.int32, (SGU_CHUNK, SGU_CHUNK), 0)
    col = lax.broadcasted_iota(jnp.int32, (SGU_CHUNK, SGU_CHUNK), 1)
    chunks = []
    for ci in range(t // SGU_CHUNK):
        groups = []
        for g in range(SGU_GROUPS):
            wg = jnp.where(row >= col, wt_ref[g], 0.0).astype(BF16)
            v_cg = vn[ci * SGU_CHUNK:(ci + 1) * SGU_CHUNK, g * SGU_GROUP_DIM:(g + 1) * SGU_GROUP_DIM]
            groups.append(_dot(wg, v_cg.astype(BF16)) + bst_ref[:, g:g + 1])
        chunks.append(jnp.concatenate(groups, axis=1))
    return jnp.concatenate(chunks, axis=0)


def _sgu_fwd(p, ln_g, ln_b, w_s, b_s_t, tag):
    s = p.shape[0]
    t = min(ROW_TILE, s)

    def body(uv_ref, g_ref, b_ref, ws_ref, bst_ref, hc_ref):
        uv = uv_ref[...].astype(F32)
        u = _gelu(uv[:, :SGU_DIM])
        vv = _gelu(uv[:, SGU_DIM:])
        xhat, _ = _ln_stats(vv)
        vn = xhat * g_ref[...] + b_ref[...]
        mixed = _sgu_mix(vn, ws_ref, bst_ref, t)
        hc_ref[...] = (u * mixed).astype(BF16)

    return pl.pallas_call(
        body, name="sgu_fwd" + tag, grid=(s // t,),
        in_specs=[_row_spec(t, 2 * SGU_DIM, COL_UV // (2 * SGU_DIM)),
                  _const_spec((1, SGU_DIM)), _const_spec((1, SGU_DIM)),
                  _const_spec((SGU_GROUPS, SGU_CHUNK, SGU_CHUNK)), _const_spec((SGU_CHUNK, LANES))],
        out_specs=_row_spec(t, SGU_DIM),
        out_shape=jax.ShapeDtypeStruct((s, SGU_DIM), BF16),
        compiler_params=_params(("parallel",)),
    )(p, ln_g, ln_b, w_s, b_s_t)


def _merge_fwd(p, ha, hb, hc, w_pa, w_pb, w_pc, b_gate, tag):
    s = p.shape[0]
    t = min(ROW_TILE, s)
    gb = COL_GATE // D_MODEL

    def body(ga_ref, gb_ref, gc_ref, ha_ref, hb_ref, hc_ref, wa_ref, wb_ref, wc_ref, bg_ref, y_ref, m_ref):
        merged = jnp.zeros((t, D_MODEL), F32)
        for idx, (g_ref, h_ref, w_ref) in enumerate(((ga_ref, ha_ref, wa_ref), (gb_ref, hb_ref, wb_ref), (gc_ref, hc_ref, wc_ref))):
            y = _dot(h_ref[...], w_ref[...])
            sg = _sigmoid(g_ref[...].astype(F32) + bg_ref[:, idx * D_MODEL:(idx + 1) * D_MODEL])
            y_ref[:, idx * D_MODEL:(idx + 1) * D_MODEL] = y.astype(BF16)
            merged = merged + sg * y
        m_ref[...] = merged.astype(BF16)

    hspec = _row_spec(t, CONV_DIM)
    wspec = _const_spec((CONV_DIM, D_MODEL))
    return pl.pallas_call(
        body, name="merge_fwd" + tag, grid=(s // t,),
        in_specs=[_row_spec(t, D_MODEL, gb), _row_spec(t, D_MODEL, gb + 1), _row_spec(t, D_MODEL, gb + 2),
                  hspec, hspec, hspec, wspec, wspec, wspec, _const_spec((1, 3 * D_MODEL))],
        out_specs=[_row_spec(t, 3 * D_MODEL), _row_spec(t, D_MODEL)],
        out_shape=[jax.ShapeDtypeStruct((s, 3 * D_MODEL), BF16), jax.ShapeDtypeStruct((s, D_MODEL), BF16)],
        compiler_params=_params(("parallel",)),
    )(p, p, p, ha, hb, hc, w_pa, w_pb, w_pc, b_gate)


def _matmul_res_ln(a, w, bias, x_res, ln_g, ln_b, *, name, exchange=None):
    s, k = a.shape
    t = min(ROW_TILE, s)

    def body(ins, outs, scratch):
        a_ref, w_ref, bias_ref, x_ref, g_ref, b_ref = ins
        r_ref, y_ref, ybf_ref = outs
        r = DN_ALPHA * x_ref[...] + _dot(a_ref[...], w_ref[...]) + bias_ref[...]
        xhat, _ = _ln_stats(r)
        y = xhat * g_ref[...] + b_ref[...]
        r_ref[...] = r
        y_ref[...] = y
        ybf_ref[...] = y.astype(BF16)

    vec = _const_spec((1, D_MODEL))
    res = _call(
        body, name=name, grid=(s // t,),
        in_specs=[_row_spec(t, k), _const_spec((k, D_MODEL)), vec, _row_spec(t, D_MODEL), vec, vec],
        out_specs=[_row_spec(t, D_MODEL)] * 3,
        out_shape=[jax.ShapeDtypeStruct((s, D_MODEL), F32), jax.ShapeDtypeStruct((s, D_MODEL), F32),
                   jax.ShapeDtypeStruct((s, D_MODEL), BF16)],
        args=[a, w, bias, x_res, ln_g, ln_b], sem=("parallel",), exchange=exchange)
    return (*res, None) if exchange is None else (*res[0], res[1])


def _ff1_fwd(x_bf, w, b, tag, exchange=None):
    s = x_bf.shape[0]
    tm = min(1024, s)
    tn = 1024

    def body(ins, outs, scratch):
        x_ref, w_ref, b_ref = ins
        hp_ref, h_ref = outs
        hp = _dot(x_ref[...], w_ref[...]) + b_ref[...]
        hp_ref[...] = hp.astype(BF16)
        r = jnp.maximum(hp, 0.0)
        h_ref[...] = (r * r).astype(BF16)

    res = _call(
        body, name="ff1_fwd" + tag, grid=(s // tm, D_FF // tn),
        in_specs=[pl.BlockSpec((tm, D_MODEL), lambda i, j: (i, 0)), pl.BlockSpec((D_MODEL, tn), lambda i, j: (0, j)),
                  pl.BlockSpec((1, tn), lambda i, j: (0, j))],
        out_specs=[pl.BlockSpec((tm, tn), lambda i, j: (i, j))] * 2,
        out_shape=[jax.ShapeDtypeStruct((s, D_FF), BF16), jax.ShapeDtypeStruct((s, D_FF), BF16)],
        args=[x_bf, w, b], sem=("parallel", "parallel"), exchange=exchange)
    return (res[0], res[1], None) if exchange is None else (res[0][0], res[0][1], res[1])


def _loss_fwd_bwd(y, target):
    s = y.shape[0]
    t = min(ROW_TILE, s)

    def body(y_ref, t_ref, dy_ref, loss_ref):
        @pl.when(pl.program_id(0) == 0)
        def _():
            loss_ref[...] = jnp.zeros_like(loss_ref)

        err = y_ref[...] - t_ref[...]
        dy_ref[...] = err * (1.0 / D_MODEL)
        per_row = jnp.mean(err * err, axis=-1, keepdims=True)
        loss_ref[...] += 0.5 * jnp.sum(per_row, axis=0, keepdims=True)

    return pl.pallas_call(
        body, name="loss_fwd_bwd", grid=(s // t,),
        in_specs=[_row_spec(t, D_MODEL), _row_spec(t, D_MODEL)],
        out_specs=[_row_spec(t, D_MODEL), _const_spec((8, LANES))],
        out_shape=[jax.ShapeDtypeStruct((s, D_MODEL), F32), jax.ShapeDtypeStruct((8, LANES), F32)],
        compiler_params=_params(("arbitrary",)),
    )(y, target)


def _softplus(x):
    return jnp.maximum(x, 0.0) + jnp.log1p(jnp.exp(-jnp.abs(x)))


def _gdn_conv_silu_norm(q_ref, k_ref, v_ref, hq_ref, hk_ref, hv_ref, cw_ref, ext, first):
    t = q_ref.shape[0]
    for n, (r, h) in enumerate(((q_ref, hq_ref), (k_ref, hk_ref), (v_ref, hv_ref))):
        ext[0:GDN_HALO, n * GDN_QK:(n + 1) * GDN_QK] = jnp.where(first, 0.0, h[...].astype(F32))
        ext[GDN_HALO:GDN_HALO + t, n * GDN_QK:(n + 1) * GDN_QK] = r[...].astype(F32)
    pre = jnp.zeros((t, 3 * GDN_QK), F32)
    for j in range(GDN_CONV):
        pre = pre + cw_ref[j:j + 1, :] * ext[pl.ds(GDN_HALO - (GDN_CONV - 1) + j, t), :]
    act = _silu(pre)
    rq, rk = [], []
    for h in range(GDN_HEADS):
        qh = act[:, h * GDN_DK:(h + 1) * GDN_DK]
        kh = act[:, GDN_QK + h * GDN_DK:GDN_QK + (h + 1) * GDN_DK]
        rq.append(lax.rsqrt(jnp.sum(qh * qh, axis=-1, keepdims=True) + RMS_EPS))
        rk.append(lax.rsqrt(jnp.sum(kh * kh, axis=-1, keepdims=True) + RMS_EPS))
    return pre, act, rq, rk


def _gdn_gates(ba, alog_ref, dtb_ref):
    lane = lax.broadcasted_iota(jnp.int32, ba.shape, 1)
    beta = _sigmoid(ba)
    g = -jnp.exp(alog_ref[...]) * _softplus(ba + dtb_ref[...])
    g = jnp.where((lane >= GDN_HEADS) & (lane < 2 * GDN_HEADS), g, 0.0)
    return beta, g


def _chunk_cumsum_matrix(t, upper):
    row = lax.broadcasted_iota(jnp.int32, (t, t), 0)
    col = lax.broadcasted_iota(jnp.int32, (t, t), 1)
    same = (row // GDN_CHUNK) == (col // GDN_CHUNK)
    tri = (col >= row) if upper else (col <= row)
    return jnp.where(same & tri, 1.0, 0.0).astype(F32)


def _each(fn, *lists):
    return [fn(*a) for a in zip(*lists)]


def _gdn_pair_items(qn_s, kn_s, vc_s, beta_s, gam_s, nc):
    items = []
    for ci in range(nc):
        rows = slice(ci * GDN_CHUNK, (ci + 1) * GDN_CHUNK)
        gam_blk = gam_s[rows, :]
        gam_t = gam_blk.T
        beta_blk = beta_s[rows, :]
        for h in range(GDN_HEADS):
            sl = slice(h * GDN_DK, (h + 1) * GDN_DK)
            items.append((qn_s[rows, sl], kn_s[rows, sl], vc_s[rows, sl], beta_blk[:, h:h + 1],
                          gam_blk[:, GDN_HEADS + h:GDN_HEADS + h + 1], gam_t[GDN_HEADS + h:GDN_HEADS + h + 1, :]))
    return items


def _gdn_prep(items):
    c = GDN_CHUNK
    row = lax.broadcasted_iota(jnp.int32, (c, c), 0)
    col = lax.broadcasted_iota(jnp.int32, (c, c), 1)
    causal = row >= col
    strict = row > col
    kbs = [k.astype(BF16) for _, k, _, _, _, _ in items]
    kks = _each(_dot_nt, kbs, kbs)
    qks = _each(_dot_nt, [q.astype(BF16) for q, _, _, _, _, _ in items], kbs)
    out = []
    for (q, k, v, beta_c, gam_c, gam_r), kk, qk in zip(items, kks, qks):
        decay = jnp.where(causal, jnp.exp(jnp.where(causal, gam_c - gam_r, 0.0)), 0.0)
        gm = jnp.exp(gam_c)
        glast = gam_c[c - 1:c, :]
        elast = jnp.exp(glast - gam_c)
        out.append(dict(causal=causal, strict=strict, decay=decay, kk=kk, low=jnp.where(strict, beta_c * kk * decay, 0.0),
                        a_qk=qk * decay, gm=gm, glast_exp=jnp.exp(glast), elast=elast, q=q, k=k, v=v, beta_c=beta_c,
                        r=jnp.concatenate([beta_c * v, beta_c * k * gm], axis=1), qd=q * gm, kd=k * elast))
    return out


def _unit_lower_inverses_minus_identity(lows):
    ps = [-low for low in lows]
    mis = list(ps)
    for _ in range(5):
        ps = _each(_bdot, ps, ps)
        ts = _each(_bdot, mis, ps)
        mis = [mi + p + t for mi, p, t in zip(mis, ps, ts)]
    return mis


def _apply_inverses(mis, rs, dot=_dot):
    return [r + t for r, t in zip(rs, _each(functools.partial(_bdot, dot=dot), mis, rs))]


def _gdn_fwd(p, p_ba, conv_w, a_log, dt_bias, norm_g, tag):
    s = p.shape[0]
    t = min(GDN_BLOCK, s)
    nc = t // GDN_CHUNK
    nblk = s // t
    qb, kb_, vb, zb = COL_Q // GDN_QK, COL_Q // GDN_QK + 1, COL_Q // GDN_QK + 2, COL_Z // GDN_QK
    scale = GDN_DK ** -0.5

    def body(q_ref, k_ref, v_ref, hq_ref, hk_ref, hv_ref, z_ref, ba_ref, cw_ref, alog_ref, dtb_ref, ng_ref,
             o_ref, hb_ref, st_ref, m_ref, ext, qn_s, kn_s, vc_s, beta_s, gam_s, state):
        i = pl.program_id(0)

        @pl.when(i == 0)
        def _():
            state[...] = jnp.zeros_like(state)

        _, act, rq, rk = _gdn_conv_silu_norm(q_ref, k_ref, v_ref, hq_ref, hk_ref, hv_ref, cw_ref, ext, i == 0)
        for h in range(GDN_HEADS):
            sl = slice(h * GDN_DK, (h + 1) * GDN_DK)
            qn_s[:, sl] = act[:, sl] * (rq[h] * scale)
            kn_s[:, sl] = act[:, GDN_QK + h * GDN_DK:GDN_QK + (h + 1) * GDN_DK] * rk[h]
        vc_s[...] = act[:, 2 * GDN_QK:]
        beta, g = _gdn_gates(ba_ref[...], alog_ref, dtb_ref)
        beta_s[...] = beta
        gam_s[...] = jnp.dot(_chunk_cumsum_matrix(t, False), g, preferred_element_type=F32, precision=lax.Precision.HIGHEST)

        prs = _gdn_prep(_gdn_pair_items(qn_s, kn_s, vc_s, beta_s, gam_s, nc))
        mis = _unit_lower_inverses_minus_identity([pr["low"] for pr in prs])
        xs = _apply_inverses(mis, [pr["r"] for pr in prs])
        heads = range(GDN_HEADS)
        for ci in range(nc):
            rows = slice(ci * GDN_CHUNK, (ci + 1) * GDN_CHUNK)
            pc, xc = prs[ci * GDN_HEADS:(ci + 1) * GDN_HEADS], xs[ci * GDN_HEADS:(ci + 1) * GDN_HEADS]
            m_ref[rows, :] = jnp.concatenate(mis[ci * GDN_HEADS:(ci + 1) * GDN_HEADS], axis=1)
            sts = [state[h * GDN_DK:(h + 1) * GDN_DK, :] for h in heads]
            for h in heads:
                st_ref[(ci * GDN_HEADS + h) * GDN_DK:(ci * GDN_HEADS + h + 1) * GDN_DK, :] = sts[h]
            w_st = _each(_bdot, [x[:, GDN_DK:] for x in xc], sts)
            q_st = _each(_bdot, [pr["qd"] for pr in pc], sts)
            vns = [x[:, :GDN_DK] - ws for x, ws in zip(xc, w_st)]
            a_vn = _each(_bdot, [pr["a_qk"] for pr in pc], vns)
            k_vn = _each(functools.partial(_bdot, dot=_dot_tn), [pr["kd"] for pr in pc], vns)
            for h in heads:
                o_ref[rows, h * GDN_DK:(h + 1) * GDN_DK] = q_st[h] + a_vn[h]
                state[h * GDN_DK:(h + 1) * GDN_DK, :] = sts[h] * pc[h]["glast_exp"] + k_vn[h]

        z = z_ref[...].astype(F32)
        for h in range(GDN_HEADS):
            sl = slice(h * GDN_DK, (h + 1) * GDN_DK)
            o = o_ref[:, sl]
            on = o * lax.rsqrt(jnp.mean(o * o, axis=-1, keepdims=True) + RMS_EPS)
            hb_ref[:, sl] = (on * ng_ref[...] * _silu(z[:, sl])).astype(BF16)

    col = lambda cb: pl.BlockSpec((t, GDN_QK), lambda i, cb=cb: (i, cb))
    halo = lambda cb: _prev_halo_spec(t, GDN_HALO, GDN_QK, cb)
    vec = _const_spec((1, LANES))
    return pl.pallas_call(
        body, name="gdn_fwd" + tag, grid=(nblk,),
        in_specs=[col(qb), col(kb_), col(vb), halo(qb), halo(kb_), halo(vb), col(zb),
                  _row_spec(t, LANES), _const_spec((GDN_CONV, 3 * GDN_QK)), vec, vec, vec],
        out_specs=[_row_spec(t, GDN_QK), _row_spec(t, GDN_QK),
                   pl.BlockSpec((nc * GDN_HEADS * GDN_DK, GDN_DK), lambda i: (i, 0)),
                   _row_spec(t, GDN_HEADS * GDN_CHUNK)],
        out_shape=[jax.ShapeDtypeStruct((s, GDN_QK), F32), jax.ShapeDtypeStruct((s, GDN_QK), BF16),
                   jax.ShapeDtypeStruct((s // GDN_CHUNK * GDN_HEADS * GDN_DK, GDN_DK), F32),
                   jax.ShapeDtypeStruct((s, GDN_HEADS * GDN_CHUNK), F32)],
        scratch_shapes=[pltpu.VMEM((GDN_HALO + t, 3 * GDN_QK), F32), pltpu.VMEM((t, GDN_QK), F32),
                        pltpu.VMEM((t, GDN_QK), F32), pltpu.VMEM((t, GDN_QK), F32),
                        pltpu.VMEM((t, LANES), F32), pltpu.VMEM((t, LANES), F32),
                        pltpu.VMEM((GDN_HEADS * GDN_DK, GDN_DK), F32)],
        compiler_params=_params(("arbitrary",)),
    )(p, p, p, p, p, p, p, p_ba, conv_w, a_log, dt_bias, norm_g)


def _lane_place(col, lane_idx, shape):
    lane = lax.broadcasted_iota(jnp.int32, shape, 1)
    return jnp.where(lane == lane_idx, col, 0.0)


def _gdn_bwd(dhb, p, p_ba, o, states, minv, conv_w, a_log, dt_bias, norm_g, tag):
    s = p.shape[0]
    t = min(GDN_BLOCK, s)
    nc = t // GDN_CHUNK
    nblk = s // t
    qb, kb_, vb, zb = COL_Q // GDN_QK, COL_Q // GDN_QK + 1, COL_Q // GDN_QK + 2, COL_Z // GDN_QK
    scale = GDN_DK ** -0.5
    c = GDN_CHUNK

    def body(dhb_ref, q_ref, k_ref, v_ref, hq_ref, hk_ref, hv_ref, z_ref, ba_ref, o_ref, st_ref, m_ref,
             cw_ref, alog_ref, dtb_ref, ng_ref,
             dpre_ref, dz_ref, dba_ref, dng_ref, dalog_ref, ddtb_ref,
             ext, qn_s, kn_s, vc_s, beta_s, gam_s, do_s, dqn_s, dkn_s, dvc_s, dgam_s, dbeta_s, dstate):
        i = pl.program_id(0)

        @pl.when(i == 0)
        def _():
            dstate[...] = jnp.zeros_like(dstate)
            dng_ref[...] = jnp.zeros_like(dng_ref)
            dalog_ref[...] = jnp.zeros_like(dalog_ref)
            ddtb_ref[...] = jnp.zeros_like(ddtb_ref)

        pre, act, rq, rk = _gdn_conv_silu_norm(q_ref, k_ref, v_ref, hq_ref, hk_ref, hv_ref, cw_ref, ext, i == nblk - 1)
        for h in range(GDN_HEADS):
            sl = slice(h * GDN_DK, (h + 1) * GDN_DK)
            qn_s[:, sl] = act[:, sl] * (rq[h] * scale)
            kn_s[:, sl] = act[:, GDN_QK + h * GDN_DK:GDN_QK + (h + 1) * GDN_DK] * rk[h]
        vc_s[...] = act[:, 2 * GDN_QK:]
        ba = ba_ref[...]
        beta, g = _gdn_gates(ba, alog_ref, dtb_ref)
        beta_s[...] = beta
        gam_s[...] = jnp.dot(_chunk_cumsum_matrix(t, False), g, preferred_element_type=F32, precision=lax.Precision.HIGHEST)

        z = z_ref[...].astype(F32)
        dhb = dhb_ref[...]
        dng = jnp.zeros((1, GDN_DK), F32)
        for h in range(GDN_HEADS):
            sl = slice(h * GDN_DK, (h + 1) * GDN_DK)
            oh = o_ref[:, sl]
            r = lax.rsqrt(jnp.mean(oh * oh, axis=-1, keepdims=True) + RMS_EPS)
            on = oh * r
            sz = _silu(z[:, sl])
            dyh = dhb[:, sl]
            dng = dng + _colsum(dyh * on * sz)
            dz_ref[:, sl] = (dyh * on * ng_ref[...] * _dsilu(z[:, sl])).astype(BF16)
            don = dyh * ng_ref[...] * sz
            do_s[:, sl] = r * (don - on * jnp.mean(don * on, axis=-1, keepdims=True))
        dng_ref[...] += dng

        heads = range(GDN_HEADS)
        npairs = nc * GDN_HEADS
        tn = functools.partial(_bdot, dot=_dot_tn)
        nt = functools.partial(_bdot, dot=_dot_nt)
        rsum = lambda a: jnp.sum(a, axis=-1, keepdims=True)
        left = lambda a: a[:, :GDN_DK]
        right = lambda a: a[:, GDN_DK:]

        prs = _gdn_prep(_gdn_pair_items(qn_s, kn_s, vc_s, beta_s, gam_s, nc))
        mis = [m_ref[(n // GDN_HEADS) * c:(n // GDN_HEADS + 1) * c, (n % GDN_HEADS) * c:(n % GDN_HEADS + 1) * c] for n in range(npairs)]
        xs = _apply_inverses(mis, [pr["r"] for pr in prs])
        sts = [st_ref[n * GDN_DK:(n + 1) * GDN_DK, :] for n in range(npairs)]
        dos = [do_s[(n // GDN_HEADS) * c:(n // GDN_HEADS + 1) * c, (n % GDN_HEADS) * GDN_DK:(n % GDN_HEADS + 1) * GDN_DK] for n in range(npairs)]
        w_st = _each(_bdot, [right(x) for x in xs], sts)
        vns = [left(x) - ws for x, ws in zip(xs, w_st)]
        at_do = _each(tn, [pr["a_qk"] for pr in prs], dos)
        dqds = _each(nt, dos, sts)
        d_as = [jnp.where(pr["causal"], a, 0.0) for pr, a in zip(prs, _each(nt, dos, vns))]
        qt_do = _each(tn, [pr["qd"] for pr in prs], dos)

        dvns, dkds, ds_st = [None] * npairs, [None] * npairs, [None] * npairs
        for ci in reversed(range(nc)):
            ids = [ci * GDN_HEADS + h for h in heads]
            dss = [dstate[h * GDN_DK:(h + 1) * GDN_DK, :] for h in heads]
            kd_ds = _each(_bdot, [prs[n]["kd"] for n in ids], dss)
            vn_ds = _each(nt, [vns[n] for n in ids], dss)
            for h, n in enumerate(ids):
                dvns[n] = kd_ds[h] + at_do[n]
                dkds[n] = vn_ds[h]
                ds_st[n] = jnp.sum(rsum(dss[h] * sts[n]), axis=0, keepdims=True)
            wt_dvn = _each(tn, [right(xs[n]) for n in ids], [dvns[n] for n in ids])
            for h, n in enumerate(ids):
                dstate[h * GDN_DK:(h + 1) * GDN_DK, :] = dss[h] * prs[n]["glast_exp"] + qt_do[n] - wt_dvn[h]

        dws = [-a for a in _each(nt, dvns, sts)]
        d_rs = _apply_inverses(mis, [jnp.concatenate([dvn, dw], axis=1) for dvn, dw in zip(dvns, dws)], _dot_tn)
        d_ls = [jnp.where(pr["strict"], -a, 0.0) for pr, a in zip(prs, _each(nt, d_rs, xs))]
        d_l_kds = [d_l * pr["kk"] * pr["decay"] for d_l, pr in zip(d_ls, prs)]
        dkks = [d_l * pr["beta_c"] * pr["decay"] for d_l, pr in zip(d_ls, prs)]
        dqks = [d_a * pr["decay"] for d_a, pr in zip(d_as, prs)]
        ks, qs = [pr["k"] for pr in prs], [pr["q"] for pr in prs]
        dk1, dk2, dk3 = _each(_bdot, dkks, ks), _each(tn, dkks, ks), _each(tn, dqks, qs)
        dq1 = _each(_bdot, dqks, ks)
        rowi = lax.broadcasted_iota(jnp.int32, (c, 1), 0)
        for ci in range(nc):
            rows = slice(ci * c, (ci + 1) * c)
            dgam_blk = jnp.zeros((c, LANES), F32)
            dbeta_blk = jnp.zeros((c, LANES), F32)
            for h in heads:
                n = ci * GDN_HEADS + h
                sl = slice(h * GDN_DK, (h + 1) * GDN_DK)
                pr, d_r = prs[n], d_rs[n]
                d_ru, d_rw = left(d_r), right(d_r)
                gmat = pr["beta_c"] * d_l_kds[n] + d_as[n] * pr["a_qk"]
                dkd_kd = rsum(dkds[n] * pr["kd"])
                dgam = rsum(gmat) - rsum(gmat.T) + rsum(d_rw * right(pr["r"])) + rsum(dqds[n] * pr["qd"]) - dkd_kd
                dglast = jnp.sum(dkd_kd, axis=0, keepdims=True) + ds_st[n] * pr["glast_exp"]
                dgam = dgam + jnp.where(rowi == c - 1, dglast, 0.0)
                dbeta = rsum(d_l_kds[n]) + rsum(d_ru * pr["v"]) + rsum(d_rw * pr["k"]) * pr["gm"]
                dvc_s[rows, sl] = pr["beta_c"] * d_ru
                dkn_s[rows, sl] = dk1[n] + dk2[n] + dk3[n] + d_rw * (pr["beta_c"] * pr["gm"]) + dkds[n] * pr["elast"]
                dqn_s[rows, sl] = dq1[n] + dqds[n] * pr["gm"]
                dgam_blk = dgam_blk + _lane_place(dgam, GDN_HEADS + h, (c, LANES))
                dbeta_blk = dbeta_blk + _lane_place(dbeta, h, (c, LANES))
            dgam_s[rows, :] = dgam_blk
            dbeta_s[rows, :] = dbeta_blk

        dg = jnp.dot(_chunk_cumsum_matrix(t, True), dgam_s[...], preferred_element_type=F32, precision=lax.Precision.HIGHEST)
        lane = lax.broadcasted_iota(jnp.int32, (t, LANES), 1)
        g_lanes = (lane >= GDN_HEADS) & (lane < 2 * GDN_HEADS)
        da_logit = jnp.where(g_lanes, dg * (-jnp.exp(alog_ref[...])) * _sigmoid(ba + dtb_ref[...]), 0.0)
        db_logit = jnp.where(lane < GDN_HEADS, dbeta_s[...] * beta * (1.0 - beta), 0.0)
        dba_ref[...] = (da_logit + db_logit).astype(BF16)
        dalog_ref[...] += _colsum(dg * g)
        ddtb_ref[...] += _colsum(da_logit)

        dact = []
        for n, (dn_s, rr, sc) in enumerate(((dqn_s, rq, scale), (dkn_s, rk, 1.0))):
            for h in range(GDN_HEADS):
                sl = slice(h * GDN_DK, (h + 1) * GDN_DK)
                y = act[:, n * GDN_QK + h * GDN_DK:n * GDN_QK + (h + 1) * GDN_DK] * rr[h]
                dy = dn_s[:, sl] * sc
                dact.append(rr[h] * (dy - y * jnp.sum(dy * y, axis=-1, keepdims=True)))
        dact.append(dvc_s[...])
        dpre_ref[...] = jnp.concatenate(dact, axis=1) * _dsilu(pre)

    rb = lambda i: nblk - 1 - i
    per = t // GDN_HALO
    col = lambda cb, wd=GDN_QK: pl.BlockSpec((t, wd), lambda i, cb=cb: (rb(i), cb))
    halo = lambda cb: pl.BlockSpec((GDN_HALO, GDN_QK), lambda i, cb=cb: (jnp.maximum(rb(i) * per - 1, 0), cb))
    vec = _const_spec((1, LANES))
    return pl.pallas_call(
        body, name="gdn_bwd" + tag, grid=(nblk,),
        in_specs=[col(0), col(qb), col(kb_), col(vb), halo(qb), halo(kb_), halo(vb), col(zb),
                  pl.BlockSpec((t, LANES), lambda i: (rb(i), 0)), col(0),
                  pl.BlockSpec((nc * GDN_HEADS * GDN_DK, GDN_DK), lambda i: (rb(i), 0)),
                  pl.BlockSpec((t, GDN_HEADS * c), lambda i: (rb(i), 0)),
                  _const_spec((GDN_CONV, 3 * GDN_QK)), vec, vec, vec],
        out_specs=[pl.BlockSpec((t, 3 * GDN_QK), lambda i: (rb(i), 0)), col(0), pl.BlockSpec((t, LANES), lambda i: (rb(i), 0)),
                   vec, vec, vec],
        out_shape=[jax.ShapeDtypeStruct((s, 3 * GDN_QK), F32), jax.ShapeDtypeStruct((s, GDN_QK), BF16),
                   jax.ShapeDtypeStruct((s, LANES), BF16),
                   jax.ShapeDtypeStruct((1, LANES), F32), jax.ShapeDtypeStruct((1, LANES), F32), jax.ShapeDtypeStruct((1, LANES), F32)],
        scratch_shapes=[pltpu.VMEM((GDN_HALO + t, 3 * GDN_QK), F32)] + [pltpu.VMEM((t, GDN_QK), F32)] * 3
                       + [pltpu.VMEM((t, LANES), F32)] * 2 + [pltpu.VMEM((t, GDN_QK), F32)] * 4
                       + [pltpu.VMEM((t, LANES), F32)] * 2 + [pltpu.VMEM((GDN_HEADS * GDN_DK, GDN_DK), F32)],
        compiler_params=_params(("arbitrary",)),
    )(dhb, p, p, p, p, p, p, p, p_ba, o, states, minv, conv_w, a_log, dt_bias, norm_g)


def _zero_at_first_step(*refs):
    @pl.when(pl.program_id(0) == 0)
    def _():
        for r in refs:
            r[...] = jnp.zeros_like(r)


def _ln_bwd_call(dy, x_in, g, *, name):
    s, d = dy.shape
    t = min(ROW_TILE, s)

    def body(dy_ref, x_ref, g_ref, dx_ref, dxbf_ref, dg_ref, db_ref, ds_ref):
        _zero_at_first_step(dg_ref, db_ref, ds_ref)
        dy = dy_ref[...]
        xhat, rstd = _ln_stats(x_ref[...])
        dx = _ln_bwd(dy, xhat, rstd, g_ref[...])
        dx_ref[...] = dx
        dxbf_ref[...] = dx.astype(BF16)
        dg_ref[...] += _colsum(dy * xhat)
        db_ref[...] += _colsum(dy)
        ds_ref[...] += _colsum(dx)

    vec = _const_spec((1, d))
    return pl.pallas_call(
        body, name=name, grid=(s // t,),
        in_specs=[_row_spec(t, d), _row_spec(t, d), vec],
        out_specs=[_row_spec(t, d), _row_spec(t, d), vec, vec, vec],
        out_shape=[jax.ShapeDtypeStruct((s, d), F32), jax.ShapeDtypeStruct((s, d), BF16)] + [jax.ShapeDtypeStruct((1, d), F32)] * 3,
        compiler_params=_params(("arbitrary",)),
    )(dy, x_in, g)


def _ff2_bwd(dr2_bf, w_ff2, hpre, tag):
    s = dr2_bf.shape[0]
    tm = min(1024, s)
    tn = 1024

    def body(d_ref, w_ref, hp_ref, o_ref, db_ref):
        @pl.when(pl.program_id(1) == 0)
        def _():
            db_ref[...] = jnp.zeros_like(db_ref)

        dh = _dot_nt(d_ref[...], w_ref[...]) * (2.0 * jnp.maximum(hp_ref[...].astype(F32), 0.0))
        o_ref[...] = dh.astype(BF16)
        db_ref[...] += _colsum(dh)

    return pl.pallas_call(
        body, name="ff2_bwd" + tag, grid=(D_FF // tn, s // tm),
        in_specs=[pl.BlockSpec((tm, D_MODEL), lambda j, i: (i, 0)), pl.BlockSpec((tn, D_MODEL), lambda j, i: (j, 0)),
                  pl.BlockSpec((tm, tn), lambda j, i: (i, j))],
        out_specs=[pl.BlockSpec((tm, tn), lambda j, i: (i, j)), pl.BlockSpec((1, tn), lambda j, i: (0, j))],
        out_shape=[jax.ShapeDtypeStruct((s, D_FF), BF16), jax.ShapeDtypeStruct((1, D_FF), F32)],
        compiler_params=_params(("parallel", "arbitrary")),
    )(dr2_bf, w_ff2, hpre)


def _ff1_bwd_ln(dhpre_bf, w_ff1, dr2, r1, ln1_g, tag):
    s = dr2.shape[0]
    t = min(ROW_TILE, s)

    def body(dh_ref, w_ref, dr2_ref, r1_ref, g_ref, dr_ref, drbf_ref, dg_ref, db_ref):
        _zero_at_first_step(dg_ref, db_ref)
        dx1 = DN_ALPHA * dr2_ref[...] + _dot_nt(dh_ref[...], w_ref[...])
        xhat, rstd = _ln_stats(r1_ref[...])
        dr = _ln_bwd(dx1, xhat, rstd, g_ref[...])
        dr_ref[...] = dr
        drbf_ref[...] = dr.astype(BF16)
        dg_ref[...] += _colsum(dx1 * xhat)
        db_ref[...] += _colsum(dx1)

    vec = _const_spec((1, D_MODEL))
    return pl.pallas_call(
        body, name="ff1_bwd_ln" + tag, grid=(s // t,),
        in_specs=[_row_spec(t, D_FF), _const_spec((D_MODEL, D_FF)), _row_spec(t, D_MODEL), _row_spec(t, D_MODEL), vec],
        out_specs=[_row_spec(t, D_MODEL), _row_spec(t, D_MODEL), vec, vec],
        out_shape=[jax.ShapeDtypeStruct((s, D_MODEL), F32), jax.ShapeDtypeStruct((s, D_MODEL), BF16),
                   jax.ShapeDtypeStruct((1, D_MODEL), F32), jax.ShapeDtypeStruct((1, D_MODEL), F32)],
        compiler_params=_params(("arbitrary",)),
    )(dhpre_bf, w_ff1, dr2, r1, ln1_g)


def _merge_bwd(dr1_bf, w_o, y3, p, b_gate, w_pa, w_pb, w_pc, tag):
    s = p.shape[0]
    t = min(ROW_TILE, s)
    gb = COL_GATE // D_MODEL

    def body(dr_ref, wo_ref, y_ref, ga_ref, gb_ref, gc_ref, bg_ref, wa_ref, wb_ref, wc_ref,
             dgate_ref, dy_ref, dha_ref, dhb_ref, dhc_ref, dbg_ref):
        _zero_at_first_step(dbg_ref)
        dm = _dot_nt(dr_ref[...], wo_ref[...])
        for idx, (g_ref, w_ref, dh_ref) in enumerate(((ga_ref, wa_ref, dha_ref), (gb_ref, wb_ref, dhb_ref), (gc_ref, wc_ref, dhc_ref))):
            sl = slice(idx * D_MODEL, (idx + 1) * D_MODEL)
            sg = _sigmoid(g_ref[...].astype(F32) + bg_ref[:, sl])
            dgate = dm * y_ref[:, sl].astype(F32) * sg * (1.0 - sg)
            dgate_ref[:, sl] = dgate.astype(BF16)
            dbg_ref[:, sl] += _colsum(dgate)
            dy = (dm * sg).astype(BF16)
            dy_ref[:, sl] = dy
            dh_ref[...] = _dot_nt(dy, w_ref[...])

    hspec = _row_spec(t, CONV_DIM)
    wspec = _const_spec((CONV_DIM, D_MODEL))
    return pl.pallas_call(
        body, name="merge_bwd" + tag, grid=(s // t,),
        in_specs=[_row_spec(t, D_MODEL), _const_spec((D_MODEL, D_MODEL)), _row_spec(t, 3 * D_MODEL),
                  _row_spec(t, D_MODEL, gb), _row_spec(t, D_MODEL, gb + 1), _row_spec(t, D_MODEL, gb + 2),
                  _const_spec((1, 3 * D_MODEL)), wspec, wspec, wspec],
        out_specs=[_row_spec(t, 3 * D_MODEL), _row_spec(t, 3 * D_MODEL), hspec, hspec, hspec, _const_spec((1, 3 * D_MODEL))],
        out_shape=[jax.ShapeDtypeStruct((s, 3 * D_MODEL), BF16), jax.ShapeDtypeStruct((s, 3 * D_MODEL), BF16)]
                  + [jax.ShapeDtypeStruct((s, CONV_DIM), F32)] * 3 + [jax.ShapeDtypeStruct((1, 3 * D_MODEL), F32)],
        compiler_params=_params(("arbitrary",)),
    )(dr1_bf, w_o, y3, p, p, p, b_gate, w_pa, w_pb, w_pc)


def _conv_a_bwd_pre(dha, c, ln_g, ln_b, tag):
    s = c.shape[0]
    t = min(ROW_TILE, s)

    def body(dh_ref, c_ref, g_ref, b_ref, dc_ref, dg_ref, db_ref, ds_ref):
        _zero_at_first_step(dg_ref, db_ref, ds_ref)
        xhat, rstd = _ln_stats(c_ref[...])
        n = xhat * g_ref[...] + b_ref[...]
        dn = dh_ref[...] * _dsilu(n)
        dc = _ln_bwd(dn, xhat, rstd, g_ref[...])
        dc_ref[...] = dc
        dg_ref[...] += _colsum(dn * xhat)
        db_ref[...] += _colsum(dn)
        ds_ref[...] += _colsum(dc)

    vec = _const_spec((1, CONV_DIM))
    return pl.pallas_call(
        body, name="conv_a_bwd_pre" + tag, grid=(s // t,),
        in_specs=[_row_spec(t, CONV_DIM), _row_spec(t, CONV_DIM), vec, vec],
        out_specs=[_row_spec(t, CONV_DIM), vec, vec, vec],
        out_shape=[jax.ShapeDtypeStruct((s, CONV_DIM), F32)] + [jax.ShapeDtypeStruct((1, CONV_DIM), F32)] * 3,
        compiler_params=_params(("arbitrary",)),
    )(dha, c, ln_g, ln_b)


def _dwconv_bwd(dy, x, w, *, width, halo, x_col_block, glu_p, name):
    s, ctot = dy.shape
    ct = CONV_DIM
    t = min(ROW_TILE, s)
    nblk = s // t
    glu = glu_p is not None

    def body(*refs):
        if glu:
            dy_ref, dyh_ref, x_ref, xh_ref, w_ref, a_ref, dx_ref, dw_ref, dye, xe, dwacc = refs
        else:
            dy_ref, dyh_ref, x_ref, xh_ref, w_ref, dx_ref, dw_ref, dye, xe, dwacc = refs
        i = pl.program_id(1)

        @pl.when(i == 0)
        def _():
            dw_ref[...] = jnp.zeros_like(dw_ref)

        dye[0:t, :] = dy_ref[...]
        dye[t:t + halo, :] = jnp.where(i == nblk - 1, 0.0, dyh_ref[...])
        xe[0:halo, :] = jnp.where(i == 0, 0.0, xh_ref[...].astype(F32))
        xe[halo:halo + t, :] = x_ref[...].astype(F32)
        dwacc[...] = jnp.zeros_like(dwacc)
        dx_groups = _tap_groups([width - 1 - j for j in range(width)])
        dw_groups = _tap_groups([halo - (width - 1) + j for j in range(width)])
        for r0 in range(0, t, CONV_ROWS):
            rows = slice(r0, r0 + CONV_ROWS)
            dx = _tap_sum(dye, w_ref, dx_groups, r0)
            if glu:
                a = a_ref[rows, :].astype(F32)
                a1 = a[:, :ct]
                sg = _sigmoid(a[:, ct:])
                dx_ref[rows, :ct] = (dx * sg).astype(BF16)
                dx_ref[rows, ct:] = (dx * a1 * sg * (1.0 - sg)).astype(BF16)
            else:
                dx_ref[rows, :] = dx.astype(BF16)
            dyt = dy_ref[rows, :]
            for (_, _, _, taps), win in zip(dw_groups, _tap_windows(xe, dw_groups, r0)):
                for j, a in taps:
                    prod = dyt * win[a:a + CONV_ROWS]
                    part = prod[0:8]
                    for q in range(8, CONV_ROWS, 8):
                        part = part + prod[q:q + 8]
                    dwacc[8 * j:8 * j + 8, :] += part
        for j in range(width):
            dw_ref[j:j + 1, :] += _colsum(dwacc[8 * j:8 * j + 8, :])

    per = t // halo
    in_specs = [pl.BlockSpec((t, ct), lambda cb, i: (i, cb)),
                pl.BlockSpec((halo, ct), lambda cb, i: (jnp.minimum((i + 1) * per, nblk * per - 1), cb)),
                pl.BlockSpec((t, ct), lambda cb, i: (i, cb + x_col_block)),
                pl.BlockSpec((halo, ct), lambda cb, i: (jnp.maximum(i * per - 1, 0), cb + x_col_block)),
                pl.BlockSpec((width, ct), lambda cb, i: (0, cb))]
    args = [dy, dy, x, x, w]
    out_cols = ctot
    if glu:
        in_specs.append(pl.BlockSpec((t, 2 * ct), lambda cb, i: (i, COL_A // (2 * ct))))
        args.append(glu_p)
        out_cols = 2 * ct
    ocol = 2 * ct if glu else ct
    return pl.pallas_call(
        body, name=name, grid=(ctot // ct, nblk), in_specs=in_specs,
        out_specs=[pl.BlockSpec((t, ocol), lambda cb, i: (i, cb)), pl.BlockSpec((width, ct), lambda cb, i: (0, cb))],
        out_shape=[jax.ShapeDtypeStruct((s, out_cols), BF16), jax.ShapeDtypeStruct((width, ctot), F32)],
        scratch_shapes=[pltpu.VMEM((t + halo, ct), F32), pltpu.VMEM((halo + t, ct), F32), pltpu.VMEM((8 * width, ct), F32)],
        compiler_params=_params(("parallel", "arbitrary")),
    )(*args)


def _sgu_bwd(dhc, p, ln_g, ln_b, w_s, b_s_t, tag):
    s = p.shape[0]
    t = min(ROW_TILE, s)
    cs = SGU_CHUNK

    def body(dh_ref, uv_ref, g_ref, b_ref, ws_ref, bst_ref, duv_ref, dg_ref, db_ref, dws_ref, dbs_ref):
        _zero_at_first_step(dg_ref, db_ref, dws_ref, dbs_ref)
        uv = uv_ref[...].astype(F32)
        u_raw, v_raw = uv[:, :SGU_DIM], uv[:, SGU_DIM:]
        u = _gelu(u_raw)
        xhat, rstd = _ln_stats(_gelu(v_raw))
        vn = xhat * g_ref[...] + b_ref[...]
        mixed = _sgu_mix(vn, ws_ref, bst_ref, t)
        dh = dh_ref[...]
        duv_ref[:, :SGU_DIM] = (dh * mixed * _dgelu(u_raw)).astype(BF16)
        dmix = dh * u
        row = lax.broadcasted_iota(jnp.int32, (cs, cs), 0)
        col = lax.broadcasted_iota(jnp.int32, (cs, cs), 1)
        dbs = jnp.zeros((cs, LANES), F32)
        chunks = []
        for g in range(SGU_GROUPS):
            wg = jnp.where(row >= col, ws_ref[g], 0.0).astype(BF16)
            dwg = jnp.zeros((cs, cs), F32)
            parts = []
            for ci in range(t // cs):
                rs = slice(ci * cs, (ci + 1) * cs)
                cl = slice(g * SGU_GROUP_DIM, (g + 1) * SGU_GROUP_DIM)
                dm = dmix[rs, cl]
                dmb = dm.astype(BF16)
                dwg = dwg + _dot_nt(dmb, vn[rs, cl].astype(BF16))
                dbs = dbs + _lane_place(jnp.sum(dm, axis=-1, keepdims=True), g, (cs, LANES))
                parts.append(_dot_tn(wg, dmb))
            dws_ref[g] += jnp.where(row >= col, dwg, 0.0)
            chunks.append(jnp.concatenate(parts, axis=0))
        dbs_ref[...] += dbs
        dvn = jnp.concatenate(chunks, axis=1)
        dvv = _ln_bwd(dvn, xhat, rstd, g_ref[...])
        duv_ref[:, SGU_DIM:] = (dvv * _dgelu(v_raw)).astype(BF16)
        dg_ref[...] += _colsum(dvn * xhat)
        db_ref[...] += _colsum(dvn)

    vec = _const_spec((1, SGU_DIM))
    wss = _const_spec((SGU_GROUPS, cs, cs))
    return pl.pallas_call(
        body, name="sgu_bwd" + tag, grid=(s // t,),
        in_specs=[_row_spec(t, SGU_DIM), _row_spec(t, 2 * SGU_DIM, COL_UV // (2 * SGU_DIM)), vec, vec, wss, _const_spec((cs, LANES))],
        out_specs=[_row_spec(t, 2 * SGU_DIM), vec, vec, wss, _const_spec((cs, LANES))],
        out_shape=[jax.ShapeDtypeStruct((s, 2 * SGU_DIM), BF16), jax.ShapeDtypeStruct((1, SGU_DIM), F32),
                   jax.ShapeDtypeStruct((1, SGU_DIM), F32), jax.ShapeDtypeStruct((SGU_GROUPS, cs, cs), F32),
                   jax.ShapeDtypeStruct((cs, LANES), F32)],
        compiler_params=_params(("arbitrary",)),
    )(dhc, p, ln_g, ln_b, w_s, b_s_t)


def _reorder_proj_cols(w):
    pad = jnp.zeros(w.shape[:-1] + (P_COLS - PROJ_COLS,), w.dtype)
    return jnp.concatenate([w[..., :3072], w[..., 3080:PROJ_COLS], w[..., 3072:3080], pad], axis=-1)


def _restore_proj_cols(g):
    return jnp.concatenate([g[..., :3072], g[..., COL_BA:COL_BA + 8], g[..., 3072:COL_BA]], axis=-1)


def _pad_lanes(v, offset):
    return jnp.pad(v, (offset, LANES - offset - v.shape[0]))[None, :]


def _proj_weights(w_in_l):
    w_all = _reorder_proj_cols(w_in_l)
    return dict(w_all=w_all, w_ba=w_all[:, COL_BA:])


def _rest_weights(l, full, rep):
    row = lambda v: v[l][None, :]
    return dict(
        conv_w=full["conv_dw_w"], conv_b=row(rep["conv_dw_b"]), conv_ln_g=row(rep["conv_ln_g"]), conv_ln_b=row(rep["conv_ln_b"]),
        w_pa=full["w_pa"], w_pb=full["w_pb"], w_pc=full["w_pc"],
        gdn_cw=jnp.concatenate([full["gdn_conv_q"], full["gdn_conv_k"], full["gdn_conv_v"]], axis=-1),
        a_log=_pad_lanes(rep["gdn_a_log"][l], GDN_HEADS), dt_bias=_pad_lanes(rep["gdn_dt_bias"][l], GDN_HEADS),
        norm_g=row(rep["gdn_norm_g"]),
        sgu_ln_g=row(rep["sgu_ln_g"]), sgu_ln_b=row(rep["sgu_ln_b"]), sgu_w_s=rep["sgu_w_s"][l],
        sgu_b_s_t=jnp.pad(rep["sgu_b_s"][l].T, ((0, 0), (0, LANES - SGU_GROUPS))),
        b_gate=row(rep["b_gate"]),
        w_o=full["w_o"], ln1_g=row(rep["ln1_g"]), ln1_b=row(rep["ln1_b"]),
        w_ff1=full["w_ff1"], b_ff1=row(rep["b_ff1"]),
        w_ff2=full["w_ff2"], b_ff2=row(rep["b_ff2"]),
        ln2_g=row(rep["ln2_g"]), ln2_b=row(rep["ln2_b"]),
    )


def _layer_fwd(x, x_bf, w_proj, rest_of, tag, carry=None):
    carry = carry or {}
    got = {}
    s = x.shape[0]
    p = _matmul(x_bf, w_proj["w_all"], name="proj_fwd" + tag, tm=min(1024, s), tn=P_TILE, tk=D_MODEL, out_dtype=BF16,
                exchange=carry.get("proj_fwd"))
    if "proj_fwd" in carry:
        p, got["proj_fwd"] = p
    p_ba = _matmul(x_bf, w_proj["w_ba"], name="proj_ba_fwd" + tag, tm=min(2048, s), tn=LANES, tk=D_MODEL)
    w = dict(w_proj, **rest_of(got.get("proj_fwd")))
    h_glu, c, ha = _conv_a_fwd(p, w["conv_w"], w["conv_b"], w["conv_ln_g"], w["conv_ln_b"], tag)
    o, hb, states, minv = _gdn_fwd(p, p_ba, w["gdn_cw"], w["a_log"], w["dt_bias"], w["norm_g"], tag)
    hc = _sgu_fwd(p, w["sgu_ln_g"], w["sgu_ln_b"], w["sgu_w_s"], w["sgu_b_s_t"], tag)
    y3, merged = _merge_fwd(p, ha, hb, hc, w["w_pa"], w["w_pb"], w["w_pc"], w["b_gate"], tag)
    r1, x1, x1_bf, _ = _matmul_res_ln(merged, w["w_o"], jnp.zeros((1, D_MODEL), F32), x, w["ln1_g"], w["ln1_b"], name="o_res_ln" + tag)
    hpre, h_bf, got["ff1_fwd"] = _ff1_fwd(x1_bf, w["w_ff1"], w["b_ff1"], tag, exchange=carry.get("ff1_fwd"))
    r2, x2, x2_bf, got["ff2_res_ln"] = _matmul_res_ln(h_bf, w["w_ff2"], w["b_ff2"], x1, w["ln2_g"], w["ln2_b"],
                                                      name="ff2_res_ln" + tag, exchange=carry.get("ff2_res_ln"))
    saved = dict(x=x, x_bf=x_bf, p=p, p_ba=p_ba, h_glu=h_glu, c=c, ha=ha, o=o, hb=hb, states=states, minv=minv, hc=hc, y3=y3,
                 merged=merged, r1=r1, x1=x1, x1_bf=x1_bf, hpre=hpre, h_bf=h_bf, r2=r2)
    return x2, x2_bf, saved, w, got


def _layer_bwd(dx2, w, sv, tag, carry=None):
    carry = carry or {}
    got = {}
    g = {}

    def behind(key, call):
        if key not in carry:
            return call(None)
        out, got[key] = call(carry[key](g))
        return out

    s = dx2.shape[0]
    ts = min(1024, s)
    p = sv["p"]
    dr2, dr2_bf, d_ln2_g, d_ln2_b, d_b_ff2 = _ln_bwd_call(dx2, sv["r2"], w["ln2_g"], name="ln2_bwd" + tag)
    dhpre_bf, d_b_ff1 = _ff2_bwd(dr2_bf, w["w_ff2"], sv["hpre"], tag)
    g["w_ff2"] = behind("dw_ff2", lambda ex: _matmul_tn(sv["h_bf"], dr2_bf, name="dw_ff2" + tag, ka=D_FF, tka=1024, tn=1024,
                                                        ts=ts, exchange=ex))
    g["w_ff1"] = behind("dw_ff1", lambda ex: _matmul_tn(sv["x1_bf"], dhpre_bf, name="dw_ff1" + tag, ka=D_MODEL, tka=1024, tn=1024,
                                                        ts=ts, exchange=ex))
    dr1, dr1_bf, d_ln1_g, d_ln1_b = _ff1_bwd_ln(dhpre_bf, w["w_ff1"], dr2, sv["r1"], w["ln1_g"], tag)
    g["w_o"] = behind("dw_o", lambda ex: _matmul_tn(sv["merged"], dr1_bf, name="dw_o" + tag, ka=D_MODEL, tka=1024, tn=1024,
                                                    ts=ts, exchange=ex))
    dgate_bf, dy3_bf, dha, dhb, dhc, d_b_gate = _merge_bwd(dr1_bf, w["w_o"], sv["y3"], p, w["b_gate"],
                                                          w["w_pa"], w["w_pb"], w["w_pc"], tag)
    for n, (name, h) in enumerate((("w_pa", sv["ha"]), ("w_pb", sv["hb"]), ("w_pc", sv["hc"]))):
        g[name] = _matmul_tn(h, dy3_bf, name=f"dw_p{n}" + tag, ka=CONV_DIM, tka=CONV_DIM, tn=1024, ts=ts, n=D_MODEL, b_col_block=n)
    dc, d_conv_ln_g, d_conv_ln_b, d_conv_b = _conv_a_bwd_pre(dha, sv["c"], w["conv_ln_g"], w["conv_ln_b"], tag)
    da_bf, d_conv_w = _dwconv_bwd(dc, sv["h_glu"], w["conv_w"], width=CONV_WIDTH, halo=CONV_HALO, x_col_block=0,
                                  glu_p=p, name="conv_a_bwd" + tag)
    duv_bf, d_sgu_ln_g, d_sgu_ln_b, d_sgu_w_s, d_sgu_b_s_t = _sgu_bwd(dhc, p, w["sgu_ln_g"], w["sgu_ln_b"], w["sgu_w_s"], w["sgu_b_s_t"], tag)
    dpre, dz_bf, dba_bf, d_norm_g, d_a_log, d_dt_bias = _gdn_bwd(dhb, p, sv["p_ba"], sv["o"], sv["states"], sv["minv"], w["gdn_cw"],
                                                                 w["a_log"], w["dt_bias"], w["norm_g"], tag)
    dqkv_bf, d_gdn_cw = _dwconv_bwd(dpre, p, w["gdn_cw"], width=GDN_CONV, halo=GDN_HALO, x_col_block=COL_Q // CONV_DIM,
                                    glu_p=None, name="gdn_conv_bwd" + tag)
    dp_bf = jnp.concatenate([da_bf, dqkv_bf, dz_bf, duv_bf, dgate_bf, dba_bf], axis=1)
    d_w_all = behind("dw_proj", lambda ex: _matmul_tn(sv["x_bf"], dp_bf, name="dw_proj" + tag, ka=D_MODEL, tka=1024, tn=P_TILE,
                                                      ts=ts, exchange=ex))
    g["w_in"] = _restore_proj_cols(d_w_all)
    dx = behind("proj_bwd", lambda ex: _matmul(dp_bf, w["w_all"], name="proj_bwd" + tag, tm=min(1024, s), tn=D_MODEL, tk=P_TILE,
                                               add=dr1, add_scale=DN_ALPHA, w_is_nk=True, exchange=ex))
    g.update(
        b_gate=d_b_gate[0], conv_dw_w=d_conv_w, conv_dw_b=d_conv_b[0], conv_ln_g=d_conv_ln_g[0], conv_ln_b=d_conv_ln_b[0],
        gdn_conv_q=d_gdn_cw[:, :GDN_QK], gdn_conv_k=d_gdn_cw[:, GDN_QK:2 * GDN_QK], gdn_conv_v=d_gdn_cw[:, 2 * GDN_QK:],
        gdn_a_log=d_a_log[0, GDN_HEADS:2 * GDN_HEADS], gdn_dt_bias=d_dt_bias[0, GDN_HEADS:2 * GDN_HEADS], gdn_norm_g=d_norm_g[0],
        sgu_ln_g=d_sgu_ln_g[0], sgu_ln_b=d_sgu_ln_b[0], sgu_w_s=d_sgu_w_s, sgu_b_s=d_sgu_b_s_t[:, :SGU_GROUPS].T,
        ln1_g=d_ln1_g[0], ln1_b=d_ln1_b[0], b_ff1=d_b_ff1[0], b_ff2=d_b_ff2[0], ln2_g=d_ln2_g[0], ln2_b=d_ln2_b[0],
    )
    return dx, g, got


MESH_AXES = ("x", "y", "c")


def _exchange(arrays, scatter, *, name):
    n = len(arrays)
    ex = _Exchange(arrays, scatter)

    def body(*refs):
        ins, outs, sems = refs[:n], refs[n:2 * n], refs[2 * n:]
        _exchange_start(ex, ins, outs, sems)
        _exchange_wait(ex, ins, outs, sems)

    return pl.pallas_call(
        body, name=name, in_specs=ex.in_specs(), out_specs=ex.in_specs(), out_shape=ex.out_shapes(),
        scratch_shapes=ex.scratch(),
    )(*arrays)


class _Exchange:
    def __init__(self, arrays, scatter):
        self.arrays = list(arrays)
        self.scatter = list(scatter)
        self.n = len(self.arrays)

    def in_specs(self):
        return [pl.BlockSpec(memory_space=pl.ANY)] * self.n

    def out_shapes(self):
        return [jax.ShapeDtypeStruct(a.shape if s else (N_DEV,) + a.shape, a.dtype) for a, s in zip(self.arrays, self.scatter)]

    def scratch(self):
        return [pltpu.SemaphoreType.DMA((self.n, N_DEV - 1)), pltpu.SemaphoreType.DMA((self.n, N_DEV - 1)),
                pltpu.SemaphoreType.DMA((self.n,))]


def _exchange_copies(ex, ins, outs, sems, with_arrivals):
    send_sems, recv_sems, local_sems = sems
    x, y, c = lax.axis_index("x"), lax.axis_index("y"), lax.axis_index("c")
    me = 4 * x + 2 * y + c

    def slot(a, d):
        return ins[a].at[d] if ex.scatter[a] else ins[a]

    local = [pltpu.make_async_copy(slot(a, me), outs[a].at[me], local_sems.at[a]) for a in range(ex.n)]
    remote = []
    for k in range(1, N_DEV):
        px = 1 - x if k & 4 else x
        py = 1 - y if k & 2 else y
        pc = 1 - c if k & 1 else c
        peer = 4 * px + 2 * py + pc
        for a in range(ex.n):
            send = pltpu.make_async_remote_copy(
                src_ref=slot(a, peer), dst_ref=outs[a].at[me], send_sem=send_sems.at[a, k - 1],
                recv_sem=recv_sems.at[a, k - 1], device_id=(px, py, pc), device_id_type=pl.DeviceIdType.MESH)
            arrival = pltpu.make_async_remote_copy(
                src_ref=slot(a, peer), dst_ref=outs[a].at[peer], send_sem=send_sems.at[a, k - 1],
                recv_sem=recv_sems.at[a, k - 1], device_id=(px, py, pc), device_id_type=pl.DeviceIdType.MESH) if with_arrivals else None
            remote.append((send, arrival))
    return local, remote


def _exchange_start(ex, ins, outs, sems):
    local, remote = _exchange_copies(ex, ins, outs, sems, False)
    for cp in local:
        cp.start()
    for send, _ in remote:
        send.start()


def _exchange_wait(ex, ins, outs, sems):
    local, remote = _exchange_copies(ex, ins, outs, sems, True)
    for _, arrival in remote:
        arrival.wait_recv()
    for send, _ in remote:
        send.wait_send()
    for cp in local:
        cp.wait()


def _adamw(w, m, v, g_parts, *, name):
    r, c = w.shape
    tr = 256 if r % 256 == 0 else r
    bc1 = 1.0 - ADAM_B1 ** ADAM_STEP
    bc2 = 1.0 - ADAM_B2 ** ADAM_STEP

    def body(w_ref, m_ref, v_ref, gp_ref, g_ref, d_ref, nm_ref, nv_ref):
        g = gp_ref[0].astype(F32)
        for d in range(1, N_DEV):
            g = g + gp_ref[d].astype(F32)
        nm = ADAM_B1 * m_ref[...] + (1.0 - ADAM_B1) * g
        nv = ADAM_B2 * v_ref[...] + (1.0 - ADAM_B2) * (g * g)
        g_ref[...] = g
        nm_ref[...] = nm
        nv_ref[...] = nv
        d_ref[...] = -ADAM_LR * ((nm / bc1) / (jnp.sqrt(nv / bc2) + ADAM_EPS) + ADAM_WD * w_ref[...])

    spec = pl.BlockSpec((tr, c), lambda i: (i, 0))
    return pl.pallas_call(
        body, name=name, grid=(r // tr,),
        in_specs=[spec, spec, spec, pl.BlockSpec((N_DEV, tr, c), lambda i: (0, i, 0))],
        out_specs=[spec] * 4, out_shape=[jax.ShapeDtypeStruct((r, c), F32)] * 4,
        compiler_params=_params(("parallel",)),
    )(w, m, v, g_parts)


SHARDED = dict(w_in=2, conv_dw_w=2, w_pa=2, gdn_conv_q=2, gdn_conv_k=2, gdn_conv_v=2, w_pb=2, w_pc=2, w_o=1, w_ff1=2, w_ff2=1)
WEIGHTS = ["ln_in_g", "ln_in_b", "w_in", "b_gate", "conv_dw_w", "conv_dw_b", "conv_ln_g", "conv_ln_b", "w_pa", "gdn_conv_q",
           "gdn_conv_k", "gdn_conv_v", "gdn_a_log", "gdn_dt_bias", "gdn_norm_g", "w_pb", "sgu_ln_g", "sgu_ln_b", "sgu_w_s",
           "sgu_b_s", "w_pc", "w_o", "ln1_g", "ln1_b", "w_ff1", "b_ff1", "w_ff2", "b_ff2", "ln2_g", "ln2_b"]
REPLICATED = [n for n in WEIGHTS if n not in SHARDED]
CONV_PACK = ["conv_dw_w", "gdn_conv_q", "gdn_conv_k", "gdn_conv_v"]
PROJ_PACK = ["w_pa", "w_pb", "w_pc"]


def _to_slots(full, axis):
    shp = full.shape
    split = full.reshape(shp[:axis] + (N_DEV, shp[axis] // N_DEV) + shp[axis + 1:])
    return jnp.moveaxis(split, axis, 0)


def _from_slots(slots, axis):
    merged = jnp.moveaxis(slots, 0, axis)
    shp = merged.shape
    return merged.reshape(shp[:axis] + (shp[axis] * shp[axis + 1],) + shp[axis + 2:])


def _pack_rows(arrs):
    rows = []
    for a in arrs:
        flat = a.reshape(-1)
        pad = (-flat.shape[0]) % LANES
        rows.append(jnp.pad(flat, (0, pad)).reshape(-1, LANES))
    out = jnp.concatenate(rows, axis=0)
    return jnp.pad(out, ((0, (-out.shape[0]) % 8), (0, 0)))


def _unpack_rows(packed, shapes):
    out, r = [], 0
    for shp in shapes:
        size = math.prod(shp)
        nrows = -(-size // LANES)
        out.append(packed[r:r + nrows].reshape(-1)[:size].reshape(shp))
        r += nrows
    return out


def kernel(x, ln_in_g, ln_in_b, w_in, b_gate, conv_dw_w, conv_dw_b, conv_ln_g, conv_ln_b, w_pa, gdn_conv_q, gdn_conv_k, gdn_conv_v, gdn_a_log, gdn_dt_bias, gdn_norm_g, w_pb, sgu_ln_g, sgu_ln_b, sgu_w_s, sgu_b_s, w_pc, w_o, ln1_g, ln1_b, w_ff1, b_ff1, w_ff2, b_ff2, ln2_g, ln2_b, loss_target, m_ln_in_g, m_ln_in_b, m_w_in, m_b_gate, m_conv_dw_w, m_conv_dw_b, m_conv_ln_g, m_conv_ln_b, m_w_pa, m_gdn_conv_q, m_gdn_conv_k, m_gdn_conv_v, m_gdn_a_log, m_gdn_dt_bias, m_gdn_norm_g, m_w_pb, m_sgu_ln_g, m_sgu_ln_b, m_sgu_w_s, m_sgu_b_s, m_w_pc, m_w_o, m_ln1_g, m_ln1_b, m_w_ff1, m_b_ff1, m_w_ff2, m_b_ff2, m_ln2_g, m_ln2_b, v_ln_in_g, v_ln_in_b, v_w_in, v_b_gate, v_conv_dw_w, v_conv_dw_b, v_conv_ln_g, v_conv_ln_b, v_w_pa, v_gdn_conv_q, v_gdn_conv_k, v_gdn_conv_v, v_gdn_a_log, v_gdn_dt_bias, v_gdn_norm_g, v_w_pb, v_sgu_ln_g, v_sgu_ln_b, v_sgu_w_s, v_sgu_b_s, v_w_pc, v_w_o, v_ln1_g, v_ln1_b, v_w_ff1, v_b_ff1, v_w_ff2, v_b_ff2, v_ln2_g, v_ln2_b):
    args = locals()
    w = {n: args[n] for n in WEIGHTS}
    m = {n: args["m_" + n] for n in WEIGHTS}
    v = {n: args["v_" + n] for n in WEIGHTS}

    rep = {n: w[n] for n in REPLICATED}
    conv_local = jnp.concatenate([w[n] for n in CONV_PACK], axis=1)
    proj_local = jnp.stack([w[n] for n in PROJ_PACK], axis=1).astype(BF16)
    big = ["w_in", "w_o", "w_ff1", "w_ff2"]
    big_local = {n: w[n].astype(BF16) for n in big}
    rest_local = lambda l: [big_local[n][l] for n in big[1:]] + [proj_local[l], conv_local[l]]

    def rest_full(got):
        full_l = {n: _from_slots(g, SHARDED[n] - 1) for n, g in zip(big[1:], got[:3])}
        proj_full = _from_slots(got[3], 2)
        for i, n in enumerate(PROJ_PACK):
            full_l[n] = proj_full[i]
        conv_full = _from_slots(got[4], 1)
        tap0 = 0
        for n in CONV_PACK:
            taps = w[n].shape[1]
            full_l[n] = conv_full[tap0:tap0 + taps]
            tap0 += taps
        return full_l

    ln_g, ln_b = w["ln_in_g"][None, :], w["ln_in_b"][None, :]
    xs, xs_bf = _ln_in_fwd(x[0], ln_g, ln_b)
    (w_in0,) = _exchange([big_local["w_in"][0]], [False], name="gather_w_in_l0")
    gather = lambda arrays: _Exchange(arrays, [False] * len(arrays))
    xs, xs_bf, sv0, w0, got0 = _layer_fwd(
        xs, xs_bf, _proj_weights(_from_slots(w_in0, 1)), lambda got: _rest_weights(0, rest_full(got), rep), "_l0",
        carry={"proj_fwd": gather(rest_local(0)), "ff1_fwd": gather([big_local["w_in"][1]]), "ff2_res_ln": gather(rest_local(1))})
    xs, xs_bf, sv1, w1, _ = _layer_fwd(
        xs, xs_bf, _proj_weights(_from_slots(got0["ff1_fwd"][0], 1)),
        lambda _: _rest_weights(1, rest_full(got0["ff2_res_ln"]), rep), "_l1")
    d, loss_acc = _loss_fwd_bwd(xs, loss_target[0])
    loss = lax.psum(loss_acc[0, 0], MESH_AXES)

    def scatter_of(g, names):
        def slots(n):
            if n == "proj":
                return _to_slots(jnp.stack([g[q] for q in PROJ_PACK], axis=0).astype(BF16), 2)
            return _to_slots(g[n].astype(BF16), SHARDED[n] - 1)

        return _Exchange([slots(n) for n in names], [True] * len(names))

    d, grads1, _ = _layer_bwd(d, w1, sv1, "_l1")
    plan = {"dw_ff2": (1, ["w_ff1", "w_ff2"]), "dw_ff1": (1, ["w_in"]), "dw_o": (1, ["w_o", "proj"]),
            "dw_proj": (0, ["w_ff1", "w_ff2", "w_o", "proj"]), "proj_bwd": (0, ["w_in"])}
    d, grads0, got = _layer_bwd(
        d, w0, sv0, "_l0",
        carry={call: (lambda g, l=l, names=names: scatter_of(grads1 if l == 1 else g, names)) for call, (l, names) in plan.items()})
    layer_parts = {(l, n): a for call, (l, names) in plan.items() for n, a in zip(names, got[call])}
    dx, _, d_ln_in_g, d_ln_in_b, _ = _ln_bwd_call(d, x[0], ln_g, name="ln_in_bwd")
    grads = {k: jnp.stack([grads0[k], grads1[k]]) for k in grads0}
    grads["ln_in_g"] = d_ln_in_g[0]
    grads["ln_in_b"] = d_ln_in_b[0]
    conv_grad = jnp.concatenate([grads[n] for n in CONV_PACK], axis=1)
    small_parts = _exchange([_to_slots(conv_grad, 2), _pack_rows([grads[n] for n in REPLICATED])], [True, False],
                            name="exchange_small_grads")
    parts = [jnp.stack([layer_parts[(0, n)], layer_parts[(1, n)]], axis=1) for n in big + ["proj"]] + list(small_parts)

    def adam_sharded(g_parts, w_l, m_l, v_l, name):
        shp = w_l.shape
        two_d = lambda a: a.reshape(-1, shp[-1])
        outs = _adamw(two_d(w_l), two_d(m_l), two_d(v_l), g_parts.reshape(N_DEV, -1, shp[-1]), name=name)
        return [o.reshape(shp) for o in outs]

    res = {}
    for n, gp in zip(big, parts[:4]):
        res[n] = adam_sharded(gp, w[n], m[n], v[n], "adamw_" + n)
    proj_res = adam_sharded(parts[4], jnp.stack([w[n] for n in PROJ_PACK], axis=1), jnp.stack([m[n] for n in PROJ_PACK], axis=1),
                            jnp.stack([v[n] for n in PROJ_PACK], axis=1), "adamw_proj")
    for i, n in enumerate(PROJ_PACK):
        res[n] = [o[:, i] for o in proj_res]
    conv_res = adam_sharded(parts[5], conv_local, jnp.concatenate([m[n] for n in CONV_PACK], axis=1),
                            jnp.concatenate([v[n] for n in CONV_PACK], axis=1), "adamw_conv")
    tap0 = 0
    for n in CONV_PACK:
        taps = w[n].shape[1]
        res[n] = [o[:, tap0:tap0 + taps] for o in conv_res]
        tap0 += taps
    rep_shapes = [w[n].shape for n in REPLICATED]
    rep_res = _adamw(_pack_rows([w[n] for n in REPLICATED]), _pack_rows([m[n] for n in REPLICATED]),
                     _pack_rows([v[n] for n in REPLICATED]), parts[6], name="adamw_replicated")
    rep_res = [_unpack_rows(o, rep_shapes) for o in rep_res]
    for i, n in enumerate(REPLICATED):
        res[n] = [o[i] for o in rep_res]

    outs = [loss, dx[None]]
    for j in range(4):
        outs += [res[n][j] for n in WEIGHTS]
    return tuple(outs)
```

```python
import functools
import math

import jax
import jax.numpy as jnp
from jax import lax
from jax.experimental import pallas as pl
from jax.experimental.pallas import tpu as pltpu

F32 = jnp.float32
BF16 = jnp.bfloat16

N_DEV = 8
DEPTH = 2
D_MODEL = 1024
CONV_DIM = 512
CONV_WIDTH = 31
GDN_HEADS = 4
GDN_DK = 128
GDN_QK = 512
GDN_CONV = 4
GDN_CHUNK = 64
SGU_GROUPS = 4
SGU_GROUP_DIM = 128
SGU_DIM = 512
SGU_CHUNK = 128
D_FF = 4096
DN_ALPHA = (2 * DEPTH) ** 0.25
LN_EPS = 1e-5
RMS_EPS = 1e-6
PROJ_COLS = 7176
SHARD_COLS = PROJ_COLS // N_DEV

COL_A = 0
COL_Q = 1024
COL_Z = 2560
COL_UV = 3072
COL_GATE = 4096
COL_BA = 7168
P_COLS = 7296
P_TILE = 2432

ADAM_LR = 0.001
ADAM_B1 = 0.9
ADAM_B2 = 0.999
ADAM_EPS = 1e-08
ADAM_WD = 0.01
ADAM_STEP = 10

VMEM_LIMIT_BYTES = 56 * 1024 * 1024
LANES = 128
ROW_TILE = 512
GDN_BLOCK = 256
CONV_HALO = 32
GDN_HALO = 16


def _params(sem):
    return pltpu.CompilerParams(dimension_semantics=sem, vmem_limit_bytes=VMEM_LIMIT_BYTES)


def _dot(a, b):
    return jnp.dot(a, b, preferred_element_type=F32)


def _dot_nt(a, b):
    return lax.dot_general(a, b, (((1,), (1,)), ((), ())), preferred_element_type=F32)


def _dot_tn(a, b):
    return lax.dot_general(a, b, (((0,), (0,)), ((), ())), preferred_element_type=F32)


def _split(a):
    hi = a.astype(BF16)
    lo = (a - hi.astype(F32)).astype(BF16)
    return hi, lo


def _dot3(a, b, dot=_dot):
    ah, al = _split(a)
    bh, bl = _split(b)
    return dot(ah, bh) + (dot(ah, bl) + dot(al, bh))


def _bdot(a, b, dot=_dot):
    return dot(a.astype(BF16), b.astype(BF16))


def _sigmoid(x):
    return jax.nn.sigmoid(x)


def _silu(x):
    return x * _sigmoid(x)


def _dsilu(x):
    s = _sigmoid(x)
    return s * (1.0 + x * (1.0 - s))


_GELU_C = math.sqrt(2.0 / math.pi)


def _gelu(x):
    return 0.5 * x * (1.0 + jnp.tanh(_GELU_C * (x + 0.044715 * (x * x * x))))


def _dgelu(x):
    t = jnp.tanh(_GELU_C * (x + 0.044715 * (x * x * x)))
    return 0.5 * (1.0 + t) + 0.5 * x * (1.0 - t * t) * (_GELU_C * (1.0 + 3.0 * 0.044715 * (x * x)))


def _ln_stats(x):
    mu = jnp.mean(x, axis=-1, keepdims=True)
    xc = x - mu
    var = jnp.mean(xc * xc, axis=-1, keepdims=True)
    rstd = lax.rsqrt(var + LN_EPS)
    return xc * rstd, rstd


def _ln_bwd(dy, xhat, rstd, g):
    dxh = dy * g
    return rstd * (dxh - jnp.mean(dxh, axis=-1, keepdims=True) - xhat * jnp.mean(dxh * xhat, axis=-1, keepdims=True))


def _colsum(x):
    return jnp.sum(x, axis=0, keepdims=True)


def _row_spec(t, cols, col_block=0):
    return pl.BlockSpec((t, cols), lambda i, cb=col_block: (i, cb))


def _const_spec(shape):
    nd = len(shape)
    return pl.BlockSpec(shape, lambda *_: (0,) * nd)


def _call(body, *, name, grid, in_specs, out_specs, out_shape, args, sem, scratch=(), exchange=None):
    n_in, n_out, n_sc = len(in_specs), len(out_specs), len(scratch)
    if exchange is None:
        def plain(*refs):
            body(refs[:n_in], refs[n_in:n_in + n_out], refs[n_in + n_out:])

        return pl.pallas_call(plain, name=name, grid=grid, in_specs=in_specs, out_specs=out_specs, out_shape=out_shape,
                              scratch_shapes=list(scratch), compiler_params=_params(sem))(*args)
    nex = exchange.n

    def carrying(*refs):
        ins, ex_ins = refs[:n_in], refs[n_in:n_in + nex]
        outs, ex_outs = refs[n_in + nex:n_in + nex + n_out], refs[n_in + nex + n_out:n_in + 2 * nex + n_out]
        sc, sems = refs[n_in + 2 * nex + n_out:n_in + 2 * nex + n_out + n_sc], refs[n_in + 2 * nex + n_out + n_sc:]
        ids = [pl.program_id(d) for d in range(len(grid))]
        first = functools.reduce(jnp.logical_and, [i == 0 for i in ids])
        last = functools.reduce(jnp.logical_and, [i == g - 1 for i, g in zip(ids, grid)])

        @pl.when(first)
        def _():
            _exchange_start(exchange, ex_ins, ex_outs, sems)

        body(ins, outs, sc)

        @pl.when(last)
        def _():
            _exchange_wait(exchange, ex_ins, ex_outs, sems)

    res = pl.pallas_call(
        carrying, name=name, grid=grid, in_specs=list(in_specs) + exchange.in_specs(),
        out_specs=list(out_specs) + exchange.in_specs(), out_shape=list(out_shape) + exchange.out_shapes(),
        scratch_shapes=list(scratch) + exchange.scratch(), compiler_params=_params(("arbitrary",) * len(grid)),
    )(*args, *exchange.arrays)
    return res[:n_out], res[n_out:]


def _matmul(a, w, *, name, tm, tn, tk, out_dtype=F32, a_col_block=0, add=None, add_scale=1.0, w_is_nk=False, exchange=None):
    m = a.shape[0]
    k, n = w.shape[::-1] if w_is_nk else w.shape
    nk = k // tk
    has_add = add is not None

    def body(ins, outs, scratch):
        a_ref, w_ref = ins[:2]
        o_ref, acc_ref = outs[0], scratch[0]
        if has_add:
            add_ref = ins[2]
        kk = pl.program_id(2)

        @pl.when(kk == 0)
        def _():
            acc_ref[...] = jnp.zeros_like(acc_ref)

        acc_ref[...] += (_dot_nt if w_is_nk else _dot)(a_ref[...], w_ref[...])

        @pl.when(kk == nk - 1)
        def _():
            r = acc_ref[...]
            if has_add:
                r = r + add_scale * add_ref[...]
            o_ref[...] = r.astype(out_dtype)

    in_specs = [pl.BlockSpec((tm, tk), lambda j, i, kk: (i, kk + a_col_block)),
                pl.BlockSpec((tn, tk), lambda j, i, kk: (j, kk)) if w_is_nk else pl.BlockSpec((tk, tn), lambda j, i, kk: (kk, j))]
    args = [a, w]
    if has_add:
        in_specs.append(pl.BlockSpec((tm, tn), lambda j, i, kk: (i, j)))
        args.append(add)
    res = _call(body, name=name, grid=(n // tn, m // tm, nk), in_specs=in_specs,
                out_specs=[pl.BlockSpec((tm, tn), lambda j, i, kk: (i, j))],
                out_shape=[jax.ShapeDtypeStruct((m, n), out_dtype)], scratch=[pltpu.VMEM((tm, tn), F32)],
                args=args, sem=("parallel", "parallel", "arbitrary"), exchange=exchange)
    return res[0] if exchange is None else (res[0][0], res[1])


def _matmul_tn(a, b, *, name, ka, tka, tn, ts, n=None, a_col_block=0, b_col_block=0, exchange=None):
    s = a.shape[0]
    n = b.shape[1] if n is None else n
    ns = s // ts

    def body(ins, outs, scratch):
        a_ref, b_ref = ins
        o_ref = outs[0]

        @pl.when(pl.program_id(2) == 0)
        def _():
            o_ref[...] = jnp.zeros_like(o_ref)

        o_ref[...] += _dot_tn(a_ref[...], b_ref[...])

    res = _call(body, name=name, grid=(ka // tka, n // tn, ns),
                in_specs=[pl.BlockSpec((ts, tka), lambda i, j, t: (t, i + a_col_block)),
                          pl.BlockSpec((ts, tn), lambda i, j, t: (t, j + b_col_block))],
                out_specs=[pl.BlockSpec((tka, tn), lambda i, j, t: (i, j))],
                out_shape=[jax.ShapeDtypeStruct((ka, n), F32)], args=[a, b],
                sem=("parallel", "parallel", "arbitrary"), exchange=exchange)
    return res[0] if exchange is None else (res[0][0], res[1])


def _ln_in_fwd(x, g, b):
    s = x.shape[0]
    t = min(ROW_TILE, s)

    def body(x_ref, g_ref, b_ref, y_ref, ybf_ref):
        xhat, _ = _ln_stats(x_ref[...])
        y = xhat * g_ref[...] + b_ref[...]
        y_ref[...] = y
        ybf_ref[...] = y.astype(BF16)

    return pl.pallas_call(
        body, name="ln_in_fwd", grid=(s // t,),
        in_specs=[_row_spec(t, D_MODEL), _const_spec((1, D_MODEL)), _const_spec((1, D_MODEL))],
        out_specs=[_row_spec(t, D_MODEL), _row_spec(t, D_MODEL)],
        out_shape=[jax.ShapeDtypeStruct((s, D_MODEL), F32), jax.ShapeDtypeStruct((s, D_MODEL), BF16)],
        compiler_params=_params(("parallel",)),
    )(x, g, b)


def _prev_halo_spec(t, halo, cols, col_block):
    per = t // halo
    return pl.BlockSpec((halo, cols), lambda i, cb=col_block: (jnp.maximum(i * per - 1, 0), cb))


def _next_halo_spec(t, halo, cols, col_block, n_blocks):
    per = t // halo
    last = n_blocks * per - 1
    return pl.BlockSpec((halo, cols), lambda i, cb=col_block: (jnp.minimum((i + 1) * per, last), cb))


CONV_ROWS = 32


def _tap_groups(offsets):
    by_shift = {}
    for j, off in enumerate(offsets):
        by_shift.setdefault(off % 8, []).append((j, off // 8))
    groups = []
    for shift, taps in sorted(by_shift.items()):
        first = min(a for _, a in taps)
        last = max(a for _, a in taps)
        groups.append((shift, 8 * first, 8 * (last - first), [(j, 8 * (a - first)) for j, a in taps]))
    return groups


def _tap_windows(ext, groups, r0):
    return [ext[r0 + first + shift:r0 + first + shift + CONV_ROWS + extra, :] for shift, first, extra, _ in groups]


def _tap_sum(ext, w_ref, groups, r0):
    acc = jnp.zeros((CONV_ROWS, ext.shape[1]), F32)
    for (_, _, _, taps), win in zip(groups, _tap_windows(ext, groups, r0)):
        for j, a in taps:
            acc = acc + w_ref[j:j + 1, :] * win[a:a + CONV_ROWS]
    return acc


def _conv_a_fwd(p, w, b, ln_g, ln_b, tag):
    s = p.shape[0]
    t = min(ROW_TILE, s)
    width = CONV_WIDTH

    def body(a_ref, halo_ref, w_ref, b_ref, g_ref, bb_ref, h_ref, c_ref, ha_ref, ext):
        i = pl.program_id(0)
        a = a_ref[...].astype(F32)
        h = a[:, :CONV_DIM] * _sigmoid(a[:, CONV_DIM:])
        ah = halo_ref[...].astype(F32)
        hh = ah[:, :CONV_DIM] * _sigmoid(ah[:, CONV_DIM:])
        ext[0:CONV_HALO, :] = jnp.where(i == 0, 0.0, hh)
        ext[CONV_HALO:CONV_HALO + t, :] = h
        h_ref[...] = h
        groups = _tap_groups([CONV_HALO - (width - 1) + j for j in range(width)])
        for r0 in range(0, t, CONV_ROWS):
            c = _tap_sum(ext, w_ref, groups, r0) + b_ref[...]
            xhat, _ = _ln_stats(c)
            n = xhat * g_ref[...] + bb_ref[...]
            c_ref[r0:r0 + CONV_ROWS, :] = c
            ha_ref[r0:r0 + CONV_ROWS, :] = _silu(n).astype(BF16)

    return pl.pallas_call(
        body, name="conv_a_fwd" + tag, grid=(s // t,),
        in_specs=[_row_spec(t, 2 * CONV_DIM, COL_A // (2 * CONV_DIM)),
                  _prev_halo_spec(t, CONV_HALO, 2 * CONV_DIM, COL_A // (2 * CONV_DIM)),
                  _const_spec((width, CONV_DIM)), _const_spec((1, CONV_DIM)),
                  _const_spec((1, CONV_DIM)), _const_spec((1, CONV_DIM))],
        out_specs=[_row_spec(t, CONV_DIM)] * 3,
        out_shape=[jax.ShapeDtypeStruct((s, CONV_DIM), F32), jax.ShapeDtypeStruct((s, CONV_DIM), F32),
                   jax.ShapeDtypeStruct((s, CONV_DIM), BF16)],
        scratch_shapes=[pltpu.VMEM((CONV_HALO + t, CONV_DIM), F32)],
        compiler_params=_params(("parallel",)),
    )(p, p, w, b, ln_g, ln_b)


def _sgu_mix(vn, wt_ref, bst_ref, t):
    row = lax.broadcasted_iota(jnp.int32, (SGU_CHUNK, SGU_CHUNK), 0)
    col = lax.broadcasted_iota(jnp.int32, (SGU_CHUNK, SGU_CHUNK), 1)
    chunks = []
    for ci in range(t // SGU_CHUNK):
        groups = []
        for g in range(SGU_GROUPS):
            wg = jnp.where(row >= col, wt_ref[g], 0.0).astype(BF16)
            v_cg = vn[ci * SGU_CHUNK:(ci + 1) * SGU_CHUNK, g * SGU_GROUP_DIM:(g + 1) * SGU_GROUP_DIM]
            groups.append(_dot(wg, v_cg.astype(BF16)) + bst_ref[:, g:g + 1])
        chunks.append(jnp.concatenate(groups, axis=1))
    return jnp.concatenate(chunks, axis=0)


def _sgu_fwd(p, ln_g, ln_b, w_s, b_s_t, tag):
    s = p.shape[0]
    t = min(ROW_TILE, s)

    def body(uv_ref, g_ref, b_ref, ws_ref, bst_ref, hc_ref):
        uv = uv_ref[...].astype(F32)
        u = _gelu(uv[:, :SGU_DIM])
        vv = _gelu(uv[:, SGU_DIM:])
        xhat, _ = _ln_stats(vv)
        vn = xhat * g_ref[...] + b_ref[...]
        mixed = _sgu_mix(vn, ws_ref, bst_ref, t)
        hc_ref[...] = (u * mixed).astype(BF16)

    return pl.pallas_call(
        body, name="sgu_fwd" + tag, grid=(s // t,),
        in_specs=[_row_spec(t, 2 * SGU_DIM, COL_UV // (2 * SGU_DIM)),
                  _const_spec((1, SGU_DIM)), _const_spec((1, SGU_DIM)),
                  _const_spec((SGU_GROUPS, SGU_CHUNK, SGU_CHUNK)), _const_spec((SGU_CHUNK, LANES))],
        out_specs=_row_spec(t, SGU_DIM),
        out_shape=jax.ShapeDtypeStruct((s, SGU_DIM), BF16),
        compiler_params=_params(("parallel",)),
    )(p, ln_g, ln_b, w_s, b_s_t)


def _merge_fwd(p, ha, hb, hc, w_pa, w_pb, w_pc, b_gate, tag):
    s = p.shape[0]
    t = min(ROW_TILE, s)
    gb = COL_GATE // D_MODEL

    def body(ga_ref, gb_ref, gc_ref, ha_ref, hb_ref, hc_ref, wa_ref, wb_ref, wc_ref, bg_ref, y_ref, m_ref):
        merged = jnp.zeros((t, D_MODEL), F32)
        for idx, (g_ref, h_ref, w_ref) in enumerate(((ga_ref, ha_ref, wa_ref), (gb_ref, hb_ref, wb_ref), (gc_ref, hc_ref, wc_ref))):
            y = _dot(h_ref[...], w_ref[...])
            sg = _sigmoid(g_ref[...].astype(F32) + bg_ref[:, idx * D_MODEL:(idx + 1) * D_MODEL])
            y_ref[:, idx * D_MODEL:(idx + 1) * D_MODEL] = y.astype(BF16)
            merged = merged + sg * y
        m_ref[...] = merged.astype(BF16)

    hspec = _row_spec(t, CONV_DIM)
    wspec = _const_spec((CONV_DIM, D_MODEL))
    return pl.pallas_call(
        body, name="merge_fwd" + tag, grid=(s // t,),
        in_specs=[_row_spec(t, D_MODEL, gb), _row_spec(t, D_MODEL, gb + 1), _row_spec(t, D_MODEL, gb + 2),
                  hspec, hspec, hspec, wspec, wspec, wspec, _const_spec((1, 3 * D_MODEL))],
        out_specs=[_row_spec(t, 3 * D_MODEL), _row_spec(t, D_MODEL)],
        out_shape=[jax.ShapeDtypeStruct((s, 3 * D_MODEL), BF16), jax.ShapeDtypeStruct((s, D_MODEL), BF16)],
        compiler_params=_params(("parallel",)),
    )(p, p, p, ha, hb, hc, w_pa, w_pb, w_pc, b_gate)


def _matmul_res_ln(a, w, bias, x_res, ln_g, ln_b, *, name, exchange=None):
    s, k = a.shape
    t = min(ROW_TILE, s)

    def body(ins, outs, scratch):
        a_ref, w_ref, bias_ref, x_ref, g_ref, b_ref = ins
        r_ref, y_ref, ybf_ref = outs
        r = DN_ALPHA * x_ref[...] + _dot(a_ref[...], w_ref[...]) + bias_ref[...]
        xhat, _ = _ln_stats(r)
        y = xhat * g_ref[...] + b_ref[...]
        r_ref[...] = r
        y_ref[...] = y
        ybf_ref[...] = y.astype(BF16)

    vec = _const_spec((1, D_MODEL))
    res = _call(
        body, name=name, grid=(s // t,),
        in_specs=[_row_spec(t, k), _const_spec((k, D_MODEL)), vec, _row_spec(t, D_MODEL), vec, vec],
        out_specs=[_row_spec(t, D_MODEL)] * 3,
        out_shape=[jax.ShapeDtypeStruct((s, D_MODEL), F32), jax.ShapeDtypeStruct((s, D_MODEL), F32),
                   jax.ShapeDtypeStruct((s, D_MODEL), BF16)],
        args=[a, w, bias, x_res, ln_g, ln_b], sem=("parallel",), exchange=exchange)
    return (*res, None) if exchange is None else (*res[0], res[1])


def _ff1_fwd(x_bf, w, b, tag, exchange=None):
    s = x_bf.shape[0]
    tm = min(1024, s)
    tn = 1024

    def body(ins, outs, scratch):
        x_ref, w_ref, b_ref = ins
        r = jnp.maximum(_dot(x_ref[...], w_ref[...]) + b_ref[...], 0.0)
        outs[0][...] = (r * r).astype(BF16)

    res = _call(
        body, name="ff1_fwd" + tag, grid=(s // tm, D_FF // tn),
        in_specs=[pl.BlockSpec((tm, D_MODEL), lambda i, j: (i, 0)), pl.BlockSpec((D_MODEL, tn), lambda i, j: (0, j)),
                  pl.BlockSpec((1, tn), lambda i, j: (0, j))],
        out_specs=[pl.BlockSpec((tm, tn), lambda i, j: (i, j))],
        out_shape=[jax.ShapeDtypeStruct((s, D_FF), BF16)],
        args=[x_bf, w, b], sem=("parallel", "parallel"), exchange=exchange)
    return (res[0], None) if exchange is None else (res[0][0], res[1])


def _loss_fwd_bwd(y, target):
    s = y.shape[0]
    t = min(ROW_TILE, s)

    def body(y_ref, t_ref, dy_ref, loss_ref):
        @pl.when(pl.program_id(0) == 0)
        def _():
            loss_ref[...] = jnp.zeros_like(loss_ref)

        err = y_ref[...] - t_ref[...]
        dy_ref[...] = err * (1.0 / D_MODEL)
        per_row = jnp.mean(err * err, axis=-1, keepdims=True)
        loss_ref[...] += 0.5 * jnp.sum(per_row, axis=0, keepdims=True)

    return pl.pallas_call(
        body, name="loss_fwd_bwd", grid=(s // t,),
        in_specs=[_row_spec(t, D_MODEL), _row_spec(t, D_MODEL)],
        out_specs=[_row_spec(t, D_MODEL), _const_spec((8, LANES))],
        out_shape=[jax.ShapeDtypeStruct((s, D_MODEL), F32), jax.ShapeDtypeStruct((8, LANES), F32)],
        compiler_params=_params(("arbitrary",)),
    )(y, target)


def _softplus(x):
    return jnp.maximum(x, 0.0) + jnp.log1p(jnp.exp(-jnp.abs(x)))


def _gdn_conv_silu_norm(q_ref, k_ref, v_ref, hq_ref, hk_ref, hv_ref, cw_ref, ext, first):
    t = q_ref.shape[0]
    for n, (r, h) in enumerate(((q_ref, hq_ref), (k_ref, hk_ref), (v_ref, hv_ref))):
        ext[0:GDN_HALO, n * GDN_QK:(n + 1) * GDN_QK] = jnp.where(first, 0.0, h[...].astype(F32))
        ext[GDN_HALO:GDN_HALO + t, n * GDN_QK:(n + 1) * GDN_QK] = r[...].astype(F32)
    pre = jnp.zeros((t, 3 * GDN_QK), F32)
    for j in range(GDN_CONV):
        pre = pre + cw_ref[j:j + 1, :] * ext[pl.ds(GDN_HALO - (GDN_CONV - 1) + j, t), :]
    act = _silu(pre)
    rq, rk = [], []
    for h in range(GDN_HEADS):
        qh = act[:, h * GDN_DK:(h + 1) * GDN_DK]
        kh = act[:, GDN_QK + h * GDN_DK:GDN_QK + (h + 1) * GDN_DK]
        rq.append(lax.rsqrt(jnp.sum(qh * qh, axis=-1, keepdims=True) + RMS_EPS))
        rk.append(lax.rsqrt(jnp.sum(kh * kh, axis=-1, keepdims=True) + RMS_EPS))
    return pre, act, rq, rk


def _gdn_gates(ba, alog_ref, dtb_ref):
    lane = lax.broadcasted_iota(jnp.int32, ba.shape, 1)
    beta = _sigmoid(ba)
    g = -jnp.exp(alog_ref[...]) * _softplus(ba + dtb_ref[...])
    g = jnp.where((lane >= GDN_HEADS) & (lane < 2 * GDN_HEADS), g, 0.0)
    return beta, g


def _chunk_cumsum_matrix(t, upper):
    row = lax.broadcasted_iota(jnp.int32, (t, t), 0)
    col = lax.broadcasted_iota(jnp.int32, (t, t), 1)
    same = (row // GDN_CHUNK) == (col // GDN_CHUNK)
    tri = (col >= row) if upper else (col <= row)
    return jnp.where(same & tri, 1.0, 0.0).astype(F32)


def _each(fn, *lists):
    return [fn(*a) for a in zip(*lists)]


def _gdn_pair_items(qn_s, kn_s, vc_s, beta_s, gam_s, nc):
    items = []
    for ci in range(nc):
        rows = slice(ci * GDN_CHUNK, (ci + 1) * GDN_CHUNK)
        gam_blk = gam_s[rows, :]
        gam_t = gam_blk.T
        beta_blk = beta_s[rows, :]
        for h in range(GDN_HEADS):
            sl = slice(h * GDN_DK, (h + 1) * GDN_DK)
            items.append((qn_s[rows, sl], kn_s[rows, sl], vc_s[rows, sl], beta_blk[:, h:h + 1],
                          gam_blk[:, GDN_HEADS + h:GDN_HEADS + h + 1], gam_t[GDN_HEADS + h:GDN_HEADS + h + 1, :]))
    return items


def _gdn_prep(items):
    c = GDN_CHUNK
    row = lax.broadcasted_iota(jnp.int32, (c, c), 0)
    col = lax.broadcasted_iota(jnp.int32, (c, c), 1)
    causal = row >= col
    strict = row > col
    kbs = [k.astype(BF16) for _, k, _, _, _, _ in items]
    kks = _each(_dot_nt, kbs, kbs)
    qks = _each(_dot_nt, [q.astype(BF16) for q, _, _, _, _, _ in items], kbs)
    out = []
    for (q, k, v, beta_c, gam_c, gam_r), kk, qk in zip(items, kks, qks):
        decay = jnp.where(causal, jnp.exp(jnp.where(causal, gam_c - gam_r, 0.0)), 0.0)
        gm = jnp.exp(gam_c)
        glast = gam_c[c - 1:c, :]
        elast = jnp.exp(glast - gam_c)
        out.append(dict(causal=causal, strict=strict, decay=decay, kk=kk, low=jnp.where(strict, beta_c * kk * decay, 0.0),
                        a_qk=qk * decay, gm=gm, glast_exp=jnp.exp(glast), elast=elast, q=q, k=k, v=v, beta_c=beta_c,
                        r=jnp.concatenate([beta_c * v, beta_c * k * gm], axis=1), qd=q * gm, kd=k * elast))
    return out


def _unit_lower_inverses_minus_identity(lows):
    ps = [-low for low in lows]
    mis = list(ps)
    for _ in range(5):
        ps = _each(_bdot, ps, ps)
        ts = _each(_bdot, mis, ps)
        mis = [mi + p + t for mi, p, t in zip(mis, ps, ts)]
    return mis


def _apply_inverses(mis, rs, dot=_dot):
    return [r + t for r, t in zip(rs, _each(functools.partial(_bdot, dot=dot), mis, rs))]


def _gdn_fwd(p, p_ba, conv_w, a_log, dt_bias, norm_g, tag):
    s = p.shape[0]
    t = min(GDN_BLOCK, s)
    nc = t // GDN_CHUNK
    nblk = s // t
    qb, kb_, vb, zb = COL_Q // GDN_QK, COL_Q // GDN_QK + 1, COL_Q // GDN_QK + 2, COL_Z // GDN_QK
    scale = GDN_DK ** -0.5

    def body(q_ref, k_ref, v_ref, hq_ref, hk_ref, hv_ref, z_ref, ba_ref, cw_ref, alog_ref, dtb_ref, ng_ref,
             o_ref, hb_ref, st_ref, m_ref, ext, qn_s, kn_s, vc_s, beta_s, gam_s, state):
        i = pl.program_id(0)

        @pl.when(i == 0)
        def _():
            state[...] = jnp.zeros_like(state)

        _, act, rq, rk = _gdn_conv_silu_norm(q_ref, k_ref, v_ref, hq_ref, hk_ref, hv_ref, cw_ref, ext, i == 0)
        for h in range(GDN_HEADS):
            sl = slice(h * GDN_DK, (h + 1) * GDN_DK)
            qn_s[:, sl] = act[:, sl] * (rq[h] * scale)
            kn_s[:, sl] = act[:, GDN_QK + h * GDN_DK:GDN_QK + (h + 1) * GDN_DK] * rk[h]
        vc_s[...] = act[:, 2 * GDN_QK:]
        beta, g = _gdn_gates(ba_ref[...], alog_ref, dtb_ref)
        beta_s[...] = beta
        gam_s[...] = jnp.dot(_chunk_cumsum_matrix(t, False), g, preferred_element_type=F32, precision=lax.Precision.HIGHEST)

        prs = _gdn_prep(_gdn_pair_items(qn_s, kn_s, vc_s, beta_s, gam_s, nc))
        mis = _unit_lower_inverses_minus_identity([pr["low"] for pr in prs])
        xs = _apply_inverses(mis, [pr["r"] for pr in prs])
        heads = range(GDN_HEADS)
        for ci in range(nc):
            rows = slice(ci * GDN_CHUNK, (ci + 1) * GDN_CHUNK)
            pc, xc = prs[ci * GDN_HEADS:(ci + 1) * GDN_HEADS], xs[ci * GDN_HEADS:(ci + 1) * GDN_HEADS]
            m_ref[rows, :] = jnp.concatenate(mis[ci * GDN_HEADS:(ci + 1) * GDN_HEADS], axis=1)
            sts = [state[h * GDN_DK:(h + 1) * GDN_DK, :] for h in heads]
            for h in heads:
                st_ref[(ci * GDN_HEADS + h) * GDN_DK:(ci * GDN_HEADS + h + 1) * GDN_DK, :] = sts[h]
            w_st = _each(_bdot, [x[:, GDN_DK:] for x in xc], sts)
            q_st = _each(_bdot, [pr["qd"] for pr in pc], sts)
            vns = [x[:, :GDN_DK] - ws for x, ws in zip(xc, w_st)]
            a_vn = _each(_bdot, [pr["a_qk"] for pr in pc], vns)
            k_vn = _each(functools.partial(_bdot, dot=_dot_tn), [pr["kd"] for pr in pc], vns)
            for h in heads:
                o_ref[rows, h * GDN_DK:(h + 1) * GDN_DK] = q_st[h] + a_vn[h]
                state[h * GDN_DK:(h + 1) * GDN_DK, :] = sts[h] * pc[h]["glast_exp"] + k_vn[h]

        z = z_ref[...].astype(F32)
        for h in range(GDN_HEADS):
            sl = slice(h * GDN_DK, (h + 1) * GDN_DK)
            o = o_ref[:, sl]
            on = o * lax.rsqrt(jnp.mean(o * o, axis=-1, keepdims=True) + RMS_EPS)
            hb_ref[:, sl] = (on * ng_ref[...] * _silu(z[:, sl])).astype(BF16)

    col = lambda cb: pl.BlockSpec((t, GDN_QK), lambda i, cb=cb: (i, cb))
    halo = lambda cb: _prev_halo_spec(t, GDN_HALO, GDN_QK, cb)
    vec = _const_spec((1, LANES))
    return pl.pallas_call(
        body, name="gdn_fwd" + tag, grid=(nblk,),
        in_specs=[col(qb), col(kb_), col(vb), halo(qb), halo(kb_), halo(vb), col(zb),
                  _row_spec(t, LANES), _const_spec((GDN_CONV, 3 * GDN_QK)), vec, vec, vec],
        out_specs=[_row_spec(t, GDN_QK), _row_spec(t, GDN_QK),
                   pl.BlockSpec((nc * GDN_HEADS * GDN_DK, GDN_DK), lambda i: (i, 0)),
                   _row_spec(t, GDN_HEADS * GDN_CHUNK)],
        out_shape=[jax.ShapeDtypeStruct((s, GDN_QK), F32), jax.ShapeDtypeStruct((s, GDN_QK), BF16),
                   jax.ShapeDtypeStruct((s // GDN_CHUNK * GDN_HEADS * GDN_DK, GDN_DK), F32),
                   jax.ShapeDtypeStruct((s, GDN_HEADS * GDN_CHUNK), F32)],
        scratch_shapes=[pltpu.VMEM((GDN_HALO + t, 3 * GDN_QK), F32), pltpu.VMEM((t, GDN_QK), F32),
                        pltpu.VMEM((t, GDN_QK), F32), pltpu.VMEM((t, GDN_QK), F32),
                        pltpu.VMEM((t, LANES), F32), pltpu.VMEM((t, LANES), F32),
                        pltpu.VMEM((GDN_HEADS * GDN_DK, GDN_DK), F32)],
        compiler_params=_params(("arbitrary",)),
    )(p, p, p, p, p, p, p, p_ba, conv_w, a_log, dt_bias, norm_g)


def _lane_place(col, lane_idx, shape):
    lane = lax.broadcasted_iota(jnp.int32, shape, 1)
    return jnp.where(lane == lane_idx, col, 0.0)


def _gdn_bwd(dhb, p, p_ba, o, states, minv, conv_w, a_log, dt_bias, norm_g, tag):
    s = p.shape[0]
    t = min(GDN_BLOCK, s)
    nc = t // GDN_CHUNK
    nblk = s // t
    qb, kb_, vb, zb = COL_Q // GDN_QK, COL_Q // GDN_QK + 1, COL_Q // GDN_QK + 2, COL_Z // GDN_QK
    scale = GDN_DK ** -0.5
    c = GDN_CHUNK

    def body(dhb_ref, q_ref, k_ref, v_ref, hq_ref, hk_ref, hv_ref, z_ref, ba_ref, o_ref, st_ref, m_ref,
             cw_ref, alog_ref, dtb_ref, ng_ref,
             dpre_ref, dz_ref, dba_ref, dng_ref, dalog_ref, ddtb_ref,
             ext, qn_s, kn_s, vc_s, beta_s, gam_s, do_s, dqn_s, dkn_s, dvc_s, dgam_s, dbeta_s, dstate):
        i = pl.program_id(0)

        @pl.when(i == 0)
        def _():
            dstate[...] = jnp.zeros_like(dstate)
            dng_ref[...] = jnp.zeros_like(dng_ref)
            dalog_ref[...] = jnp.zeros_like(dalog_ref)
            ddtb_ref[...] = jnp.zeros_like(ddtb_ref)

        pre, act, rq, rk = _gdn_conv_silu_norm(q_ref, k_ref, v_ref, hq_ref, hk_ref, hv_ref, cw_ref, ext, i == nblk - 1)
        for h in range(GDN_HEADS):
            sl = slice(h * GDN_DK, (h + 1) * GDN_DK)
            qn_s[:, sl] = act[:, sl] * (rq[h] * scale)
            kn_s[:, sl] = act[:, GDN_QK + h * GDN_DK:GDN_QK + (h + 1) * GDN_DK] * rk[h]
        vc_s[...] = act[:, 2 * GDN_QK:]
        ba = ba_ref[...]
        beta, g = _gdn_gates(ba, alog_ref, dtb_ref)
        beta_s[...] = beta
        gam_s[...] = jnp.dot(_chunk_cumsum_matrix(t, False), g, preferred_element_type=F32, precision=lax.Precision.HIGHEST)

        z = z_ref[...].astype(F32)
        dhb = dhb_ref[...]
        dng = jnp.zeros((1, GDN_DK), F32)
        for h in range(GDN_HEADS):
            sl = slice(h * GDN_DK, (h + 1) * GDN_DK)
            oh = o_ref[:, sl]
            r = lax.rsqrt(jnp.mean(oh * oh, axis=-1, keepdims=True) + RMS_EPS)
            on = oh * r
            sz = _silu(z[:, sl])
            dyh = dhb[:, sl]
            dng = dng + _colsum(dyh * on * sz)
            dz_ref[:, sl] = (dyh * on * ng_ref[...] * _dsilu(z[:, sl])).astype(BF16)
            don = dyh * ng_ref[...] * sz
            do_s[:, sl] = r * (don - on * jnp.mean(don * on, axis=-1, keepdims=True))
        dng_ref[...] += dng

        heads = range(GDN_HEADS)
        npairs = nc * GDN_HEADS
        tn = functools.partial(_bdot, dot=_dot_tn)
        nt = functools.partial(_bdot, dot=_dot_nt)
        rsum = lambda a: jnp.sum(a, axis=-1, keepdims=True)
        left = lambda a: a[:, :GDN_DK]
        right = lambda a: a[:, GDN_DK:]

        prs = _gdn_prep(_gdn_pair_items(qn_s, kn_s, vc_s, beta_s, gam_s, nc))
        mis = [m_ref[(n // GDN_HEADS) * c:(n // GDN_HEADS + 1) * c, (n % GDN_HEADS) * c:(n % GDN_HEADS + 1) * c] for n in range(npairs)]
        xs = _apply_inverses(mis, [pr["r"] for pr in prs])
        sts = [st_ref[n * GDN_DK:(n + 1) * GDN_DK, :] for n in range(npairs)]
        dos = [do_s[(n // GDN_HEADS) * c:(n // GDN_HEADS + 1) * c, (n % GDN_HEADS) * GDN_DK:(n % GDN_HEADS + 1) * GDN_DK] for n in range(npairs)]
        w_st = _each(_bdot, [right(x) for x in xs], sts)
        vns = [left(x) - ws for x, ws in zip(xs, w_st)]
        at_do = _each(tn, [pr["a_qk"] for pr in prs], dos)
        dqds = _each(nt, dos, sts)
        d_as = [jnp.where(pr["causal"], a, 0.0) for pr, a in zip(prs, _each(nt, dos, vns))]
        qt_do = _each(tn, [pr["qd"] for pr in prs], dos)

        dvns, dkds, ds_st = [None] * npairs, [None] * npairs, [None] * npairs
        for ci in reversed(range(nc)):
            ids = [ci * GDN_HEADS + h for h in heads]
            dss = [dstate[h * GDN_DK:(h + 1) * GDN_DK, :] for h in heads]
            kd_ds = _each(_bdot, [prs[n]["kd"] for n in ids], dss)
            vn_ds = _each(nt, [vns[n] for n in ids], dss)
            for h, n in enumerate(ids):
                dvns[n] = kd_ds[h] + at_do[n]
                dkds[n] = vn_ds[h]
                ds_st[n] = jnp.sum(rsum(dss[h] * sts[n]), axis=0, keepdims=True)
            wt_dvn = _each(tn, [right(xs[n]) for n in ids], [dvns[n] for n in ids])
            for h, n in enumerate(ids):
                dstate[h * GDN_DK:(h + 1) * GDN_DK, :] = dss[h] * prs[n]["glast_exp"] + qt_do[n] - wt_dvn[h]

        dws = [-a for a in _each(nt, dvns, sts)]
        d_rs = _apply_inverses(mis, [jnp.concatenate([dvn, dw], axis=1) for dvn, dw in zip(dvns, dws)], _dot_tn)
        d_ls = [jnp.where(pr["strict"], -a, 0.0) for pr, a in zip(prs, _each(nt, d_rs, xs))]
        d_l_kds = [d_l * pr["kk"] * pr["decay"] for d_l, pr in zip(d_ls, prs)]
        dkks = [d_l * pr["beta_c"] * pr["decay"] for d_l, pr in zip(d_ls, prs)]
        dqks = [d_a * pr["decay"] for d_a, pr in zip(d_as, prs)]
        ks, qs = [pr["k"] for pr in prs], [pr["q"] for pr in prs]
        dk1, dk2, dk3 = _each(_bdot, dkks, ks), _each(tn, dkks, ks), _each(tn, dqks, qs)
        dq1 = _each(_bdot, dqks, ks)
        rowi = lax.broadcasted_iota(jnp.int32, (c, 1), 0)
        for ci in range(nc):
            rows = slice(ci * c, (ci + 1) * c)
            dgam_blk = jnp.zeros((c, LANES), F32)
            dbeta_blk = jnp.zeros((c, LANES), F32)
            for h in heads:
                n = ci * GDN_HEADS + h
                sl = slice(h * GDN_DK, (h + 1) * GDN_DK)
                pr, d_r = prs[n], d_rs[n]
                d_ru, d_rw = left(d_r), right(d_r)
                gmat = pr["beta_c"] * d_l_kds[n] + d_as[n] * pr["a_qk"]
                dkd_kd = rsum(dkds[n] * pr["kd"])
                dgam = rsum(gmat) - rsum(gmat.T) + rsum(d_rw * right(pr["r"])) + rsum(dqds[n] * pr["qd"]) - dkd_kd
                dglast = jnp.sum(dkd_kd, axis=0, keepdims=True) + ds_st[n] * pr["glast_exp"]
                dgam = dgam + jnp.where(rowi == c - 1, dglast, 0.0)
                dbeta = rsum(d_l_kds[n]) + rsum(d_ru * pr["v"]) + rsum(d_rw * pr["k"]) * pr["gm"]
                dvc_s[rows, sl] = pr["beta_c"] * d_ru
                dkn_s[rows, sl] = dk1[n] + dk2[n] + dk3[n] + d_rw * (pr["beta_c"] * pr["gm"]) + dkds[n] * pr["elast"]
                dqn_s[rows, sl] = dq1[n] + dqds[n] * pr["gm"]
                dgam_blk = dgam_blk + _lane_place(dgam, GDN_HEADS + h, (c, LANES))
                dbeta_blk = dbeta_blk + _lane_place(dbeta, h, (c, LANES))
            dgam_s[rows, :] = dgam_blk
            dbeta_s[rows, :] = dbeta_blk

        dg = jnp.dot(_chunk_cumsum_matrix(t, True), dgam_s[...], preferred_element_type=F32, precision=lax.Precision.HIGHEST)
        lane = lax.broadcasted_iota(jnp.int32, (t, LANES), 1)
        g_lanes = (lane >= GDN_HEADS) & (lane < 2 * GDN_HEADS)
        da_logit = jnp.where(g_lanes, dg * (-jnp.exp(alog_ref[...])) * _sigmoid(ba + dtb_ref[...]), 0.0)
        db_logit = jnp.where(lane < GDN_HEADS, dbeta_s[...] * beta * (1.0 - beta), 0.0)
        dba_ref[...] = (da_logit + db_logit).astype(BF16)
        dalog_ref[...] += _colsum(dg * g)
        ddtb_ref[...] += _colsum(da_logit)

        dact = []
        for n, (dn_s, rr, sc) in enumerate(((dqn_s, rq, scale), (dkn_s, rk, 1.0))):
            for h in range(GDN_HEADS):
                sl = slice(h * GDN_DK, (h + 1) * GDN_DK)
                y = act[:, n * GDN_QK + h * GDN_DK:n * GDN_QK + (h + 1) * GDN_DK] * rr[h]
                dy = dn_s[:, sl] * sc
                dact.append(rr[h] * (dy - y * jnp.sum(dy * y, axis=-1, keepdims=True)))
        dact.append(dvc_s[...])
        dpre_ref[...] = jnp.concatenate(dact, axis=1) * _dsilu(pre)

    rb = lambda i: nblk - 1 - i
    per = t // GDN_HALO
    col = lambda cb, wd=GDN_QK: pl.BlockSpec((t, wd), lambda i, cb=cb: (rb(i), cb))
    halo = lambda cb: pl.BlockSpec((GDN_HALO, GDN_QK), lambda i, cb=cb: (jnp.maximum(rb(i) * per - 1, 0), cb))
    vec = _const_spec((1, LANES))
    return pl.pallas_call(
        body, name="gdn_bwd" + tag, grid=(nblk,),
        in_specs=[col(0), col(qb), col(kb_), col(vb), halo(qb), halo(kb_), halo(vb), col(zb),
                  pl.BlockSpec((t, LANES), lambda i: (rb(i), 0)), col(0),
                  pl.BlockSpec((nc * GDN_HEADS * GDN_DK, GDN_DK), lambda i: (rb(i), 0)),
                  pl.BlockSpec((t, GDN_HEADS * c), lambda i: (rb(i), 0)),
                  _const_spec((GDN_CONV, 3 * GDN_QK)), vec, vec, vec],
        out_specs=[pl.BlockSpec((t, 3 * GDN_QK), lambda i: (rb(i), 0)), col(0), pl.BlockSpec((t, LANES), lambda i: (rb(i), 0)),
                   vec, vec, vec],
        out_shape=[jax.ShapeDtypeStruct((s, 3 * GDN_QK), F32), jax.ShapeDtypeStruct((s, GDN_QK), BF16),
                   jax.ShapeDtypeStruct((s, LANES), BF16),
                   jax.ShapeDtypeStruct((1, LANES), F32), jax.ShapeDtypeStruct((1, LANES), F32), jax.ShapeDtypeStruct((1, LANES), F32)],
        scratch_shapes=[pltpu.VMEM((GDN_HALO + t, 3 * GDN_QK), F32)] + [pltpu.VMEM((t, GDN_QK), F32)] * 3
                       + [pltpu.VMEM((t, LANES), F32)] * 2 + [pltpu.VMEM((t, GDN_QK), F32)] * 4
                       + [pltpu.VMEM((t, LANES), F32)] * 2 + [pltpu.VMEM((GDN_HEADS * GDN_DK, GDN_DK), F32)],
        compiler_params=_params(("arbitrary",)),
    )(dhb, p, p, p, p, p, p, p, p_ba, o, states, minv, conv_w, a_log, dt_bias, norm_g)


def _zero_at_first_step(*refs):
    @pl.when(pl.program_id(0) == 0)
    def _():
        for r in refs:
            r[...] = jnp.zeros_like(r)


def _ln_bwd_call(dy, x_in, g, *, name):
    s, d = dy.shape
    t = min(ROW_TILE, s)

    def body(dy_ref, x_ref, g_ref, dx_ref, dxbf_ref, dg_ref, db_ref, ds_ref):
        _zero_at_first_step(dg_ref, db_ref, ds_ref)
        dy = dy_ref[...]
        xhat, rstd = _ln_stats(x_ref[...])
        dx = _ln_bwd(dy, xhat, rstd, g_ref[...])
        dx_ref[...] = dx
        dxbf_ref[...] = dx.astype(BF16)
        dg_ref[...] += _colsum(dy * xhat)
        db_ref[...] += _colsum(dy)
        ds_ref[...] += _colsum(dx)

    vec = _const_spec((1, d))
    return pl.pallas_call(
        body, name=name, grid=(s // t,),
        in_specs=[_row_spec(t, d), _row_spec(t, d), vec],
        out_specs=[_row_spec(t, d), _row_spec(t, d), vec, vec, vec],
        out_shape=[jax.ShapeDtypeStruct((s, d), F32), jax.ShapeDtypeStruct((s, d), BF16)] + [jax.ShapeDtypeStruct((1, d), F32)] * 3,
        compiler_params=_params(("arbitrary",)),
    )(dy, x_in, g)


def _ff2_bwd(dr2_bf, w_ff2, h_bf, tag):
    s = dr2_bf.shape[0]
    tm = min(1024, s)
    tn = 1024

    def body(d_ref, w_ref, h_ref, o_ref, db_ref):
        @pl.when(pl.program_id(1) == 0)
        def _():
            db_ref[...] = jnp.zeros_like(db_ref)

        dh = _dot_nt(d_ref[...], w_ref[...]) * (2.0 * jnp.sqrt(h_ref[...].astype(F32)))
        o_ref[...] = dh.astype(BF16)
        db_ref[...] += _colsum(dh)

    return pl.pallas_call(
        body, name="ff2_bwd" + tag, grid=(D_FF // tn, s // tm),
        in_specs=[pl.BlockSpec((tm, D_MODEL), lambda j, i: (i, 0)), pl.BlockSpec((tn, D_MODEL), lambda j, i: (j, 0)),
                  pl.BlockSpec((tm, tn), lambda j, i: (i, j))],
        out_specs=[pl.BlockSpec((tm, tn), lambda j, i: (i, j)), pl.BlockSpec((1, tn), lambda j, i: (0, j))],
        out_shape=[jax.ShapeDtypeStruct((s, D_FF), BF16), jax.ShapeDtypeStruct((1, D_FF), F32)],
        compiler_params=_params(("parallel", "arbitrary")),
    )(dr2_bf, w_ff2, h_bf)


def _ff1_bwd_ln(dhpre_bf, w_ff1, dr2, r1, ln1_g, tag):
    s = dr2.shape[0]
    t = min(ROW_TILE, s)

    def body(dh_ref, w_ref, dr2_ref, r1_ref, g_ref, dr_ref, drbf_ref, dg_ref, db_ref):
        _zero_at_first_step(dg_ref, db_ref)
        dx1 = DN_ALPHA * dr2_ref[...] + _dot_nt(dh_ref[...], w_ref[...])
        xhat, rstd = _ln_stats(r1_ref[...])
        dr = _ln_bwd(dx1, xhat, rstd, g_ref[...])
        dr_ref[...] = dr
        drbf_ref[...] = dr.astype(BF16)
        dg_ref[...] += _colsum(dx1 * xhat)
        db_ref[...] += _colsum(dx1)

    vec = _const_spec((1, D_MODEL))
    return pl.pallas_call(
        body, name="ff1_bwd_ln" + tag, grid=(s // t,),
        in_specs=[_row_spec(t, D_FF), _const_spec((D_MODEL, D_FF)), _row_spec(t, D_MODEL), _row_spec(t, D_MODEL), vec],
        out_specs=[_row_spec(t, D_MODEL), _row_spec(t, D_MODEL), vec, vec],
        out_shape=[jax.ShapeDtypeStruct((s, D_MODEL), F32), jax.ShapeDtypeStruct((s, D_MODEL), BF16),
                   jax.ShapeDtypeStruct((1, D_MODEL), F32), jax.ShapeDtypeStruct((1, D_MODEL), F32)],
        compiler_params=_params(("arbitrary",)),
    )(dhpre_bf, w_ff1, dr2, r1, ln1_g)


def _merge_bwd(dr1_bf, w_o, y3, p, b_gate, w_pa, w_pb, w_pc, tag):
    s = p.shape[0]
    t = min(ROW_TILE, s)
    gb = COL_GATE // D_MODEL

    def body(dr_ref, wo_ref, y_ref, ga_ref, gb_ref, gc_ref, bg_ref, wa_ref, wb_ref, wc_ref,
             dgate_ref, dy_ref, dha_ref, dhb_ref, dhc_ref, dbg_ref):
        _zero_at_first_step(dbg_ref)
        dm = _dot_nt(dr_ref[...], wo_ref[...])
        for idx, (g_ref, w_ref, dh_ref) in enumerate(((ga_ref, wa_ref, dha_ref), (gb_ref, wb_ref, dhb_ref), (gc_ref, wc_ref, dhc_ref))):
            sl = slice(idx * D_MODEL, (idx + 1) * D_MODEL)
            sg = _sigmoid(g_ref[...].astype(F32) + bg_ref[:, sl])
            dgate = dm * y_ref[:, sl].astype(F32) * sg * (1.0 - sg)
            dgate_ref[:, sl] = dgate.astype(BF16)
            dbg_ref[:, sl] += _colsum(dgate)
            dy = (dm * sg).astype(BF16)
            dy_ref[:, sl] = dy
            dh_ref[...] = _dot_nt(dy, w_ref[...])

    hspec = _row_spec(t, CONV_DIM)
    wspec = _const_spec((CONV_DIM, D_MODEL))
    return pl.pallas_call(
        body, name="merge_bwd" + tag, grid=(s // t,),
        in_specs=[_row_spec(t, D_MODEL), _const_spec((D_MODEL, D_MODEL)), _row_spec(t, 3 * D_MODEL),
                  _row_spec(t, D_MODEL, gb), _row_spec(t, D_MODEL, gb + 1), _row_spec(t, D_MODEL, gb + 2),
                  _const_spec((1, 3 * D_MODEL)), wspec, wspec, wspec],
        out_specs=[_row_spec(t, 3 * D_MODEL), _row_spec(t, 3 * D_MODEL), hspec, hspec, hspec, _const_spec((1, 3 * D_MODEL))],
        out_shape=[jax.ShapeDtypeStruct((s, 3 * D_MODEL), BF16), jax.ShapeDtypeStruct((s, 3 * D_MODEL), BF16)]
                  + [jax.ShapeDtypeStruct((s, CONV_DIM), F32)] * 3 + [jax.ShapeDtypeStruct((1, 3 * D_MODEL), F32)],
        compiler_params=_params(("arbitrary",)),
    )(dr1_bf, w_o, y3, p, p, p, b_gate, w_pa, w_pb, w_pc)


def _conv_a_bwd_pre(dha, c, ln_g, ln_b, tag):
    s = c.shape[0]
    t = min(ROW_TILE, s)

    def body(dh_ref, c_ref, g_ref, b_ref, dc_ref, dg_ref, db_ref, ds_ref):
        _zero_at_first_step(dg_ref, db_ref, ds_ref)
        xhat, rstd = _ln_stats(c_ref[...])
        n = xhat * g_ref[...] + b_ref[...]
        dn = dh_ref[...] * _dsilu(n)
        dc = _ln_bwd(dn, xhat, rstd, g_ref[...])
        dc_ref[...] = dc
        dg_ref[...] += _colsum(dn * xhat)
        db_ref[...] += _colsum(dn)
        ds_ref[...] += _colsum(dc)

    vec = _const_spec((1, CONV_DIM))
    return pl.pallas_call(
        body, name="conv_a_bwd_pre" + tag, grid=(s // t,),
        in_specs=[_row_spec(t, CONV_DIM), _row_spec(t, CONV_DIM), vec, vec],
        out_specs=[_row_spec(t, CONV_DIM), vec, vec, vec],
        out_shape=[jax.ShapeDtypeStruct((s, CONV_DIM), F32)] + [jax.ShapeDtypeStruct((1, CONV_DIM), F32)] * 3,
        compiler_params=_params(("arbitrary",)),
    )(dha, c, ln_g, ln_b)


def _dwconv_bwd(dy, x, w, *, width, halo, x_col_block, glu_p, name):
    s, ctot = dy.shape
    ct = CONV_DIM
    t = min(ROW_TILE, s)
    nblk = s // t
    glu = glu_p is not None

    def body(*refs):
        if glu:
            dy_ref, dyh_ref, x_ref, xh_ref, w_ref, a_ref, dx_ref, dw_ref, dye, xe, dwacc = refs
        else:
            dy_ref, dyh_ref, x_ref, xh_ref, w_ref, dx_ref, dw_ref, dye, xe, dwacc = refs
        i = pl.program_id(1)

        @pl.when(i == 0)
        def _():
            dw_ref[...] = jnp.zeros_like(dw_ref)

        dye[0:t, :] = dy_ref[...]
        dye[t:t + halo, :] = jnp.where(i == nblk - 1, 0.0, dyh_ref[...])
        xe[0:halo, :] = jnp.where(i == 0, 0.0, xh_ref[...].astype(F32))
        xe[halo:halo + t, :] = x_ref[...].astype(F32)
        dwacc[...] = jnp.zeros_like(dwacc)
        dx_groups = _tap_groups([width - 1 - j for j in range(width)])
        dw_groups = _tap_groups([halo - (width - 1) + j for j in range(width)])
        for r0 in range(0, t, CONV_ROWS):
            rows = slice(r0, r0 + CONV_ROWS)
            dx = _tap_sum(dye, w_ref, dx_groups, r0)
            if glu:
                a = a_ref[rows, :].astype(F32)
                a1 = a[:, :ct]
                sg = _sigmoid(a[:, ct:])
                dx_ref[rows, :ct] = (dx * sg).astype(BF16)
                dx_ref[rows, ct:] = (dx * a1 * sg * (1.0 - sg)).astype(BF16)
            else:
                dx_ref[rows, :] = dx.astype(BF16)
            dyt = dy_ref[rows, :]
            for (_, _, _, taps), win in zip(dw_groups, _tap_windows(xe, dw_groups, r0)):
                for j, a in taps:
                    prod = dyt * win[a:a + CONV_ROWS]
                    part = prod[0:8]
                    for q in range(8, CONV_ROWS, 8):
                        part = part + prod[q:q + 8]
                    dwacc[8 * j:8 * j + 8, :] += part
        for j in range(width):
            dw_ref[j:j + 1, :] += _colsum(dwacc[8 * j:8 * j + 8, :])

    per = t // halo
    in_specs = [pl.BlockSpec((t, ct), lambda cb, i: (i, cb)),
                pl.BlockSpec((halo, ct), lambda cb, i: (jnp.minimum((i + 1) * per, nblk * per - 1), cb)),
                pl.BlockSpec((t, ct), lambda cb, i: (i, cb + x_col_block)),
                pl.BlockSpec((halo, ct), lambda cb, i: (jnp.maximum(i * per - 1, 0), cb + x_col_block)),
                pl.BlockSpec((width, ct), lambda cb, i: (0, cb))]
    args = [dy, dy, x, x, w]
    out_cols = ctot
    if glu:
        in_specs.append(pl.BlockSpec((t, 2 * ct), lambda cb, i: (i, COL_A // (2 * ct))))
        args.append(glu_p)
        out_cols = 2 * ct
    ocol = 2 * ct if glu else ct
    return pl.pallas_call(
        body, name=name, grid=(ctot // ct, nblk), in_specs=in_specs,
        out_specs=[pl.BlockSpec((t, ocol), lambda cb, i: (i, cb)), pl.BlockSpec((width, ct), lambda cb, i: (0, cb))],
        out_shape=[jax.ShapeDtypeStruct((s, out_cols), BF16), jax.ShapeDtypeStruct((width, ctot), F32)],
        scratch_shapes=[pltpu.VMEM((t + halo, ct), F32), pltpu.VMEM((halo + t, ct), F32), pltpu.VMEM((8 * width, ct), F32)],
        compiler_params=_params(("parallel", "arbitrary")),
    )(*args)


def _sgu_bwd(dhc, p, ln_g, ln_b, w_s, b_s_t, tag):
    s = p.shape[0]
    t = min(ROW_TILE, s)
    cs = SGU_CHUNK

    def body(dh_ref, uv_ref, g_ref, b_ref, ws_ref, bst_ref, duv_ref, dg_ref, db_ref, dws_ref, dbs_ref):
        _zero_at_first_step(dg_ref, db_ref, dws_ref, dbs_ref)
        uv = uv_ref[...].astype(F32)
        u_raw, v_raw = uv[:, :SGU_DIM], uv[:, SGU_DIM:]
        u = _gelu(u_raw)
        xhat, rstd = _ln_stats(_gelu(v_raw))
        vn = xhat * g_ref[...] + b_ref[...]
        mixed = _sgu_mix(vn, ws_ref, bst_ref, t)
        dh = dh_ref[...]
        duv_ref[:, :SGU_DIM] = (dh * mixed * _dgelu(u_raw)).astype(BF16)
        dmix = dh * u
        row = lax.broadcasted_iota(jnp.int32, (cs, cs), 0)
        col = lax.broadcasted_iota(jnp.int32, (cs, cs), 1)
        dbs = jnp.zeros((cs, LANES), F32)
        chunks = []
        for g in range(SGU_GROUPS):
            wg = jnp.where(row >= col, ws_ref[g], 0.0).astype(BF16)
            dwg = jnp.zeros((cs, cs), F32)
            parts = []
            for ci in range(t // cs):
                rs = slice(ci * cs, (ci + 1) * cs)
                cl = slice(g * SGU_GROUP_DIM, (g + 1) * SGU_GROUP_DIM)
                dm = dmix[rs, cl]
                dmb = dm.astype(BF16)
                dwg = dwg + _dot_nt(dmb, vn[rs, cl].astype(BF16))
                dbs = dbs + _lane_place(jnp.sum(dm, axis=-1, keepdims=True), g, (cs, LANES))
                parts.append(_dot_tn(wg, dmb))
            dws_ref[g] += jnp.where(row >= col, dwg, 0.0)
            chunks.append(jnp.concatenate(parts, axis=0))
        dbs_ref[...] += dbs
        dvn = jnp.concatenate(chunks, axis=1)
        dvv = _ln_bwd(dvn, xhat, rstd, g_ref[...])
        duv_ref[:, SGU_DIM:] = (dvv * _dgelu(v_raw)).astype(BF16)
        dg_ref[...] += _colsum(dvn * xhat)
        db_ref[...] += _colsum(dvn)

    vec = _const_spec((1, SGU_DIM))
    wss = _const_spec((SGU_GROUPS, cs, cs))
    return pl.pallas_call(
        body, name="sgu_bwd" + tag, grid=(s // t,),
        in_specs=[_row_spec(t, SGU_DIM), _row_spec(t, 2 * SGU_DIM, COL_UV // (2 * SGU_DIM)), vec, vec, wss, _const_spec((cs, LANES))],
        out_specs=[_row_spec(t, 2 * SGU_DIM), vec, vec, wss, _const_spec((cs, LANES))],
        out_shape=[jax.ShapeDtypeStruct((s, 2 * SGU_DIM), BF16), jax.ShapeDtypeStruct((1, SGU_DIM), F32),
                   jax.ShapeDtypeStruct((1, SGU_DIM), F32), jax.ShapeDtypeStruct((SGU_GROUPS, cs, cs), F32),
                   jax.ShapeDtypeStruct((cs, LANES), F32)],
        compiler_params=_params(("arbitrary",)),
    )(dhc, p, ln_g, ln_b, w_s, b_s_t)


def _reorder_proj_cols(w):
    pad = jnp.zeros(w.shape[:-1] + (P_COLS - PROJ_COLS,), w.dtype)
    return jnp.concatenate([w[..., :3072], w[..., 3080:PROJ_COLS], w[..., 3072:3080], pad], axis=-1)


def _restore_proj_cols(g):
    return jnp.concatenate([g[..., :3072], g[..., COL_BA:COL_BA + 8], g[..., 3072:COL_BA]], axis=-1)


def _pad_lanes(v, offset):
    return jnp.pad(v, (offset, LANES - offset - v.shape[0]))[None, :]


def _proj_weights(w_in_l):
    w_all = _reorder_proj_cols(w_in_l)
    return dict(w_all=w_all, w_ba=w_all[:, COL_BA:])


def _rest_weights(l, full, rep):
    row = lambda v: v[l][None, :]
    return dict(
        conv_w=full["conv_dw_w"], conv_b=row(rep["conv_dw_b"]), conv_ln_g=row(rep["conv_ln_g"]), conv_ln_b=row(rep["conv_ln_b"]),
        w_pa=full["w_pa"], w_pb=full["w_pb"], w_pc=full["w_pc"],
        gdn_cw=jnp.concatenate([full["gdn_conv_q"], full["gdn_conv_k"], full["gdn_conv_v"]], axis=-1),
        a_log=_pad_lanes(rep["gdn_a_log"][l], GDN_HEADS), dt_bias=_pad_lanes(rep["gdn_dt_bias"][l], GDN_HEADS),
        norm_g=row(rep["gdn_norm_g"]),
        sgu_ln_g=row(rep["sgu_ln_g"]), sgu_ln_b=row(rep["sgu_ln_b"]), sgu_w_s=rep["sgu_w_s"][l],
        sgu_b_s_t=jnp.pad(rep["sgu_b_s"][l].T, ((0, 0), (0, LANES - SGU_GROUPS))),
        b_gate=row(rep["b_gate"]),
        w_o=full["w_o"], ln1_g=row(rep["ln1_g"]), ln1_b=row(rep["ln1_b"]),
        w_ff1=full["w_ff1"], b_ff1=row(rep["b_ff1"]),
        w_ff2=full["w_ff2"], b_ff2=row(rep["b_ff2"]),
        ln2_g=row(rep["ln2_g"]), ln2_b=row(rep["ln2_b"]),
    )


def _layer_fwd(x, x_bf, w_proj, rest_of, tag, carry=None):
    carry = carry or {}
    got = {}
    s = x.shape[0]
    p = _matmul(x_bf, w_proj["w_all"], name="proj_fwd" + tag, tm=min(1024, s), tn=P_TILE, tk=D_MODEL, out_dtype=BF16,
                exchange=carry.get("proj_fwd"))
    if "proj_fwd" in carry:
        p, got["proj_fwd"] = p
    p_ba = _matmul(x_bf, w_proj["w_ba"], name="proj_ba_fwd" + tag, tm=min(2048, s), tn=LANES, tk=D_MODEL)
    w = dict(w_proj, **rest_of(got.get("proj_fwd")))
    h_glu, c, ha = _conv_a_fwd(p, w["conv_w"], w["conv_b"], w["conv_ln_g"], w["conv_ln_b"], tag)
    o, hb, states, minv = _gdn_fwd(p, p_ba, w["gdn_cw"], w["a_log"], w["dt_bias"], w["norm_g"], tag)
    hc = _sgu_fwd(p, w["sgu_ln_g"], w["sgu_ln_b"], w["sgu_w_s"], w["sgu_b_s_t"], tag)
    y3, merged = _merge_fwd(p, ha, hb, hc, w["w_pa"], w["w_pb"], w["w_pc"], w["b_gate"], tag)
    r1, x1, x1_bf, _ = _matmul_res_ln(merged, w["w_o"], jnp.zeros((1, D_MODEL), F32), x, w["ln1_g"], w["ln1_b"], name="o_res_ln" + tag)
    h_bf, got["ff1_fwd"] = _ff1_fwd(x1_bf, w["w_ff1"], w["b_ff1"], tag, exchange=carry.get("ff1_fwd"))
    r2, x2, x2_bf, got["ff2_res_ln"] = _matmul_res_ln(h_bf, w["w_ff2"], w["b_ff2"], x1, w["ln2_g"], w["ln2_b"],
                                                      name="ff2_res_ln" + tag, exchange=carry.get("ff2_res_ln"))
    saved = dict(x=x, x_bf=x_bf, p=p, p_ba=p_ba, h_glu=h_glu, c=c, ha=ha, o=o, hb=hb, states=states, minv=minv, hc=hc, y3=y3,
                 merged=merged, r1=r1, x1=x1, x1_bf=x1_bf, h_bf=h_bf, r2=r2)
    return x2, x2_bf, saved, w, got


def _layer_bwd(dx2, w, sv, tag, carry=None):
    carry = carry or {}
    got = {}
    g = {}

    def behind(key, call):
        if key not in carry:
            return call(None)
        out, got[key] = call(carry[key](g))
        return out

    s = dx2.shape[0]
    ts = min(1024, s)
    p = sv["p"]
    dr2, dr2_bf, d_ln2_g, d_ln2_b, d_b_ff2 = _ln_bwd_call(dx2, sv["r2"], w["ln2_g"], name="ln2_bwd" + tag)
    dhpre_bf, d_b_ff1 = _ff2_bwd(dr2_bf, w["w_ff2"], sv["h_bf"], tag)
    g["w_ff2"] = behind("dw_ff2", lambda ex: _matmul_tn(sv["h_bf"], dr2_bf, name="dw_ff2" + tag, ka=D_FF, tka=1024, tn=1024,
                                                        ts=ts, exchange=ex))
    g["w_ff1"] = behind("dw_ff1", lambda ex: _matmul_tn(sv["x1_bf"], dhpre_bf, name="dw_ff1" + tag, ka=D_MODEL, tka=1024, tn=1024,
                                                        ts=ts, exchange=ex))
    dr1, dr1_bf, d_ln1_g, d_ln1_b = _ff1_bwd_ln(dhpre_bf, w["w_ff1"], dr2, sv["r1"], w["ln1_g"], tag)
    g["w_o"] = behind("dw_o", lambda ex: _matmul_tn(sv["merged"], dr1_bf, name="dw_o" + tag, ka=D_MODEL, tka=1024, tn=1024,
                                                    ts=ts, exchange=ex))
    dgate_bf, dy3_bf, dha, dhb, dhc, d_b_gate = _merge_bwd(dr1_bf, w["w_o"], sv["y3"], p, w["b_gate"],
                                                          w["w_pa"], w["w_pb"], w["w_pc"], tag)
    for n, (name, h) in enumerate((("w_pa", sv["ha"]), ("w_pb", sv["hb"]), ("w_pc", sv["hc"]))):
        g[name] = _matmul_tn(h, dy3_bf, name=f"dw_p{n}" + tag, ka=CONV_DIM, tka=CONV_DIM, tn=1024, ts=ts, n=D_MODEL, b_col_block=n)
    dc, d_conv_ln_g, d_conv_ln_b, d_conv_b = _conv_a_bwd_pre(dha, sv["c"], w["conv_ln_g"], w["conv_ln_b"], tag)
    da_bf, d_conv_w = _dwconv_bwd(dc, sv["h_glu"], w["conv_w"], width=CONV_WIDTH, halo=CONV_HALO, x_col_block=0,
                                  glu_p=p, name="conv_a_bwd" + tag)
    duv_bf, d_sgu_ln_g, d_sgu_ln_b, d_sgu_w_s, d_sgu_b_s_t = _sgu_bwd(dhc, p, w["sgu_ln_g"], w["sgu_ln_b"], w["sgu_w_s"], w["sgu_b_s_t"], tag)
    dpre, dz_bf, dba_bf, d_norm_g, d_a_log, d_dt_bias = _gdn_bwd(dhb, p, sv["p_ba"], sv["o"], sv["states"], sv["minv"], w["gdn_cw"],
                                                                 w["a_log"], w["dt_bias"], w["norm_g"], tag)
    dqkv_bf, d_gdn_cw = _dwconv_bwd(dpre, p, w["gdn_cw"], width=GDN_CONV, halo=GDN_HALO, x_col_block=COL_Q // CONV_DIM,
                                    glu_p=None, name="gdn_conv_bwd" + tag)
    dp_bf = jnp.concatenate([da_bf, dqkv_bf, dz_bf, duv_bf, dgate_bf, dba_bf], axis=1)
    d_w_all = behind("dw_proj", lambda ex: _matmul_tn(sv["x_bf"], dp_bf, name="dw_proj" + tag, ka=D_MODEL, tka=1024, tn=P_TILE,
                                                      ts=ts, exchange=ex))
    g["w_in"] = _restore_proj_cols(d_w_all)
    dx = behind("proj_bwd", lambda ex: _matmul(dp_bf, w["w_all"], name="proj_bwd" + tag, tm=min(1024, s), tn=D_MODEL, tk=P_TILE,
                                               add=dr1, add_scale=DN_ALPHA, w_is_nk=True, exchange=ex))
    g.update(
        b_gate=d_b_gate[0], conv_dw_w=d_conv_w, conv_dw_b=d_conv_b[0], conv_ln_g=d_conv_ln_g[0], conv_ln_b=d_conv_ln_b[0],
        gdn_conv_q=d_gdn_cw[:, :GDN_QK], gdn_conv_k=d_gdn_cw[:, GDN_QK:2 * GDN_QK], gdn_conv_v=d_gdn_cw[:, 2 * GDN_QK:],
        gdn_a_log=d_a_log[0, GDN_HEADS:2 * GDN_HEADS], gdn_dt_bias=d_dt_bias[0, GDN_HEADS:2 * GDN_HEADS], gdn_norm_g=d_norm_g[0],
        sgu_ln_g=d_sgu_ln_g[0], sgu_ln_b=d_sgu_ln_b[0], sgu_w_s=d_sgu_w_s, sgu_b_s=d_sgu_b_s_t[:, :SGU_GROUPS].T,
        ln1_g=d_ln1_g[0], ln1_b=d_ln1_b[0], b_ff1=d_b_ff1[0], b_ff2=d_b_ff2[0], ln2_g=d_ln2_g[0], ln2_b=d_ln2_b[0],
    )
    return dx, g, got


MESH_AXES = ("x", "y", "c")


def _exchange(arrays, scatter, *, name):
    n = len(arrays)
    ex = _Exchange(arrays, scatter)

    def body(*refs):
        ins, outs, sems = refs[:n], refs[n:2 * n], refs[2 * n:]
        _exchange_start(ex, ins, outs, sems)
        _exchange_wait(ex, ins, outs, sems)

    return pl.pallas_call(
        body, name=name, in_specs=ex.in_specs(), out_specs=ex.in_specs(), out_shape=ex.out_shapes(),
        scratch_shapes=ex.scratch(),
    )(*arrays)


class _Exchange:
    def __init__(self, arrays, scatter):
        self.arrays = list(arrays)
        self.scatter = list(scatter)
        self.n = len(self.arrays)

    def in_specs(self):
        return [pl.BlockSpec(memory_space=pl.ANY)] * self.n

    def out_shapes(self):
        return [jax.ShapeDtypeStruct(a.shape if s else (N_DEV,) + a.shape, a.dtype) for a, s in zip(self.arrays, self.scatter)]

    def scratch(self):
        return [pltpu.SemaphoreType.DMA((self.n, N_DEV - 1)), pltpu.SemaphoreType.DMA((self.n, N_DEV - 1)),
                pltpu.SemaphoreType.DMA((self.n,))]


def _exchange_copies(ex, ins, outs, sems, with_arrivals):
    send_sems, recv_sems, local_sems = sems
    x, y, c = lax.axis_index("x"), lax.axis_index("y"), lax.axis_index("c")
    me = 4 * x + 2 * y + c

    def slot(a, d):
        return ins[a].at[d] if ex.scatter[a] else ins[a]

    local = [pltpu.make_async_copy(slot(a, me), outs[a].at[me], local_sems.at[a]) for a in range(ex.n)]
    remote = []
    for k in range(1, N_DEV):
        px = 1 - x if k & 4 else x
        py = 1 - y if k & 2 else y
        pc = 1 - c if k & 1 else c
        peer = 4 * px + 2 * py + pc
        for a in range(ex.n):
            send = pltpu.make_async_remote_copy(
                src_ref=slot(a, peer), dst_ref=outs[a].at[me], send_sem=send_sems.at[a, k - 1],
                recv_sem=recv_sems.at[a, k - 1], device_id=(px, py, pc), device_id_type=pl.DeviceIdType.MESH)
            arrival = pltpu.make_async_remote_copy(
                src_ref=slot(a, peer), dst_ref=outs[a].at[peer], send_sem=send_sems.at[a, k - 1],
                recv_sem=recv_sems.at[a, k - 1], device_id=(px, py, pc), device_id_type=pl.DeviceIdType.MESH) if with_arrivals else None
            remote.append((send, arrival))
    return local, remote


def _exchange_start(ex, ins, outs, sems):
    local, remote = _exchange_copies(ex, ins, outs, sems, False)
    for cp in local:
        cp.start()
    for send, _ in remote:
        send.start()


def _exchange_wait(ex, ins, outs, sems):
    local, remote = _exchange_copies(ex, ins, outs, sems, True)
    for _, arrival in remote:
        arrival.wait_recv()
    for send, _ in remote:
        send.wait_send()
    for cp in local:
        cp.wait()


def _adamw(w, m, v, g_parts, *, name):
    r, c = w.shape
    tr = 256 if r % 256 == 0 else r
    bc1 = 1.0 - ADAM_B1 ** ADAM_STEP
    bc2 = 1.0 - ADAM_B2 ** ADAM_STEP

    def body(w_ref, m_ref, v_ref, gp_ref, g_ref, d_ref, nm_ref, nv_ref):
        g = gp_ref[0].astype(F32)
        for d in range(1, N_DEV):
            g = g + gp_ref[d].astype(F32)
        nm = ADAM_B1 * m_ref[...] + (1.0 - ADAM_B1) * g
        nv = ADAM_B2 * v_ref[...] + (1.0 - ADAM_B2) * (g * g)
        g_ref[...] = g
        nm_ref[...] = nm
        nv_ref[...] = nv
        d_ref[...] = -ADAM_LR * ((nm / bc1) / (jnp.sqrt(nv / bc2) + ADAM_EPS) + ADAM_WD * w_ref[...])

    spec = pl.BlockSpec((tr, c), lambda i: (i, 0))
    return pl.pallas_call(
        body, name=name, grid=(r // tr,),
        in_specs=[spec, spec, spec, pl.BlockSpec((N_DEV, tr, c), lambda i: (0, i, 0))],
        out_specs=[spec] * 4, out_shape=[jax.ShapeDtypeStruct((r, c), F32)] * 4,
        compiler_params=_params(("parallel",)),
    )(w, m, v, g_parts)


SHARDED = dict(w_in=2, conv_dw_w=2, w_pa=2, gdn_conv_q=2, gdn_conv_k=2, gdn_conv_v=2, w_pb=2, w_pc=2, w_o=1, w_ff1=2, w_ff2=1)
WEIGHTS = ["ln_in_g", "ln_in_b", "w_in", "b_gate", "conv_dw_w", "conv_dw_b", "conv_ln_g", "conv_ln_b", "w_pa", "gdn_conv_q",
           "gdn_conv_k", "gdn_conv_v", "gdn_a_log", "gdn_dt_bias", "gdn_norm_g", "w_pb", "sgu_ln_g", "sgu_ln_b", "sgu_w_s",
           "sgu_b_s", "w_pc", "w_o", "ln1_g", "ln1_b", "w_ff1", "b_ff1", "w_ff2", "b_ff2", "ln2_g", "ln2_b"]
REPLICATED = [n for n in WEIGHTS if n not in SHARDED]
CONV_PACK = ["conv_dw_w", "gdn_conv_q", "gdn_conv_k", "gdn_conv_v"]
PROJ_PACK = ["w_pa", "w_pb", "w_pc"]


def _to_slots(full, axis):
    shp = full.shape
    split = full.reshape(shp[:axis] + (N_DEV, shp[axis] // N_DEV) + shp[axis + 1:])
    return jnp.moveaxis(split, axis, 0)


def _from_slots(slots, axis):
    merged = jnp.moveaxis(slots, 0, axis)
    shp = merged.shape
    return merged.reshape(shp[:axis] + (shp[axis] * shp[axis + 1],) + shp[axis + 2:])


def _pack_rows(arrs):
    rows = []
    for a in arrs:
        flat = a.reshape(-1)
        pad = (-flat.shape[0]) % LANES
        rows.append(jnp.pad(flat, (0, pad)).reshape(-1, LANES))
    out = jnp.concatenate(rows, axis=0)
    return jnp.pad(out, ((0, (-out.shape[0]) % 8), (0, 0)))


def _unpack_rows(packed, shapes):
    out, r = [], 0
    for shp in shapes:
        size = math.prod(shp)
        nrows = -(-size // LANES)
        out.append(packed[r:r + nrows].reshape(-1)[:size].reshape(shp))
        r += nrows
    return out


def kernel(x, ln_in_g, ln_in_b, w_in, b_gate, conv_dw_w, conv_dw_b, conv_ln_g, conv_ln_b, w_pa, gdn_conv_q, gdn_conv_k, gdn_conv_v, gdn_a_log, gdn_dt_bias, gdn_norm_g, w_pb, sgu_ln_g, sgu_ln_b, sgu_w_s, sgu_b_s, w_pc, w_o, ln1_g, ln1_b, w_ff1, b_ff1, w_ff2, b_ff2, ln2_g, ln2_b, loss_target, m_ln_in_g, m_ln_in_b, m_w_in, m_b_gate, m_conv_dw_w, m_conv_dw_b, m_conv_ln_g, m_conv_ln_b, m_w_pa, m_gdn_conv_q, m_gdn_conv_k, m_gdn_conv_v, m_gdn_a_log, m_gdn_dt_bias, m_gdn_norm_g, m_w_pb, m_sgu_ln_g, m_sgu_ln_b, m_sgu_w_s, m_sgu_b_s, m_w_pc, m_w_o, m_ln1_g, m_ln1_b, m_w_ff1, m_b_ff1, m_w_ff2, m_b_ff2, m_ln2_g, m_ln2_b, v_ln_in_g, v_ln_in_b, v_w_in, v_b_gate, v_conv_dw_w, v_conv_dw_b, v_conv_ln_g, v_conv_ln_b, v_w_pa, v_gdn_conv_q, v_gdn_conv_k, v_gdn_conv_v, v_gdn_a_log, v_gdn_dt_bias, v_gdn_norm_g, v_w_pb, v_sgu_ln_g, v_sgu_ln_b, v_sgu_w_s, v_sgu_b_s, v_w_pc, v_w_o, v_ln1_g, v_ln1_b, v_w_ff1, v_b_ff1, v_w_ff2, v_b_ff2, v_ln2_g, v_ln2_b):
    args = locals()
    w = {n: args[n] for n in WEIGHTS}
    m = {n: args["m_" + n] for n in WEIGHTS}
    v = {n: args["v_" + n] for n in WEIGHTS}

    rep = {n: w[n] for n in REPLICATED}
    conv_local = jnp.concatenate([w[n] for n in CONV_PACK], axis=1)
    proj_local = jnp.stack([w[n] for n in PROJ_PACK], axis=1).astype(BF16)
    big = ["w_in", "w_o", "w_ff1", "w_ff2"]
    big_local = {n: w[n].astype(BF16) for n in big}
    rest_local = lambda l: [big_local[n][l] for n in big[1:]] + [proj_local[l], conv_local[l]]

    def rest_full(got):
        full_l = {n: _from_slots(g, SHARDED[n] - 1) for n, g in zip(big[1:], got[:3])}
        proj_full = _from_slots(got[3], 2)
        for i, n in enumerate(PROJ_PACK):
            full_l[n] = proj_full[i]
        conv_full = _from_slots(got[4], 1)
        tap0 = 0
        for n in CONV_PACK:
            taps = w[n].shape[1]
            full_l[n] = conv_full[tap0:tap0 + taps]
            tap0 += taps
        return full_l

    ln_g, ln_b = w["ln_in_g"][None, :], w["ln_in_b"][None, :]
    xs, xs_bf = _ln_in_fwd(x[0], ln_g, ln_b)
    (w_in0,) = _exchange([big_local["w_in"][0]], [False], name="gather_w_in_l0")
    gather = lambda arrays: _Exchange(arrays, [False] * len(arrays))
    xs, xs_bf, sv0, w0, got0 = _layer_fwd(
        xs, xs_bf, _proj_weights(_from_slots(w_in0, 1)), lambda got: _rest_weights(0, rest_full(got), rep), "_l0",
        carry={"proj_fwd": gather(rest_local(0)), "ff1_fwd": gather([big_local["w_in"][1], conv_local[1]])})
    xs, xs_bf, sv1, w1, _ = _layer_fwd(
        xs, xs_bf, _proj_weights(_from_slots(got0["ff1_fwd"][0], 1)),
        lambda got: _rest_weights(1, rest_full(list(got) + [got0["ff1_fwd"][1]]), rep), "_l1",
        carry={"proj_fwd": gather(rest_local(1)[:4])})
    d, loss_acc = _loss_fwd_bwd(xs, loss_target[0])
    loss = lax.psum(loss_acc[0, 0], MESH_AXES)

    def scatter_of(g, names):
        def slots(n):
            if n == "proj":
                return _to_slots(jnp.stack([g[q] for q in PROJ_PACK], axis=0).astype(BF16), 2)
            return _to_slots(g[n].astype(BF16), SHARDED[n] - 1)

        return _Exchange([slots(n) for n in names], [True] * len(names))

    d, grads1, _ = _layer_bwd(d, w1, sv1, "_l1")
    plan = {"dw_ff2": (1, ["w_ff1", "w_ff2"]), "dw_ff1": (1, ["w_in"]), "dw_o": (1, ["w_o", "proj"]),
            "dw_proj": (0, ["w_ff1", "w_ff2", "w_o", "proj"]), "proj_bwd": (0, ["w_in"])}
    d, grads0, got = _layer_bwd(
        d, w0, sv0, "_l0",
        carry={call: (lambda g, l=l, names=names: scatter_of(grads1 if l == 1 else g, names)) for call, (l, names) in plan.items()})
    layer_parts = {(l, n): a for call, (l, names) in plan.items() for n, a in zip(names, got[call])}
    dx, _, d_ln_in_g, d_ln_in_b, _ = _ln_bwd_call(d, x[0], ln_g, name="ln_in_bwd")
    grads = {k: jnp.stack([grads0[k], grads1[k]]) for k in grads0}
    grads["ln_in_g"] = d_ln_in_g[0]
    grads["ln_in_b"] = d_ln_in_b[0]
    conv_grad = jnp.concatenate([grads[n] for n in CONV_PACK], axis=1)
    small_parts = _exchange([_to_slots(conv_grad, 2), _pack_rows([grads[n] for n in REPLICATED])], [True, False],
                            name="exchange_small_grads")
    parts = [jnp.stack([layer_parts[(0, n)], layer_parts[(1, n)]], axis=1) for n in big + ["proj"]] + list(small_parts)

    def adam_sharded(g_parts, w_l, m_l, v_l, name):
        shp = w_l.shape
        two_d = lambda a: a.reshape(-1, shp[-1])
        outs = _adamw(two_d(w_l), two_d(m_l), two_d(v_l), g_parts.reshape(N_DEV, -1, shp[-1]), name=name)
        return [o.reshape(shp) for o in outs]

    res = {}
    for n, gp in zip(big, parts[:4]):
        res[n] = adam_sharded(gp, w[n], m[n], v[n], "adamw_" + n)
    proj_res = adam_sharded(parts[4], jnp.stack([w[n] for n in PROJ_PACK], axis=1), jnp.stack([m[n] for n in PROJ_PACK], axis=1),
                            jnp.stack([v[n] for n in PROJ_PACK], axis=1), "adamw_proj")
    for i, n in enumerate(PROJ_PACK):
        res[n] = [o[:, i] for o in proj_res]
    conv_res = adam_sharded(parts[5], conv_local, jnp.concatenate([m[n] for n in CONV_PACK], axis=1),
                            jnp.concatenate([v[n] for n in CONV_PACK], axis=1), "adamw_conv")
    tap0 = 0
    for n in CONV_PACK:
        taps = w[n].shape[1]
        res[n] = [o[:, tap0:tap0 + taps] for o in conv_res]
        tap0 += taps
    rep_shapes = [w[n].shape for n in REPLICATED]
    rep_res = _adamw(_pack_rows([w[n] for n in REPLICATED]), _pack_rows([m[n] for n in REPLICATED]),
                     _pack_rows([v[n] for n in REPLICATED]), parts[6], name="adamw_replicated")
    rep_res = [_unpack_rows(o, rep_shapes) for o in rep_res]
    for i, n in enumerate(REPLICATED):
        res[n] = [o[i] for o in rep_res]

    outs = [loss, dx[None]]
    for j in range(4):
        outs += [res[n][j] for n in WEIGHTS]
    return tuple(outs)
```

```python
import functools
import math

import jax
import jax.numpy as jnp
from jax import lax
from jax.experimental import pallas as pl
from jax.experimental.pallas import tpu as pltpu

F32 = jnp.float32
BF16 = jnp.bfloat16

N_DEV = 8
DEPTH = 2
D_MODEL = 1024
CONV_DIM = 512
CONV_WIDTH = 31
GDN_HEADS = 4
GDN_DK = 128
GDN_QK = 512
GDN_CONV = 4
GDN_CHUNK = 64
SGU_GROUPS = 4
SGU_GROUP_DIM = 128
SGU_DIM = 512
SGU_CHUNK = 128
D_FF = 4096
DN_ALPHA = (2 * DEPTH) ** 0.25
LN_EPS = 1e-5
RMS_EPS = 1e-6
PROJ_COLS = 7176
SHARD_COLS = PROJ_COLS // N_DEV

COL_A = 0
COL_Q = 1024
COL_Z = 2560
COL_UV = 3072
COL_GATE = 4096
COL_BA = 7168
P_COLS = 7296
P_TILE = 2432

ADAM_LR = 0.001
ADAM_B1 = 0.9
ADAM_B2 = 0.999
ADAM_EPS = 1e-08
ADAM_WD = 0.01
ADAM_STEP = 10

VMEM_LIMIT_BYTES = 56 * 1024 * 1024
LANES = 128
ROW_TILE = 512
GDN_BLOCK = 256
CONV_HALO = 32
GDN_HALO = 16


def _params(sem):
    return pltpu.CompilerParams(dimension_semantics=sem, vmem_limit_bytes=VMEM_LIMIT_BYTES)


def _dot(a, b):
    return jnp.dot(a, b, preferred_element_type=F32)


def _dot_nt(a, b):
    return lax.dot_general(a, b, (((1,), (1,)), ((), ())), preferred_element_type=F32)


def _dot_tn(a, b):
    return lax.dot_general(a, b, (((0,), (0,)), ((), ())), preferred_element_type=F32)


def _split(a):
    hi = a.astype(BF16)
    lo = (a - hi.astype(F32)).astype(BF16)
    return hi, lo


def _dot3(a, b, dot=_dot):
    ah, al = _split(a)
    bh, bl = _split(b)
    return dot(ah, bh) + (dot(ah, bl) + dot(al, bh))


def _bdot(a, b, dot=_dot):
    return dot(a.astype(BF16), b.astype(BF16))


def _sigmoid(x):
    return jax.nn.sigmoid(x)


def _silu(x):
    return x * _sigmoid(x)


def _dsilu(x):
    s = _sigmoid(x)
    return s * (1.0 + x * (1.0 - s))


_GELU_C = math.sqrt(2.0 / math.pi)


def _gelu(x):
    return 0.5 * x * (1.0 + jnp.tanh(_GELU_C * (x + 0.044715 * (x * x * x))))


def _dgelu(x):
    t = jnp.tanh(_GELU_C * (x + 0.044715 * (x * x * x)))
    return 0.5 * (1.0 + t) + 0.5 * x * (1.0 - t * t) * (_GELU_C * (1.0 + 3.0 * 0.044715 * (x * x)))


def _ln_stats(x):
    mu = jnp.mean(x, axis=-1, keepdims=True)
    xc = x - mu
    var = jnp.mean(xc * xc, axis=-1, keepdims=True)
    rstd = lax.rsqrt(var + LN_EPS)
    return xc * rstd, rstd


def _ln_bwd(dy, xhat, rstd, g):
    dxh = dy * g
    return rstd * (dxh - jnp.mean(dxh, axis=-1, keepdims=True) - xhat * jnp.mean(dxh * xhat, axis=-1, keepdims=True))


def _colsum(x):
    return jnp.sum(x, axis=0, keepdims=True)


def _row_spec(t, cols, col_block=0):
    return pl.BlockSpec((t, cols), lambda i, cb=col_block: (i, cb))


def _const_spec(shape):
    nd = len(shape)
    return pl.BlockSpec(shape, lambda *_: (0,) * nd)


def _call(body, *, name, grid, in_specs, out_specs, out_shape, args, sem, scratch=(), exchange=None):
    n_in, n_out, n_sc = len(in_specs), len(out_specs), len(scratch)
    if exchange is None:
        def plain(*refs):
            body(refs[:n_in], refs[n_in:n_in + n_out], refs[n_in + n_out:])

        return pl.pallas_call(plain, name=name, grid=grid, in_specs=in_specs, out_specs=out_specs, out_shape=out_shape,
                              scratch_shapes=list(scratch), compiler_params=_params(sem))(*args)
    nex = exchange.n

    def carrying(*refs):
        ins, ex_ins = refs[:n_in], refs[n_in:n_in + nex]
        outs, ex_outs = refs[n_in + nex:n_in + nex + n_out], refs[n_in + nex + n_out:n_in + 2 * nex + n_out]
        sc, sems = refs[n_in + 2 * nex + n_out:n_in + 2 * nex + n_out + n_sc], refs[n_in + 2 * nex + n_out + n_sc:]
        ids = [pl.program_id(d) for d in range(len(grid))]
        first = functools.reduce(jnp.logical_and, [i == 0 for i in ids])
        last = functools.reduce(jnp.logical_and, [i == g - 1 for i, g in zip(ids, grid)])

        @pl.when(first)
        def _():
            _exchange_start(exchange, ex_ins, ex_outs, sems)

        body(ins, outs, sc)

        @pl.when(last)
        def _():
            _exchange_wait(exchange, ex_ins, ex_outs, sems)

    res = pl.pallas_call(
        carrying, name=name, grid=grid, in_specs=list(in_specs) + exchange.in_specs(),
        out_specs=list(out_specs) + exchange.in_specs(), out_shape=list(out_shape) + exchange.out_shapes(),
        scratch_shapes=list(scratch) + exchange.scratch(), compiler_params=_params(("arbitrary",) * len(grid)),
    )(*args, *exchange.arrays)
    return res[:n_out], res[n_out:]


def _matmul(a, w, *, name, tm, tn, tk, out_dtype=F32, a_col_block=0, add=None, add_scale=1.0, w_is_nk=False, exchange=None):
    m = a.shape[0]
    k, n = w.shape[::-1] if w_is_nk else w.shape
    nk = k // tk
    has_add = add is not None

    def body(ins, outs, scratch):
        a_ref, w_ref = ins[:2]
        o_ref, acc_ref = outs[0], scratch[0]
        if has_add:
            add_ref = ins[2]
        kk = pl.program_id(2)

        @pl.when(kk == 0)
        def _():
            acc_ref[...] = jnp.zeros_like(acc_ref)

        acc_ref[...] += (_dot_nt if w_is_nk else _dot)(a_ref[...], w_ref[...])

        @pl.when(kk == nk - 1)
        def _():
            r = acc_ref[...]
            if has_add:
                r = r + add_scale * add_ref[...]
            o_ref[...] = r.astype(out_dtype)

    in_specs = [pl.BlockSpec((tm, tk), lambda j, i, kk: (i, kk + a_col_block)),
                pl.BlockSpec((tn, tk), lambda j, i, kk: (j, kk)) if w_is_nk else pl.BlockSpec((tk, tn), lambda j, i, kk: (kk, j))]
    args = [a, w]
    if has_add:
        in_specs.append(pl.BlockSpec((tm, tn), lambda j, i, kk: (i, j)))
        args.append(add)
    res = _call(body, name=name, grid=(n // tn, m // tm, nk), in_specs=in_specs,
                out_specs=[pl.BlockSpec((tm, tn), lambda j, i, kk: (i, j))],
                out_shape=[jax.ShapeDtypeStruct((m, n), out_dtype)], scratch=[pltpu.VMEM((tm, tn), F32)],
                args=args, sem=("parallel", "parallel", "arbitrary"), exchange=exchange)
    return res[0] if exchange is None else (res[0][0], res[1])


def _matmul_tn(a, b, *, name, ka, tka, tn, ts, n=None, a_col_block=0, b_col_block=0, exchange=None):
    s = a.shape[0]
    n = b.shape[1] if n is None else n
    ns = s // ts

    def body(ins, outs, scratch):
        a_ref, b_ref = ins
        o_ref = outs[0]

        @pl.when(pl.program_id(2) == 0)
        def _():
            o_ref[...] = jnp.zeros_like(o_ref)

        o_ref[...] += _dot_tn(a_ref[...], b_ref[...])

    res = _call(body, name=name, grid=(ka // tka, n // tn, ns),
                in_specs=[pl.BlockSpec((ts, tka), lambda i, j, t: (t, i + a_col_block)),
                          pl.BlockSpec((ts, tn), lambda i, j, t: (t, j + b_col_block))],
                out_specs=[pl.BlockSpec((tka, tn), lambda i, j, t: (i, j))],
                out_shape=[jax.ShapeDtypeStruct((ka, n), F32)], args=[a, b],
                sem=("parallel", "parallel", "arbitrary"), exchange=exchange)
    return res[0] if exchange is None else (res[0][0], res[1])


def _ln_in_fwd(x, g, b):
    s = x.shape[0]
    t = min(ROW_TILE, s)

    def body(x_ref, g_ref, b_ref, y_ref, ybf_ref):
        xhat, _ = _ln_stats(x_ref[...])
        y = xhat * g_ref[...] + b_ref[...]
        y_ref[...] = y
        ybf_ref[...] = y.astype(BF16)

    return pl.pallas_call(
        body, name="ln_in_fwd", grid=(s // t,),
        in_specs=[_row_spec(t, D_MODEL), _const_spec((1, D_MODEL)), _const_spec((1, D_MODEL))],
        out_specs=[_row_spec(t, D_MODEL), _row_spec(t, D_MODEL)],
        out_shape=[jax.ShapeDtypeStruct((s, D_MODEL), F32), jax.ShapeDtypeStruct((s, D_MODEL), BF16)],
        compiler_params=_params(("parallel",)),
    )(x, g, b)


def _prev_halo_spec(t, halo, cols, col_block):
    per = t // halo
    return pl.BlockSpec((halo, cols), lambda i, cb=col_block: (jnp.maximum(i * per - 1, 0), cb))


def _next_halo_spec(t, halo, cols, col_block, n_blocks):
    per = t // halo
    last = n_blocks * per - 1
    return pl.BlockSpec((halo, cols), lambda i, cb=col_block: (jnp.minimum((i + 1) * per, last), cb))


CONV_ROWS = 32


def _tap_groups(offsets):
    by_shift = {}
    for j, off in enumerate(offsets):
        by_shift.setdefault(off % 8, []).append((j, off // 8))
    groups = []
    for shift, taps in sorted(by_shift.items()):
        first = min(a for _, a in taps)
        last = max(a for _, a in taps)
        groups.append((shift, 8 * first, 8 * (last - first), [(j, 8 * (a - first)) for j, a in taps]))
    return groups


def _tap_windows(ext, groups, r0):
    return [ext[r0 + first + shift:r0 + first + shift + CONV_ROWS + extra, :] for shift, first, extra, _ in groups]


def _tap_sum(ext, w_ref, groups, r0):
    acc = jnp.zeros((CONV_ROWS, ext.shape[1]), F32)
    for (_, _, _, taps), win in zip(groups, _tap_windows(ext, groups, r0)):
        for j, a in taps:
            acc = acc + w_ref[j:j + 1, :] * win[a:a + CONV_ROWS]
    return acc


def _conv_a_fwd(p, w, b, ln_g, ln_b, tag):
    s = p.shape[0]
    t = min(ROW_TILE, s)
    width = CONV_WIDTH

    def body(a_ref, halo_ref, w_ref, b_ref, g_ref, bb_ref, h_ref, c_ref, ha_ref, ext):
        i = pl.program_id(0)
        a = a_ref[...].astype(F32)
        h = a[:, :CONV_DIM] * _sigmoid(a[:, CONV_DIM:])
        ah = halo_ref[...].astype(F32)
        hh = ah[:, :CONV_DIM] * _sigmoid(ah[:, CONV_DIM:])
        ext[0:CONV_HALO, :] = jnp.where(i == 0, 0.0, hh)
        ext[CONV_HALO:CONV_HALO + t, :] = h
        h_ref[...] = h
        groups = _tap_groups([CONV_HALO - (width - 1) + j for j in range(width)])
        for r0 in range(0, t, CONV_ROWS):
            c = _tap_sum(ext, w_ref, groups, r0) + b_ref[...]
            xhat, _ = _ln_stats(c)
            n = xhat * g_ref[...] + bb_ref[...]
            c_ref[r0:r0 + CONV_ROWS, :] = c
            ha_ref[r0:r0 + CONV_ROWS, :] = _silu(n).astype(BF16)

    return pl.pallas_call(
        body, name="conv_a_fwd" + tag, grid=(s // t,),
        in_specs=[_row_spec(t, 2 * CONV_DIM, COL_A // (2 * CONV_DIM)),
                  _prev_halo_spec(t, CONV_HALO, 2 * CONV_DIM, COL_A // (2 * CONV_DIM)),
                  _const_spec((width, CONV_DIM)), _const_spec((1, CONV_DIM)),
                  _const_spec((1, CONV_DIM)), _const_spec((1, CONV_DIM))],
        out_specs=[_row_spec(t, CONV_DIM)] * 3,
        out_shape=[jax.ShapeDtypeStruct((s, CONV_DIM), F32), jax.ShapeDtypeStruct((s, CONV_DIM), F32),
                   jax.ShapeDtypeStruct((s, CONV_DIM), BF16)],
        scratch_shapes=[pltpu.VMEM((CONV_HALO + t, CONV_DIM), F32)],
        compiler_params=_params(("parallel",)),
    )(p, p, w, b, ln_g, ln_b)


def _sgu_mix(vn, wt_ref, bst_ref, t):
    row = lax.broadcasted_iota(jnp.int32, (SGU_CHUNK, SGU_CHUNK), 0)
    col = lax.broadcasted_iota(jnp.int32, (SGU_CHUNK, SGU_CHUNK), 1)
    chunks = []
    for ci in range(t // SGU_CHUNK):
        groups = []
        for g in range(SGU_GROUPS):
            wg = jnp.where(row >= col, wt_ref[g], 0.0).astype(BF16)
            v_cg = vn[ci * SGU_CHUNK:(ci + 1) * SGU_CHUNK, g * SGU_GROUP_DIM:(g + 1) * SGU_GROUP_DIM]
            groups.append(_dot(wg, v_cg.astype(BF16)) + bst_ref[:, g:g + 1])
        chunks.append(jnp.concatenate(groups, axis=1))
    return jnp.concatenate(chunks, axis=0)


def _sgu_fwd(p, ln_g, ln_b, w_s, b_s_t, tag):
    s = p.shape[0]
    t = min(ROW_TILE, s)

    def body(uv_ref, g_ref, b_ref, ws_ref, bst_ref, hc_ref):
        uv = uv_ref[...].astype(F32)
        u = _gelu(uv[:, :SGU_DIM])
        vv = _gelu(uv[:, SGU_DIM:])
        xhat, _ = _ln_stats(vv)
        vn = xhat * g_ref[...] + b_ref[...]
        mixed = _sgu_mix(vn, ws_ref, bst_ref, t)
        hc_ref[...] = (u * mixed).astype(BF16)

    return pl.pallas_call(
        body, name="sgu_fwd" + tag, grid=(s // t,),
        in_specs=[_row_spec(t, 2 * SGU_DIM, COL_UV // (2 * SGU_DIM)),
                  _const_spec((1, SGU_DIM)), _const_spec((1, SGU_DIM)),
                  _const_spec((SGU_GROUPS, SGU_CHUNK, SGU_CHUNK)), _const_spec((SGU_CHUNK, LANES))],
        out_specs=_row_spec(t, SGU_DIM),
        out_shape=jax.ShapeDtypeStruct((s, SGU_DIM), BF16),
        compiler_params=_params(("parallel",)),
    )(p, ln_g, ln_b, w_s, b_s_t)


def _merge_fwd(p, ha, hb, hc, w_pa, w_pb, w_pc, b_gate, tag):
    s = p.shape[0]
    t = min(ROW_TILE, s)
    gb = COL_GATE // D_MODEL

    def body(ga_ref, gb_ref, gc_ref, ha_ref, hb_ref, hc_ref, wa_ref, wb_ref, wc_ref, bg_ref, y_ref, m_ref):
        merged = jnp.zeros((t, D_MODEL), F32)
        for idx, (g_ref, h_ref, w_ref) in enumerate(((ga_ref, ha_ref, wa_ref), (gb_ref, hb_ref, wb_ref), (gc_ref, hc_ref, wc_ref))):
            y = _dot(h_ref[...], w_ref[...])
            sg = _sigmoid(g_ref[...].astype(F32) + bg_ref[:, idx * D_MODEL:(idx + 1) * D_MODEL])
            y_ref[:, idx * D_MODEL:(idx + 1) * D_MODEL] = y.astype(BF16)
            merged = merged + sg * y
        m_ref[...] = merged.astype(BF16)

    hspec = _row_spec(t, CONV_DIM)
    wspec = _const_spec((CONV_DIM, D_MODEL))
    return pl.pallas_call(
        body, name="merge_fwd" + tag, grid=(s // t,),
        in_specs=[_row_spec(t, D_MODEL, gb), _row_spec(t, D_MODEL, gb + 1), _row_spec(t, D_MODEL, gb + 2),
                  hspec, hspec, hspec, wspec, wspec, wspec, _const_spec((1, 3 * D_MODEL))],
        out_specs=[_row_spec(t, 3 * D_MODEL), _row_spec(t, D_MODEL)],
        out_shape=[jax.ShapeDtypeStruct((s, 3 * D_MODEL), BF16), jax.ShapeDtypeStruct((s, D_MODEL), BF16)],
        compiler_params=_params(("parallel",)),
    )(p, p, p, ha, hb, hc, w_pa, w_pb, w_pc, b_gate)


def _matmul_res_ln(a, w, bias, x_res, ln_g, ln_b, *, name, exchange=None):
    s, k = a.shape
    t = min(ROW_TILE, s)

    def body(ins, outs, scratch):
        a_ref, w_ref, bias_ref, x_ref, g_ref, b_ref = ins
        r_ref, y_ref, ybf_ref = outs
        r = DN_ALPHA * x_ref[...] + _dot(a_ref[...], w_ref[...]) + bias_ref[...]
        xhat, _ = _ln_stats(r)
        y = xhat * g_ref[...] + b_ref[...]
        r_ref[...] = r
        y_ref[...] = y
        ybf_ref[...] = y.astype(BF16)

    vec = _const_spec((1, D_MODEL))
    res = _call(
        body, name=name, grid=(s // t,),
        in_specs=[_row_spec(t, k), _const_spec((k, D_MODEL)), vec, _row_spec(t, D_MODEL), vec, vec],
        out_specs=[_row_spec(t, D_MODEL)] * 3,
        out_shape=[jax.ShapeDtypeStruct((s, D_MODEL), F32), jax.ShapeDtypeStruct((s, D_MODEL), F32),
                   jax.ShapeDtypeStruct((s, D_MODEL), BF16)],
        args=[a, w, bias, x_res, ln_g, ln_b], sem=("parallel",), exchange=exchange)
    return (*res, None) if exchange is None else (*res[0], res[1])


def _ff1_fwd(x_bf, w, b, tag, exchange=None):
    s = x_bf.shape[0]
    tm = min(1024, s)
    tn = 1024

    def body(ins, outs, scratch):
        x_ref, w_ref, b_ref = ins
        hp_ref, h_ref = outs
        hp = _dot(x_ref[...], w_ref[...]) + b_ref[...]
        hp_ref[...] = hp.astype(BF16)
        r = jnp.maximum(hp, 0.0)
        h_ref[...] = (r * r).astype(BF16)

    res = _call(
        body, name="ff1_fwd" + tag, grid=(s // tm, D_FF // tn),
        in_specs=[pl.BlockSpec((tm, D_MODEL), lambda i, j: (i, 0)), pl.BlockSpec((D_MODEL, tn), lambda i, j: (0, j)),
                  pl.BlockSpec((1, tn), lambda i, j: (0, j))],
        out_specs=[pl.BlockSpec((tm, tn), lambda i, j: (i, j))] * 2,
        out_shape=[jax.ShapeDtypeStruct((s, D_FF), BF16), jax.ShapeDtypeStruct((s, D_FF), BF16)],
        args=[x_bf, w, b], sem=("parallel", "parallel"), exchange=exchange)
    return (res[0], res[1], None) if exchange is None else (res[0][0], res[0][1], res[1])


def _loss_fwd_bwd(y, target):
    s = y.shape[0]
    t = min(ROW_TILE, s)

    def body(y_ref, t_ref, dy_ref, loss_ref):
        @pl.when(pl.program_id(0) == 0)
        def _():
            loss_ref[...] = jnp.zeros_like(loss_ref)

        err = y_ref[...] - t_ref[...]
        dy_ref[...] = err * (1.0 / D_MODEL)
        per_row = jnp.mean(err * err, axis=-1, keepdims=True)
        loss_ref[...] += 0.5 * jnp.sum(per_row, axis=0, keepdims=True)

    return pl.pallas_call(
        body, name="loss_fwd_bwd", grid=(s // t,),
        in_specs=[_row_spec(t, D_MODEL), _row_spec(t, D_MODEL)],
        out_specs=[_row_spec(t, D_MODEL), _const_spec((8, LANES))],
        out_shape=[jax.ShapeDtypeStruct((s, D_MODEL), F32), jax.ShapeDtypeStruct((8, LANES), F32)],
        compiler_params=_params(("arbitrary",)),
    )(y, target)


def _softplus(x):
    return jnp.maximum(x, 0.0) + jnp.log1p(jnp.exp(-jnp.abs(x)))


def _gdn_conv_silu_norm(q_ref, k_ref, v_ref, hq_ref, hk_ref, hv_ref, cw_ref, ext, first):
    t = q_ref.shape[0]
    for n, (r, h) in enumerate(((q_ref, hq_ref), (k_ref, hk_ref), (v_ref, hv_ref))):
        ext[0:GDN_HALO, n * GDN_QK:(n + 1) * GDN_QK] = jnp.where(first, 0.0, h[...].astype(F32))
        ext[GDN_HALO:GDN_HALO + t, n * GDN_QK:(n + 1) * GDN_QK] = r[...].astype(F32)
    pre = jnp.zeros((t, 3 * GDN_QK), F32)
    for j in range(GDN_CONV):
        pre = pre + cw_ref[j:j + 1, :] * ext[pl.ds(GDN_HALO - (GDN_CONV - 1) + j, t), :]
    act = _silu(pre)
    rq, rk = [], []
    for h in range(GDN_HEADS):
        qh = act[:, h * GDN_DK:(h + 1) * GDN_DK]
        kh = act[:, GDN_QK + h * GDN_DK:GDN_QK + (h + 1) * GDN_DK]
        rq.append(lax.rsqrt(jnp.sum(qh * qh, axis=-1, keepdims=True) + RMS_EPS))
        rk.append(lax.rsqrt(jnp.sum(kh * kh, axis=-1, keepdims=True) + RMS_EPS))
    return pre, act, rq, rk


def _gdn_gates(ba, alog_ref, dtb_ref):
    lane = lax.broadcasted_iota(jnp.int32, ba.shape, 1)
    beta = _sigmoid(ba)
    g = -jnp.exp(alog_ref[...]) * _softplus(ba + dtb_ref[...])
    g = jnp.where((lane >= GDN_HEADS) & (lane < 2 * GDN_HEADS), g, 0.0)
    return beta, g


def _chunk_cumsum_matrix(t, upper):
    row = lax.broadcasted_iota(jnp.int32, (t, t), 0)
    col = lax.broadcasted_iota(jnp.int32, (t, t), 1)
    same = (row // GDN_CHUNK) == (col // GDN_CHUNK)
    tri = (col >= row) if upper else (col <= row)
    return jnp.where(same & tri, 1.0, 0.0).astype(F32)


def _each(fn, *lists):
    return [fn(*a) for a in zip(*lists)]


def _gdn_pair_items(qn_s, kn_s, vc_s, beta_s, gam_s, nc):
    items = []
    for ci in range(nc):
        rows = slice(ci * GDN_CHUNK, (ci + 1) * GDN_CHUNK)
        gam_blk = gam_s[rows, :]
        gam_t = gam_blk.T
        beta_blk = beta_s[rows, :]
        for h in range(GDN_HEADS):
            sl = slice(h * GDN_DK, (h + 1) * GDN_DK)
            items.append((qn_s[rows, sl], kn_s[rows, sl], vc_s[rows, sl], beta_blk[:, h:h + 1],
                          gam_blk[:, GDN_HEADS + h:GDN_HEADS + h + 1], gam_t[GDN_HEADS + h:GDN_HEADS + h + 1, :]))
    return items


def _gdn_prep(items):
    c = GDN_CHUNK
    row = lax.broadcasted_iota(jnp.int32, (c, c), 0)
    col = lax.broadcasted_iota(jnp.int32, (c, c), 1)
    causal = row >= col
    strict = row > col
    kbs = [k.astype(BF16) for _, k, _, _, _, _ in items]
    kks = _each(_dot_nt, kbs, kbs)
    qks = _each(_dot_nt, [q.astype(BF16) for q, _, _, _, _, _ in items], kbs)
    out = []
    for (q, k, v, beta_c, gam_c, gam_r), kk, qk in zip(items, kks, qks):
        decay = jnp.where(causal, jnp.exp(jnp.where(causal, gam_c - gam_r, 0.0)), 0.0)
        gm = jnp.exp(gam_c)
        glast = gam_c[c - 1:c, :]
        elast = jnp.exp(glast - gam_c)
        out.append(dict(causal=causal, strict=strict, decay=decay, kk=kk, low=jnp.where(strict, beta_c * kk * decay, 0.0),
                        a_qk=qk * decay, gm=gm, glast_exp=jnp.exp(glast), elast=elast, q=q, k=k, v=v, beta_c=beta_c,
                        r=jnp.concatenate([beta_c * v, beta_c * k * gm], axis=1), qd=q * gm, kd=k * elast))
    return out


def _unit_lower_inverses_minus_identity(lows):
    ps = [-low for low in lows]
    mis = list(ps)
    for _ in range(5):
        ps = _each(_bdot, ps, ps)
        ts = _each(_bdot, mis, ps)
        mis = [mi + p + t for mi, p, t in zip(mis, ps, ts)]
    return mis


def _apply_inverses(mis, rs, dot=_dot):
    return [r + t for r, t in zip(rs, _each(functools.partial(_bdot, dot=dot), mis, rs))]


def _gdn_fwd(p, p_ba, conv_w, a_log, dt_bias, norm_g, tag):
    s = p.shape[0]
    t = min(GDN_BLOCK, s)
    nc = t // GDN_CHUNK
    nblk = s // t
    qb, kb_, vb, zb = COL_Q // GDN_QK, COL_Q // GDN_QK + 1, COL_Q // GDN_QK + 2, COL_Z // GDN_QK
    scale = GDN_DK ** -0.5

    def body(q_ref, k_ref, v_ref, hq_ref, hk_ref, hv_ref, z_ref, ba_ref, cw_ref, alog_ref, dtb_ref, ng_ref,
             o_ref, hb_ref, st_ref, m_ref, ext, qn_s, kn_s, vc_s, beta_s, gam_s, state):
        i = pl.program_id(0)

        @pl.when(i == 0)
        def _():
            state[...] = jnp.zeros_like(state)

        _, act, rq, rk = _gdn_conv_silu_norm(q_ref, k_ref, v_ref, hq_ref, hk_ref, hv_ref, cw_ref, ext, i == 0)
        for h in range(GDN_HEADS):
            sl = slice(h * GDN_DK, (h + 1) * GDN_DK)
            qn_s[:, sl] = act[:, sl] * (rq[h] * scale)
            kn_s[:, sl] = act[:, GDN_QK + h * GDN_DK:GDN_QK + (h + 1) * GDN_DK] * rk[h]
        vc_s[...] = act[:, 2 * GDN_QK:]
        beta, g = _gdn_gates(ba_ref[...], alog_ref, dtb_ref)
        beta_s[...] = beta
        gam_s[...] = jnp.dot(_chunk_cumsum_matrix(t, False), g, preferred_element_type=F32, precision=lax.Precision.HIGHEST)

        prs = _gdn_prep(_gdn_pair_items(qn_s, kn_s, vc_s, beta_s, gam_s, nc))
        mis = _unit_lower_inverses_minus_identity([pr["low"] for pr in prs])
        xs = _apply_inverses(mis, [pr["r"] for pr in prs])
        heads = range(GDN_HEADS)
        for ci in range(nc):
            rows = slice(ci * GDN_CHUNK, (ci + 1) * GDN_CHUNK)
            pc, xc = prs[ci * GDN_HEADS:(ci + 1) * GDN_HEADS], xs[ci * GDN_HEADS:(ci + 1) * GDN_HEADS]
            m_ref[rows, :] = jnp.concatenate(mis[ci * GDN_HEADS:(ci + 1) * GDN_HEADS], axis=1)
            sts = [state[h * GDN_DK:(h + 1) * GDN_DK, :] for h in heads]
            for h in heads:
                st_ref[(ci * GDN_HEADS + h) * GDN_DK:(ci * GDN_HEADS + h + 1) * GDN_DK, :] = sts[h]
            w_st = _each(_bdot, [x[:, GDN_DK:] for x in xc], sts)
            q_st = _each(_bdot, [pr["qd"] for pr in pc], sts)
            vns = [x[:, :GDN_DK] - ws for x, ws in zip(xc, w_st)]
            a_vn = _each(_bdot, [pr["a_qk"] for pr in pc], vns)
            k_vn = _each(functools.partial(_bdot, dot=_dot_tn), [pr["kd"] for pr in pc], vns)
            for h in heads:
                o_ref[rows, h * GDN_DK:(h + 1) * GDN_DK] = q_st[h] + a_vn[h]
                state[h * GDN_DK:(h + 1) * GDN_DK, :] = sts[h] * pc[h]["glast_exp"] + k_vn[h]

        z = z_ref[...].astype(F32)
        for h in range(GDN_HEADS):
            sl = slice(h * GDN_DK, (h + 1) * GDN_DK)
            o = o_ref[:, sl]
            on = o * lax.rsqrt(jnp.mean(o * o, axis=-1, keepdims=True) + RMS_EPS)
            hb_ref[:, sl] = (on * ng_ref[...] * _silu(z[:, sl])).astype(BF16)

    col = lambda cb: pl.BlockSpec((t, GDN_QK), lambda i, cb=cb: (i, cb))
    halo = lambda cb: _prev_halo_spec(t, GDN_HALO, GDN_QK, cb)
    vec = _const_spec((1, LANES))
    return pl.pallas_call(
        body, name="gdn_fwd" + tag, grid=(nblk,),
        in_specs=[col(qb), col(kb_), col(vb), halo(qb), halo(kb_), halo(vb), col(zb),
                  _row_spec(t, LANES), _const_spec((GDN_CONV, 3 * GDN_QK)), vec, vec, vec],
        out_specs=[_row_spec(t, GDN_QK), _row_spec(t, GDN_QK),
                   pl.BlockSpec((nc * GDN_HEADS * GDN_DK, GDN_DK), lambda i: (i, 0)),
                   _row_spec(t, GDN_HEADS * GDN_CHUNK)],
        out_shape=[jax.ShapeDtypeStruct((s, GDN_QK), F32), jax.ShapeDtypeStruct((s, GDN_QK), BF16),
                   jax.ShapeDtypeStruct((s // GDN_CHUNK * GDN_HEADS * GDN_DK, GDN_DK), F32),
                   jax.ShapeDtypeStruct((s, GDN_HEADS * GDN_CHUNK), F32)],
        scratch_shapes=[pltpu.VMEM((GDN_HALO + t, 3 * GDN_QK), F32), pltpu.VMEM((t, GDN_QK), F32),
                        pltpu.VMEM((t, GDN_QK), F32), pltpu.VMEM((t, GDN_QK), F32),
                        pltpu.VMEM((t, LANES), F32), pltpu.VMEM((t, LANES), F32),
                        pltpu.VMEM((GDN_HEADS * GDN_DK, GDN_DK), F32)],
        compiler_params=_params(("arbitrary",)),
    )(p, p, p, p, p, p, p, p_ba, conv_w, a_log, dt_bias, norm_g)


def _lane_place(col, lane_idx, shape):
    lane = lax.broadcasted_iota(jnp.int32, shape, 1)
    return jnp.where(lane == lane_idx, col, 0.0)


def _gdn_bwd(dhb, p, p_ba, o, states, minv, conv_w, a_log, dt_bias, norm_g, tag):
    s = p.shape[0]
    t = min(GDN_BLOCK, s)
    nc = t // GDN_CHUNK
    nblk = s // t
    qb, kb_, vb, zb = COL_Q // GDN_QK, COL_Q // GDN_QK + 1, COL_Q // GDN_QK + 2, COL_Z // GDN_QK
    scale = GDN_DK ** -0.5
    c = GDN_CHUNK

    def body(dhb_ref, q_ref, k_ref, v_ref, hq_ref, hk_ref, hv_ref, z_ref, ba_ref, o_ref, st_ref, m_ref,
             cw_ref, alog_ref, dtb_ref, ng_ref,
             dpre_ref, dz_ref, dba_ref, dng_ref, dalog_ref, ddtb_ref,
             ext, qn_s, kn_s, vc_s, beta_s, gam_s, do_s, dqn_s, dkn_s, dvc_s, dgam_s, dbeta_s, dstate):
        i = pl.program_id(0)

        @pl.when(i == 0)
        def _():
            dstate[...] = jnp.zeros_like(dstate)
            dng_ref[...] = jnp.zeros_like(dng_ref)
            dalog_ref[...] = jnp.zeros_like(dalog_ref)
            ddtb_ref[...] = jnp.zeros_like(ddtb_ref)

        pre, act, rq, rk = _gdn_conv_silu_norm(q_ref, k_ref, v_ref, hq_ref, hk_ref, hv_ref, cw_ref, ext, i == nblk - 1)
        for h in range(GDN_HEADS):
            sl = slice(h * GDN_DK, (h + 1) * GDN_DK)
            qn_s[:, sl] = act[:, sl] * (rq[h] * scale)
            kn_s[:, sl] = act[:, GDN_QK + h * GDN_DK:GDN_QK + (h + 1) * GDN_DK] * rk[h]
        vc_s[...] = act[:, 2 * GDN_QK:]
        ba = ba_ref[...]
        beta, g = _gdn_gates(ba, alog_ref, dtb_ref)
        beta_s[...] = beta
        gam_s[...] = jnp.dot(_chunk_cumsum_matrix(t, False), g, preferred_element_type=F32, precision=lax.Precision.HIGHEST)

        z = z_ref[...].astype(F32)
        dhb = dhb_ref[...]
        dng = jnp.zeros((1, GDN_DK), F32)
        for h in range(GDN_HEADS):
            sl = slice(h * GDN_DK, (h + 1) * GDN_DK)
            oh = o_ref[:, sl]
            r = lax.rsqrt(jnp.mean(oh * oh, axis=-1, keepdims=True) + RMS_EPS)
            on = oh * r
            sz = _silu(z[:, sl])
            dyh = dhb[:, sl]
            dng = dng + _colsum(dyh * on * sz)
            dz_ref[:, sl] = (dyh * on * ng_ref[...] * _dsilu(z[:, sl])).astype(BF16)
            don = dyh * ng_ref[...] * sz
            do_s[:, sl] = r * (don - on * jnp.mean(don * on, axis=-1, keepdims=True))
        dng_ref[...] += dng

        heads = range(GDN_HEADS)
        npairs = nc * GDN_HEADS
        tn = functools.partial(_bdot, dot=_dot_tn)
        nt = functools.partial(_bdot, dot=_dot_nt)
        rsum = lambda a: jnp.sum(a, axis=-1, keepdims=True)
        left = lambda a: a[:, :GDN_DK]
        right = lambda a: a[:, GDN_DK:]

        prs = _gdn_prep(_gdn_pair_items(qn_s, kn_s, vc_s, beta_s, gam_s, nc))
        mis = [m_ref[(n // GDN_HEADS) * c:(n // GDN_HEADS + 1) * c, (n % GDN_HEADS) * c:(n % GDN_HEADS + 1) * c] for n in range(npairs)]
        xs = _apply_inverses(mis, [pr["r"] for pr in prs])
        sts = [st_ref[n * GDN_DK:(n + 1) * GDN_DK, :] for n in range(npairs)]
        dos = [do_s[(n // GDN_HEADS) * c:(n // GDN_HEADS + 1) * c, (n % GDN_HEADS) * GDN_DK:(n % GDN_HEADS + 1) * GDN_DK] for n in range(npairs)]
        w_st = _each(_bdot, [right(x) for x in xs], sts)
        vns = [left(x) - ws for x, ws in zip(xs, w_st)]
        at_do = _each(tn, [pr["a_qk"] for pr in prs], dos)
        dqds = _each(nt, dos, sts)
        d_as = [jnp.where(pr["causal"], a, 0.0) for pr, a in zip(prs, _each(nt, dos, vns))]
        qt_do = _each(tn, [pr["qd"] for pr in prs], dos)

        dvns, dkds, ds_st = [None] * npairs, [None] * npairs, [None] * npairs
        for ci in reversed(range(nc)):
            ids = [ci * GDN_HEADS + h for h in heads]
            dss = [dstate[h * GDN_DK:(h + 1) * GDN_DK, :] for h in heads]
            kd_ds = _each(_bdot, [prs[n]["kd"] for n in ids], dss)
            vn_ds = _each(nt, [vns[n] for n in ids], dss)
            for h, n in enumerate(ids):
                dvns[n] = kd_ds[h] + at_do[n]
                dkds[n] = vn_ds[h]
                ds_st[n] = jnp.sum(rsum(dss[h] * sts[n]), axis=0, keepdims=True)
            wt_dvn = _each(tn, [right(xs[n]) for n in ids], [dvns[n] for n in ids])
            for h, n in enumerate(ids):
                dstate[h * GDN_DK:(h + 1) * GDN_DK, :] = dss[h] * prs[n]["glast_exp"] + qt_do[n] - wt_dvn[h]

        dws = [-a for a in _each(nt, dvns, sts)]
        d_rs = _apply_inverses(mis, [jnp.concatenate([dvn, dw], axis=1) for dvn, dw in zip(dvns, dws)], _dot_tn)
        d_ls = [jnp.where(pr["strict"], -a, 0.0) for pr, a in zip(prs, _each(nt, d_rs, xs))]
        d_l_kds = [d_l * pr["kk"] * pr["decay"] for d_l, pr in zip(d_ls, prs)]
        dkks = [d_l * pr["beta_c"] * pr["decay"] for d_l, pr in zip(d_ls, prs)]
        dqks = [d_a * pr["decay"] for d_a, pr in zip(d_as, prs)]
        ks, qs = [pr["k"] for pr in prs], [pr["q"] for pr in prs]
        dk1, dk2, dk3 = _each(_bdot, dkks, ks), _each(tn, dkks, ks), _each(tn, dqks, qs)
        dq1 = _each(_bdot, dqks, ks)
        rowi = lax.broadcasted_iota(jnp.int32, (c, 1), 0)
        for ci in range(nc):
            rows = slice(ci * c, (ci + 1) * c)
            dgam_blk = jnp.zeros((c, LANES), F32)
            dbeta_blk = jnp.zeros((c, LANES), F32)
            for h in heads:
                n = ci * GDN_HEADS + h
                sl = slice(h * GDN_DK, (h + 1) * GDN_DK)
                pr, d_r = prs[n], d_rs[n]
                d_ru, d_rw = left(d_r), right(d_r)
                gmat = pr["beta_c"] * d_l_kds[n] + d_as[n] * pr["a_qk"]
                dkd_kd = rsum(dkds[n] * pr["kd"])
                dgam = rsum(gmat) - rsum(gmat.T) + rsum(d_rw * right(pr["r"])) + rsum(dqds[n] * pr["qd"]) - dkd_kd
                dglast = jnp.sum(dkd_kd, axis=0, keepdims=True) + ds_st[n] * pr["glast_exp"]
                dgam = dgam + jnp.where(rowi == c - 1, dglast, 0.0)
                dbeta = rsum(d_l_kds[n]) + rsum(d_ru * pr["v"]) + rsum(d_rw * pr["k"]) * pr["gm"]
                dvc_s[rows, sl] = pr["beta_c"] * d_ru
                dkn_s[rows, sl] = dk1[n] + dk2[n] + dk3[n] + d_rw * (pr["beta_c"] * pr["gm"]) + dkds[n] * pr["elast"]
                dqn_s[rows, sl] = dq1[n] + dqds[n] * pr["gm"]
                dgam_blk = dgam_blk + _lane_place(dgam, GDN_HEADS + h, (c, LANES))
                dbeta_blk = dbeta_blk + _lane_place(dbeta, h, (c, LANES))
            dgam_s[rows, :] = dgam_blk
            dbeta_s[rows, :] = dbeta_blk

        dg = jnp.dot(_chunk_cumsum_matrix(t, True), dgam_s[...], preferred_element_type=F32, precision=lax.Precision.HIGHEST)
        lane = lax.broadcasted_iota(jnp.int32, (t, LANES), 1)
        g_lanes = (lane >= GDN_HEADS) & (lane < 2 * GDN_HEADS)
        da_logit = jnp.where(g_lanes, dg * (-jnp.exp(alog_ref[...])) * _sigmoid(ba + dtb_ref[...]), 0.0)
        db_logit = jnp.where(lane < GDN_HEADS, dbeta_s[...] * beta * (1.0 - beta), 0.0)
        dba_ref[...] = (da_logit + db_logit).astype(BF16)
        dalog_ref[...] += _colsum(dg * g)
        ddtb_ref[...] += _colsum(da_logit)

        dact = []
        for n, (dn_s, rr, sc) in enumerate(((dqn_s, rq, scale), (dkn_s, rk, 1.0))):
            for h in range(GDN_HEADS):
                sl = slice(h * GDN_DK, (h + 1) * GDN_DK)
                y = act[:, n * GDN_QK + h * GDN_DK:n * GDN_QK + (h + 1) * GDN_DK] * rr[h]
                dy = dn_s[:, sl] * sc
                dact.append(rr[h] * (dy - y * jnp.sum(dy * y, axis=-1, keepdims=True)))
        dact.append(dvc_s[...])
        dpre_ref[...] = jnp.concatenate(dact, axis=1) * _dsilu(pre)

    rb = lambda i: nblk - 1 - i
    per = t // GDN_HALO
    col = lambda cb, wd=GDN_QK: pl.BlockSpec((t, wd), lambda i, cb=cb: (rb(i), cb))
    halo = lambda cb: pl.BlockSpec((GDN_HALO, GDN_QK), lambda i, cb=cb: (jnp.maximum(rb(i) * per - 1, 0), cb))
    vec = _const_spec((1, LANES))
    return pl.pallas_call(
        body, name="gdn_bwd" + tag, grid=(nblk,),
        in_specs=[col(0), col(qb), col(kb_), col(vb), halo(qb), halo(kb_), halo(vb), col(zb),
                  pl.BlockSpec((t, LANES), lambda i: (rb(i), 0)), col(0),
                  pl.BlockSpec((nc * GDN_HEADS * GDN_DK, GDN_DK), lambda i: (rb(i), 0)),
                  pl.BlockSpec((t, GDN_HEADS * c), lambda i: (rb(i), 0)),
                  _const_spec((GDN_CONV, 3 * GDN_QK)), vec, vec, vec],
        out_specs=[pl.BlockSpec((t, 3 * GDN_QK), lambda i: (rb(i), 0)), col(0), pl.BlockSpec((t, LANES), lambda i: (rb(i), 0)),
                   vec, vec, vec],
        out_shape=[jax.ShapeDtypeStruct((s, 3 * GDN_QK), F32), jax.ShapeDtypeStruct((s, GDN_QK), BF16),
                   jax.ShapeDtypeStruct((s, LANES), BF16),
                   jax.ShapeDtypeStruct((1, LANES), F32), jax.ShapeDtypeStruct((1, LANES), F32), jax.ShapeDtypeStruct((1, LANES), F32)],
        scratch_shapes=[pltpu.VMEM((GDN_HALO + t, 3 * GDN_QK), F32)] + [pltpu.VMEM((t, GDN_QK), F32)] * 3
                       + [pltpu.VMEM((t, LANES), F32)] * 2 + [pltpu.VMEM((t, GDN_QK), F32)] * 4
                       + [pltpu.VMEM((t, LANES), F32)] * 2 + [pltpu.VMEM((GDN_HEADS * GDN_DK, GDN_DK), F32)],
        compiler_params=_params(("arbitrary",)),
    )(dhb, p, p, p, p, p, p, p, p_ba, o, states, minv, conv_w, a_log, dt_bias, norm_g)


def _zero_at_first_step(*refs):
    @pl.when(pl.program_id(0) == 0)
    def _():
        for r in refs:
            r[...] = jnp.zeros_like(r)


def _ln_bwd_call(dy, x_in, g, *, name):
    s, d = dy.shape
    t = min(ROW_TILE, s)

    def body(dy_ref, x_ref, g_ref, dx_ref, dxbf_ref, dg_ref, db_ref, ds_ref):
        _zero_at_first_step(dg_ref, db_ref, ds_ref)
        dy = dy_ref[...]
        xhat, rstd = _ln_stats(x_ref[...])
        dx = _ln_bwd(dy, xhat, rstd, g_ref[...])
        dx_ref[...] = dx
        dxbf_ref[...] = dx.astype(BF16)
        dg_ref[...] += _colsum(dy * xhat)
        db_ref[...] += _colsum(dy)
        ds_ref[...] += _colsum(dx)

    vec = _const_spec((1, d))
    return pl.pallas_call(
        body, name=name, grid=(s // t,),
        in_specs=[_row_spec(t, d), _row_spec(t, d), vec],
        out_specs=[_row_spec(t, d), _row_spec(t, d), vec, vec, vec],
        out_shape=[jax.ShapeDtypeStruct((s, d), F32), jax.ShapeDtypeStruct((s, d), BF16)] + [jax.ShapeDtypeStruct((1, d), F32)] * 3,
        compiler_params=_params(("arbitrary",)),
    )(dy, x_in, g)


def _ff2_bwd(dr2_bf, w_ff2, hpre, tag):
    s = dr2_bf.shape[0]
    tm = min(1024, s)
    tn = 1024

    def body(d_ref, w_ref, hp_ref, o_ref, db_ref):
        @pl.when(pl.program_id(1) == 0)
        def _():
            db_ref[...] = jnp.zeros_like(db_ref)

        dh = _dot_nt(d_ref[...], w_ref[...]) * (2.0 * jnp.maximum(hp_ref[...].astype(F32), 0.0))
        o_ref[...] = dh.astype(BF16)
        db_ref[...] += _colsum(dh)

    return pl.pallas_call(
        body, name="ff2_bwd" + tag, grid=(D_FF // tn, s // tm),
        in_specs=[pl.BlockSpec((tm, D_MODEL), lambda j, i: (i, 0)), pl.BlockSpec((tn, D_MODEL), lambda j, i: (j, 0)),
                  pl.BlockSpec((tm, tn), lambda j, i: (i, j))],
        out_specs=[pl.BlockSpec((tm, tn), lambda j, i: (i, j)), pl.BlockSpec((1, tn), lambda j, i: (0, j))],
        out_shape=[jax.ShapeDtypeStruct((s, D_FF), BF16), jax.ShapeDtypeStruct((1, D_FF), F32)],
        compiler_params=_params(("parallel", "arbitrary")),
    )(dr2_bf, w_ff2, hpre)


def _ff1_bwd_ln(dhpre_bf, w_ff1, dr2, r1, ln1_g, tag):
    s = dr2.shape[0]
    t = min(ROW_TILE, s)

    def body(dh_ref, w_ref, dr2_ref, r1_ref, g_ref, dr_ref, drbf_ref, dg_ref, db_ref):
        _zero_at_first_step(dg_ref, db_ref)
        dx1 = DN_ALPHA * dr2_ref[...] + _dot_nt(dh_ref[...], w_ref[...])
        xhat, rstd = _ln_stats(r1_ref[...])
        dr = _ln_bwd(dx1, xhat, rstd, g_ref[...])
        dr_ref[...] = dr
        drbf_ref[...] = dr.astype(BF16)
        dg_ref[...] += _colsum(dx1 * xhat)
        db_ref[...] += _colsum(dx1)

    vec = _const_spec((1, D_MODEL))
    return pl.pallas_call(
        body, name="ff1_bwd_ln" + tag, grid=(s // t,),
        in_specs=[_row_spec(t, D_FF), _const_spec((D_MODEL, D_FF)), _row_spec(t, D_MODEL), _row_spec(t, D_MODEL), vec],
        out_specs=[_row_spec(t, D_MODEL), _row_spec(t, D_MODEL), vec, vec],
        out_shape=[jax.ShapeDtypeStruct((s, D_MODEL), F32), jax.ShapeDtypeStruct((s, D_MODEL), BF16),
                   jax.ShapeDtypeStruct((1, D_MODEL), F32), jax.ShapeDtypeStruct((1, D_MODEL), F32)],
        compiler_params=_params(("arbitrary",)),
    )(dhpre_bf, w_ff1, dr2, r1, ln1_g)


def _merge_bwd(dr1_bf, w_o, y3, p, b_gate, w_pa, w_pb, w_pc, tag):
    s = p.shape[0]
    t = min(ROW_TILE, s)
    gb = COL_GATE // D_MODEL

    def body(dr_ref, wo_ref, y_ref, ga_ref, gb_ref, gc_ref, bg_ref, wa_ref, wb_ref, wc_ref,
             dgate_ref, dy_ref, dha_ref, dhb_ref, dhc_ref, dbg_ref):
        _zero_at_first_step(dbg_ref)
        dm = _dot_nt(dr_ref[...], wo_ref[...])
        for idx, (g_ref, w_ref, dh_ref) in enumerate(((ga_ref, wa_ref, dha_ref), (gb_ref, wb_ref, dhb_ref), (gc_ref, wc_ref, dhc_ref))):
            sl = slice(idx * D_MODEL, (idx + 1) * D_MODEL)
            sg = _sigmoid(g_ref[...].astype(F32) + bg_ref[:, sl])
            dgate = dm * y_ref[:, sl].astype(F32) * sg * (1.0 - sg)
            dgate_ref[:, sl] = dgate.astype(BF16)
            dbg_ref[:, sl] += _colsum(dgate)
            dy = (dm * sg).astype(BF16)
            dy_ref[:, sl] = dy
            dh_ref[...] = _dot_nt(dy, w_ref[...])

    hspec = _row_spec(t, CONV_DIM)
    wspec = _const_spec((CONV_DIM, D_MODEL))
    return pl.pallas_call(
        body, name="merge_bwd" + tag, grid=(s // t,),
        in_specs=[_row_spec(t, D_MODEL), _const_spec((D_MODEL, D_MODEL)), _row_spec(t, 3 * D_MODEL),
                  _row_spec(t, D_MODEL, gb), _row_spec(t, D_MODEL, gb + 1), _row_spec(t, D_MODEL, gb + 2),
                  _const_spec((1, 3 * D_MODEL)), wspec, wspec, wspec],
        out_specs=[_row_spec(t, 3 * D_MODEL), _row_spec(t, 3 * D_MODEL), hspec, hspec, hspec, _const_spec((1, 3 * D_MODEL))],
        out_shape=[jax.ShapeDtypeStruct((s, 3 * D_MODEL), BF16), jax.ShapeDtypeStruct((s, 3 * D_MODEL), BF16)]
                  + [jax.ShapeDtypeStruct((s, CONV_DIM), F32)] * 3 + [jax.ShapeDtypeStruct((1, 3 * D_MODEL), F32)],
        compiler_params=_params(("arbitrary",)),
    )(dr1_bf, w_o, y3, p, p, p, b_gate, w_pa, w_pb, w_pc)


def _conv_a_bwd_pre(dha, c, ln_g, ln_b, tag):
    s = c.shape[0]
    t = min(ROW_TILE, s)

    def body(dh_ref, c_ref, g_ref, b_ref, dc_ref, dg_ref, db_ref, ds_ref):
        _zero_at_first_step(dg_ref, db_ref, ds_ref)
        xhat, rstd = _ln_stats(c_ref[...])
        n = xhat * g_ref[...] + b_ref[...]
        dn = dh_ref[...] * _dsilu(n)
        dc = _ln_bwd(dn, xhat, rstd, g_ref[...])
        dc_ref[...] = dc
        dg_ref[...] += _colsum(dn * xhat)
        db_ref[...] += _colsum(dn)
        ds_ref[...] += _colsum(dc)

    vec = _const_spec((1, CONV_DIM))
    return pl.pallas_call(
        body, name="conv_a_bwd_pre" + tag, grid=(s // t,),
        in_specs=[_row_spec(t, CONV_DIM), _row_spec(t, CONV_DIM), vec, vec],
        out_specs=[_row_spec(t, CONV_DIM), vec, vec, vec],
        out_shape=[jax.ShapeDtypeStruct((s, CONV_DIM), F32)] + [jax.ShapeDtypeStruct((1, CONV_DIM), F32)] * 3,
        compiler_params=_params(("arbitrary",)),
    )(dha, c, ln_g, ln_b)


def _dwconv_bwd(dy, x, w, *, width, halo, x_col_block, glu_p, name):
    s, ctot = dy.shape
    ct = CONV_DIM
    t = min(ROW_TILE, s)
    nblk = s // t
    glu = glu_p is not None

    def body(*refs):
        if glu:
            dy_ref, dyh_ref, x_ref, xh_ref, w_ref, a_ref, dx_ref, dw_ref, dye, xe, dwacc = refs
        else:
            dy_ref, dyh_ref, x_ref, xh_ref, w_ref, dx_ref, dw_ref, dye, xe, dwacc = refs
        i = pl.program_id(1)

        @pl.when(i == 0)
        def _():
            dw_ref[...] = jnp.zeros_like(dw_ref)

        dye[0:t, :] = dy_ref[...]
        dye[t:t + halo, :] = jnp.where(i == nblk - 1, 0.0, dyh_ref[...])
        xe[0:halo, :] = jnp.where(i == 0, 0.0, xh_ref[...].astype(F32))
        xe[halo:halo + t, :] = x_ref[...].astype(F32)
        dwacc[...] = jnp.zeros_like(dwacc)
        dx_groups = _tap_groups([width - 1 - j for j in range(width)])
        dw_groups = _tap_groups([halo - (width - 1) + j for j in range(width)])
        for r0 in range(0, t, CONV_ROWS):
            rows = slice(r0, r0 + CONV_ROWS)
            dx = _tap_sum(dye, w_ref, dx_groups, r0)
            if glu:
                a = a_ref[rows, :].astype(F32)
                a1 = a[:, :ct]
                sg = _sigmoid(a[:, ct:])
                dx_ref[rows, :ct] = (dx * sg).astype(BF16)
                dx_ref[rows, ct:] = (dx * a1 * sg * (1.0 - sg)).astype(BF16)
            else:
                dx_ref[rows, :] = dx.astype(BF16)
            dyt = dy_ref[rows, :]
            for (_, _, _, taps), win in zip(dw_groups, _tap_windows(xe, dw_groups, r0)):
                for j, a in taps:
                    prod = dyt * win[a:a + CONV_ROWS]
                    part = prod[0:8]
                    for q in range(8, CONV_ROWS, 8):
                        part = part + prod[q:q + 8]
                    dwacc[8 * j:8 * j + 8, :] += part
        for j in range(width):
            dw_ref[j:j + 1, :] += _colsum(dwacc[8 * j:8 * j + 8, :])

    per = t // halo
    in_specs = [pl.BlockSpec((t, ct), lambda cb, i: (i, cb)),
                pl.BlockSpec((halo, ct), lambda cb, i: (jnp.minimum((i + 1) * per, nblk * per - 1), cb)),
                pl.BlockSpec((t, ct), lambda cb, i: (i, cb + x_col_block)),
                pl.BlockSpec((halo, ct), lambda cb, i: (jnp.maximum(i * per - 1, 0), cb + x_col_block)),
                pl.BlockSpec((width, ct), lambda cb, i: (0, cb))]
    args = [dy, dy, x, x, w]
    out_cols = ctot
    if glu:
        in_specs.append(pl.BlockSpec((t, 2 * ct), lambda cb, i: (i, COL_A // (2 * ct))))
        args.append(glu_p)
        out_cols = 2 * ct
    ocol = 2 * ct if glu else ct
    return pl.pallas_call(
        body, name=name, grid=(ctot // ct, nblk), in_specs=in_specs,
        out_specs=[pl.BlockSpec((t, ocol), lambda cb, i: (i, cb)), pl.BlockSpec((width, ct), lambda cb, i: (0, cb))],
        out_shape=[jax.ShapeDtypeStruct((s, out_cols), BF16), jax.ShapeDtypeStruct((width, ctot), F32)],
        scratch_shapes=[pltpu.VMEM((t + halo, ct), F32), pltpu.VMEM((halo + t, ct), F32), pltpu.VMEM((8 * width, ct), F32)],
        compiler_params=_params(("parallel", "arbitrary")),
    )(*args)


def _sgu_bwd(dhc, p, ln_g, ln_b, w_s, b_s_t, tag):
    s = p.shape[0]
    t = min(ROW_TILE, s)
    cs = SGU_CHUNK

    def body(dh_ref, uv_ref, g_ref, b_ref, ws_ref, bst_ref, duv_ref, dg_ref, db_ref, dws_ref, dbs_ref):
        _zero_at_first_step(dg_ref, db_ref, dws_ref, dbs_ref)
        uv = uv_ref[...].astype(F32)
        u_raw, v_raw = uv[:, :SGU_DIM], uv[:, SGU_DIM:]
        u = _gelu(u_raw)
        xhat, rstd = _ln_stats(_gelu(v_raw))
        vn = xhat * g_ref[...] + b_ref[...]
        mixed = _sgu_mix(vn, ws_ref, bst_ref, t)
        dh = dh_ref[...]
        duv_ref[:, :SGU_DIM] = (dh * mixed * _dgelu(u_raw)).astype(BF16)
        dmix = dh * u
        row = lax.broadcasted_iota(jnp.int32, (cs, cs), 0)
        col = lax.broadcasted_iota(jnp.int32, (cs, cs), 1)
        dbs = jnp.zeros((cs, LANES), F32)
        chunks = []
        for g in range(SGU_GROUPS):
            wg = jnp.where(row >= col, ws_ref[g], 0.0).astype(BF16)
            dwg = jnp.zeros((cs, cs), F32)
            parts = []
            for ci in range(t // cs):
                rs = slice(ci * cs, (ci + 1) * cs)
                cl = slice(g * SGU_GROUP_DIM, (g + 1) * SGU_GROUP_DIM)
                dm = dmix[rs, cl]
                dmb = dm.astype(BF16)
                dwg = dwg + _dot_nt(dmb, vn[rs, cl].astype(BF16))
                dbs = dbs + _lane_place(jnp.sum(dm, axis=-1, keepdims=True), g, (cs, LANES))
                parts.append(_dot_tn(wg, dmb))
            dws_ref[g] += jnp.where(row >= col, dwg, 0.0)
            chunks.append(jnp.concatenate(parts, axis=0))
        dbs_ref[...] += dbs
        dvn = jnp.concatenate(chunks, axis=1)
        dvv = _ln_bwd(dvn, xhat, rstd, g_ref[...])
        duv_ref[:, SGU_DIM:] = (dvv * _dgelu(v_raw)).astype(BF16)
        dg_ref[...] += _colsum(dvn * xhat)
        db_ref[...] += _colsum(dvn)

    vec = _const_spec((1, SGU_DIM))
    wss = _const_spec((SGU_GROUPS, cs, cs))
    return pl.pallas_call(
        body, name="sgu_bwd" + tag, grid=(s // t,),
        in_specs=[_row_spec(t, SGU_DIM), _row_spec(t, 2 * SGU_DIM, COL_UV // (2 * SGU_DIM)), vec, vec, wss, _const_spec((cs, LANES))],
        out_specs=[_row_spec(t, 2 * SGU_DIM), vec, vec, wss, _const_spec((cs, LANES))],
        out_shape=[jax.ShapeDtypeStruct((s, 2 * SGU_DIM), BF16), jax.ShapeDtypeStruct((1, SGU_DIM), F32),
                   jax.ShapeDtypeStruct((1, SGU_DIM), F32), jax.ShapeDtypeStruct((SGU_GROUPS, cs, cs), F32),
                   jax.ShapeDtypeStruct((cs, LANES), F32)],
        compiler_params=_params(("arbitrary",)),
    )(dhc, p, ln_g, ln_b, w_s, b_s_t)


def _reorder_proj_cols(w):
    pad = jnp.zeros(w.shape[:-1] + (P_COLS - PROJ_COLS,), w.dtype)
    return jnp.concatenate([w[..., :3072], w[..., 3080:PROJ_COLS], w[..., 3072:3080], pad], axis=-1)


def _restore_proj_cols(g):
    return jnp.concatenate([g[..., :3072], g[..., COL_BA:COL_BA + 8], g[..., 3072:COL_BA]], axis=-1)


def _pad_lanes(v, offset):
    return jnp.pad(v, (offset, LANES - offset - v.shape[0]))[None, :]


def _proj_weights(w_in_l):
    w_all = _reorder_proj_cols(w_in_l)
    return dict(w_all=w_all, w_ba=w_all[:, COL_BA:])


def _rest_weights(l, full, rep):
    row = lambda v: v[l][None, :]
    return dict(
        conv_w=full["conv_dw_w"], conv_b=row(rep["conv_dw_b"]), conv_ln_g=row(rep["conv_ln_g"]), conv_ln_b=row(rep["conv_ln_b"]),
        w_pa=full["w_pa"], w_pb=full["w_pb"], w_pc=full["w_pc"],
        gdn_cw=jnp.concatenate([full["gdn_conv_q"], full["gdn_conv_k"], full["gdn_conv_v"]], axis=-1),
        a_log=_pad_lanes(rep["gdn_a_log"][l], GDN_HEADS), dt_bias=_pad_lanes(rep["gdn_dt_bias"][l], GDN_HEADS),
        norm_g=row(rep["gdn_norm_g"]),
        sgu_ln_g=row(rep["sgu_ln_g"]), sgu_ln_b=row(rep["sgu_ln_b"]), sgu_w_s=rep["sgu_w_s"][l],
        sgu_b_s_t=jnp.pad(rep["sgu_b_s"][l].T, ((0, 0), (0, LANES - SGU_GROUPS))),
        b_gate=row(rep["b_gate"]),
        w_o=full["w_o"], ln1_g=row(rep["ln1_g"]), ln1_b=row(rep["ln1_b"]),
        w_ff1=full["w_ff1"], b_ff1=row(rep["b_ff1"]),
        w_ff2=full["w_ff2"], b_ff2=row(rep["b_ff2"]),
        ln2_g=row(rep["ln2_g"]), ln2_b=row(rep["ln2_b"]),
    )


def _layer_fwd(x, x_bf, w_proj, rest_of, tag, carry=None):
    carry = carry or {}
    got = {}
    s = x.shape[0]
    p = _matmul(x_bf, w_proj["w_all"], name="proj_fwd" + tag, tm=min(1024, s), tn=P_TILE, tk=D_MODEL, out_dtype=BF16,
                exchange=carry.get("proj_fwd"))
    if "proj_fwd" in carry:
        p, got["proj_fwd"] = p
    p_ba = _matmul(x_bf, w_proj["w_ba"], name="proj_ba_fwd" + tag, tm=min(2048, s), tn=LANES, tk=D_MODEL)
    w = dict(w_proj, **rest_of(got.get("proj_fwd")))
    h_glu, c, ha = _conv_a_fwd(p, w["conv_w"], w["conv_b"], w["conv_ln_g"], w["conv_ln_b"], tag)
    o, hb, states, minv = _gdn_fwd(p, p_ba, w["gdn_cw"], w["a_log"], w["dt_bias"], w["norm_g"], tag)
    hc = _sgu_fwd(p, w["sgu_ln_g"], w["sgu_ln_b"], w["sgu_w_s"], w["sgu_b_s_t"], tag)
    y3, merged = _merge_fwd(p, ha, hb, hc, w["w_pa"], w["w_pb"], w["w_pc"], w["b_gate"], tag)
    r1, x1, x1_bf, _ = _matmul_res_ln(merged, w["w_o"], jnp.zeros((1, D_MODEL), F32), x, w["ln1_g"], w["ln1_b"], name="o_res_ln" + tag)
    hpre, h_bf, got["ff1_fwd"] = _ff1_fwd(x1_bf, w["w_ff1"], w["b_ff1"], tag, exchange=carry.get("ff1_fwd"))
    r2, x2, x2_bf, got["ff2_res_ln"] = _matmul_res_ln(h_bf, w["w_ff2"], w["b_ff2"], x1, w["ln2_g"], w["ln2_b"],
                                                      name="ff2_res_ln" + tag, exchange=carry.get("ff2_res_ln"))
    saved = dict(x=x, x_bf=x_bf, p=p, p_ba=p_ba, h_glu=h_glu, c=c, ha=ha, o=o, hb=hb, states=states, minv=minv, hc=hc, y3=y3,
                 merged=merged, r1=r1, x1=x1, x1_bf=x1_bf, hpre=hpre, h_bf=h_bf, r2=r2)
    return x2, x2_bf, saved, w, got


def _layer_bwd(dx2, w, sv, tag, carry=None):
    carry = carry or {}
    got = {}
    g = {}

    def behind(key, call):
        if key not in carry:
            return call(None)
        out, got[key] = call(carry[key](g))
        return out

    s = dx2.shape[0]
    ts = min(1024, s)
    p = sv["p"]
    dr2, dr2_bf, d_ln2_g, d_ln2_b, d_b_ff2 = _ln_bwd_call(dx2, sv["r2"], w["ln2_g"], name="ln2_bwd" + tag)
    dhpre_bf, d_b_ff1 = _ff2_bwd(dr2_bf, w["w_ff2"], sv["hpre"], tag)
    g["w_ff2"] = behind("dw_ff2", lambda ex: _matmul_tn(sv["h_bf"], dr2_bf, name="dw_ff2" + tag, ka=D_FF, tka=1024, tn=1024,
                                                        ts=ts, exchange=ex))
    g["w_ff1"] = behind("dw_ff1", lambda ex: _matmul_tn(sv["x1_bf"], dhpre_bf, name="dw_ff1" + tag, ka=D_MODEL, tka=1024, tn=1024,
                                                        ts=ts, exchange=ex))
    dr1, dr1_bf, d_ln1_g, d_ln1_b = _ff1_bwd_ln(dhpre_bf, w["w_ff1"], dr2, sv["r1"], w["ln1_g"], tag)
    g["w_o"] = behind("dw_o", lambda ex: _matmul_tn(sv["merged"], dr1_bf, name="dw_o" + tag, ka=D_MODEL, tka=1024, tn=1024,
                                                    ts=ts, exchange=ex))
    dgate_bf, dy3_bf, dha, dhb, dhc, d_b_gate = _merge_bwd(dr1_bf, w["w_o"], sv["y3"], p, w["b_gate"],
                                                          w["w_pa"], w["w_pb"], w["w_pc"], tag)
    for n, (name, h) in enumerate((("w_pa", sv["ha"]), ("w_pb", sv["hb"]), ("w_pc", sv["hc"]))):
        g[name] = _matmul_tn(h, dy3_bf, name=f"dw_p{n}" + tag, ka=CONV_DIM, tka=CONV_DIM, tn=1024, ts=ts, n=D_MODEL, b_col_block=n)
    dc, d_conv_ln_g, d_conv_ln_b, d_conv_b = _conv_a_bwd_pre(dha, sv["c"], w["conv_ln_g"], w["conv_ln_b"], tag)
    da_bf, d_conv_w = _dwconv_bwd(dc, sv["h_glu"], w["conv_w"], width=CONV_WIDTH, halo=CONV_HALO, x_col_block=0,
                                  glu_p=p, name="conv_a_bwd" + tag)
    duv_bf, d_sgu_ln_g, d_sgu_ln_b, d_sgu_w_s, d_sgu_b_s_t = _sgu_bwd(dhc, p, w["sgu_ln_g"], w["sgu_ln_b"], w["sgu_w_s"], w["sgu_b_s_t"], tag)
    dpre, dz_bf, dba_bf, d_norm_g, d_a_log, d_dt_bias = _gdn_bwd(dhb, p, sv["p_ba"], sv["o"], sv["states"], sv["minv"], w["gdn_cw"],
                                                                 w["a_log"], w["dt_bias"], w["norm_g"], tag)
    dqkv_bf, d_gdn_cw = _dwconv_bwd(dpre, p, w["gdn_cw"], width=GDN_CONV, halo=GDN_HALO, x_col_block=COL_Q // CONV_DIM,
                                    glu_p=None, name="gdn_conv_bwd" + tag)
    dp_bf = jnp.concatenate([da_bf, dqkv_bf, dz_bf, duv_bf, dgate_bf, dba_bf], axis=1)
    d_w_all = behind("dw_proj", lambda ex: _matmul_tn(sv["x_bf"], dp_bf, name="dw_proj" + tag, ka=D_MODEL, tka=1024, tn=P_TILE,
                                                      ts=ts, exchange=ex))
    g["w_in"] = _restore_proj_cols(d_w_all)
    dx = behind("proj_bwd", lambda ex: _matmul(dp_bf, w["w_all"], name="proj_bwd" + tag, tm=min(1024, s), tn=D_MODEL, tk=P_TILE,
                                               add=dr1, add_scale=DN_ALPHA, w_is_nk=True, exchange=ex))
    g.update(
        b_gate=d_b_gate[0], conv_dw_w=d_conv_w, conv_dw_b=d_conv_b[0], conv_ln_g=d_conv_ln_g[0], conv_ln_b=d_conv_ln_b[0],
        gdn_conv_q=d_gdn_cw[:, :GDN_QK], gdn_conv_k=d_gdn_cw[:, GDN_QK:2 * GDN_QK], gdn_conv_v=d_gdn_cw[:, 2 * GDN_QK:],
        gdn_a_log=d_a_log[0, GDN_HEADS:2 * GDN_HEADS], gdn_dt_bias=d_dt_bias[0, GDN_HEADS:2 * GDN_HEADS], gdn_norm_g=d_norm_g[0],
        sgu_ln_g=d_sgu_ln_g[0], sgu_ln_b=d_sgu_ln_b[0], sgu_w_s=d_sgu_w_s, sgu_b_s=d_sgu_b_s_t[:, :SGU_GROUPS].T,
        ln1_g=d_ln1_g[0], ln1_b=d_ln1_b[0], b_ff1=d_b_ff1[0], b_ff2=d_b_ff2[0], ln2_g=d_ln2_g[0], ln2_b=d_ln2_b[0],
    )
    return dx, g, got


MESH_AXES = ("x", "y", "c")


def _exchange(arrays, scatter, *, name):
    n = len(arrays)
    ex = _Exchange(arrays, scatter)

    def body(*refs):
        ins, outs, sems = refs[:n], refs[n:2 * n], refs[2 * n:]
        _exchange_start(ex, ins, outs, sems)
        _exchange_wait(ex, ins, outs, sems)

    return pl.pallas_call(
        body, name=name, in_specs=ex.in_specs(), out_specs=ex.in_specs(), out_shape=ex.out_shapes(),
        scratch_shapes=ex.scratch(),
    )(*arrays)


class _Exchange:
    def __init__(self, arrays, scatter):
        self.arrays = list(arrays)
        self.scatter = list(scatter)
        self.n = len(self.arrays)

    def in_specs(self):
        return [pl.BlockSpec(memory_space=pl.ANY)] * self.n

    def out_shapes(self):
        return [jax.ShapeDtypeStruct(a.shape if s else (N_DEV,) + a.shape, a.dtype) for a, s in zip(self.arrays, self.scatter)]

    def scratch(self):
        return [pltpu.SemaphoreType.DMA((self.n, N_DEV - 1)), pltpu.SemaphoreType.DMA((self.n, N_DEV - 1)),
                pltpu.SemaphoreType.DMA((self.n,))]


def _exchange_copies(ex, ins, outs, sems, with_arrivals):
    send_sems, recv_sems, local_sems = sems
    x, y, c = lax.axis_index("x"), lax.axis_index("y"), lax.axis_index("c")
    me = 4 * x + 2 * y + c

    def slot(a, d):
        return ins[a].at[d] if ex.scatter[a] else ins[a]

    local = [pltpu.make_async_copy(slot(a, me), outs[a].at[me], local_sems.at[a]) for a in range(ex.n)]
    remote = []
    for k in range(1, N_DEV):
        px = 1 - x if k & 4 else x
        py = 1 - y if k & 2 else y
        pc = 1 - c if k & 1 else c
        peer = 4 * px + 2 * py + pc
        for a in range(ex.n):
            send = pltpu.make_async_remote_copy(
                src_ref=slot(a, peer), dst_ref=outs[a].at[me], send_sem=send_sems.at[a, k - 1],
                recv_sem=recv_sems.at[a, k - 1], device_id=(px, py, pc), device_id_type=pl.DeviceIdType.MESH)
            arrival = pltpu.make_async_remote_copy(
                src_ref=slot(a, peer), dst_ref=outs[a].at[peer], send_sem=send_sems.at[a, k - 1],
                recv_sem=recv_sems.at[a, k - 1], device_id=(px, py, pc), device_id_type=pl.DeviceIdType.MESH) if with_arrivals else None
            remote.append((send, arrival))
    return local, remote


def _exchange_start(ex, ins, outs, sems):
    local, remote = _exchange_copies(ex, ins, outs, sems, False)
    for cp in local:
        cp.start()
    for send, _ in remote:
        send.start()


def _exchange_wait(ex, ins, outs, sems):
    local, remote = _exchange_copies(ex, ins, outs, sems, True)
    for _, arrival in remote:
        arrival.wait_recv()
    for send, _ in remote:
        send.wait_send()
    for cp in local:
        cp.wait()


def _adamw(w, m, v, g_parts, *, name):
    r, c = w.shape
    tr = 256 if r % 256 == 0 else r
    bc1 = 1.0 - ADAM_B1 ** ADAM_STEP
    bc2 = 1.0 - ADAM_B2 ** ADAM_STEP

    def body(w_ref, m_ref, v_ref, gp_ref, g_ref, d_ref, nm_ref, nv_ref):
        g = gp_ref[0].astype(F32)
        for d in range(1, N_DEV):
            g = g + gp_ref[d].astype(F32)
        nm = ADAM_B1 * m_ref[...] + (1.0 - ADAM_B1) * g
        nv = ADAM_B2 * v_ref[...] + (1.0 - ADAM_B2) * (g * g)
        g_ref[...] = g
        nm_ref[...] = nm
        nv_ref[...] = nv
        d_ref[...] = -ADAM_LR * ((nm / bc1) / (jnp.sqrt(nv / bc2) + ADAM_EPS) + ADAM_WD * w_ref[...])

    spec = pl.BlockSpec((tr, c), lambda i: (i, 0))
    return pl.pallas_call(
        body, name=name, grid=(r // tr,),
        in_specs=[spec, spec, spec, pl.BlockSpec((N_DEV, tr, c), lambda i: (0, i, 0))],
        out_specs=[spec] * 4, out_shape=[jax.ShapeDtypeStruct((r, c), F32)] * 4,
        compiler_params=_params(("parallel",)),
    )(w, m, v, g_parts)


SHARDED = dict(w_in=2, conv_dw_w=2, w_pa=2, gdn_conv_q=2, gdn_conv_k=2, gdn_conv_v=2, w_pb=2, w_pc=2, w_o=1, w_ff1=2, w_ff2=1)
WEIGHTS = ["ln_in_g", "ln_in_b", "w_in", "b_gate", "conv_dw_w", "conv_dw_b", "conv_ln_g", "conv_ln_b", "w_pa", "gdn_conv_q",
           "gdn_conv_k", "gdn_conv_v", "gdn_a_log", "gdn_dt_bias", "gdn_norm_g", "w_pb", "sgu_ln_g", "sgu_ln_b", "sgu_w_s",
           "sgu_b_s", "w_pc", "w_o", "ln1_g", "ln1_b", "w_ff1", "b_ff1", "w_ff2", "b_ff2", "ln2_g", "ln2_b"]
REPLICATED = [n for n in WEIGHTS if n not in SHARDED]
CONV_PACK = ["conv_dw_w", "gdn_conv_q", "gdn_conv_k", "gdn_conv_v"]
PROJ_PACK = ["w_pa", "w_pb", "w_pc"]


def _to_slots(full, axis):
    shp = full.shape
    split = full.reshape(shp[:axis] + (N_DEV, shp[axis] // N_DEV) + shp[axis + 1:])
    return jnp.moveaxis(split, axis, 0)


def _from_slots(slots, axis):
    merged = jnp.moveaxis(slots, 0, axis)
    shp = merged.shape
    return merged.reshape(shp[:axis] + (shp[axis] * shp[axis + 1],) + shp[axis + 2:])


def _pack_rows(arrs):
    rows = []
    for a in arrs:
        flat = a.reshape(-1)
        pad = (-flat.shape[0]) % LANES
        rows.append(jnp.pad(flat, (0, pad)).reshape(-1, LANES))
    out = jnp.concatenate(rows, axis=0)
    return jnp.pad(out, ((0, (-out.shape[0]) % 8), (0, 0)))


def _unpack_rows(packed, shapes):
    out, r = [], 0
    for shp in shapes:
        size = math.prod(shp)
        nrows = -(-size // LANES)
        out.append(packed[r:r + nrows].reshape(-1)[:size].reshape(shp))
        r += nrows
    return out


def kernel(x, ln_in_g, ln_in_b, w_in, b_gate, conv_dw_w, conv_dw_b, conv_ln_g, conv_ln_b, w_pa, gdn_conv_q, gdn_conv_k, gdn_conv_v, gdn_a_log, gdn_dt_bias, gdn_norm_g, w_pb, sgu_ln_g, sgu_ln_b, sgu_w_s, sgu_b_s, w_pc, w_o, ln1_g, ln1_b, w_ff1, b_ff1, w_ff2, b_ff2, ln2_g, ln2_b, loss_target, m_ln_in_g, m_ln_in_b, m_w_in, m_b_gate, m_conv_dw_w, m_conv_dw_b, m_conv_ln_g, m_conv_ln_b, m_w_pa, m_gdn_conv_q, m_gdn_conv_k, m_gdn_conv_v, m_gdn_a_log, m_gdn_dt_bias, m_gdn_norm_g, m_w_pb, m_sgu_ln_g, m_sgu_ln_b, m_sgu_w_s, m_sgu_b_s, m_w_pc, m_w_o, m_ln1_g, m_ln1_b, m_w_ff1, m_b_ff1, m_w_ff2, m_b_ff2, m_ln2_g, m_ln2_b, v_ln_in_g, v_ln_in_b, v_w_in, v_b_gate, v_conv_dw_w, v_conv_dw_b, v_conv_ln_g, v_conv_ln_b, v_w_pa, v_gdn_conv_q, v_gdn_conv_k, v_gdn_conv_v, v_gdn_a_log, v_gdn_dt_bias, v_gdn_norm_g, v_w_pb, v_sgu_ln_g, v_sgu_ln_b, v_sgu_w_s, v_sgu_b_s, v_w_pc, v_w_o, v_ln1_g, v_ln1_b, v_w_ff1, v_b_ff1, v_w_ff2, v_b_ff2, v_ln2_g, v_ln2_b):
    args = locals()
    w = {n: args[n] for n in WEIGHTS}
    m = {n: args["m_" + n] for n in WEIGHTS}
    v = {n: args["v_" + n] for n in WEIGHTS}

    rep = {n: w[n] for n in REPLICATED}
    conv_local = jnp.concatenate([w[n] for n in CONV_PACK], axis=1)
    proj_local = jnp.stack([w[n] for n in PROJ_PACK], axis=1).astype(BF16)
    big = ["w_in", "w_o", "w_ff1", "w_ff2"]
    big_local = {n: w[n].astype(BF16) for n in big}
    rest_local = lambda l: [big_local[n][l] for n in big[1:]] + [proj_local[l], conv_local[l]]

    def rest_full(got):
        full_l = {n: _from_slots(g, SHARDED[n] - 1) for n, g in zip(big[1:], got[:3])}
        proj_full = _from_slots(got[3], 2)
        for i, n in enumerate(PROJ_PACK):
            full_l[n] = proj_full[i]
        conv_full = _from_slots(got[4], 1)
        tap0 = 0
        for n in CONV_PACK:
            taps = w[n].shape[1]
            full_l[n] = conv_full[tap0:tap0 + taps]
            tap0 += taps
        return full_l

    ln_g, ln_b = w["ln_in_g"][None, :], w["ln_in_b"][None, :]
    xs, xs_bf = _ln_in_fwd(x[0], ln_g, ln_b)
    (w_in0,) = _exchange([big_local["w_in"][0]], [False], name="gather_w_in_l0")
    gather = lambda arrays: _Exchange(arrays, [False] * len(arrays))
    xs, xs_bf, sv0, w0, got0 = _layer_fwd(
        xs, xs_bf, _proj_weights(_from_slots(w_in0, 1)), lambda got: _rest_weights(0, rest_full(got), rep), "_l0",
        carry={"proj_fwd": gather(rest_local(0)), "ff1_fwd": gather([big_local["w_in"][1], conv_local[1]])})
    xs, xs_bf, sv1, w1, _ = _layer_fwd(
        xs, xs_bf, _proj_weights(_from_slots(got0["ff1_fwd"][0], 1)),
        lambda got: _rest_weights(1, rest_full(list(got) + [got0["ff1_fwd"][1]]), rep), "_l1",
        carry={"proj_fwd": gather(rest_local(1)[:4])})
    d, loss_acc = _loss_fwd_bwd(xs, loss_target[0])
    loss = lax.psum(loss_acc[0, 0], MESH_AXES)

    def scatter_of(g, names):
        def slots(n):
            if n == "proj":
                return _to_slots(jnp.stack([g[q] for q in PROJ_PACK], axis=0).astype(BF16), 2)
            return _to_slots(g[n].astype(BF16), SHARDED[n] - 1)

        return _Exchange([slots(n) for n in names], [True] * len(names))

    d, grads1, _ = _layer_bwd(d, w1, sv1, "_l1")
    plan = {"dw_ff2": (1, ["w_ff1", "w_ff2"]), "dw_ff1": (1, ["w_in"]), "dw_o": (1, ["w_o", "proj"]),
            "dw_proj": (0, ["w_ff1", "w_ff2", "w_o", "proj"]), "proj_bwd": (0, ["w_in"])}
    d, grads0, got = _layer_bwd(
        d, w0, sv0, "_l0",
        carry={call: (lambda g, l=l, names=names: scatter_of(grads1 if l == 1 else g, names)) for call, (l, names) in plan.items()})
    layer_parts = {(l, n): a for call, (l, names) in plan.items() for n, a in zip(names, got[call])}
    dx, _, d_ln_in_g, d_ln_in_b, _ = _ln_bwd_call(d, x[0], ln_g, name="ln_in_bwd")
    grads = {k: jnp.stack([grads0[k], grads1[k]]) for k in grads0}
    grads["ln_in_g"] = d_ln_in_g[0]
    grads["ln_in_b"] = d_ln_in_b[0]
    conv_grad = jnp.concatenate([grads[n] for n in CONV_PACK], axis=1)
    small_parts = _exchange([_to_slots(conv_grad, 2), _pack_rows([grads[n] for n in REPLICATED])], [True, False],
                            name="exchange_small_grads")
    parts = [jnp.stack([layer_parts[(0, n)], layer_parts[(1, n)]], axis=1) for n in big + ["proj"]] + list(small_parts)

    def adam_sharded(g_parts, w_l, m_l, v_l, name):
        shp = w_l.shape
        two_d = lambda a: a.reshape(-1, shp[-1])
        outs = _adamw(two_d(w_l), two_d(m_l), two_d(v_l), g_parts.reshape(N_DEV, -1, shp[-1]), name=name)
        return [o.reshape(shp) for o in outs]

    res = {}
    for n, gp in zip(big, parts[:4]):
        res[n] = adam_sharded(gp, w[n], m[n], v[n], "adamw_" + n)
    proj_res = adam_sharded(parts[4], jnp.stack([w[n] for n in PROJ_PACK], axis=1), jnp.stack([m[n] for n in PROJ_PACK], axis=1),
                            jnp.stack([v[n] for n in PROJ_PACK], axis=1), "adamw_proj")
    for i, n in enumerate(PROJ_PACK):
        res[n] = [o[:, i] for o in proj_res]
    conv_res = adam_sharded(parts[5], conv_local, jnp.concatenate([m[n] for n in CONV_PACK], axis=1),
                            jnp.concatenate([v[n] for n in CONV_PACK], axis=1), "adamw_conv")
    tap0 = 0
    for n in CONV_PACK:
        taps = w[n].shape[1]
        res[n] = [o[:, tap0:tap0 + taps] for o in conv_res]
        tap0 += taps
    rep_shapes = [w[n].shape for n in REPLICATED]
    rep_res = _adamw(_pack_rows([w[n] for n in REPLICATED]), _pack_rows([m[n] for n in REPLICATED]),
                     _pack_rows([v[n] for n in REPLICATED]), parts[6], name="adamw_replicated")
    rep_res = [_unpack_rows(o, rep_shapes) for o in rep_res]
    for i, n in enumerate(REPLICATED):
        res[n] = [o[i] for o in rep_res]

    outs = [loss, dx[None]]
    for j in range(4):
        outs += [res[n][j] for n in WEIGHTS]
    return tuple(outs)
```

```python
import functools
import math

import jax
import jax.numpy as jnp
from jax import lax
from jax.experimental import pallas as pl
from jax.experimental.pallas import tpu as pltpu

F32 = jnp.float32
BF16 = jnp.bfloat16

N_DEV = 8
DEPTH = 2
D_MODEL = 1024
CONV_DIM = 512
CONV_WIDTH = 31
GDN_HEADS = 4
GDN_DK = 128
GDN_QK = 512
GDN_CONV = 4
GDN_CHUNK = 64
SGU_GROUPS = 4
SGU_GROUP_DIM = 128
SGU_DIM = 512
SGU_CHUNK = 128
D_FF = 4096
DN_ALPHA = (2 * DEPTH) ** 0.25
LN_EPS = 1e-5
RMS_EPS = 1e-6
PROJ_COLS = 7176
SHARD_COLS = PROJ_COLS // N_DEV

COL_A = 0
COL_Q = 1024
COL_Z = 2560
COL_UV = 3072
COL_GATE = 4096
COL_BA = 7168
P_COLS = 7296
P_TILE = 2432

ADAM_LR = 0.001
ADAM_B1 = 0.9
ADAM_B2 = 0.999
ADAM_EPS = 1e-08
ADAM_WD = 0.01
ADAM_STEP = 10

VMEM_LIMIT_BYTES = 56 * 1024 * 1024
LANES = 128
ROW_TILE = 512
GDN_BLOCK = 256
CONV_HALO = 32
GDN_HALO = 16


def _params(sem):
    return pltpu.CompilerParams(dimension_semantics=sem, vmem_limit_bytes=VMEM_LIMIT_BYTES)


def _dot(a, b):
    return jnp.dot(a, b, preferred_element_type=F32)


def _dot_nt(a, b):
    return lax.dot_general(a, b, (((1,), (1,)), ((), ())), preferred_element_type=F32)


def _dot_tn(a, b):
    return lax.dot_general(a, b, (((0,), (0,)), ((), ())), preferred_element_type=F32)


def _split(a):
    hi = a.astype(BF16)
    lo = (a - hi.astype(F32)).astype(BF16)
    return hi, lo


def _dot3(a, b, dot=_dot):
    ah, al = _split(a)
    bh, bl = _split(b)
    return dot(ah, bh) + (dot(ah, bl) + dot(al, bh))


def _bdot(a, b, dot=_dot):
    return dot(a.astype(BF16), b.astype(BF16))


def _sigmoid(x):
    return jax.nn.sigmoid(x)


def _silu(x):
    return x * _sigmoid(x)


def _dsilu(x):
    s = _sigmoid(x)
    return s * (1.0 + x * (1.0 - s))


_GELU_C = math.sqrt(2.0 / math.pi)


def _gelu(x):
    return 0.5 * x * (1.0 + jnp.tanh(_GELU_C * (x + 0.044715 * (x * x * x))))


def _dgelu(x):
    t = jnp.tanh(_GELU_C * (x + 0.044715 * (x * x * x)))
    return 0.5 * (1.0 + t) + 0.5 * x * (1.0 - t * t) * (_GELU_C * (1.0 + 3.0 * 0.044715 * (x * x)))


def _ln_stats(x):
    mu = jnp.mean(x, axis=-1, keepdims=True)
    xc = x - mu
    var = jnp.mean(xc * xc, axis=-1, keepdims=True)
    rstd = lax.rsqrt(var + LN_EPS)
    return xc * rstd, rstd


def _ln_bwd(dy, xhat, rstd, g):
    dxh = dy * g
    return rstd * (dxh - jnp.mean(dxh, axis=-1, keepdims=True) - xhat * jnp.mean(dxh * xhat, axis=-1, keepdims=True))


def _colsum(x):
    return jnp.sum(x, axis=0, keepdims=True)


def _row_spec(t, cols, col_block=0):
    return pl.BlockSpec((t, cols), lambda i, cb=col_block: (i, cb))


def _const_spec(shape):
    nd = len(shape)
    return pl.BlockSpec(shape, lambda *_: (0,) * nd)


def _call(body, *, name, grid, in_specs, out_specs, out_shape, args, sem, scratch=(), exchange=None):
    n_in, n_out, n_sc = len(in_specs), len(out_specs), len(scratch)
    if exchange is None:
        def plain(*refs):
            body(refs[:n_in], refs[n_in:n_in + n_out], refs[n_in + n_out:])

        return pl.pallas_call(plain, name=name, grid=grid, in_specs=in_specs, out_specs=out_specs, out_shape=out_shape,
                              scratch_shapes=list(scratch), compiler_params=_params(sem))(*args)
    nex = exchange.n

    def carrying(*refs):
        ins, ex_ins = refs[:n_in], refs[n_in:n_in + nex]
        outs, ex_outs = refs[n_in + nex:n_in + nex + n_out], refs[n_in + nex + n_out:n_in + 2 * nex + n_out]
        sc, sems = refs[n_in + 2 * nex + n_out:n_in + 2 * nex + n_out + n_sc], refs[n_in + 2 * nex + n_out + n_sc:]
        ids = [pl.program_id(d) for d in range(len(grid))]
        first = functools.reduce(jnp.logical_and, [i == 0 for i in ids])
        last = functools.reduce(jnp.logical_and, [i == g - 1 for i, g in zip(ids, grid)])

        @pl.when(first)
        def _():
            _exchange_start(exchange, ex_ins, ex_outs, sems)

        body(ins, outs, sc)

        @pl.when(last)
        def _():
            _exchange_wait(exchange, ex_ins, ex_outs, sems)

    res = pl.pallas_call(
        carrying, name=name, grid=grid, in_specs=list(in_specs) + exchange.in_specs(),
        out_specs=list(out_specs) + exchange.in_specs(), out_shape=list(out_shape) + exchange.out_shapes(),
        scratch_shapes=list(scratch) + exchange.scratch(), compiler_params=_params(("arbitrary",) * len(grid)),
    )(*args, *exchange.arrays)
    return res[:n_out], res[n_out:]


def _matmul(a, w, *, name, tm, tn, tk, out_dtype=F32, a_col_block=0, add=None, add_scale=1.0, w_is_nk=False, exchange=None):
    m = a.shape[0]
    k, n = w.shape[::-1] if w_is_nk else w.shape
    nk = k // tk
    has_add = add is not None

    def body(ins, outs, scratch):
        a_ref, w_ref = ins[:2]
        o_ref, acc_ref = outs[0], scratch[0]
        if has_add:
            add_ref = ins[2]
        kk = pl.program_id(2)

        @pl.when(kk == 0)
        def _():
            acc_ref[...] = jnp.zeros_like(acc_ref)

        acc_ref[...] += (_dot_nt if w_is_nk else _dot)(a_ref[...], w_ref[...])

        @pl.when(kk == nk - 1)
        def _():
            r = acc_ref[...]
            if has_add:
                r = r + add_scale * add_ref[...]
            o_ref[...] = r.astype(out_dtype)

    in_specs = [pl.BlockSpec((tm, tk), lambda j, i, kk: (i, kk + a_col_block)),
                pl.BlockSpec((tn, tk), lambda j, i, kk: (j, kk)) if w_is_nk else pl.BlockSpec((tk, tn), lambda j, i, kk: (kk, j))]
    args = [a, w]
    if has_add:
        in_specs.append(pl.BlockSpec((tm, tn), lambda j, i, kk: (i, j)))
        args.append(add)
    res = _call(body, name=name, grid=(n // tn, m // tm, nk), in_specs=in_specs,
                out_specs=[pl.BlockSpec((tm, tn), lambda j, i, kk: (i, j))],
                out_shape=[jax.ShapeDtypeStruct((m, n), out_dtype)], scratch=[pltpu.VMEM((tm, tn), F32)],
                args=args, sem=("parallel", "parallel", "arbitrary"), exchange=exchange)
    return res[0] if exchange is None else (res[0][0], res[1])


def _matmul_tn(a, b, *, name, ka, tka, tn, ts, n=None, a_col_block=0, b_col_block=0, exchange=None):
    s = a.shape[0]
    n = b.shape[1] if n is None else n
    ns = s // ts

    def body(ins, outs, scratch):
        a_ref, b_ref = ins
        o_ref = outs[0]

        @pl.when(pl.program_id(2) == 0)
        def _():
            o_ref[...] = jnp.zeros_like(o_ref)

        o_ref[...] += _dot_tn(a_ref[...], b_ref[...])

    res = _call(body, name=name, grid=(ka // tka, n // tn, ns),
                in_specs=[pl.BlockSpec((ts, tka), lambda i, j, t: (t, i + a_col_block)),
                          pl.BlockSpec((ts, tn), lambda i, j, t: (t, j + b_col_block))],
                out_specs=[pl.BlockSpec((tka, tn), lambda i, j, t: (i, j))],
                out_shape=[jax.ShapeDtypeStruct((ka, n), F32)], args=[a, b],
                sem=("parallel", "parallel", "arbitrary"), exchange=exchange)
    return res[0] if exchange is None else (res[0][0], res[1])


def _ln_in_fwd(x, g, b, exchange):
    s = x.shape[0]
    t = min(ROW_TILE, s)

    def body(ins, outs, scratch):
        x_ref, g_ref, b_ref = ins
        y_ref, ybf_ref = outs
        xhat, _ = _ln_stats(x_ref[...])
        y = xhat * g_ref[...] + b_ref[...]
        y_ref[...] = y
        ybf_ref[...] = y.astype(BF16)

    (y, y_bf), got = _call(
        body, name="ln_in_fwd", grid=(s // t,),
        in_specs=[_row_spec(t, D_MODEL), _const_spec((1, D_MODEL)), _const_spec((1, D_MODEL))],
        out_specs=[_row_spec(t, D_MODEL), _row_spec(t, D_MODEL)],
        out_shape=[jax.ShapeDtypeStruct((s, D_MODEL), F32), jax.ShapeDtypeStruct((s, D_MODEL), BF16)],
        args=[x, g, b], sem=("parallel",), exchange=exchange)
    return y, y_bf, got


def _prev_halo_spec(t, halo, cols, col_block):
    per = t // halo
    return pl.BlockSpec((halo, cols), lambda i, cb=col_block: (jnp.maximum(i * per - 1, 0), cb))


def _next_halo_spec(t, halo, cols, col_block, n_blocks):
    per = t // halo
    last = n_blocks * per - 1
    return pl.BlockSpec((halo, cols), lambda i, cb=col_block: (jnp.minimum((i + 1) * per, last), cb))


CONV_ROWS = 32


def _tap_groups(offsets):
    by_shift = {}
    for j, off in enumerate(offsets):
        by_shift.setdefault(off % 8, []).append((j, off // 8))
    groups = []
    for shift, taps in sorted(by_shift.items()):
        first = min(a for _, a in taps)
        last = max(a for _, a in taps)
        groups.append((shift, 8 * first, 8 * (last - first), [(j, 8 * (a - first)) for j, a in taps]))
    return groups


def _tap_windows(ext, groups, r0):
    return [ext[r0 + first + shift:r0 + first + shift + CONV_ROWS + extra, :] for shift, first, extra, _ in groups]


def _tap_sum(ext, w_ref, groups, r0):
    acc = jnp.zeros((CONV_ROWS, ext.shape[1]), F32)
    for (_, _, _, taps), win in zip(groups, _tap_windows(ext, groups, r0)):
        for j, a in taps:
            acc = acc + w_ref[j:j + 1, :] * win[a:a + CONV_ROWS]
    return acc


def _conv_a_fwd(p, w, b, ln_g, ln_b, tag):
    s = p.shape[0]
    t = min(ROW_TILE, s)
    width = CONV_WIDTH

    def body(a_ref, halo_ref, w_ref, b_ref, g_ref, bb_ref, h_ref, c_ref, ha_ref, ext):
        i = pl.program_id(0)
        a = a_ref[...].astype(F32)
        h = a[:, :CONV_DIM] * _sigmoid(a[:, CONV_DIM:])
        ah = halo_ref[...].astype(F32)
        hh = ah[:, :CONV_DIM] * _sigmoid(ah[:, CONV_DIM:])
        ext[0:CONV_HALO, :] = jnp.where(i == 0, 0.0, hh)
        ext[CONV_HALO:CONV_HALO + t, :] = h
        h_ref[...] = h
        groups = _tap_groups([CONV_HALO - (width - 1) + j for j in range(width)])
        for r0 in range(0, t, CONV_ROWS):
            c = _tap_sum(ext, w_ref, groups, r0) + b_ref[...]
            xhat, _ = _ln_stats(c)
            n = xhat * g_ref[...] + bb_ref[...]
            c_ref[r0:r0 + CONV_ROWS, :] = c
            ha_ref[r0:r0 + CONV_ROWS, :] = _silu(n).astype(BF16)

    return pl.pallas_call(
        body, name="conv_a_fwd" + tag, grid=(s // t,),
        in_specs=[_row_spec(t, 2 * CONV_DIM, COL_A // (2 * CONV_DIM)),
                  _prev_halo_spec(t, CONV_HALO, 2 * CONV_DIM, COL_A // (2 * CONV_DIM)),
                  _const_spec((width, CONV_DIM)), _const_spec((1, CONV_DIM)),
                  _const_spec((1, CONV_DIM)), _const_spec((1, CONV_DIM))],
        out_specs=[_row_spec(t, CONV_DIM)] * 3,
        out_shape=[jax.ShapeDtypeStruct((s, CONV_DIM), F32), jax.ShapeDtypeStruct((s, CONV_DIM), F32),
                   jax.ShapeDtypeStruct((s, CONV_DIM), BF16)],
        scratch_shapes=[pltpu.VMEM((CONV_HALO + t, CONV_DIM), F32)],
        compiler_params=_params(("parallel",)),
    )(p, p, w, b, ln_g, ln_b)


def _sgu_mix(vn, wt_ref, bst_ref, t):
    row = lax.broadcasted_iota(jnp.int32, (SGU_CHUNK, SGU_CHUNK), 0)
    col = lax.broadcasted_iota(jnp.int32, (SGU_CHUNK, SGU_CHUNK), 1)
    chunks = []
    for ci in range(t // SGU_CHUNK):
        groups = []
        for g in range(SGU_GROUPS):
            wg = jnp.where(row >= col, wt_ref[g], 0.0).astype(BF16)
            v_cg = vn[ci * SGU_CHUNK:(ci + 1) * SGU_CHUNK, g * SGU_GROUP_DIM:(g + 1) * SGU_GROUP_DIM]
            groups.append(_dot(wg, v_cg.astype(BF16)) + bst_ref[:, g:g + 1])
        chunks.append(jnp.concatenate(groups, axis=1))
    return jnp.concatenate(chunks, axis=0)


def _sgu_fwd(p, ln_g, ln_b, w_s, b_s_t, tag):
    s = p.shape[0]
    t = min(ROW_TILE, s)

    def body(uv_ref, g_ref, b_ref, ws_ref, bst_ref, hc_ref):
        uv = uv_ref[...].astype(F32)
        u = _gelu(uv[:, :SGU_DIM])
        vv = _gelu(uv[:, SGU_DIM:])
        xhat, _ = _ln_stats(vv)
        vn = xhat * g_ref[...] + b_ref[...]
        mixed = _sgu_mix(vn, ws_ref, bst_ref, t)
        hc_ref[...] = (u * mixed).astype(BF16)

    return pl.pallas_call(
        body, name="sgu_fwd" + tag, grid=(s // t,),
        in_specs=[_row_spec(t, 2 * SGU_DIM, COL_UV // (2 * SGU_DIM)),
                  _const_spec((1, SGU_DIM)), _const_spec((1, SGU_DIM)),
                  _const_spec((SGU_GROUPS, SGU_CHUNK, SGU_CHUNK)), _const_spec((SGU_CHUNK, LANES))],
        out_specs=_row_spec(t, SGU_DIM),
        out_shape=jax.ShapeDtypeStruct((s, SGU_DIM), BF16),
        compiler_params=_params(("parallel",)),
    )(p, ln_g, ln_b, w_s, b_s_t)


def _merge_fwd(p, ha, hb, hc, w_pa, w_pb, w_pc, b_gate, tag):
    s = p.shape[0]
    t = min(ROW_TILE, s)
    gb = COL_GATE // D_MODEL

    def body(ga_ref, gb_ref, gc_ref, ha_ref, hb_ref, hc_ref, wa_ref, wb_ref, wc_ref, bg_ref, y_ref, m_ref):
        merged = jnp.zeros((t, D_MODEL), F32)
        for idx, (g_ref, h_ref, w_ref) in enumerate(((ga_ref, ha_ref, wa_ref), (gb_ref, hb_ref, wb_ref), (gc_ref, hc_ref, wc_ref))):
            y = _dot(h_ref[...], w_ref[...])
            sg = _sigmoid(g_ref[...].astype(F32) + bg_ref[:, idx * D_MODEL:(idx + 1) * D_MODEL])
            y_ref[:, idx * D_MODEL:(idx + 1) * D_MODEL] = y.astype(BF16)
            merged = merged + sg * y
        m_ref[...] = merged.astype(BF16)

    hspec = _row_spec(t, CONV_DIM)
    wspec = _const_spec((CONV_DIM, D_MODEL))
    return pl.pallas_call(
        body, name="merge_fwd" + tag, grid=(s // t,),
        in_specs=[_row_spec(t, D_MODEL, gb), _row_spec(t, D_MODEL, gb + 1), _row_spec(t, D_MODEL, gb + 2),
                  hspec, hspec, hspec, wspec, wspec, wspec, _const_spec((1, 3 * D_MODEL))],
        out_specs=[_row_spec(t, 3 * D_MODEL), _row_spec(t, D_MODEL)],
        out_shape=[jax.ShapeDtypeStruct((s, 3 * D_MODEL), BF16), jax.ShapeDtypeStruct((s, D_MODEL), BF16)],
        compiler_params=_params(("parallel",)),
    )(p, p, p, ha, hb, hc, w_pa, w_pb, w_pc, b_gate)


def _matmul_res_ln(a, w, bias, x_res, ln_g, ln_b, *, name, exchange=None):
    s, k = a.shape
    t = min(ROW_TILE, s)

    def body(ins, outs, scratch):
        a_ref, w_ref, bias_ref, x_ref, g_ref, b_ref = ins
        r_ref, y_ref, ybf_ref = outs
        r = DN_ALPHA * x_ref[...] + _dot(a_ref[...], w_ref[...]) + bias_ref[...]
        xhat, _ = _ln_stats(r)
        y = xhat * g_ref[...] + b_ref[...]
        r_ref[...] = r
        y_ref[...] = y
        ybf_ref[...] = y.astype(BF16)

    vec = _const_spec((1, D_MODEL))
    res = _call(
        body, name=name, grid=(s // t,),
        in_specs=[_row_spec(t, k), _const_spec((k, D_MODEL)), vec, _row_spec(t, D_MODEL), vec, vec],
        out_specs=[_row_spec(t, D_MODEL)] * 3,
        out_shape=[jax.ShapeDtypeStruct((s, D_MODEL), F32), jax.ShapeDtypeStruct((s, D_MODEL), F32),
                   jax.ShapeDtypeStruct((s, D_MODEL), BF16)],
        args=[a, w, bias, x_res, ln_g, ln_b], sem=("parallel",), exchange=exchange)
    return (*res, None) if exchange is None else (*res[0], res[1])


def _ff1_fwd(x_bf, w, b, tag, exchange=None):
    s = x_bf.shape[0]
    tm = min(1024, s)
    tn = 1024

    def body(ins, outs, scratch):
        x_ref, w_ref, b_ref = ins
        hp_ref, h_ref = outs
        hp = _dot(x_ref[...], w_ref[...]) + b_ref[...]
        hp_ref[...] = hp.astype(BF16)
        r = jnp.maximum(hp, 0.0)
        h_ref[...] = (r * r).astype(BF16)

    res = _call(
        body, name="ff1_fwd" + tag, grid=(s // tm, D_FF // tn),
        in_specs=[pl.BlockSpec((tm, D_MODEL), lambda i, j: (i, 0)), pl.BlockSpec((D_MODEL, tn), lambda i, j: (0, j)),
                  pl.BlockSpec((1, tn), lambda i, j: (0, j))],
        out_specs=[pl.BlockSpec((tm, tn), lambda i, j: (i, j))] * 2,
        out_shape=[jax.ShapeDtypeStruct((s, D_FF), BF16), jax.ShapeDtypeStruct((s, D_FF), BF16)],
        args=[x_bf, w, b], sem=("parallel", "parallel"), exchange=exchange)
    return (res[0], res[1], None) if exchange is None else (res[0][0], res[0][1], res[1])


def _loss_fwd_bwd(y, target):
    s = y.shape[0]
    t = min(ROW_TILE, s)

    def body(y_ref, t_ref, dy_ref, loss_ref):
        @pl.when(pl.program_id(0) == 0)
        def _():
            loss_ref[...] = jnp.zeros_like(loss_ref)

        err = y_ref[...] - t_ref[...]
        dy_ref[...] = err * (1.0 / D_MODEL)
        per_row = jnp.mean(err * err, axis=-1, keepdims=True)
        loss_ref[...] += 0.5 * jnp.sum(per_row, axis=0, keepdims=True)

    return pl.pallas_call(
        body, name="loss_fwd_bwd", grid=(s // t,),
        in_specs=[_row_spec(t, D_MODEL), _row_spec(t, D_MODEL)],
        out_specs=[_row_spec(t, D_MODEL), _const_spec((8, LANES))],
        out_shape=[jax.ShapeDtypeStruct((s, D_MODEL), F32), jax.ShapeDtypeStruct((8, LANES), F32)],
        compiler_params=_params(("arbitrary",)),
    )(y, target)


def _softplus(x):
    return jnp.maximum(x, 0.0) + jnp.log1p(jnp.exp(-jnp.abs(x)))


def _gdn_conv_silu_norm(q_ref, k_ref, v_ref, hq_ref, hk_ref, hv_ref, cw_ref, ext, first):
    t = q_ref.shape[0]
    for n, (r, h) in enumerate(((q_ref, hq_ref), (k_ref, hk_ref), (v_ref, hv_ref))):
        ext[0:GDN_HALO, n * GDN_QK:(n + 1) * GDN_QK] = jnp.where(first, 0.0, h[...].astype(F32))
        ext[GDN_HALO:GDN_HALO + t, n * GDN_QK:(n + 1) * GDN_QK] = r[...].astype(F32)
    pre = jnp.zeros((t, 3 * GDN_QK), F32)
    for j in range(GDN_CONV):
        pre = pre + cw_ref[j:j + 1, :] * ext[pl.ds(GDN_HALO - (GDN_CONV - 1) + j, t), :]
    act = _silu(pre)
    rq, rk = [], []
    for h in range(GDN_HEADS):
        qh = act[:, h * GDN_DK:(h + 1) * GDN_DK]
        kh = act[:, GDN_QK + h * GDN_DK:GDN_QK + (h + 1) * GDN_DK]
        rq.append(lax.rsqrt(jnp.sum(qh * qh, axis=-1, keepdims=True) + RMS_EPS))
        rk.append(lax.rsqrt(jnp.sum(kh * kh, axis=-1, keepdims=True) + RMS_EPS))
    return pre, act, rq, rk


def _gdn_gates(ba, alog_ref, dtb_ref):
    lane = lax.broadcasted_iota(jnp.int32, ba.shape, 1)
    beta = _sigmoid(ba)
    g = -jnp.exp(alog_ref[...]) * _softplus(ba + dtb_ref[...])
    g = jnp.where((lane >= GDN_HEADS) & (lane < 2 * GDN_HEADS), g, 0.0)
    return beta, g


def _chunk_cumsum_matrix(t, upper):
    row = lax.broadcasted_iota(jnp.int32, (t, t), 0)
    col = lax.broadcasted_iota(jnp.int32, (t, t), 1)
    same = (row // GDN_CHUNK) == (col // GDN_CHUNK)
    tri = (col >= row) if upper else (col <= row)
    return jnp.where(same & tri, 1.0, 0.0).astype(F32)


def _each(fn, *lists):
    return [fn(*a) for a in zip(*lists)]


def _gdn_pair_items(qn_s, kn_s, vc_s, beta_s, gam_s, nc):
    items = []
    for ci in range(nc):
        rows = slice(ci * GDN_CHUNK, (ci + 1) * GDN_CHUNK)
        gam_blk = gam_s[rows, :]
        gam_t = gam_blk.T
        beta_blk = beta_s[rows, :]
        for h in range(GDN_HEADS):
            sl = slice(h * GDN_DK, (h + 1) * GDN_DK)
            items.append((qn_s[rows, sl], kn_s[rows, sl], vc_s[rows, sl], beta_blk[:, h:h + 1],
                          gam_blk[:, GDN_HEADS + h:GDN_HEADS + h + 1], gam_t[GDN_HEADS + h:GDN_HEADS + h + 1, :]))
    return items


def _gdn_prep(items):
    c = GDN_CHUNK
    row = lax.broadcasted_iota(jnp.int32, (c, c), 0)
    col = lax.broadcasted_iota(jnp.int32, (c, c), 1)
    causal = row >= col
    strict = row > col
    kbs = [k.astype(BF16) for _, k, _, _, _, _ in items]
    kks = _each(_dot_nt, kbs, kbs)
    qks = _each(_dot_nt, [q.astype(BF16) for q, _, _, _, _, _ in items], kbs)
    out = []
    for (q, k, v, beta_c, gam_c, gam_r), kk, qk in zip(items, kks, qks):
        decay = jnp.where(causal, jnp.exp(jnp.where(causal, gam_c - gam_r, 0.0)), 0.0)
        gm = jnp.exp(gam_c)
        glast = gam_c[c - 1:c, :]
        elast = jnp.exp(glast - gam_c)
        out.append(dict(causal=causal, strict=strict, decay=decay, kk=kk, low=jnp.where(strict, beta_c * kk * decay, 0.0),
                        a_qk=qk * decay, gm=gm, glast_exp=jnp.exp(glast), elast=elast, q=q, k=k, v=v, beta_c=beta_c,
                        r=jnp.concatenate([beta_c * v, beta_c * k * gm], axis=1), qd=q * gm, kd=k * elast))
    return out


def _unit_lower_inverses_minus_identity(lows):
    ps = [-low for low in lows]
    mis = list(ps)
    for _ in range(5):
        ps = _each(_bdot, ps, ps)
        ts = _each(_bdot, mis, ps)
        mis = [mi + p + t for mi, p, t in zip(mis, ps, ts)]
    return mis


def _apply_inverses(mis, rs, dot=_dot):
    return [r + t for r, t in zip(rs, _each(functools.partial(_bdot, dot=dot), mis, rs))]


def _gdn_fwd(p, p_ba, conv_w, a_log, dt_bias, norm_g, tag):
    s = p.shape[0]
    t = min(GDN_BLOCK, s)
    nc = t // GDN_CHUNK
    nblk = s // t
    qb, kb_, vb, zb = COL_Q // GDN_QK, COL_Q // GDN_QK + 1, COL_Q // GDN_QK + 2, COL_Z // GDN_QK
    scale = GDN_DK ** -0.5

    def body(q_ref, k_ref, v_ref, hq_ref, hk_ref, hv_ref, z_ref, ba_ref, cw_ref, alog_ref, dtb_ref, ng_ref,
             o_ref, hb_ref, st_ref, m_ref, ext, qn_s, kn_s, vc_s, beta_s, gam_s, state):
        i = pl.program_id(0)

        @pl.when(i == 0)
        def _():
            state[...] = jnp.zeros_like(state)

        _, act, rq, rk = _gdn_conv_silu_norm(q_ref, k_ref, v_ref, hq_ref, hk_ref, hv_ref, cw_ref, ext, i == 0)
        for h in range(GDN_HEADS):
            sl = slice(h * GDN_DK, (h + 1) * GDN_DK)
            qn_s[:, sl] = act[:, sl] * (rq[h] * scale)
            kn_s[:, sl] = act[:, GDN_QK + h * GDN_DK:GDN_QK + (h + 1) * GDN_DK] * rk[h]
        vc_s[...] = act[:, 2 * GDN_QK:]
        beta, g = _gdn_gates(ba_ref[...], alog_ref, dtb_ref)
        beta_s[...] = beta
        gam_s[...] = jnp.dot(_chunk_cumsum_matrix(t, False), g, preferred_element_type=F32, precision=lax.Precision.HIGHEST)

        prs = _gdn_prep(_gdn_pair_items(qn_s, kn_s, vc_s, beta_s, gam_s, nc))
        mis = _unit_lower_inverses_minus_identity([pr["low"] for pr in prs])
        xs = _apply_inverses(mis, [pr["r"] for pr in prs])
        heads = range(GDN_HEADS)
        for ci in range(nc):
            rows = slice(ci * GDN_CHUNK, (ci + 1) * GDN_CHUNK)
            pc, xc = prs[ci * GDN_HEADS:(ci + 1) * GDN_HEADS], xs[ci * GDN_HEADS:(ci + 1) * GDN_HEADS]
            m_ref[rows, :] = jnp.concatenate(mis[ci * GDN_HEADS:(ci + 1) * GDN_HEADS], axis=1)
            sts = [state[h * GDN_DK:(h + 1) * GDN_DK, :] for h in heads]
            for h in heads:
                st_ref[(ci * GDN_HEADS + h) * GDN_DK:(ci * GDN_HEADS + h + 1) * GDN_DK, :] = sts[h]
            w_st = _each(_bdot, [x[:, GDN_DK:] for x in xc], sts)
            q_st = _each(_bdot, [pr["qd"] for pr in pc], sts)
            vns = [x[:, :GDN_DK] - ws for x, ws in zip(xc, w_st)]
            a_vn = _each(_bdot, [pr["a_qk"] for pr in pc], vns)
            k_vn = _each(functools.partial(_bdot, dot=_dot_tn), [pr["kd"] for pr in pc], vns)
            for h in heads:
                o_ref[rows, h * GDN_DK:(h + 1) * GDN_DK] = q_st[h] + a_vn[h]
                state[h * GDN_DK:(h + 1) * GDN_DK, :] = sts[h] * pc[h]["glast_exp"] + k_vn[h]

        z = z_ref[...].astype(F32)
        for h in range(GDN_HEADS):
            sl = slice(h * GDN_DK, (h + 1) * GDN_DK)
            o = o_ref[:, sl]
            on = o * lax.rsqrt(jnp.mean(o * o, axis=-1, keepdims=True) + RMS_EPS)
            hb_ref[:, sl] = (on * ng_ref[...] * _silu(z[:, sl])).astype(BF16)

    col = lambda cb: pl.BlockSpec((t, GDN_QK), lambda i, cb=cb: (i, cb))
    halo = lambda cb: _prev_halo_spec(t, GDN_HALO, GDN_QK, cb)
    vec = _const_spec((1, LANES))
    return pl.pallas_call(
        body, name="gdn_fwd" + tag, grid=(nblk,),
        in_specs=[col(qb), col(kb_), col(vb), halo(qb), halo(kb_), halo(vb), col(zb),
                  _row_spec(t, LANES), _const_spec((GDN_CONV, 3 * GDN_QK)), vec, vec, vec],
        out_specs=[_row_spec(t, GDN_QK), _row_spec(t, GDN_QK),
                   pl.BlockSpec((nc * GDN_HEADS * GDN_DK, GDN_DK), lambda i: (i, 0)),
                   _row_spec(t, GDN_HEADS * GDN_CHUNK)],
        out_shape=[jax.ShapeDtypeStruct((s, GDN_QK), F32), jax.ShapeDtypeStruct((s, GDN_QK), BF16),
                   jax.ShapeDtypeStruct((s // GDN_CHUNK * GDN_HEADS * GDN_DK, GDN_DK), F32),
                   jax.ShapeDtypeStruct((s, GDN_HEADS * GDN_CHUNK), F32)],
        scratch_shapes=[pltpu.VMEM((GDN_HALO + t, 3 * GDN_QK), F32), pltpu.VMEM((t, GDN_QK), F32),
                        pltpu.VMEM((t, GDN_QK), F32), pltpu.VMEM((t, GDN_QK), F32),
                        pltpu.VMEM((t, LANES), F32), pltpu.VMEM((t, LANES), F32),
                        pltpu.VMEM((GDN_HEADS * GDN_DK, GDN_DK), F32)],
        compiler_params=_params(("arbitrary",)),
    )(p, p, p, p, p, p, p, p_ba, conv_w, a_log, dt_bias, norm_g)


def _lane_place(col, lane_idx, shape):
    lane = lax.broadcasted_iota(jnp.int32, shape, 1)
    return jnp.where(lane == lane_idx, col, 0.0)


def _gdn_bwd(dhb, p, p_ba, o, states, minv, conv_w, a_log, dt_bias, norm_g, tag):
    s = p.shape[0]
    t = min(GDN_BLOCK, s)
    nc = t // GDN_CHUNK
    nblk = s // t
    qb, kb_, vb, zb = COL_Q // GDN_QK, COL_Q // GDN_QK + 1, COL_Q // GDN_QK + 2, COL_Z // GDN_QK
    scale = GDN_DK ** -0.5
    c = GDN_CHUNK

    def body(dhb_ref, q_ref, k_ref, v_ref, hq_ref, hk_ref, hv_ref, z_ref, ba_ref, o_ref, st_ref, m_ref,
             cw_ref, alog_ref, dtb_ref, ng_ref,
             dpre_ref, dz_ref, dba_ref, dng_ref, dalog_ref, ddtb_ref,
             ext, qn_s, kn_s, vc_s, beta_s, gam_s, do_s, dqn_s, dkn_s, dvc_s, dgam_s, dbeta_s, dstate):
        i = pl.program_id(0)

        @pl.when(i == 0)
        def _():
            dstate[...] = jnp.zeros_like(dstate)
            dng_ref[...] = jnp.zeros_like(dng_ref)
            dalog_ref[...] = jnp.zeros_like(dalog_ref)
            ddtb_ref[...] = jnp.zeros_like(ddtb_ref)

        pre, act, rq, rk = _gdn_conv_silu_norm(q_ref, k_ref, v_ref, hq_ref, hk_ref, hv_ref, cw_ref, ext, i == nblk - 1)
        for h in range(GDN_HEADS):
            sl = slice(h * GDN_DK, (h + 1) * GDN_DK)
            qn_s[:, sl] = act[:, sl] * (rq[h] * scale)
            kn_s[:, sl] = act[:, GDN_QK + h * GDN_DK:GDN_QK + (h + 1) * GDN_DK] * rk[h]
        vc_s[...] = act[:, 2 * GDN_QK:]
        ba = ba_ref[...]
        beta, g = _gdn_gates(ba, alog_ref, dtb_ref)
        beta_s[...] = beta
        gam_s[...] = jnp.dot(_chunk_cumsum_matrix(t, False), g, preferred_element_type=F32, precision=lax.Precision.HIGHEST)

        z = z_ref[...].astype(F32)
        dhb = dhb_ref[...]
        dng = jnp.zeros((1, GDN_DK), F32)
        for h in range(GDN_HEADS):
            sl = slice(h * GDN_DK, (h + 1) * GDN_DK)
            oh = o_ref[:, sl]
            r = lax.rsqrt(jnp.mean(oh * oh, axis=-1, keepdims=True) + RMS_EPS)
            on = oh * r
            sz = _silu(z[:, sl])
            dyh = dhb[:, sl]
            dng = dng + _colsum(dyh * on * sz)
            dz_ref[:, sl] = (dyh * on * ng_ref[...] * _dsilu(z[:, sl])).astype(BF16)
            don = dyh * ng_ref[...] * sz
            do_s[:, sl] = r * (don - on * jnp.mean(don * on, axis=-1, keepdims=True))
        dng_ref[...] += dng

        heads = range(GDN_HEADS)
        npairs = nc * GDN_HEADS
        tn = functools.partial(_bdot, dot=_dot_tn)
        nt = functools.partial(_bdot, dot=_dot_nt)
        rsum = lambda a: jnp.sum(a, axis=-1, keepdims=True)
        left = lambda a: a[:, :GDN_DK]
        right = lambda a: a[:, GDN_DK:]

        prs = _gdn_prep(_gdn_pair_items(qn_s, kn_s, vc_s, beta_s, gam_s, nc))
        mis = [m_ref[(n // GDN_HEADS) * c:(n // GDN_HEADS + 1) * c, (n % GDN_HEADS) * c:(n % GDN_HEADS + 1) * c] for n in range(npairs)]
        xs = _apply_inverses(mis, [pr["r"] for pr in prs])
        sts = [st_ref[n * GDN_DK:(n + 1) * GDN_DK, :] for n in range(npairs)]
        dos = [do_s[(n // GDN_HEADS) * c:(n // GDN_HEADS + 1) * c, (n % GDN_HEADS) * GDN_DK:(n % GDN_HEADS + 1) * GDN_DK] for n in range(npairs)]
        w_st = _each(_bdot, [right(x) for x in xs], sts)
        vns = [left(x) - ws for x, ws in zip(xs, w_st)]
        at_do = _each(tn, [pr["a_qk"] for pr in prs], dos)
        dqds = _each(nt, dos, sts)
        d_as = [jnp.where(pr["causal"], a, 0.0) for pr, a in zip(prs, _each(nt, dos, vns))]
        qt_do = _each(tn, [pr["qd"] for pr in prs], dos)

        dvns, dkds, ds_st = [None] * npairs, [None] * npairs, [None] * npairs
        for ci in reversed(range(nc)):
            ids = [ci * GDN_HEADS + h for h in heads]
            dss = [dstate[h * GDN_DK:(h + 1) * GDN_DK, :] for h in heads]
            kd_ds = _each(_bdot, [prs[n]["kd"] for n in ids], dss)
            vn_ds = _each(nt, [vns[n] for n in ids], dss)
            for h, n in enumerate(ids):
                dvns[n] = kd_ds[h] + at_do[n]
                dkds[n] = vn_ds[h]
                ds_st[n] = jnp.sum(rsum(dss[h] * sts[n]), axis=0, keepdims=True)
            wt_dvn = _each(tn, [right(xs[n]) for n in ids], [dvns[n] for n in ids])
            for h, n in enumerate(ids):
                dstate[h * GDN_DK:(h + 1) * GDN_DK, :] = dss[h] * prs[n]["glast_exp"] + qt_do[n] - wt_dvn[h]

        dws = [-a for a in _each(nt, dvns, sts)]
        d_rs = _apply_inverses(mis, [jnp.concatenate([dvn, dw], axis=1) for dvn, dw in zip(dvns, dws)], _dot_tn)
        d_ls = [jnp.where(pr["strict"], -a, 0.0) for pr, a in zip(prs, _each(nt, d_rs, xs))]
        d_l_kds = [d_l * pr["kk"] * pr["decay"] for d_l, pr in zip(d_ls, prs)]
        dkks = [d_l * pr["beta_c"] * pr["decay"] for d_l, pr in zip(d_ls, prs)]
        dqks = [d_a * pr["decay"] for d_a, pr in zip(d_as, prs)]
        ks, qs = [pr["k"] for pr in prs], [pr["q"] for pr in prs]
        dk1, dk2, dk3 = _each(_bdot, dkks, ks), _each(tn, dkks, ks), _each(tn, dqks, qs)
        dq1 = _each(_bdot, dqks, ks)
        rowi = lax.broadcasted_iota(jnp.int32, (c, 1), 0)
        for ci in range(nc):
            rows = slice(ci * c, (ci + 1) * c)
            dgam_blk = jnp.zeros((c, LANES), F32)
            dbeta_blk = jnp.zeros((c, LANES), F32)
            for h in heads:
                n = ci * GDN_HEADS + h
                sl = slice(h * GDN_DK, (h + 1) * GDN_DK)
                pr, d_r = prs[n], d_rs[n]
                d_ru, d_rw = left(d_r), right(d_r)
                gmat = pr["beta_c"] * d_l_kds[n] + d_as[n] * pr["a_qk"]
                dkd_kd = rsum(dkds[n] * pr["kd"])
                dgam = rsum(gmat) - rsum(gmat.T) + rsum(d_rw * right(pr["r"])) + rsum(dqds[n] * pr["qd"]) - dkd_kd
                dglast = jnp.sum(dkd_kd, axis=0, keepdims=True) + ds_st[n] * pr["glast_exp"]
                dgam = dgam + jnp.where(rowi == c - 1, dglast, 0.0)
                dbeta = rsum(d_l_kds[n]) + rsum(d_ru * pr["v"]) + rsum(d_rw * pr["k"]) * pr["gm"]
                dvc_s[rows, sl] = pr["beta_c"] * d_ru
                dkn_s[rows, sl] = dk1[n] + dk2[n] + dk3[n] + d_rw * (pr["beta_c"] * pr["gm"]) + dkds[n] * pr["elast"]
                dqn_s[rows, sl] = dq1[n] + dqds[n] * pr["gm"]
                dgam_blk = dgam_blk + _lane_place(dgam, GDN_HEADS + h, (c, LANES))
                dbeta_blk = dbeta_blk + _lane_place(dbeta, h, (c, LANES))
            dgam_s[rows, :] = dgam_blk
            dbeta_s[rows, :] = dbeta_blk

        dg = jnp.dot(_chunk_cumsum_matrix(t, True), dgam_s[...], preferred_element_type=F32, precision=lax.Precision.HIGHEST)
        lane = lax.broadcasted_iota(jnp.int32, (t, LANES), 1)
        g_lanes = (lane >= GDN_HEADS) & (lane < 2 * GDN_HEADS)
        da_logit = jnp.where(g_lanes, dg * (-jnp.exp(alog_ref[...])) * _sigmoid(ba + dtb_ref[...]), 0.0)
        db_logit = jnp.where(lane < GDN_HEADS, dbeta_s[...] * beta * (1.0 - beta), 0.0)
        dba_ref[...] = (da_logit + db_logit).astype(BF16)
        dalog_ref[...] += _colsum(dg * g)
        ddtb_ref[...] += _colsum(da_logit)

        dact = []
        for n, (dn_s, rr, sc) in enumerate(((dqn_s, rq, scale), (dkn_s, rk, 1.0))):
            for h in range(GDN_HEADS):
                sl = slice(h * GDN_DK, (h + 1) * GDN_DK)
                y = act[:, n * GDN_QK + h * GDN_DK:n * GDN_QK + (h + 1) * GDN_DK] * rr[h]
                dy = dn_s[:, sl] * sc
                dact.append(rr[h] * (dy - y * jnp.sum(dy * y, axis=-1, keepdims=True)))
        dact.append(dvc_s[...])
        dpre_ref[...] = jnp.concatenate(dact, axis=1) * _dsilu(pre)

    rb = lambda i: nblk - 1 - i
    per = t // GDN_HALO
    col = lambda cb, wd=GDN_QK: pl.BlockSpec((t, wd), lambda i, cb=cb: (rb(i), cb))
    halo = lambda cb: pl.BlockSpec((GDN_HALO, GDN_QK), lambda i, cb=cb: (jnp.maximum(rb(i) * per - 1, 0), cb))
    vec = _const_spec((1, LANES))
    return pl.pallas_call(
        body, name="gdn_bwd" + tag, grid=(nblk,),
        in_specs=[col(0), col(qb), col(kb_), col(vb), halo(qb), halo(kb_), halo(vb), col(zb),
                  pl.BlockSpec((t, LANES), lambda i: (rb(i), 0)), col(0),
                  pl.BlockSpec((nc * GDN_HEADS * GDN_DK, GDN_DK), lambda i: (rb(i), 0)),
                  pl.BlockSpec((t, GDN_HEADS * c), lambda i: (rb(i), 0)),
                  _const_spec((GDN_CONV, 3 * GDN_QK)), vec, vec, vec],
        out_specs=[pl.BlockSpec((t, 3 * GDN_QK), lambda i: (rb(i), 0)), col(0), pl.BlockSpec((t, LANES), lambda i: (rb(i), 0)),
                   vec, vec, vec],
        out_shape=[jax.ShapeDtypeStruct((s, 3 * GDN_QK), F32), jax.ShapeDtypeStruct((s, GDN_QK), BF16),
                   jax.ShapeDtypeStruct((s, LANES), BF16),
                   jax.ShapeDtypeStruct((1, LANES), F32), jax.ShapeDtypeStruct((1, LANES), F32), jax.ShapeDtypeStruct((1, LANES), F32)],
        scratch_shapes=[pltpu.VMEM((GDN_HALO + t, 3 * GDN_QK), F32)] + [pltpu.VMEM((t, GDN_QK), F32)] * 3
                       + [pltpu.VMEM((t, LANES), F32)] * 2 + [pltpu.VMEM((t, GDN_QK), F32)] * 4
                       + [pltpu.VMEM((t, LANES), F32)] * 2 + [pltpu.VMEM((GDN_HEADS * GDN_DK, GDN_DK), F32)],
        compiler_params=_params(("arbitrary",)),
    )(dhb, p, p, p, p, p, p, p, p_ba, o, states, minv, conv_w, a_log, dt_bias, norm_g)


def _zero_at_first_step(*refs):
    @pl.when(pl.program_id(0) == 0)
    def _():
        for r in refs:
            r[...] = jnp.zeros_like(r)


def _ln_bwd_call(dy, x_in, g, *, name):
    s, d = dy.shape
    t = min(ROW_TILE, s)

    def body(dy_ref, x_ref, g_ref, dx_ref, dxbf_ref, dg_ref, db_ref, ds_ref):
        _zero_at_first_step(dg_ref, db_ref, ds_ref)
        dy = dy_ref[...]
        xhat, rstd = _ln_stats(x_ref[...])
        dx = _ln_bwd(dy, xhat, rstd, g_ref[...])
        dx_ref[...] = dx
        dxbf_ref[...] = dx.astype(BF16)
        dg_ref[...] += _colsum(dy * xhat)
        db_ref[...] += _colsum(dy)
        ds_ref[...] += _colsum(dx)

    vec = _const_spec((1, d))
    return pl.pallas_call(
        body, name=name, grid=(s // t,),
        in_specs=[_row_spec(t, d), _row_spec(t, d), vec],
        out_specs=[_row_spec(t, d), _row_spec(t, d), vec, vec, vec],
        out_shape=[jax.ShapeDtypeStruct((s, d), F32), jax.ShapeDtypeStruct((s, d), BF16)] + [jax.ShapeDtypeStruct((1, d), F32)] * 3,
        compiler_params=_params(("arbitrary",)),
    )(dy, x_in, g)


def _ff2_bwd(dr2_bf, w_ff2, hpre, tag):
    s = dr2_bf.shape[0]
    tm = min(1024, s)
    tn = 1024

    def body(d_ref, w_ref, hp_ref, o_ref, db_ref):
        @pl.when(pl.program_id(1) == 0)
        def _():
            db_ref[...] = jnp.zeros_like(db_ref)

        dh = _dot_nt(d_ref[...], w_ref[...]) * (2.0 * jnp.maximum(hp_ref[...].astype(F32), 0.0))
        o_ref[...] = dh.astype(BF16)
        db_ref[...] += _colsum(dh)

    return pl.pallas_call(
        body, name="ff2_bwd" + tag, grid=(D_FF // tn, s // tm),
        in_specs=[pl.BlockSpec((tm, D_MODEL), lambda j, i: (i, 0)), pl.BlockSpec((tn, D_MODEL), lambda j, i: (j, 0)),
                  pl.BlockSpec((tm, tn), lambda j, i: (i, j))],
        out_specs=[pl.BlockSpec((tm, tn), lambda j, i: (i, j)), pl.BlockSpec((1, tn), lambda j, i: (0, j))],
        out_shape=[jax.ShapeDtypeStruct((s, D_FF), BF16), jax.ShapeDtypeStruct((1, D_FF), F32)],
        compiler_params=_params(("parallel", "arbitrary")),
    )(dr2_bf, w_ff2, hpre)


def _ff1_bwd_ln(dhpre_bf, w_ff1, dr2, r1, ln1_g, tag):
    s = dr2.shape[0]
    t = min(ROW_TILE, s)

    def body(dh_ref, w_ref, dr2_ref, r1_ref, g_ref, dr_ref, drbf_ref, dg_ref, db_ref):
        _zero_at_first_step(dg_ref, db_ref)
        dx1 = DN_ALPHA * dr2_ref[...] + _dot_nt(dh_ref[...], w_ref[...])
        xhat, rstd = _ln_stats(r1_ref[...])
        dr = _ln_bwd(dx1, xhat, rstd, g_ref[...])
        dr_ref[...] = dr
        drbf_ref[...] = dr.astype(BF16)
        dg_ref[...] += _colsum(dx1 * xhat)
        db_ref[...] += _colsum(dx1)

    vec = _const_spec((1, D_MODEL))
    return pl.pallas_call(
        body, name="ff1_bwd_ln" + tag, grid=(s // t,),
        in_specs=[_row_spec(t, D_FF), _const_spec((D_MODEL, D_FF)), _row_spec(t, D_MODEL), _row_spec(t, D_MODEL), vec],
        out_specs=[_row_spec(t, D_MODEL), _row_spec(t, D_MODEL), vec, vec],
        out_shape=[jax.ShapeDtypeStruct((s, D_MODEL), F32), jax.ShapeDtypeStruct((s, D_MODEL), BF16),
                   jax.ShapeDtypeStruct((1, D_MODEL), F32), jax.ShapeDtypeStruct((1, D_MODEL), F32)],
        compiler_params=_params(("arbitrary",)),
    )(dhpre_bf, w_ff1, dr2, r1, ln1_g)


def _merge_bwd(dr1_bf, w_o, y3, p, b_gate, w_pa, w_pb, w_pc, tag):
    s = p.shape[0]
    t = min(ROW_TILE, s)
    gb = COL_GATE // D_MODEL

    def body(dr_ref, wo_ref, y_ref, ga_ref, gb_ref, gc_ref, bg_ref, wa_ref, wb_ref, wc_ref,
             dgate_ref, dy_ref, dha_ref, dhb_ref, dhc_ref, dbg_ref):
        _zero_at_first_step(dbg_ref)
        dm = _dot_nt(dr_ref[...], wo_ref[...])
        for idx, (g_ref, w_ref, dh_ref) in enumerate(((ga_ref, wa_ref, dha_ref), (gb_ref, wb_ref, dhb_ref), (gc_ref, wc_ref, dhc_ref))):
            sl = slice(idx * D_MODEL, (idx + 1) * D_MODEL)
            sg = _sigmoid(g_ref[...].astype(F32) + bg_ref[:, sl])
            dgate = dm * y_ref[:, sl].astype(F32) * sg * (1.0 - sg)
            dgate_ref[:, sl] = dgate.astype(BF16)
            dbg_ref[:, sl] += _colsum(dgate)
            dy = (dm * sg).astype(BF16)
            dy_ref[:, sl] = dy
            dh_ref[...] = _dot_nt(dy, w_ref[...])

    hspec = _row_spec(t, CONV_DIM)
    wspec = _const_spec((CONV_DIM, D_MODEL))
    return pl.pallas_call(
        body, name="merge_bwd" + tag, grid=(s // t,),
        in_specs=[_row_spec(t, D_MODEL), _const_spec((D_MODEL, D_MODEL)), _row_spec(t, 3 * D_MODEL),
                  _row_spec(t, D_MODEL, gb), _row_spec(t, D_MODEL, gb + 1), _row_spec(t, D_MODEL, gb + 2),
                  _const_spec((1, 3 * D_MODEL)), wspec, wspec, wspec],
        out_specs=[_row_spec(t, 3 * D_MODEL), _row_spec(t, 3 * D_MODEL), hspec, hspec, hspec, _const_spec((1, 3 * D_MODEL))],
        out_shape=[jax.ShapeDtypeStruct((s, 3 * D_MODEL), BF16), jax.ShapeDtypeStruct((s, 3 * D_MODEL), BF16)]
                  + [jax.ShapeDtypeStruct((s, CONV_DIM), F32)] * 3 + [jax.ShapeDtypeStruct((1, 3 * D_MODEL), F32)],
        compiler_params=_params(("arbitrary",)),
    )(dr1_bf, w_o, y3, p, p, p, b_gate, w_pa, w_pb, w_pc)


def _conv_a_bwd_pre(dha, c, ln_g, ln_b, tag):
    s = c.shape[0]
    t = min(ROW_TILE, s)

    def body(dh_ref, c_ref, g_ref, b_ref, dc_ref, dg_ref, db_ref, ds_ref):
        _zero_at_first_step(dg_ref, db_ref, ds_ref)
        xhat, rstd = _ln_stats(c_ref[...])
        n = xhat * g_ref[...] + b_ref[...]
        dn = dh_ref[...] * _dsilu(n)
        dc = _ln_bwd(dn, xhat, rstd, g_ref[...])
        dc_ref[...] = dc
        dg_ref[...] += _colsum(dn * xhat)
        db_ref[...] += _colsum(dn)
        ds_ref[...] += _colsum(dc)

    vec = _const_spec((1, CONV_DIM))
    return pl.pallas_call(
        body, name="conv_a_bwd_pre" + tag, grid=(s // t,),
        in_specs=[_row_spec(t, CONV_DIM), _row_spec(t, CONV_DIM), vec, vec],
        out_specs=[_row_spec(t, CONV_DIM), vec, vec, vec],
        out_shape=[jax.ShapeDtypeStruct((s, CONV_DIM), F32)] + [jax.ShapeDtypeStruct((1, CONV_DIM), F32)] * 3,
        compiler_params=_params(("arbitrary",)),
    )(dha, c, ln_g, ln_b)


def _dwconv_bwd(dy, x, w, *, width, halo, x_col_block, glu_p, name):
    s, ctot = dy.shape
    ct = CONV_DIM
    t = min(ROW_TILE, s)
    nblk = s // t
    glu = glu_p is not None

    def body(*refs):
        if glu:
            dy_ref, dyh_ref, x_ref, xh_ref, w_ref, a_ref, dx_ref, dw_ref, dye, xe, dwacc = refs
        else:
            dy_ref, dyh_ref, x_ref, xh_ref, w_ref, dx_ref, dw_ref, dye, xe, dwacc = refs
        i = pl.program_id(1)

        @pl.when(i == 0)
        def _():
            dw_ref[...] = jnp.zeros_like(dw_ref)

        dye[0:t, :] = dy_ref[...]
        dye[t:t + halo, :] = jnp.where(i == nblk - 1, 0.0, dyh_ref[...])
        xe[0:halo, :] = jnp.where(i == 0, 0.0, xh_ref[...].astype(F32))
        xe[halo:halo + t, :] = x_ref[...].astype(F32)
        dwacc[...] = jnp.zeros_like(dwacc)
        dx_groups = _tap_groups([width - 1 - j for j in range(width)])
        dw_groups = _tap_groups([halo - (width - 1) + j for j in range(width)])
        for r0 in range(0, t, CONV_ROWS):
            rows = slice(r0, r0 + CONV_ROWS)
            dx = _tap_sum(dye, w_ref, dx_groups, r0)
            if glu:
                a = a_ref[rows, :].astype(F32)
                a1 = a[:, :ct]
                sg = _sigmoid(a[:, ct:])
                dx_ref[rows, :ct] = (dx * sg).astype(BF16)
                dx_ref[rows, ct:] = (dx * a1 * sg * (1.0 - sg)).astype(BF16)
            else:
                dx_ref[rows, :] = dx.astype(BF16)
            dyt = dy_ref[rows, :]
            for (_, _, _, taps), win in zip(dw_groups, _tap_windows(xe, dw_groups, r0)):
                for j, a in taps:
                    prod = dyt * win[a:a + CONV_ROWS]
                    part = prod[0:8]
                    for q in range(8, CONV_ROWS, 8):
                        part = part + prod[q:q + 8]
                    dwacc[8 * j:8 * j + 8, :] += part
        for j in range(width):
            dw_ref[j:j + 1, :] += _colsum(dwacc[8 * j:8 * j + 8, :])

    per = t // halo
    in_specs = [pl.BlockSpec((t, ct), lambda cb, i: (i, cb)),
                pl.BlockSpec((halo, ct), lambda cb, i: (jnp.minimum((i + 1) * per, nblk * per - 1), cb)),
                pl.BlockSpec((t, ct), lambda cb, i: (i, cb + x_col_block)),
                pl.BlockSpec((halo, ct), lambda cb, i: (jnp.maximum(i * per - 1, 0), cb + x_col_block)),
                pl.BlockSpec((width, ct), lambda cb, i: (0, cb))]
    args = [dy, dy, x, x, w]
    out_cols = ctot
    if glu:
        in_specs.append(pl.BlockSpec((t, 2 * ct), lambda cb, i: (i, COL_A // (2 * ct))))
        args.append(glu_p)
        out_cols = 2 * ct
    ocol = 2 * ct if glu else ct
    return pl.pallas_call(
        body, name=name, grid=(ctot // ct, nblk), in_specs=in_specs,
        out_specs=[pl.BlockSpec((t, ocol), lambda cb, i: (i, cb)), pl.BlockSpec((width, ct), lambda cb, i: (0, cb))],
        out_shape=[jax.ShapeDtypeStruct((s, out_cols), BF16), jax.ShapeDtypeStruct((width, ctot), F32)],
        scratch_shapes=[pltpu.VMEM((t + halo, ct), F32), pltpu.VMEM((halo + t, ct), F32), pltpu.VMEM((8 * width, ct), F32)],
        compiler_params=_params(("parallel", "arbitrary")),
    )(*args)


def _sgu_bwd(dhc, p, ln_g, ln_b, w_s, b_s_t, tag):
    s = p.shape[0]
    t = min(ROW_TILE, s)
    cs = SGU_CHUNK

    def body(dh_ref, uv_ref, g_ref, b_ref, ws_ref, bst_ref, duv_ref, dg_ref, db_ref, dws_ref, dbs_ref):
        _zero_at_first_step(dg_ref, db_ref, dws_ref, dbs_ref)
        uv = uv_ref[...].astype(F32)
        u_raw, v_raw = uv[:, :SGU_DIM], uv[:, SGU_DIM:]
        u = _gelu(u_raw)
        xhat, rstd = _ln_stats(_gelu(v_raw))
        vn = xhat * g_ref[...] + b_ref[...]
        mixed = _sgu_mix(vn, ws_ref, bst_ref, t)
        dh = dh_ref[...]
        duv_ref[:, :SGU_DIM] = (dh * mixed * _dgelu(u_raw)).astype(BF16)
        dmix = dh * u
        row = lax.broadcasted_iota(jnp.int32, (cs, cs), 0)
        col = lax.broadcasted_iota(jnp.int32, (cs, cs), 1)
        dbs = jnp.zeros((cs, LANES), F32)
        chunks = []
        for g in range(SGU_GROUPS):
            wg = jnp.where(row >= col, ws_ref[g], 0.0).astype(BF16)
            dwg = jnp.zeros((cs, cs), F32)
            parts = []
            for ci in range(t // cs):
                rs = slice(ci * cs, (ci + 1) * cs)
                cl = slice(g * SGU_GROUP_DIM, (g + 1) * SGU_GROUP_DIM)
                dm = dmix[rs, cl]
                dmb = dm.astype(BF16)
                dwg = dwg + _dot_nt(dmb, vn[rs, cl].astype(BF16))
                dbs = dbs + _lane_place(jnp.sum(dm, axis=-1, keepdims=True), g, (cs, LANES))
                parts.append(_dot_tn(wg, dmb))
            dws_ref[g] += jnp.where(row >= col, dwg, 0.0)
            chunks.append(jnp.concatenate(parts, axis=0))
        dbs_ref[...] += dbs
        dvn = jnp.concatenate(chunks, axis=1)
        dvv = _ln_bwd(dvn, xhat, rstd, g_ref[...])
        duv_ref[:, SGU_DIM:] = (dvv * _dgelu(v_raw)).astype(BF16)
        dg_ref[...] += _colsum(dvn * xhat)
        db_ref[...] += _colsum(dvn)

    vec = _const_spec((1, SGU_DIM))
    wss = _const_spec((SGU_GROUPS, cs, cs))
    return pl.pallas_call(
        body, name="sgu_bwd" + tag, grid=(s // t,),
        in_specs=[_row_spec(t, SGU_DIM), _row_spec(t, 2 * SGU_DIM, COL_UV // (2 * SGU_DIM)), vec, vec, wss, _const_spec((cs, LANES))],
        out_specs=[_row_spec(t, 2 * SGU_DIM), vec, vec, wss, _const_spec((cs, LANES))],
        out_shape=[jax.ShapeDtypeStruct((s, 2 * SGU_DIM), BF16), jax.ShapeDtypeStruct((1, SGU_DIM), F32),
                   jax.ShapeDtypeStruct((1, SGU_DIM), F32), jax.ShapeDtypeStruct((SGU_GROUPS, cs, cs), F32),
                   jax.ShapeDtypeStruct((cs, LANES), F32)],
        compiler_params=_params(("arbitrary",)),
    )(dhc, p, ln_g, ln_b, w_s, b_s_t)


def _reorder_proj_cols(w):
    pad = jnp.zeros(w.shape[:-1] + (P_COLS - PROJ_COLS,), w.dtype)
    return jnp.concatenate([w[..., :3072], w[..., 3080:PROJ_COLS], w[..., 3072:3080], pad], axis=-1)


def _restore_proj_cols(g):
    return jnp.concatenate([g[..., :3072], g[..., COL_BA:COL_BA + 8], g[..., 3072:COL_BA]], axis=-1)


def _pad_lanes(v, offset):
    return jnp.pad(v, (offset, LANES - offset - v.shape[0]))[None, :]


def _proj_weights(w_in_l):
    w_all = _reorder_proj_cols(w_in_l)
    return dict(w_all=w_all, w_ba=w_all[:, COL_BA:])


def _rest_weights(l, full, rep):
    row = lambda v: v[l][None, :]
    return dict(
        conv_w=full["conv_dw_w"], conv_b=row(rep["conv_dw_b"]), conv_ln_g=row(rep["conv_ln_g"]), conv_ln_b=row(rep["conv_ln_b"]),
        w_pa=full["w_pa"], w_pb=full["w_pb"], w_pc=full["w_pc"],
        gdn_cw=jnp.concatenate([full["gdn_conv_q"], full["gdn_conv_k"], full["gdn_conv_v"]], axis=-1),
        a_log=_pad_lanes(rep["gdn_a_log"][l], GDN_HEADS), dt_bias=_pad_lanes(rep["gdn_dt_bias"][l], GDN_HEADS),
        norm_g=row(rep["gdn_norm_g"]),
        sgu_ln_g=row(rep["sgu_ln_g"]), sgu_ln_b=row(rep["sgu_ln_b"]), sgu_w_s=rep["sgu_w_s"][l],
        sgu_b_s_t=jnp.pad(rep["sgu_b_s"][l].T, ((0, 0), (0, LANES - SGU_GROUPS))),
        b_gate=row(rep["b_gate"]),
        w_o=full["w_o"], ln1_g=row(rep["ln1_g"]), ln1_b=row(rep["ln1_b"]),
        w_ff1=full["w_ff1"], b_ff1=row(rep["b_ff1"]),
        w_ff2=full["w_ff2"], b_ff2=row(rep["b_ff2"]),
        ln2_g=row(rep["ln2_g"]), ln2_b=row(rep["ln2_b"]),
    )


def _layer_fwd(x, x_bf, w_proj, rest_of, tag, carry=None):
    carry = carry or {}
    got = {}
    s = x.shape[0]
    p = _matmul(x_bf, w_proj["w_all"], name="proj_fwd" + tag, tm=min(1024, s), tn=P_TILE, tk=D_MODEL, out_dtype=BF16,
                exchange=carry.get("proj_fwd"))
    if "proj_fwd" in carry:
        p, got["proj_fwd"] = p
    p_ba = _matmul(x_bf, w_proj["w_ba"], name="proj_ba_fwd" + tag, tm=min(2048, s), tn=LANES, tk=D_MODEL)
    w = dict(w_proj, **rest_of(got.get("proj_fwd")))
    h_glu, c, ha = _conv_a_fwd(p, w["conv_w"], w["conv_b"], w["conv_ln_g"], w["conv_ln_b"], tag)
    o, hb, states, minv = _gdn_fwd(p, p_ba, w["gdn_cw"], w["a_log"], w["dt_bias"], w["norm_g"], tag)
    hc = _sgu_fwd(p, w["sgu_ln_g"], w["sgu_ln_b"], w["sgu_w_s"], w["sgu_b_s_t"], tag)
    y3, merged = _merge_fwd(p, ha, hb, hc, w["w_pa"], w["w_pb"], w["w_pc"], w["b_gate"], tag)
    r1, x1, x1_bf, _ = _matmul_res_ln(merged, w["w_o"], jnp.zeros((1, D_MODEL), F32), x, w["ln1_g"], w["ln1_b"], name="o_res_ln" + tag)
    hpre, h_bf, got["ff1_fwd"] = _ff1_fwd(x1_bf, w["w_ff1"], w["b_ff1"], tag, exchange=carry.get("ff1_fwd"))
    r2, x2, x2_bf, got["ff2_res_ln"] = _matmul_res_ln(h_bf, w["w_ff2"], w["b_ff2"], x1, w["ln2_g"], w["ln2_b"],
                                                      name="ff2_res_ln" + tag, exchange=carry.get("ff2_res_ln"))
    saved = dict(x=x, x_bf=x_bf, p=p, p_ba=p_ba, h_glu=h_glu, c=c, ha=ha, o=o, hb=hb, states=states, minv=minv, hc=hc, y3=y3,
                 merged=merged, r1=r1, x1=x1, x1_bf=x1_bf, hpre=hpre, h_bf=h_bf, r2=r2)
    return x2, x2_bf, saved, w, got


def _layer_bwd(dx2, w, sv, tag, carry=None):
    carry = carry or {}
    got = {}
    g = {}

    def behind(key, call):
        if key not in carry:
            return call(None)
        out, got[key] = call(carry[key](g))
        return out

    s = dx2.shape[0]
    ts = min(1024, s)
    p = sv["p"]
    dr2, dr2_bf, d_ln2_g, d_ln2_b, d_b_ff2 = _ln_bwd_call(dx2, sv["r2"], w["ln2_g"], name="ln2_bwd" + tag)
    dhpre_bf, d_b_ff1 = _ff2_bwd(dr2_bf, w["w_ff2"], sv["hpre"], tag)
    g["w_ff2"] = behind("dw_ff2", lambda ex: _matmul_tn(sv["h_bf"], dr2_bf, name="dw_ff2" + tag, ka=D_FF, tka=1024, tn=1024,
                                                        ts=ts, exchange=ex))
    g["w_ff1"] = behind("dw_ff1", lambda ex: _matmul_tn(sv["x1_bf"], dhpre_bf, name="dw_ff1" + tag, ka=D_MODEL, tka=1024, tn=1024,
                                                        ts=ts, exchange=ex))
    dr1, dr1_bf, d_ln1_g, d_ln1_b = _ff1_bwd_ln(dhpre_bf, w["w_ff1"], dr2, sv["r1"], w["ln1_g"], tag)
    g["w_o"] = behind("dw_o", lambda ex: _matmul_tn(sv["merged"], dr1_bf, name="dw_o" + tag, ka=D_MODEL, tka=1024, tn=1024,
                                                    ts=ts, exchange=ex))
    dgate_bf, dy3_bf, dha, dhb, dhc, d_b_gate = _merge_bwd(dr1_bf, w["w_o"], sv["y3"], p, w["b_gate"],
                                                          w["w_pa"], w["w_pb"], w["w_pc"], tag)
    for n, (name, h) in enumerate((("w_pa", sv["ha"]), ("w_pb", sv["hb"]), ("w_pc", sv["hc"]))):
        g[name] = _matmul_tn(h, dy3_bf, name=f"dw_p{n}" + tag, ka=CONV_DIM, tka=CONV_DIM, tn=1024, ts=ts, n=D_MODEL, b_col_block=n)
    dc, d_conv_ln_g, d_conv_ln_b, d_conv_b = _conv_a_bwd_pre(dha, sv["c"], w["conv_ln_g"], w["conv_ln_b"], tag)
    da_bf, d_conv_w = _dwconv_bwd(dc, sv["h_glu"], w["conv_w"], width=CONV_WIDTH, halo=CONV_HALO, x_col_block=0,
                                  glu_p=p, name="conv_a_bwd" + tag)
    duv_bf, d_sgu_ln_g, d_sgu_ln_b, d_sgu_w_s, d_sgu_b_s_t = _sgu_bwd(dhc, p, w["sgu_ln_g"], w["sgu_ln_b"], w["sgu_w_s"], w["sgu_b_s_t"], tag)
    dpre, dz_bf, dba_bf, d_norm_g, d_a_log, d_dt_bias = _gdn_bwd(dhb, p, sv["p_ba"], sv["o"], sv["states"], sv["minv"], w["gdn_cw"],
                                                                 w["a_log"], w["dt_bias"], w["norm_g"], tag)
    dqkv_bf, d_gdn_cw = _dwconv_bwd(dpre, p, w["gdn_cw"], width=GDN_CONV, halo=GDN_HALO, x_col_block=COL_Q // CONV_DIM,
                                    glu_p=None, name="gdn_conv_bwd" + tag)
    dp_bf = jnp.concatenate([da_bf, dqkv_bf, dz_bf, duv_bf, dgate_bf, dba_bf], axis=1)
    d_w_all = behind("dw_proj", lambda ex: _matmul_tn(sv["x_bf"], dp_bf, name="dw_proj" + tag, ka=D_MODEL, tka=1024, tn=P_TILE,
                                                      ts=ts, exchange=ex))
    g["w_in"] = _restore_proj_cols(d_w_all)
    dx = behind("proj_bwd", lambda ex: _matmul(dp_bf, w["w_all"], name="proj_bwd" + tag, tm=min(1024, s), tn=D_MODEL, tk=P_TILE,
                                               add=dr1, add_scale=DN_ALPHA, w_is_nk=True, exchange=ex))
    g.update(
        b_gate=d_b_gate[0], conv_dw_w=d_conv_w, conv_dw_b=d_conv_b[0], conv_ln_g=d_conv_ln_g[0], conv_ln_b=d_conv_ln_b[0],
        gdn_conv_q=d_gdn_cw[:, :GDN_QK], gdn_conv_k=d_gdn_cw[:, GDN_QK:2 * GDN_QK], gdn_conv_v=d_gdn_cw[:, 2 * GDN_QK:],
        gdn_a_log=d_a_log[0, GDN_HEADS:2 * GDN_HEADS], gdn_dt_bias=d_dt_bias[0, GDN_HEADS:2 * GDN_HEADS], gdn_norm_g=d_norm_g[0],
        sgu_ln_g=d_sgu_ln_g[0], sgu_ln_b=d_sgu_ln_b[0], sgu_w_s=d_sgu_w_s, sgu_b_s=d_sgu_b_s_t[:, :SGU_GROUPS].T,
        ln1_g=d_ln1_g[0], ln1_b=d_ln1_b[0], b_ff1=d_b_ff1[0], b_ff2=d_b_ff2[0], ln2_g=d_ln2_g[0], ln2_b=d_ln2_b[0],
    )
    return dx, g, got


MESH_AXES = ("x", "y", "c")


def _exchange(arrays, scatter, *, name):
    n = len(arrays)
    ex = _Exchange(arrays, scatter)

    def body(*refs):
        ins, outs, sems = refs[:n], refs[n:2 * n], refs[2 * n:]
        _exchange_start(ex, ins, outs, sems)
        _exchange_wait(ex, ins, outs, sems)

    return pl.pallas_call(
        body, name=name, in_specs=ex.in_specs(), out_specs=ex.in_specs(), out_shape=ex.out_shapes(),
        scratch_shapes=ex.scratch(),
    )(*arrays)


class _Exchange:
    def __init__(self, arrays, scatter):
        self.arrays = list(arrays)
        self.scatter = list(scatter)
        self.n = len(self.arrays)

    def in_specs(self):
        return [pl.BlockSpec(memory_space=pl.ANY)] * self.n

    def out_shapes(self):
        return [jax.ShapeDtypeStruct(a.shape if s else (N_DEV,) + a.shape, a.dtype) for a, s in zip(self.arrays, self.scatter)]

    def scratch(self):
        return [pltpu.SemaphoreType.DMA((self.n, N_DEV - 1)), pltpu.SemaphoreType.DMA((self.n, N_DEV - 1)),
                pltpu.SemaphoreType.DMA((self.n,))]


def _exchange_copies(ex, ins, outs, sems, with_arrivals):
    send_sems, recv_sems, local_sems = sems
    x, y, c = lax.axis_index("x"), lax.axis_index("y"), lax.axis_index("c")
    me = 4 * x + 2 * y + c

    def slot(a, d):
        return ins[a].at[d] if ex.scatter[a] else ins[a]

    local = [pltpu.make_async_copy(slot(a, me), outs[a].at[me], local_sems.at[a]) for a in range(ex.n)]
    remote = []
    for k in range(1, N_DEV):
        px = 1 - x if k & 4 else x
        py = 1 - y if k & 2 else y
        pc = 1 - c if k & 1 else c
        peer = 4 * px + 2 * py + pc
        for a in range(ex.n):
            send = pltpu.make_async_remote_copy(
                src_ref=slot(a, peer), dst_ref=outs[a].at[me], send_sem=send_sems.at[a, k - 1],
                recv_sem=recv_sems.at[a, k - 1], device_id=(px, py, pc), device_id_type=pl.DeviceIdType.MESH)
            arrival = pltpu.make_async_remote_copy(
                src_ref=slot(a, peer), dst_ref=outs[a].at[peer], send_sem=send_sems.at[a, k - 1],
                recv_sem=recv_sems.at[a, k - 1], device_id=(px, py, pc), device_id_type=pl.DeviceIdType.MESH) if with_arrivals else None
            remote.append((send, arrival))
    return local, remote


def _exchange_start(ex, ins, outs, sems):
    local, remote = _exchange_copies(ex, ins, outs, sems, False)
    for cp in local:
        cp.start()
    for send, _ in remote:
        send.start()


def _exchange_wait(ex, ins, outs, sems):
    local, remote = _exchange_copies(ex, ins, outs, sems, True)
    for _, arrival in remote:
        arrival.wait_recv()
    for send, _ in remote:
        send.wait_send()
    for cp in local:
        cp.wait()


def _adamw(w, m, v, g_parts, *, name):
    r, c = w.shape
    tr = 256 if r % 256 == 0 else r
    bc1 = 1.0 - ADAM_B1 ** ADAM_STEP
    bc2 = 1.0 - ADAM_B2 ** ADAM_STEP

    def body(w_ref, m_ref, v_ref, gp_ref, g_ref, d_ref, nm_ref, nv_ref):
        g = gp_ref[0].astype(F32)
        for d in range(1, N_DEV):
            g = g + gp_ref[d].astype(F32)
        nm = ADAM_B1 * m_ref[...] + (1.0 - ADAM_B1) * g
        nv = ADAM_B2 * v_ref[...] + (1.0 - ADAM_B2) * (g * g)
        g_ref[...] = g
        nm_ref[...] = nm
        nv_ref[...] = nv
        d_ref[...] = -ADAM_LR * ((nm / bc1) / (jnp.sqrt(nv / bc2) + ADAM_EPS) + ADAM_WD * w_ref[...])

    spec = pl.BlockSpec((tr, c), lambda i: (i, 0))
    return pl.pallas_call(
        body, name=name, grid=(r // tr,),
        in_specs=[spec, spec, spec, pl.BlockSpec((N_DEV, tr, c), lambda i: (0, i, 0))],
        out_specs=[spec] * 4, out_shape=[jax.ShapeDtypeStruct((r, c), F32)] * 4,
        compiler_params=_params(("parallel",)),
    )(w, m, v, g_parts)


SHARDED = dict(w_in=2, conv_dw_w=2, w_pa=2, gdn_conv_q=2, gdn_conv_k=2, gdn_conv_v=2, w_pb=2, w_pc=2, w_o=1, w_ff1=2, w_ff2=1)
WEIGHTS = ["ln_in_g", "ln_in_b", "w_in", "b_gate", "conv_dw_w", "conv_dw_b", "conv_ln_g", "conv_ln_b", "w_pa", "gdn_conv_q",
           "gdn_conv_k", "gdn_conv_v", "gdn_a_log", "gdn_dt_bias", "gdn_norm_g", "w_pb", "sgu_ln_g", "sgu_ln_b", "sgu_w_s",
           "sgu_b_s", "w_pc", "w_o", "ln1_g", "ln1_b", "w_ff1", "b_ff1", "w_ff2", "b_ff2", "ln2_g", "ln2_b"]
REPLICATED = [n for n in WEIGHTS if n not in SHARDED]
CONV_PACK = ["conv_dw_w", "gdn_conv_q", "gdn_conv_k", "gdn_conv_v"]
PROJ_PACK = ["w_pa", "w_pb", "w_pc"]


def _to_slots(full, axis):
    shp = full.shape
    split = full.reshape(shp[:axis] + (N_DEV, shp[axis] // N_DEV) + shp[axis + 1:])
    return jnp.moveaxis(split, axis, 0)


def _from_slots(slots, axis):
    merged = jnp.moveaxis(slots, 0, axis)
    shp = merged.shape
    return merged.reshape(shp[:axis] + (shp[axis] * shp[axis + 1],) + shp[axis + 2:])


def _pack_rows(arrs):
    rows = []
    for a in arrs:
        flat = a.reshape(-1)
        pad = (-flat.shape[0]) % LANES
        rows.append(jnp.pad(flat, (0, pad)).reshape(-1, LANES))
    out = jnp.concatenate(rows, axis=0)
    return jnp.pad(out, ((0, (-out.shape[0]) % 8), (0, 0)))


def _unpack_rows(packed, shapes):
    out, r = [], 0
    for shp in shapes:
        size = math.prod(shp)
        nrows = -(-size // LANES)
        out.append(packed[r:r + nrows].reshape(-1)[:size].reshape(shp))
        r += nrows
    return out


def kernel(x, ln_in_g, ln_in_b, w_in, b_gate, conv_dw_w, conv_dw_b, conv_ln_g, conv_ln_b, w_pa, gdn_conv_q, gdn_conv_k, gdn_conv_v, gdn_a_log, gdn_dt_bias, gdn_norm_g, w_pb, sgu_ln_g, sgu_ln_b, sgu_w_s, sgu_b_s, w_pc, w_o, ln1_g, ln1_b, w_ff1, b_ff1, w_ff2, b_ff2, ln2_g, ln2_b, loss_target, m_ln_in_g, m_ln_in_b, m_w_in, m_b_gate, m_conv_dw_w, m_conv_dw_b, m_conv_ln_g, m_conv_ln_b, m_w_pa, m_gdn_conv_q, m_gdn_conv_k, m_gdn_conv_v, m_gdn_a_log, m_gdn_dt_bias, m_gdn_norm_g, m_w_pb, m_sgu_ln_g, m_sgu_ln_b, m_sgu_w_s, m_sgu_b_s, m_w_pc, m_w_o, m_ln1_g, m_ln1_b, m_w_ff1, m_b_ff1, m_w_ff2, m_b_ff2, m_ln2_g, m_ln2_b, v_ln_in_g, v_ln_in_b, v_w_in, v_b_gate, v_conv_dw_w, v_conv_dw_b, v_conv_ln_g, v_conv_ln_b, v_w_pa, v_gdn_conv_q, v_gdn_conv_k, v_gdn_conv_v, v_gdn_a_log, v_gdn_dt_bias, v_gdn_norm_g, v_w_pb, v_sgu_ln_g, v_sgu_ln_b, v_sgu_w_s, v_sgu_b_s, v_w_pc, v_w_o, v_ln1_g, v_ln1_b, v_w_ff1, v_b_ff1, v_w_ff2, v_b_ff2, v_ln2_g, v_ln2_b):
    args = locals()
    w = {n: args[n] for n in WEIGHTS}
    m = {n: args["m_" + n] for n in WEIGHTS}
    v = {n: args["v_" + n] for n in WEIGHTS}

    rep = {n: w[n] for n in REPLICATED}
    conv_local = jnp.concatenate([w[n] for n in CONV_PACK], axis=1)
    proj_local = jnp.stack([w[n] for n in PROJ_PACK], axis=1).astype(BF16)
    big = ["w_in", "w_o", "w_ff1", "w_ff2"]
    big_local = {n: w[n].astype(BF16) for n in big}
    rest_local = lambda l: [big_local[n][l] for n in big[1:]] + [proj_local[l], conv_local[l]]

    def rest_full(got):
        full_l = {n: _from_slots(g, SHARDED[n] - 1) for n, g in zip(big[1:], got[:3])}
        proj_full = _from_slots(got[3], 2)
        for i, n in enumerate(PROJ_PACK):
            full_l[n] = proj_full[i]
        conv_full = _from_slots(got[4], 1)
        tap0 = 0
        for n in CONV_PACK:
            taps = w[n].shape[1]
            full_l[n] = conv_full[tap0:tap0 + taps]
            tap0 += taps
        return full_l

    ln_g, ln_b = w["ln_in_g"][None, :], w["ln_in_b"][None, :]
    xs, xs_bf, (w_in0,) = _ln_in_fwd(x[0], ln_g, ln_b, _Exchange([big_local["w_in"][0]], [False]))
    gather = lambda arrays: _Exchange(arrays, [False] * len(arrays))
    xs, xs_bf, sv0, w0, got0 = _layer_fwd(
        xs, xs_bf, _proj_weights(_from_slots(w_in0, 1)), lambda got: _rest_weights(0, rest_full(got), rep), "_l0",
        carry={"proj_fwd": gather(rest_local(0)), "ff1_fwd": gather([big_local["w_in"][1], conv_local[1]])})
    xs, xs_bf, sv1, w1, _ = _layer_fwd(
        xs, xs_bf, _proj_weights(_from_slots(got0["ff1_fwd"][0], 1)),
        lambda got: _rest_weights(1, rest_full(list(got) + [got0["ff1_fwd"][1]]), rep), "_l1",
        carry={"proj_fwd": gather(rest_local(1)[:4])})
    d, loss_acc = _loss_fwd_bwd(xs, loss_target[0])
    loss = lax.psum(loss_acc[0, 0], MESH_AXES)

    def scatter_of(g, names):
        def slots(n):
            if n == "proj":
                return _to_slots(jnp.stack([g[q] for q in PROJ_PACK], axis=0).astype(BF16), 2)
            return _to_slots(g[n].astype(BF16), SHARDED[n] - 1)

        return _Exchange([slots(n) for n in names], [True] * len(names))

    d, grads1, _ = _layer_bwd(d, w1, sv1, "_l1")
    plan = {"dw_ff2": (1, ["w_ff1", "w_ff2"]), "dw_ff1": (1, ["w_in"]), "dw_o": (1, ["w_o", "proj"]),
            "dw_proj": (0, ["w_ff1", "w_ff2", "w_o", "proj"]), "proj_bwd": (0, ["w_in"])}
    d, grads0, got = _layer_bwd(
        d, w0, sv0, "_l0",
        carry={call: (lambda g, l=l, names=names: scatter_of(grads1 if l == 1 else g, names)) for call, (l, names) in plan.items()})
    layer_parts = {(l, n): a for call, (l, names) in plan.items() for n, a in zip(names, got[call])}
    dx, _, d_ln_in_g, d_ln_in_b, _ = _ln_bwd_call(d, x[0], ln_g, name="ln_in_bwd")
    grads = {k: jnp.stack([grads0[k], grads1[k]]) for k in grads0}
    grads["ln_in_g"] = d_ln_in_g[0]
    grads["ln_in_b"] = d_ln_in_b[0]
    conv_grad = jnp.concatenate([grads[n] for n in CONV_PACK], axis=1)
    small_parts = _exchange([_to_slots(conv_grad, 2), _pack_rows([grads[n] for n in REPLICATED])], [True, False],
                            name="exchange_small_grads")
    parts = [jnp.stack([layer_parts[(0, n)], layer_parts[(1, n)]], axis=1) for n in big + ["proj"]] + list(small_parts)

    def adam_sharded(g_parts, w_l, m_l, v_l, name):
        shp = w_l.shape
        two_d = lambda a: a.reshape(-1, shp[-1])
        outs = _adamw(two_d(w_l), two_d(m_l), two_d(v_l), g_parts.reshape(N_DEV, -1, shp[-1]), name=name)
        return [o.reshape(shp) for o in outs]

    res = {}
    for n, gp in zip(big, parts[:4]):
        res[n] = adam_sharded(gp, w[n], m[n], v[n], "adamw_" + n)
    proj_res = adam_sharded(parts[4], jnp.stack([w[n] for n in PROJ_PACK], axis=1), jnp.stack([m[n] for n in PROJ_PACK], axis=1),
                            jnp.stack([v[n] for n in PROJ_PACK], axis=1), "adamw_proj")
    for i, n in enumerate(PROJ_PACK):
        res[n] = [o[:, i] for o in proj_res]
    conv_res = adam_sharded(parts[5], conv_local, jnp.concatenate([m[n] for n in CONV_PACK], axis=1),
                            jnp.concatenate([v[n] for n in CONV_PACK], axis=1), "adamw_conv")
    tap0 = 0
    for n in CONV_PACK:
        taps = w[n].shape[1]
        res[n] = [o[:, tap0:tap0 + taps] for o in conv_res]
        tap0 += taps
    rep_shapes = [w[n].shape for n in REPLICATED]
    rep_res = _adamw(_pack_rows([w[n] for n in REPLICATED]), _pack_rows([m[n] for n in REPLICATED]),
                     _pack_rows([v[n] for n in REPLICATED]), parts[6], name="adamw_replicated")
    rep_res = [_unpack_rows(o, rep_shapes) for o in rep_res]
    for i, n in enumerate(REPLICATED):
        res[n] = [o[i] for o in rep_res]

    outs = [loss, dx[None]]
    for j in range(4):
        outs += [res[n][j] for n in WEIGHTS]
    return tuple(outs)
```
